```python
import math
import jax
import jax.numpy as jnp
from jax import lax
import numpy as np

D_MODEL = 1024
BATCH = 16
SEQ = 4096
DEPTH = 1

N_META = 16
BLOCK = 128
PAD = BLOCK - N_META
NEG = -1e30
LN_EPS = 1e-5

ATT_HEADS = 4
ATT_QK_DIM = 64
ATT_V_DIM = 2 * ATT_QK_DIM
ATT_WIDTH = ATT_HEADS * ATT_V_DIM
ROPE_THETA = 10000.0

ML_HEADS = 4
ML_DH = 128
ML_WIDTH = ML_HEADS * ML_DH
CONV_K = 4

N_EXPERTS = 32
TOP_K = 4
D_EXPERT = D_MODEL
SWIGLU_LIMIT = 7.0
SWIGLU_ALPHA = 1.702
MOE_BLOCK = 512

DN_ALPHA = (2 * DEPTH) ** 0.25
DN_BETA = (8 * DEPTH) ** -0.25

IN_SPLITS = (ATT_HEADS * 2 * ATT_QK_DIM, ATT_HEADS * 2 * ATT_QK_DIM, ATT_WIDTH,
             2 * ML_WIDTH, ML_WIDTH, ML_WIDTH, 2 * ML_HEADS, D_MODEL, D_MODEL)
N_IN = sum(IN_SPLITS)

kernel_name = 'hybrid_diffattn_mlstm_moe_block'


def layer_norm(x, g, b):
    xf = x.astype(jnp.float32)
    mu = jnp.mean(xf, axis=-1, keepdims=True)
    var = jnp.mean(jnp.square(xf - mu), axis=-1, keepdims=True)
    y = (xf - mu) * lax.rsqrt(var + LN_EPS) * g.astype(jnp.float32) + b.astype(jnp.float32)
    return y.astype(x.dtype)


def rms_norm(x, g):
    xf = x.astype(jnp.float32)
    y = xf * lax.rsqrt(jnp.mean(jnp.square(xf), axis=-1, keepdims=True) + LN_EPS) * g.astype(jnp.float32)
    return y.astype(x.dtype)


def rotary(x, pos):
    half = x.shape[-1] // 2
    inv_freq = ROPE_THETA ** (-jnp.arange(half, dtype=jnp.float32) / half)
    ang = pos.astype(jnp.float32)[:, None] * inv_freq[None, :]
    cos = jnp.cos(ang)[:, None, :]
    sin = jnp.sin(ang)[:, None, :]
    xf = x.astype(jnp.float32)
    x1, x2 = xf[..., :half], xf[..., half:]
    return jnp.concatenate([x1 * cos - x2 * sin, x2 * cos + x1 * sin], axis=-1).astype(x.dtype)


def causal_conv(x, w, b):
    y = lax.conv_general_dilated(x, w[:, None, :], window_strides=(1,), padding=((CONV_K - 1, 0),),
                                 dimension_numbers=('NWC', 'WIO', 'NWC'),
                                 feature_group_count=x.shape[-1])
    return y + b


def diff_attention(q, k, v, lam, lambda_init, norm_g):
    B, T = q.shape[0], q.shape[1]
    Tp = T + PAD
    def pad(a):
        return jnp.pad(a, ((0, 0), (PAD, 0)) + ((0, 0),) * (a.ndim - 2))
    qh = pad(q).transpose(0, 2, 3, 1, 4)
    kh = pad(k).transpose(0, 2, 3, 1, 4)
    vh = pad(v).transpose(0, 2, 1, 3)
    key_idx = jnp.arange(Tp)
    scale = ATT_QK_DIM ** -0.5

    def one_block(start):
        qb = lax.dynamic_slice_in_dim(qh, start, BLOCK, axis=3)
        s = jnp.einsum('bhmqd,bhmkd->bhmqk', qb, kh).astype(jnp.float32) * scale
        q_idx = start + jnp.arange(BLOCK)
        mask = (key_idx[None, :] <= q_idx[:, None]) & (key_idx[None, :] >= PAD)
        p = jax.nn.softmax(jnp.where(mask, s, NEG), axis=-1)
        a = p[:, :, 0] - lam * p[:, :, 1]
        return jnp.einsum('bhqk,bhke->bqhe', a.astype(vh.dtype), vh)

    starts = jnp.arange(Tp // BLOCK) * BLOCK
    o = lax.map(one_block, starts)
    o = o.transpose(1, 0, 2, 3, 4).reshape(B, Tp, ATT_HEADS, ATT_V_DIM)[:, PAD:]
    o = rms_norm(o, norm_g) * (1.0 - lambda_init)
    return o.reshape(B, T, ATT_WIDTH)


def mlstm_chunk(carry, inp):
    C, n, m = carry
    q, k, v, ig, lf = inp
    causal = jnp.tril(jnp.ones((BLOCK, BLOCK), dtype=bool))
    b = jnp.cumsum(lf, axis=-1)
    dlog = jnp.where(causal, b[..., :, None] - b[..., None, :] + ig[..., None, :], NEG)
    inter = b + m[..., None]
    m_s = jnp.maximum(inter, jnp.max(dlog, axis=-1))
    w_intra = jnp.exp(dlog - m_s[..., None])
    w_inter = jnp.exp(inter - m_s)
    s = jnp.einsum('bhsd,bhrd->bhsr', q, k) * w_intra
    num = w_inter[..., None] * jnp.einsum('bhed,bhsd->bhse', C, q) + jnp.einsum('bhsr,bhre->bhse', s, v)
    den = w_inter * jnp.einsum('bhd,bhsd->bhs', n, q) + jnp.sum(s, axis=-1)
    h = num / jnp.maximum(jnp.abs(den), jnp.exp(-m_s))[..., None]
    b_last = b[..., -1]
    upd = b_last[..., None] - b + ig
    m_new = jnp.maximum(b_last + m, jnp.max(upd, axis=-1))
    w_old = jnp.exp(b_last + m - m_new)
    w_r = jnp.exp(upd - m_new[..., None])
    C_new = w_old[..., None, None] * C + jnp.einsum('bhre,bhrd->bhed', v * w_r[..., None], k)
    n_new = w_old[..., None] * n + jnp.einsum('bhr,bhrd->bhd', w_r, k)
    return (C_new, n_new, m_new), h


def mlstm(q, k, v, ig, lf):
    B, T = q.shape[0], q.shape[1]
    Tp = T + PAD
    nc = Tp // BLOCK
    def chunks(a, fill):
        a = jnp.pad(a.astype(jnp.float32), ((0, 0), (PAD, 0)) + ((0, 0),) * (a.ndim - 2),
                    constant_values=fill)
        a = a.reshape((B, nc, BLOCK) + a.shape[2:])
        return jnp.swapaxes(jnp.swapaxes(a, 0, 1), 2, 3)
    qc = chunks(q, 0.0)
    kc = chunks(k * (ML_DH ** -0.5), 0.0)
    vc = chunks(v, 0.0)
    igc = chunks(ig, NEG)
    lfc = chunks(lf, 0.0)
    init = (jnp.zeros((B, ML_HEADS, ML_DH, ML_DH), jnp.float32),
            jnp.zeros((B, ML_HEADS, ML_DH), jnp.float32),
            jnp.zeros((B, ML_HEADS), jnp.float32))
    _, h = lax.scan(mlstm_chunk, init, (qc, kc, vc, igc, lfc))
    h = jnp.swapaxes(jnp.swapaxes(h, 2, 3), 0, 1).reshape(B, Tp, ML_HEADS, ML_DH)[:, PAD:]
    return h


def hybrid_mixer(h, pos, lambda_init, w_in, conv_w, conv_b, gate_bias, lam_q1, lam_k1, lam_q2, lam_k2,
                 att_norm_g, ml_norm_g, w_att_out, w_ml_out, w_o):
    B, T, _ = h.shape
    z = h @ w_in
    aq, ak, av, mqk, mv, mo, mgate, ga, gm = jnp.split(z, np.cumsum(IN_SPLITS)[:-1].tolist(), axis=-1)
    aq = rotary(aq.reshape(B, T, 2 * ATT_HEADS, ATT_QK_DIM), pos).reshape(B, T, ATT_HEADS, 2, ATT_QK_DIM)
    ak = rotary(ak.reshape(B, T, 2 * ATT_HEADS, ATT_QK_DIM), pos).reshape(B, T, ATT_HEADS, 2, ATT_QK_DIM)
    lam = (jnp.exp(jnp.sum(lam_q1.astype(jnp.float32) * lam_k1.astype(jnp.float32)))
           - jnp.exp(jnp.sum(lam_q2.astype(jnp.float32) * lam_k2.astype(jnp.float32))) + lambda_init)
    y_att = diff_attention(aq, ak, av.reshape(B, T, ATT_HEADS, ATT_V_DIM), lam, lambda_init, att_norm_g)
    mqk = jax.nn.silu(causal_conv(mqk, conv_w, conv_b))
    mq, mk = jnp.split(mqk, 2, axis=-1)
    gates = mgate.astype(jnp.float32) + gate_bias.astype(jnp.float32)
    ig = gates[..., :ML_HEADS]
    lf = jax.nn.log_sigmoid(gates[..., ML_HEADS:])
    hm = mlstm(mq.reshape(B, T, ML_HEADS, ML_DH), mk.reshape(B, T, ML_HEADS, ML_DH),
               mv.reshape(B, T, ML_HEADS, ML_DH), ig, lf)
    hm = rms_norm(hm, ml_norm_g.reshape(ML_HEADS, ML_DH)).astype(h.dtype)
    y_ml = jax.nn.sigmoid(mo) * hm.reshape(B, T, ML_WIDTH)
    merged = jax.nn.sigmoid(ga) * (y_att @ w_att_out) + jax.nn.sigmoid(gm) * (y_ml @ w_ml_out)
    return merged @ w_o


def moe(x, w_router, b_router, w_gu, b_gu, w_down, b_down):
    N, D = x.shape
    logits = x.astype(jnp.float32) @ w_router.astype(jnp.float32) + b_router.astype(jnp.float32)
    top_val, top_idx = lax.top_k(logits, TOP_K)
    gate = jax.nn.softmax(top_val, axis=-1)
    M = N * TOP_K
    e_flat = top_idx.reshape(M).astype(jnp.int32)
    t_flat = jnp.repeat(jnp.arange(N, dtype=jnp.int32), TOP_K)
    order = jnp.argsort(e_flat)
    e_s, t_s, g_s = e_flat[order], t_flat[order], gate.reshape(M)[order]
    counts = jnp.bincount(e_flat, length=N_EXPERTS)
    starts = jnp.cumsum(counts) - counts
    padded = (counts + MOE_BLOCK - 1) // MOE_BLOCK * MOE_BLOCK
    pends = jnp.cumsum(padded)
    pstarts = pends - padded
    dest = pstarts[e_s] + jnp.arange(M, dtype=jnp.int32) - starts[e_s]
    nb = -(-(M + N_EXPERTS * (MOE_BLOCK - 1)) // MOE_BLOCK)
    tok_buf = jnp.full((nb * MOE_BLOCK,), N, jnp.int32).at[dest].set(t_s).reshape(nb, MOE_BLOCK)
    gate_buf = jnp.zeros((nb * MOE_BLOCK,), jnp.float32).at[dest].set(g_s).reshape(nb, MOE_BLOCK)
    blk_exp = jnp.minimum(jnp.searchsorted(pends, jnp.arange(nb, dtype=jnp.int32) * MOE_BLOCK, side='right'),
                          N_EXPERTS - 1)
    x_ext = jnp.concatenate([x, jnp.zeros((1, D), x.dtype)], axis=0)

    def expert_block(acc, inp):
        toks, g, e = inp
        hid = x_ext[toks] @ w_gu[e] + b_gu[e]
        glu = jnp.minimum(hid[:, ::2], SWIGLU_LIMIT)
        lin = jnp.clip(hid[:, 1::2], -SWIGLU_LIMIT, SWIGLU_LIMIT)
        act = glu * jax.nn.sigmoid(SWIGLU_ALPHA * glu) * (lin + 1.0)
        y = (act @ w_down[e] + b_down[e]) * g.astype(x.dtype)[:, None]
        return acc.at[toks].add(y.astype(acc.dtype)), None

    acc, _ = lax.scan(expert_block, jnp.zeros((N + 1, D), x.dtype), (tok_buf, gate_buf, blk_exp))
    return acc[:N]


def setup_inputs(seed: int = 0) -> dict:
    key = jax.random.key(seed)
    ks = jax.random.split(key, 32)
    f32 = jnp.float32
    L, D, E, F = DEPTH, D_MODEL, N_EXPERTS, D_EXPERT

    def nrm(k, shape, scale):
        return jax.random.normal(k, shape, f32) * scale

    f_bias = jnp.linspace(3.0, 6.0, ML_HEADS, dtype=f32)[None, :] + nrm(ks[7], (L, ML_HEADS), 0.1)
    i_bias = nrm(ks[8], (L, ML_HEADS), 0.1)
    return {
        'x': nrm(ks[0], (BATCH, SEQ, D), 1.0),
        'meta': nrm(ks[1], (N_META, D), 1.0),
        'emb_ln_g': 1.0 + nrm(ks[2], (D,), 0.02),
        'emb_ln_b': nrm(ks[3], (D,), 0.02),
        'w_in': nrm(ks[4], (L, D, N_IN), D ** -0.5),
        'conv_w': nrm(ks[5], (L, CONV_K, 2 * ML_WIDTH), CONV_K ** -0.5),
        'conv_b': nrm(ks[6], (L, 2 * ML_WIDTH), 0.02),
        'gate_bias': jnp.concatenate([i_bias, f_bias], axis=-1),
        'lam_q1': nrm(ks[9], (L, ATT_QK_DIM), 0.1),
        'lam_k1': nrm(ks[10], (L, ATT_QK_DIM), 0.1),
        'lam_q2': nrm(ks[11], (L, ATT_QK_DIM), 0.1),
        'lam_k2': nrm(ks[12], (L, ATT_QK_DIM), 0.1),
        'att_norm_g': 1.0 + nrm(ks[13], (L, ATT_V_DIM), 0.02),
        'ml_norm_g': 1.0 + nrm(ks[14], (L, ML_WIDTH), 0.02),
        'w_att_out': nrm(ks[15], (L, ATT_WIDTH, D), ATT_WIDTH ** -0.5 * DN_BETA),
        'w_ml_out': nrm(ks[16], (L, ML_WIDTH, D), ML_WIDTH ** -0.5 * DN_BETA),
        'w_o': nrm(ks[17], (L, D, D), D ** -0.5 * DN_BETA),
        'ln1_g': 1.0 + nrm(ks[18], (L, D), 0.02),
        'ln1_b': nrm(ks[19], (L, D), 0.02),
        'w_router': nrm(ks[20], (L, D, E), D ** -0.5),
        'b_router': nrm(ks[21], (L, E), 0.01),
        'w_gu': nrm(ks[22], (L, E, D, 2 * F), D ** -0.5),
        'b_gu': nrm(ks[23], (L, E, 2 * F), 0.02),
        'w_down': nrm(ks[24], (L, E, F, D), F ** -0.5 * DN_BETA),
        'b_down': nrm(ks[25], (L, E, D), 0.02),
        'ln2_g': 1.0 + nrm(ks[26], (L, D), 0.02),
        'ln2_b': nrm(ks[27], (L, D), 0.02),
    }


def reference(x, meta, emb_ln_g, emb_ln_b, w_in, conv_w, conv_b, gate_bias, lam_q1, lam_k1, lam_q2, lam_k2,
              att_norm_g, ml_norm_g, w_att_out, w_ml_out, w_o, ln1_g, ln1_b, w_router, b_router,
              w_gu, b_gu, w_down, b_down, ln2_g, ln2_b):
    B = x.shape[0]
    h = jnp.concatenate([jnp.broadcast_to(meta[None], (B, N_META, D_MODEL)).astype(x.dtype), x], axis=1)
    h = layer_norm(h, emb_ln_g, emb_ln_b)
    T = h.shape[1]
    pos = jnp.arange(T)
    for l in range(DEPTH):
        lambda_init = 0.8 - 0.6 * math.exp(-0.3 * l)
        mix = hybrid_mixer(h, pos, lambda_init, w_in[l], conv_w[l], conv_b[l], gate_bias[l],
                           lam_q1[l], lam_k1[l], lam_q2[l], lam_k2[l], att_norm_g[l], ml_norm_g[l],
                           w_att_out[l], w_ml_out[l], w_o[l])
        h = layer_norm(DN_ALPHA * h + mix, ln1_g[l], ln1_b[l])
        ffn = moe(h.reshape(B * T, D_MODEL), w_router[l], b_router[l], w_gu[l], b_gu[l],
                  w_down[l], b_down[l]).reshape(B, T, D_MODEL)
        h = layer_norm(DN_ALPHA * h + ffn, ln2_g[l], ln2_b[l])
    return h[:, N_META:]
```

```python
import functools
import math

import jax
import jax.numpy as jnp
import numpy as np
from jax import lax
from jax.experimental import pallas as pl
from jax.experimental.pallas import tpu as pltpu

D_MODEL = 1024
N_META = 16
BLOCK = 128
PAD = BLOCK - N_META
NEG = -1e30
LN_EPS = 1e-5
ATT_HEADS = 4
ATT_QK_DIM = 64
ATT_V_DIM = 128
ROPE_THETA = 10000.0
ML_HEADS = 4
ML_DH = 128
ML_WIDTH = ML_HEADS * ML_DH
CONV_K = 4
N_EXPERTS = 32
TOP_K = 4
SWIGLU_LIMIT = 7.0
SWIGLU_ALPHA = 1.702
DEPTH = 1
DN_ALPHA = (2 * DEPTH) ** 0.25
LAMBDA_INIT = 0.8 - 0.6 * math.exp(-0.3 * 0)

LANES = 128
C_GA, C_GM, C_MQK, C_AQ, C_AK, C_AV, C_MV, C_MO = 0, 1024, 2048, 3072, 3584, 4096, 4608, 5120
ZB_W = 5632
GATE_W = 2 * LANES
W_ALL = ZB_W + GATE_W
CHUNK = 512

VMEM_LIMIT = 56 * 1024 * 1024

BF16 = jnp.bfloat16
F32 = jnp.float32


def _cparams(n_axes):
    return pltpu.CompilerParams(dimension_semantics=("arbitrary",) * n_axes, vmem_limit_bytes=VMEM_LIMIT)


def _ln(x, g, b):
    mu = jnp.mean(x, axis=-1, keepdims=True)
    xc = x - mu
    var = jnp.mean(xc * xc, axis=-1, keepdims=True)
    return xc * lax.rsqrt(var + LN_EPS) * g + b


def _sigmoid(x):
    return 1.0 / (1.0 + jnp.exp(-x))


def _inproj_kernel(x_ref, g_ref, b_ref, w_ref, cos_ref, sin_ref, zb_ref, gt_ref, *, first_valid_row):
    x = x_ref[0]
    tm = x.shape[0]
    hb = _ln(x, g_ref[...], b_ref[...]).astype(BF16)
    cos = cos_ref[...]
    sin = sin_ref[...]
    if first_valid_row:
        rowmask = lax.broadcasted_iota(jnp.int32, (tm, 1), 0) >= first_valid_row
    for c in range(ZB_W // CHUNK):
        z = jnp.dot(hb, w_ref[:, c * CHUNK:(c + 1) * CHUNK], preferred_element_type=F32)
        if c * CHUNK in (C_AQ, C_AK):
            parts = []
            for h in range(ATT_HEADS):
                zh = z[:, h * LANES:(h + 1) * LANES]
                parts.append(zh * cos + pltpu.roll(zh, 64, 1) * sin)
            z = jnp.concatenate(parts, axis=1)
            if c * CHUNK == C_AQ:
                z = z * (ATT_QK_DIM ** -0.5)
        if first_valid_row:
            z = jnp.where(rowmask, z, 0.0)
        zb_ref[0, :, c * CHUNK:(c + 1) * CHUNK] = z.astype(BF16)
    zg = jnp.dot(hb, w_ref[:, ZB_W:W_ALL], preferred_element_type=F32)
    if first_valid_row:
        zg = jnp.where(rowmask, zg, 0.0)
    gt_ref[0] = zg


def _inproj(x3, g, b, w, cos, sin, *, tm, first_valid_row=0):
    B, S, D = x3.shape
    nt = S // tm
    return pl.pallas_call(
        functools.partial(_inproj_kernel, first_valid_row=first_valid_row),
        grid=(B, nt),
        in_specs=[
            pl.BlockSpec((1, tm, D), lambda bi, i: (bi, i, 0)),
            pl.BlockSpec((1, D), lambda bi, i: (0, 0)),
            pl.BlockSpec((1, D), lambda bi, i: (0, 0)),
            pl.BlockSpec((D, W_ALL), lambda bi, i: (0, 0)),
            pl.BlockSpec((tm, LANES), lambda bi, i: (i, 0)),
            pl.BlockSpec((tm, LANES), lambda bi, i: (i, 0)),
        ],
        out_specs=[
            pl.BlockSpec((1, tm, ZB_W), lambda bi, i: (bi, i, 0)),
            pl.BlockSpec((1, tm, GATE_W), lambda bi, i: (bi, i, 0)),
        ],
        out_shape=[
            jax.ShapeDtypeStruct((B, S, ZB_W), BF16),
            jax.ShapeDtypeStruct((B, S, GATE_W), F32),
        ],
        compiler_params=_cparams(2),
        name="inproj",
    )(x3, g, b, w, cos, sin)


def _attn_kernel(q_ref, k_ref, v_ref, km_ref, vm_ref, lq1_ref, lk1_ref, lq2_ref, lk2_ref, ng_ref, o_ref,
                 m_sc, l_sc, acc_sc, *, tq, tk):
    qi = pl.program_id(2)
    q = q_ref[0]
    lane = lax.broadcasted_iota(jnp.int32, (tq, LANES), 1)
    is_map0 = (lane & 63) < 32
    zero = jnp.zeros_like(q)
    qq = jnp.concatenate([jnp.where(is_map0, q, zero), jnp.where(is_map0, zero, q)], axis=0)

    def scores(kblk):
        return lax.dot_general(qq, kblk, (((1,), (1,)), ((), ())), preferred_element_type=F32)

    s = scores(km_ref[0])
    colm = lax.broadcasted_iota(jnp.int32, s.shape, 1)
    s = jnp.where(colm >= PAD, s, NEG)
    m0 = jnp.max(s, axis=1, keepdims=True)
    p = jnp.exp(s - m0)
    m_sc[...] = m0
    l_sc[...] = jnp.sum(p, axis=1, keepdims=True)
    acc_sc[...] = jnp.dot(p.astype(BF16), vm_ref[0], preferred_element_type=F32)

    def update(s, vblk):
        m_prev = m_sc[...]
        m_new = jnp.maximum(m_prev, jnp.max(s, axis=1, keepdims=True))
        alpha = jnp.exp(m_prev - m_new)
        p = jnp.exp(s - m_new)
        l_sc[...] = alpha * l_sc[...] + jnp.sum(p, axis=1, keepdims=True)
        acc_sc[...] = alpha * acc_sc[...] + jnp.dot(p.astype(BF16), vblk, preferred_element_type=F32)
        m_sc[...] = m_new

    def full_block(j, carry):
        start = pl.multiple_of(j * tk, tk)
        update(scores(k_ref[0, pl.ds(start, tk), :]), v_ref[0, pl.ds(start, tk), :])
        return carry

    lax.fori_loop(0, qi * (tq // tk), full_block, 0)

    row = lax.broadcasted_iota(jnp.int32, (2 * tq, tk), 0)
    rowq = jnp.where(row >= tq, row - tq, row)
    col = lax.broadcasted_iota(jnp.int32, (2 * tq, tk), 1)
    for d in range(tq // tk):
        start = pl.multiple_of(qi * tq + d * tk, tk)
        s = scores(k_ref[0, pl.ds(start, tk), :])
        s = jnp.where(col + d * tk <= rowq, s, NEG)
        update(s, v_ref[0, pl.ds(start, tk), :])

    o = acc_sc[...] / l_sc[...]
    s1 = jnp.sum(lq1_ref[...] * lk1_ref[...], axis=1, keepdims=True)
    s2 = jnp.sum(lq2_ref[...] * lk2_ref[...], axis=1, keepdims=True)
    lam = jnp.exp(s1) - jnp.exp(s2) + LAMBDA_INIT
    a = o[:tq] - lam * o[tq:]
    y = a * lax.rsqrt(jnp.mean(a * a, axis=1, keepdims=True) + LN_EPS) * ng_ref[...]
    o_ref[0] = (y * (1.0 - LAMBDA_INIT)).astype(BF16)


def _attention(zb, zbm, lq1, lk1, lq2, lk2, ng, *, tq, tk):
    B, S, _ = zb.shape
    nq = S // tq
    qb, kb, vb = C_AQ // LANES, C_AK // LANES, C_AV // LANES
    lam_spec = pl.BlockSpec((1, ATT_QK_DIM), lambda b, h, i: (0, 0))
    return pl.pallas_call(
        functools.partial(_attn_kernel, tq=tq, tk=tk),
        grid=(B, ATT_HEADS, nq),
        in_specs=[
            pl.BlockSpec((1, tq, LANES), lambda b, h, i: (b, i, qb + h)),
            pl.BlockSpec((1, S, LANES), lambda b, h, i: (b, 0, kb + h)),
            pl.BlockSpec((1, S, LANES), lambda b, h, i: (b, 0, vb + h)),
            pl.BlockSpec((1, BLOCK, LANES), lambda b, h, i: (0, 0, kb + h)),
            pl.BlockSpec((1, BLOCK, LANES), lambda b, h, i: (0, 0, vb + h)),
            lam_spec, lam_spec, lam_spec, lam_spec,
            pl.BlockSpec((1, ATT_V_DIM), lambda b, h, i: (0, 0)),
        ],
        out_specs=pl.BlockSpec((1, tq, LANES), lambda b, h, i: (b, i, h)),
        out_shape=jax.ShapeDtypeStruct((B, S, ATT_HEADS * ATT_V_DIM), BF16),
        scratch_shapes=[
            pltpu.VMEM((2 * tq, 1), F32),
            pltpu.VMEM((2 * tq, 1), F32),
            pltpu.VMEM((2 * tq, ATT_V_DIM), F32),
        ],
        compiler_params=_cparams(3),
        name="diff_attention",
    )(zb, zb, zb, zbm, zbm, lq1, lk1, lq2, lk2, ng)


def _mlstm_kernel(xqk_ref, xv_ref, xo_ref, xg_ref, zm_ref, gm_ref, cw_ref, cb_ref, gb_ref, ng_ref, o_ref,
                  ct_sc, n_sc, m_sc, tail_sc, cbuf_sc):
    c = pl.program_id(1)
    L = BLOCK

    @pl.when(c == 0)
    def _():
        ct_sc[...] = jnp.zeros_like(ct_sc)
        n_sc[...] = jnp.zeros_like(n_sc)
        m_sc[...] = jnp.zeros_like(m_sc)
        tail_sc[...] = jnp.zeros_like(tail_sc)

    is_meta = c == 0
    row = lax.broadcasted_iota(jnp.int32, (L, 1), 0)
    valid = jnp.logical_or(c > 0, row >= PAD)

    qk_pre = jnp.where(is_meta, zm_ref[0, :, C_MQK:C_MQK + 2 * ML_WIDTH], xqk_ref[0]).astype(F32)
    v = jnp.where(is_meta, zm_ref[0, :, C_MV:C_MV + ML_WIDTH], xv_ref[0])
    gates = jnp.where(is_meta, gm_ref[0], xg_ref[0])

    cbuf_sc[0:8, :] = tail_sc[...]
    cbuf_sc[8:8 + L, :] = qk_pre
    tail_sc[...] = qk_pre[L - 8:L, :]
    acc = jnp.broadcast_to(cb_ref[...], (L, 2 * ML_WIDTH))
    for j in range(CONV_K):
        off = 8 - (CONV_K - 1) + j
        acc = acc + cw_ref[j:j + 1, :] * cbuf_sc[off:off + L, :]
    qk = acc * _sigmoid(acc)
    qk = jnp.where(valid, qk, 0.0)
    q = qk[:, :ML_WIDTH]
    k = qk[:, ML_WIDTH:] * (ML_DH ** -0.5)

    ig = gates[:, :LANES] + gb_ref[:, :LANES]
    fg = gates[:, LANES:] + gb_ref[:, LANES:]
    lf = jnp.minimum(fg, 0.0) - jnp.log1p(jnp.exp(-jnp.abs(fg)))
    ig = jnp.where(valid, ig, NEG)
    lf = jnp.where(valid, lf, 0.0)

    sidx = lax.broadcasted_iota(jnp.int32, (L, L), 0)
    ridx = lax.broadcasted_iota(jnp.int32, (L, L), 1)
    causal = ridx <= sidx
    tri = causal.astype(BF16)
    lf_hi = lf.astype(BF16)
    lf_lo = (lf - lf_hi.astype(F32)).astype(BF16)
    bcs = (jnp.dot(tri, lf_hi, preferred_element_type=F32)
           + jnp.dot(tri, lf_lo, preferred_element_type=F32))
    b_t = bcs.T
    ig_t = ig.T

    outs = []
    for h in range(ML_HEADS):
        sl = slice(h * ML_DH, (h + 1) * ML_DH)
        bc = bcs[:, h:h + 1]
        br = b_t[h:h + 1, :]
        igr = ig_t[h:h + 1, :]
        igc = ig[:, h:h + 1]
        m_h = m_sc[h:h + 1, 0:1]
        dlog = jnp.where(causal, (bc - br) + igr, NEG)
        inter = bc + m_h
        m_s = jnp.maximum(inter, jnp.max(dlog, axis=1, keepdims=True))
        w_intra = jnp.exp(dlog - m_s)
        w_inter = jnp.exp(inter - m_s)
        qh = q[:, sl]
        kh = k[:, sl]
        vh = v[:, sl]
        qb = qh.astype(BF16)
        kb = kh.astype(BF16)
        s = lax.dot_general(qb, kb, (((1,), (1,)), ((), ())), preferred_element_type=F32) * w_intra
        ct = ct_sc[h]
        nrow = n_sc[h:h + 1, :]
        num = (w_inter * jnp.dot(qb, ct.astype(BF16), preferred_element_type=F32)
               + jnp.dot(s.astype(BF16), vh, preferred_element_type=F32))
        den = w_inter * jnp.sum(qh * nrow, axis=1, keepdims=True) + jnp.sum(s, axis=1, keepdims=True)
        hh = num / jnp.maximum(jnp.abs(den), jnp.exp(-m_s))
        b_last = bc[L - 1:L, :]
        upd = (b_last - bc) + igc
        m_new = jnp.maximum(b_last + m_h, jnp.max(upd, axis=0, keepdims=True))
        w_old = jnp.exp(b_last + m_h - m_new)
        w_r = jnp.exp(upd - m_new)
        vw = (vh.astype(F32) * w_r).astype(BF16)
        ct_sc[h] = w_old * ct + lax.dot_general(kb, vw, (((0,), (0,)), ((), ())), preferred_element_type=F32)
        n_sc[h:h + 1, :] = w_old * nrow + jnp.sum(kh * w_r, axis=0, keepdims=True)
        m_sc[h:h + 1, :] = jnp.broadcast_to(m_new, (1, LANES))
        hn = hh * lax.rsqrt(jnp.mean(hh * hh, axis=1, keepdims=True) + LN_EPS) * ng_ref[:, sl]
        outs.append(_sigmoid(xo_ref[0, :, sl].astype(F32)) * hn)

    @pl.when(c > 0)
    def _():
        o_ref[0] = jnp.concatenate(outs, axis=1).astype(BF16)


def _mlstm(zb, gates, zbm, gatesm, conv_w, conv_b, gate_b, ng):
    B, S, _ = zb.shape
    nc = S // BLOCK + 1

    def xc(c):
        return jnp.maximum(c - 1, 0)

    return pl.pallas_call(
        _mlstm_kernel,
        grid=(B, nc),
        in_specs=[
            pl.BlockSpec((1, BLOCK, 2 * ML_WIDTH), lambda b, c: (b, xc(c), C_MQK // (2 * ML_WIDTH))),
            pl.BlockSpec((1, BLOCK, ML_WIDTH), lambda b, c: (b, xc(c), C_MV // ML_WIDTH)),
            pl.BlockSpec((1, BLOCK, ML_WIDTH), lambda b, c: (b, xc(c), C_MO // ML_WIDTH)),
            pl.BlockSpec((1, BLOCK, GATE_W), lambda b, c: (b, xc(c), 0)),
            pl.BlockSpec((1, BLOCK, ZB_W), lambda b, c: (0, 0, 0)),
            pl.BlockSpec((1, BLOCK, GATE_W), lambda b, c: (0, 0, 0)),
            pl.BlockSpec((CONV_K, 2 * ML_WIDTH), lambda b, c: (0, 0)),
            pl.BlockSpec((1, 2 * ML_WIDTH), lambda b, c: (0, 0)),
            pl.BlockSpec((1, GATE_W), lambda b, c: (0, 0)),
            pl.BlockSpec((1, ML_WIDTH), lambda b, c: (0, 0)),
        ],
        out_specs=pl.BlockSpec((1, BLOCK, ML_WIDTH), lambda b, c: (b, xc(c), 0)),
        out_shape=jax.ShapeDtypeStruct((B, S, ML_WIDTH), BF16),
        scratch_shapes=[
            pltpu.VMEM((ML_HEADS, ML_DH, ML_DH), F32),
            pltpu.VMEM((8, ML_DH), F32),
            pltpu.VMEM((8, LANES), F32),
            pltpu.VMEM((8, 2 * ML_WIDTH), F32),
            pltpu.VMEM((8 + BLOCK, 2 * ML_WIDTH), F32),
        ],
        compiler_params=_cparams(2),
        name="mlstm",
    )(zb, zb, zb, gates, zbm, gatesm, conv_w, conv_b, gate_b, ng)


def _outproj_kernel(x_ref, ya_ref, ym_ref, ga_ref, gmm_ref, eg_ref, eb_ref, wa_ref, wm_ref, wo_ref,
                    g1_ref, b1_ref, wr_ref, br_ref,
                    h1_ref, eid_ref, gate_ref, rank_ref, cnt_ref, run_sc, *, tm):
    i = pl.program_id(0)

    @pl.when(i == 0)
    def _():
        run_sc[...] = jnp.zeros_like(run_sc)

    h0 = _ln(x_ref[...], eg_ref[...], eb_ref[...])
    pa = jnp.dot(ya_ref[...], wa_ref[...], preferred_element_type=F32)
    pm = jnp.dot(ym_ref[...], wm_ref[...], preferred_element_type=F32)
    merged = _sigmoid(ga_ref[...].astype(F32)) * pa + _sigmoid(gmm_ref[...].astype(F32)) * pm
    mix = jnp.dot(merged.astype(BF16), wo_ref[...], preferred_element_type=F32)
    h1 = _ln(DN_ALPHA * h0 + mix, g1_ref[...], b1_ref[...])
    h1_ref[...] = h1

    logits = jnp.dot(h1, wr_ref[...], preferred_element_type=F32, precision=lax.Precision.HIGHEST) + br_ref[...]
    lane = lax.broadcasted_iota(jnp.int32, (tm, LANES), 1)
    work = logits
    sel_e, sel_v = [], []
    for _ in range(TOP_K):
        mv = jnp.max(work, axis=1, keepdims=True)
        e = jnp.min(jnp.where(work == mv, lane, LANES), axis=1, keepdims=True)
        sel_e.append(e)
        sel_v.append(mv)
        work = jnp.where(lane == e, -jnp.inf, work)
    ex = [jnp.exp(v - sel_v[0]) for v in sel_v]
    den = ex[0] + ex[1] + ex[2] + ex[3]

    onehots = [lane == e for e in sel_e]
    oh = jnp.zeros((tm, LANES), F32)
    for o in onehots:
        oh = oh + o.astype(F32)
    r_i = lax.broadcasted_iota(jnp.int32, (tm, tm), 0)
    c_i = lax.broadcasted_iota(jnp.int32, (tm, tm), 1)
    strict = (c_i < r_i).astype(BF16)
    before = jnp.dot(strict, oh.astype(BF16), preferred_element_type=F32) + run_sc[...]
    eid = jnp.zeros((tm, LANES), jnp.int32)
    gate = jnp.zeros((tm, LANES), F32)
    rank = jnp.zeros((tm, LANES), F32)
    for kk in range(TOP_K):
        here = lane == kk
        eid = jnp.where(here, sel_e[kk], eid)
        gate = jnp.where(here, ex[kk] / den, gate)
        rk = jnp.sum(jnp.where(onehots[kk], before, 0.0), axis=1, keepdims=True)
        rank = jnp.where(here, rk, rank)
    eid_ref[...] = eid
    gate_ref[...] = gate
    rank_ref[...] = rank.astype(jnp.int32)
    run_sc[...] = run_sc[...] + jnp.sum(oh, axis=0, keepdims=True)
    cnt_ref[...] = run_sc[...]


def _outproj(x2, yatt, yml, zb2, eg, eb, wa, wm, wo, g1, b1, wr, br, *, tm):
    N, D = x2.shape
    vec = lambda w: pl.BlockSpec((1, w), lambda i: (0, 0))
    full = lambda a, b: pl.BlockSpec((a, b), lambda i: (0, 0))
    tile = lambda w: pl.BlockSpec((tm, w), lambda i: (i, 0))
    return pl.pallas_call(
        functools.partial(_outproj_kernel, tm=tm),
        grid=(N // tm,),
        in_specs=[
            tile(D), tile(ATT_HEADS * ATT_V_DIM), tile(ML_WIDTH),
            pl.BlockSpec((tm, D), lambda i: (i, C_GA // D)),
            pl.BlockSpec((tm, D), lambda i: (i, C_GM // D)),
            vec(D), vec(D),
            full(ATT_HEADS * ATT_V_DIM, D), full(ML_WIDTH, D), full(D, D),
            vec(D), vec(D), full(D, LANES), vec(LANES),
        ],
        out_specs=[tile(D), tile(LANES), tile(LANES), tile(LANES), vec(LANES)],
        out_shape=[
            jax.ShapeDtypeStruct((N, D), F32),
            jax.ShapeDtypeStruct((N, LANES), jnp.int32),
            jax.ShapeDtypeStruct((N, LANES), F32),
            jax.ShapeDtypeStruct((N, LANES), jnp.int32),
            jax.ShapeDtypeStruct((1, LANES), F32),
        ],
        scratch_shapes=[pltpu.VMEM((1, LANES), F32)],
        compiler_params=_cparams(1),
        name="outproj_router",
    )(x2, yatt, yml, zb2, zb2, eg, eb, wa, wm, wo, g1, b1, wr, br)


def _row_copies(src_row, dst_row, sem, n, issue):
    def body(r, carry):
        for kk in range(TOP_K):
            cp = pltpu.make_async_copy(src_row(r, kk), dst_row(r, kk), sem)
            if issue:
                cp.start()
            else:
                cp.wait()
        return carry
    lax.fori_loop(0, n, body, 0)


def _dispatch_kernel(d_ref, h_ref, xs_in_ref, xs_ref, sem, *, tm):
    del xs_in_ref
    src = lambda r, kk: h_ref.at[pl.ds(r, 1)]
    dst = lambda r, kk: xs_ref.at[pl.ds(d_ref[0, 0, r * TOP_K + kk], 1)]
    _row_copies(src, dst, sem.at[0], tm, True)
    _row_copies(src, dst, sem.at[0], tm, False)


def _dispatch(dest3, h1, xs0, *, tm):
    N, D = h1.shape
    return pl.pallas_call(
        functools.partial(_dispatch_kernel, tm=tm),
        grid=(N // tm,),
        in_specs=[
            pl.BlockSpec((1, 1, tm * TOP_K), lambda i: (i, 0, 0), memory_space=pltpu.SMEM),
            pl.BlockSpec((tm, D), lambda i: (i, 0)),
            pl.BlockSpec(memory_space=pl.ANY),
        ],
        out_specs=pl.BlockSpec(memory_space=pl.ANY),
        out_shape=jax.ShapeDtypeStruct(xs0.shape, F32),
        scratch_shapes=[pltpu.SemaphoreType.DMA((1,))],
        input_output_aliases={2: 0},
        compiler_params=_cparams(1),
        name="moe_dispatch",
    )(dest3, h1, xs0)


def _expert_kernel(be_ref, nu_ref, x_ref, wg_ref, wl_ref, bg_ref, bl_ref, wd_ref, bd_ref, y_ref):
    del be_ref

    @pl.when(pl.program_id(0) < nu_ref[0])
    def _():
        xb = x_ref[...].astype(BF16)
        hg = jnp.dot(xb, wg_ref[0], preferred_element_type=F32) + bg_ref[0]
        hl = jnp.dot(xb, wl_ref[0], preferred_element_type=F32) + bl_ref[0]
        glu = jnp.minimum(hg, SWIGLU_LIMIT)
        lin = jnp.clip(hl, -SWIGLU_LIMIT, SWIGLU_LIMIT)
        act = glu * _sigmoid(SWIGLU_ALPHA * glu) * (lin + 1.0)
        y_ref[...] = jnp.dot(act.astype(BF16), wd_ref[0], preferred_element_type=F32) + bd_ref[0]


def _experts(blk_exp, n_used, xs, wg, wl, bg, bl, wd, bd, *, tm):
    MP, D = xs.shape
    F = wg.shape[2]
    row = lambda i, be, nu: (jnp.minimum(i, nu[0] - 1), 0)
    wsp = lambda a, b: pl.BlockSpec((1, a, b), lambda i, be, nu: (be[i], 0, 0))
    return pl.pallas_call(
        _expert_kernel,
        grid_spec=pltpu.PrefetchScalarGridSpec(
            num_scalar_prefetch=2,
            grid=(MP // tm,),
            in_specs=[pl.BlockSpec((tm, D), row), wsp(D, F), wsp(D, F), wsp(1, F), wsp(1, F), wsp(F, D), wsp(1, D)],
            out_specs=pl.BlockSpec((tm, D), row),
        ),
        out_shape=jax.ShapeDtypeStruct((MP, D), F32),
        compiler_params=_cparams(1),
        name="moe_experts",
    )(blk_exp, n_used, xs, wg, wl, bg, bl, wd, bd)


def _combine_kernel(dcur_ref, dnxt_ref, h1_ref, gate_ref, g2_ref, b2_ref, ys_ref, o_ref, ybuf, sem, *, tm, nsteps):
    i = pl.program_id(0)
    slot = i % 2

    def copies(d_ref, s, issue):
        src = lambda r, kk: ys_ref.at[pl.ds(d_ref[0, 0, r * TOP_K + kk], 1)]
        dst = lambda r, kk: ybuf.at[s, kk, pl.ds(r, 1)]
        _row_copies(src, dst, sem.at[s], tm, issue)

    @pl.when(i == 0)
    def _():
        copies(dcur_ref, 0, True)

    @pl.when(i + 1 < nsteps)
    def _():
        copies(dnxt_ref, 1 - slot, True)

    copies(dcur_ref, slot, False)
    acc = DN_ALPHA * h1_ref[...]
    for kk in range(TOP_K):
        acc = acc + gate_ref[:, kk:kk + 1] * ybuf[slot, kk]
    o_ref[...] = _ln(acc, g2_ref[...], b2_ref[...])


def _combine(dest3, h1, gate, g2, b2, ys, *, tm):
    N, D = h1.shape
    nsteps = N // tm
    dspec = lambda f: pl.BlockSpec((1, 1, tm * TOP_K), f, memory_space=pltpu.SMEM)
    return pl.pallas_call(
        functools.partial(_combine_kernel, tm=tm, nsteps=nsteps),
        grid=(nsteps,),
        in_specs=[
            dspec(lambda i: (i, 0, 0)),
            dspec(lambda i: (jnp.minimum(i + 1, nsteps - 1), 0, 0)),
            pl.BlockSpec((tm, D), lambda i: (i, 0)),
            pl.BlockSpec((tm, LANES), lambda i: (i, 0)),
            pl.BlockSpec((1, D), lambda i: (0, 0)),
            pl.BlockSpec((1, D), lambda i: (0, 0)),
            pl.BlockSpec(memory_space=pl.ANY),
        ],
        out_specs=pl.BlockSpec((tm, D), lambda i: (i, 0)),
        out_shape=jax.ShapeDtypeStruct((N, D), F32),
        scratch_shapes=[pltpu.VMEM((2, TOP_K, tm, D), F32), pltpu.SemaphoreType.DMA((2,))],
        compiler_params=_cparams(1),
        name="moe_combine",
    )(dest3, dest3, h1, gate, g2, b2, ys)


def _rotary_column_order():
    idx = np.empty((ATT_HEADS * LANES,), np.int32)
    for h in range(ATT_HEADS):
        for half in range(2):
            for sub in range(2):
                for dd in range(32):
                    idx[h * LANES + half * 64 + sub * 32 + dd] = (2 * h + sub) * ATT_QK_DIM + half * 32 + dd
    return idx


def _rotary_tables(pos):
    half = ATT_QK_DIM // 2
    inv_freq = ROPE_THETA ** (-jnp.arange(half, dtype=F32) / half)
    ang = pos.astype(F32)[:, None] * inv_freq[None, :]
    cos = jnp.tile(jnp.cos(ang), (1, 4))
    sin = jnp.tile(jnp.sin(ang), (1, 4))
    sign = jnp.where(jnp.arange(LANES) < 64, -1.0, 1.0).astype(F32)
    return cos, sin * sign[None, :]


def _pick(n, prefs):
    for t in prefs:
        if n % t == 0:
            return t
    raise ValueError(f"unsupported size {n}")


def kernel(x, meta, emb_ln_g, emb_ln_b, w_in, conv_w, conv_b, gate_bias, lam_q1, lam_k1, lam_q2, lam_k2,
           att_norm_g, ml_norm_g, w_att_out, w_ml_out, w_o, ln1_g, ln1_b, w_router, b_router,
           w_gu, b_gu, w_down, b_down, ln2_g, ln2_b):
    B, S, D = x.shape
    assert D == D_MODEL and S % 512 == 0 and w_in.shape[0] == DEPTH
    N = B * S
    row2 = lambda a: a.reshape(1, -1)

    w = w_in[0]
    o_aq, o_ak, o_av, o_mqk, o_mv, o_mo, o_gt, o_ga, o_gm = 0, 512, 1024, 1536, 2560, 3072, 3584, 3592, 4616
    perm = _rotary_column_order()
    gpad = jnp.zeros((D, LANES - ML_HEADS), F32)
    w_all = jnp.concatenate([
        w[:, o_ga:o_ga + D], w[:, o_gm:o_gm + D], w[:, o_mqk:o_mqk + 2 * ML_WIDTH],
        w[:, o_aq:o_aq + 512][:, perm], w[:, o_ak:o_ak + 512][:, perm], w[:, o_av:o_av + 512],
        w[:, o_mv:o_mv + ML_WIDTH], w[:, o_mo:o_mo + ML_WIDTH],
        w[:, o_gt:o_gt + ML_HEADS], gpad, w[:, o_gt + ML_HEADS:o_gt + 2 * ML_HEADS], gpad,
    ], axis=1).astype(BF16)
    gb = gate_bias[0]
    zpad = jnp.zeros((LANES - ML_HEADS,), F32)
    gate_b = jnp.concatenate([gb[:ML_HEADS], zpad, gb[ML_HEADS:], zpad]).reshape(1, GATE_W)

    cos_x, sin_x = _rotary_tables(N_META + jnp.arange(S))
    cos_m, sin_m = _rotary_tables(jnp.maximum(jnp.arange(BLOCK) - PAD, 0))

    eg, eb = row2(emb_ln_g), row2(emb_ln_b)

    tm_in = 512
    zb, gates = _inproj(x, eg, eb, w_all, cos_x, sin_x, tm=tm_in)
    xm = jnp.concatenate([jnp.zeros((PAD, D), x.dtype), meta.astype(x.dtype)], axis=0)[None]
    zbm, gatesm = _inproj(xm, eg, eb, w_all, cos_m, sin_m, tm=BLOCK, first_valid_row=PAD)

    tq = 512
    yatt = _attention(zb, zbm, row2(lam_q1[0]), row2(lam_k1[0]), row2(lam_q2[0]), row2(lam_k2[0]),
                      row2(att_norm_g[0]), tq=tq, tk=tq)
    yml = _mlstm(zb, gates, zbm, gatesm, conv_w[0], row2(conv_b[0]), gate_b, row2(ml_norm_g[0]))

    wr = jnp.concatenate([w_router[0], jnp.zeros((D, LANES - N_EXPERTS), F32)], axis=1)
    br = jnp.concatenate([b_router[0], jnp.full((LANES - N_EXPERTS,), NEG, F32)]).reshape(1, LANES)
    tm_out = 512
    h1, eid, gate, rank, cnt = _outproj(
        x.reshape(N, D), yatt.reshape(N, -1), yml.reshape(N, -1), zb.reshape(N, ZB_W), eg, eb,
        w_att_out[0].astype(BF16), w_ml_out[0].astype(BF16), w_o[0].astype(BF16),
        row2(ln1_g[0]), row2(ln1_b[0]), wr, br, tm=tm_out)

    tm_e = 512
    M = N * TOP_K
    nb = (M + N_EXPERTS * (tm_e - 1) + tm_e - 1) // tm_e
    counts = cnt[0, :N_EXPERTS].astype(jnp.int32)
    nblk = (counts + tm_e - 1) // tm_e
    cum = jnp.cumsum(nblk)
    pstart = (cum - nblk) * tm_e
    dest = pstart[eid[:, :TOP_K]] + rank[:, :TOP_K]
    n_used = cum[-1:].astype(jnp.int32)
    blk = jnp.minimum(jnp.arange(nb, dtype=jnp.int32), n_used[0] - 1)
    blk_exp = jnp.searchsorted(cum, blk, side="right").astype(jnp.int32)

    tm_d = 256
    dest3 = dest.reshape(N // tm_d, 1, tm_d * TOP_K)
    xs = _dispatch(dest3, h1, jnp.zeros((nb * tm_e, D), F32), tm=tm_d)

    wgu = w_gu[0]
    bgu = b_gu[0]
    ys = _experts(blk_exp, n_used, xs,
                  wgu[:, :, 0::2].astype(BF16), wgu[:, :, 1::2].astype(BF16),
                  bgu[:, None, 0::2], bgu[:, None, 1::2],
                  w_down[0].astype(BF16), b_down[0][:, None, :], tm=tm_e)

    out = _combine(dest3, h1, gate, row2(ln2_g[0]), row2(ln2_b[0]), ys, tm=tm_d)
    return out.reshape(B, S, D)
```

```python
import functools
import math

import jax
import jax.numpy as jnp
import numpy as np
from jax import lax
from jax.experimental import pallas as pl
from jax.experimental.pallas import tpu as pltpu

D_MODEL = 1024
N_META = 16
BLOCK = 128
PAD = BLOCK - N_META
NEG = -1e30
LN_EPS = 1e-5
ATT_HEADS = 4
ATT_QK_DIM = 64
ATT_V_DIM = 128
ROPE_THETA = 10000.0
ML_HEADS = 4
ML_DH = 128
ML_WIDTH = ML_HEADS * ML_DH
CONV_K = 4
N_EXPERTS = 32
TOP_K = 4
SWIGLU_LIMIT = 7.0
SWIGLU_ALPHA = 1.702
DEPTH = 1
DN_ALPHA = (2 * DEPTH) ** 0.25
LAMBDA_INIT = 0.8 - 0.6 * math.exp(-0.3 * 0)

LANES = 128
C_GA, C_GM, C_MQK, C_AQ, C_AK, C_AV, C_MV, C_MO = 0, 1024, 2048, 3072, 3584, 4096, 4608, 5120
ZB_W = 5632
GATE_W = 2 * LANES
W_ALL = ZB_W + GATE_W
CHUNK = 512

VMEM_LIMIT = 56 * 1024 * 1024

BF16 = jnp.bfloat16
F32 = jnp.float32


def _cparams(n_axes):
    return pltpu.CompilerParams(dimension_semantics=("arbitrary",) * n_axes, vmem_limit_bytes=VMEM_LIMIT)


def _ln(x, g, b):
    mu = jnp.mean(x, axis=-1, keepdims=True)
    xc = x - mu
    var = jnp.mean(xc * xc, axis=-1, keepdims=True)
    return xc * lax.rsqrt(var + LN_EPS) * g + b


def _sigmoid(x):
    return 1.0 / (1.0 + jnp.exp(-x))


def _inproj_kernel(x_ref, g_ref, b_ref, w_ref, cos_ref, sin_ref, zb_ref, gt_ref, *, first_valid_row):
    x = x_ref[0]
    tm = x.shape[0]
    hb = _ln(x, g_ref[...], b_ref[...]).astype(BF16)
    cos = cos_ref[...]
    sin = sin_ref[...]
    if first_valid_row:
        rowmask = lax.broadcasted_iota(jnp.int32, (tm, 1), 0) >= first_valid_row
    for c in range(ZB_W // CHUNK):
        z = jnp.dot(hb, w_ref[:, c * CHUNK:(c + 1) * CHUNK], preferred_element_type=F32)
        if c * CHUNK in (C_AQ, C_AK):
            parts = []
            for h in range(ATT_HEADS):
                zh = z[:, h * LANES:(h + 1) * LANES]
                parts.append(zh * cos + pltpu.roll(zh, 64, 1) * sin)
            z = jnp.concatenate(parts, axis=1)
            if c * CHUNK == C_AQ:
                z = z * (ATT_QK_DIM ** -0.5)
        if first_valid_row:
            z = jnp.where(rowmask, z, 0.0)
        zb_ref[0, :, c * CHUNK:(c + 1) * CHUNK] = z.astype(BF16)
    zg = jnp.dot(hb, w_ref[:, ZB_W:W_ALL], preferred_element_type=F32)
    if first_valid_row:
        zg = jnp.where(rowmask, zg, 0.0)
    gt_ref[0] = zg


def _inproj(x3, g, b, w, cos, sin, *, tm, first_valid_row=0):
    B, S, D = x3.shape
    nt = S // tm
    return pl.pallas_call(
        functools.partial(_inproj_kernel, first_valid_row=first_valid_row),
        grid=(B, nt),
        in_specs=[
            pl.BlockSpec((1, tm, D), lambda bi, i: (bi, i, 0)),
            pl.BlockSpec((1, D), lambda bi, i: (0, 0)),
            pl.BlockSpec((1, D), lambda bi, i: (0, 0)),
            pl.BlockSpec((D, W_ALL), lambda bi, i: (0, 0)),
            pl.BlockSpec((tm, LANES), lambda bi, i: (i, 0)),
            pl.BlockSpec((tm, LANES), lambda bi, i: (i, 0)),
        ],
        out_specs=[
            pl.BlockSpec((1, tm, ZB_W), lambda bi, i: (bi, i, 0)),
            pl.BlockSpec((1, tm, GATE_W), lambda bi, i: (bi, i, 0)),
        ],
        out_shape=[
            jax.ShapeDtypeStruct((B, S, ZB_W), BF16),
            jax.ShapeDtypeStruct((B, S, GATE_W), F32),
        ],
        compiler_params=_cparams(2),
        name="inproj",
    )(x3, g, b, w, cos, sin)


def _attn_kernel(q_ref, k_ref, v_ref, km_ref, vm_ref, lq1_ref, lk1_ref, lq2_ref, lk2_ref, ng_ref, o_ref,
                 sa_sc, sb_sc, pa_sc, pb_sc, aa_sc, ab_sc, m_sc, acc_sc, *, tq, tk, rc):
    assert tq == 2 * tk
    qi = pl.program_id(2)
    rows = 2 * tq
    q = q_ref[0]
    lane = lax.broadcasted_iota(jnp.int32, (tq, LANES), 1)
    is_map0 = (lane & 63) < 32
    zero = jnp.zeros_like(q)
    qq = jnp.concatenate([jnp.where(is_map0, q, zero), jnp.where(is_map0, zero, q)], axis=0)

    def qk_stage(kblk, s_ref):
        s_ref[:, 0:kblk.shape[0]] = lax.dot_general(qq, kblk, (((1,), (1,)), ((), ())),
                                                   preferred_element_type=F32)

    def pv_stage(p_ref, a_ref, vblk, first=False):
        width = vblk.shape[0]
        v1 = jnp.concatenate([vblk, jnp.ones_like(vblk)], axis=1)
        pv = jnp.dot(p_ref[:, 0:width], v1, preferred_element_type=F32)
        if first:
            acc_sc[...] = pv
        else:
            acc_sc[...] = pltpu.repeat(a_ref[...], 2, axis=1) * acc_sc[...] + pv

    def sm_stage(s_ref, p_ref, a_ref, width, vis, first=False):
        nrep = width // LANES
        for r0 in range(0, rows, rc):
            kind = vis(r0)
            if kind == "none":
                continue
            s = s_ref[r0:r0 + rc, 0:width]
            if kind != "all":
                s = kind(s)
            m_cur = jnp.max(s, axis=1, keepdims=True)
            if first:
                m_new = jnp.broadcast_to(m_cur, (rc, LANES))
            else:
                m_prev = m_sc[r0:r0 + rc, :]
                m_new = jnp.maximum(m_prev, m_cur)
                a_ref[r0:r0 + rc, :] = jnp.exp(m_prev - m_new)
            m_sc[r0:r0 + rc, :] = m_new
        for r0 in range(0, rows, rc):
            kind = vis(r0)
            if kind == "none":
                p_ref[r0:r0 + rc, 0:width] = jnp.zeros((rc, width), BF16)
                a_ref[r0:r0 + rc, :] = jnp.ones((rc, LANES), F32)
                continue
            s = s_ref[r0:r0 + rc, 0:width]
            if kind != "all":
                s = kind(s)
            m_new = m_sc[r0:r0 + rc, :]
            m_rep = m_new if nrep == 1 else pltpu.repeat(m_new, nrep, axis=1)
            p_ref[r0:r0 + rc, 0:width] = jnp.exp(s - m_rep).astype(BF16)

    def meta_vis(r0):
        def mask(s):
            col = lax.broadcasted_iota(jnp.int32, s.shape, 1)
            return jnp.where(col >= PAD, s, NEG)
        return mask

    def diag_vis(d):
        def vis(r0):
            q0 = r0 % tq
            if d * tk + tk - 1 <= q0:
                return "all"
            if d * tk > q0 + rc - 1:
                return "none"
            def mask(s):
                rowq = q0 + lax.broadcasted_iota(jnp.int32, s.shape, 0)
                col = d * tk + lax.broadcasted_iota(jnp.int32, s.shape, 1)
                return jnp.where(col <= rowq, s, NEG)
            return mask
        return vis

    all_vis = lambda r0: "all"
    kblk = lambda start: k_ref[0, pl.ds(start, tk), :]
    vblk = lambda start: v_ref[0, pl.ds(start, tk), :]

    qk_stage(km_ref[0], sa_sc)
    sm_stage(sa_sc, pa_sc, None, BLOCK, meta_vis, first=True)
    pv_stage(pa_sc, None, vm_ref[0], first=True)

    d0 = pl.multiple_of(qi * tq, tk)
    d1 = pl.multiple_of(qi * tq + tk, tk)
    qk_stage(kblk(d0), sa_sc)
    qk_stage(kblk(d1), sb_sc)
    sm_stage(sa_sc, pa_sc, aa_sc, tk, diag_vis(0))
    pv_stage(pa_sc, aa_sc, vblk(d0))
    sm_stage(sb_sc, pb_sc, ab_sc, tk, diag_vis(1))
    qk_stage(kblk(0), sa_sc)

    def pair(i, carry):
        a_start = pl.multiple_of(i * tq, tk)
        b_start = pl.multiple_of(i * tq + tk, tk)
        prev_b = pl.multiple_of(jnp.where(i == 0, d1, a_start - tk), tk)
        next_a = pl.multiple_of(jnp.minimum(a_start + tq, (qi - 1) * tq), tk)
        pv_stage(pb_sc, ab_sc, vblk(prev_b))
        sm_stage(sa_sc, pa_sc, aa_sc, tk, all_vis)
        qk_stage(kblk(b_start), sb_sc)
        pv_stage(pa_sc, aa_sc, vblk(a_start))
        sm_stage(sb_sc, pb_sc, ab_sc, tk, all_vis)
        qk_stage(kblk(next_a), sa_sc)
        return carry

    lax.fori_loop(0, qi, pair, 0)
    last_b = pl.multiple_of(jnp.where(qi == 0, d1, qi * tq - tk), tk)
    pv_stage(pb_sc, ab_sc, vblk(last_b))

    o = acc_sc[:, 0:ATT_V_DIM] / acc_sc[:, ATT_V_DIM:]
    s1 = jnp.sum(lq1_ref[...] * lk1_ref[...], axis=1, keepdims=True)
    s2 = jnp.sum(lq2_ref[...] * lk2_ref[...], axis=1, keepdims=True)
    lam = jnp.exp(s1) - jnp.exp(s2) + LAMBDA_INIT
    a = o[:tq] - lam * o[tq:]
    y = a * lax.rsqrt(jnp.mean(a * a, axis=1, keepdims=True) + LN_EPS) * ng_ref[...]
    o_ref[0] = (y * (1.0 - LAMBDA_INIT)).astype(BF16)


def _attention(zb, zbm, lq1, lk1, lq2, lk2, ng, *, tq):
    B, S, _ = zb.shape
    nq = S // tq
    tk = tq // 2
    qb, kb, vb = C_AQ // LANES, C_AK // LANES, C_AV // LANES
    lam_spec = pl.BlockSpec((1, ATT_QK_DIM), lambda b, h, i: (0, 0))
    return pl.pallas_call(
        functools.partial(_attn_kernel, tq=tq, tk=tk, rc=32),
        grid=(B, ATT_HEADS, nq),
        in_specs=[
            pl.BlockSpec((1, tq, LANES), lambda b, h, i: (b, i, qb + h)),
            pl.BlockSpec((1, S, LANES), lambda b, h, i: (b, 0, kb + h)),
            pl.BlockSpec((1, S, LANES), lambda b, h, i: (b, 0, vb + h)),
            pl.BlockSpec((1, BLOCK, LANES), lambda b, h, i: (0, 0, kb + h)),
            pl.BlockSpec((1, BLOCK, LANES), lambda b, h, i: (0, 0, vb + h)),
            lam_spec, lam_spec, lam_spec, lam_spec,
            pl.BlockSpec((1, ATT_V_DIM), lambda b, h, i: (0, 0)),
        ],
        out_specs=pl.BlockSpec((1, tq, LANES), lambda b, h, i: (b, i, h)),
        out_shape=jax.ShapeDtypeStruct((B, S, ATT_HEADS * ATT_V_DIM), BF16),
        scratch_shapes=[
            pltpu.VMEM((2 * tq, tk), F32), pltpu.VMEM((2 * tq, tk), F32),
            pltpu.VMEM((2 * tq, tk), BF16), pltpu.VMEM((2 * tq, tk), BF16),
            pltpu.VMEM((2 * tq, LANES), F32), pltpu.VMEM((2 * tq, LANES), F32),
            pltpu.VMEM((2 * tq, LANES), F32),
            pltpu.VMEM((2 * tq, 2 * ATT_V_DIM), F32),
        ],
        compiler_params=_cparams(3),
        name="diff_attention",
    )(zb, zb, zb, zbm, zbm, lq1, lk1, lq2, lk2, ng)


def _mlstm_kernel(xqk_ref, xv_ref, xo_ref, xg_ref, zm_ref, gm_ref, cw_ref, cb_ref, gb_ref, ng_ref, o_ref,
                  ct_sc, n_sc, m_sc, tail_sc, cbuf_sc):
    c = pl.program_id(1)
    L = BLOCK

    @pl.when(c == 0)
    def _():
        ct_sc[...] = jnp.zeros_like(ct_sc)
        n_sc[...] = jnp.zeros_like(n_sc)
        m_sc[...] = jnp.zeros_like(m_sc)
        tail_sc[...] = jnp.zeros_like(tail_sc)

    is_meta = c == 0
    row = lax.broadcasted_iota(jnp.int32, (L, 1), 0)
    valid = jnp.logical_or(c > 0, row >= PAD)

    qk_pre = jnp.where(is_meta, zm_ref[0, :, C_MQK:C_MQK + 2 * ML_WIDTH], xqk_ref[0]).astype(F32)
    v = jnp.where(is_meta, zm_ref[0, :, C_MV:C_MV + ML_WIDTH], xv_ref[0])
    gates = jnp.where(is_meta, gm_ref[0], xg_ref[0])

    cbuf_sc[0:8, :] = tail_sc[...]
    cbuf_sc[8:8 + L, :] = qk_pre
    tail_sc[...] = qk_pre[L - 8:L, :]
    acc = jnp.broadcast_to(cb_ref[...], (L, 2 * ML_WIDTH))
    for j in range(CONV_K):
        off = 8 - (CONV_K - 1) + j
        acc = acc + cw_ref[j:j + 1, :] * cbuf_sc[off:off + L, :]
    qk = acc * _sigmoid(acc)
    qk = jnp.where(valid, qk, 0.0)
    q = qk[:, :ML_WIDTH]
    k = qk[:, ML_WIDTH:] * (ML_DH ** -0.5)

    ig = gates[:, :LANES] + gb_ref[:, :LANES]
    fg = gates[:, LANES:] + gb_ref[:, LANES:]
    lf = jnp.minimum(fg, 0.0) - jnp.log1p(jnp.exp(-jnp.abs(fg)))
    ig = jnp.where(valid, ig, NEG)
    lf = jnp.where(valid, lf, 0.0)

    sidx = lax.broadcasted_iota(jnp.int32, (L, L), 0)
    ridx = lax.broadcasted_iota(jnp.int32, (L, L), 1)
    causal = ridx <= sidx
    tri = causal.astype(BF16)
    lf_hi = lf.astype(BF16)
    lf_lo = (lf - lf_hi.astype(F32)).astype(BF16)
    bcs = (jnp.dot(tri, lf_hi, preferred_element_type=F32)
           + jnp.dot(tri, lf_lo, preferred_element_type=F32))
    b_t = bcs.T
    ig_t = ig.T

    outs = []
    for h in range(ML_HEADS):
        sl = slice(h * ML_DH, (h + 1) * ML_DH)
        bc = bcs[:, h:h + 1]
        br = b_t[h:h + 1, :]
        igr = ig_t[h:h + 1, :]
        igc = ig[:, h:h + 1]
        m_h = m_sc[h:h + 1, 0:1]
        dlog = jnp.where(causal, (bc - br) + igr, NEG)
        inter = bc + m_h
        m_s = jnp.maximum(inter, jnp.max(dlog, axis=1, keepdims=True))
        w_intra = jnp.exp(dlog - m_s)
        w_inter = jnp.exp(inter - m_s)
        qh = q[:, sl]
        kh = k[:, sl]
        vh = v[:, sl]
        qb = qh.astype(BF16)
        kb = kh.astype(BF16)
        s = lax.dot_general(qb, kb, (((1,), (1,)), ((), ())), preferred_element_type=F32) * w_intra
        ct = ct_sc[h]
        nrow = n_sc[h:h + 1, :]
        num = (w_inter * jnp.dot(qb, ct.astype(BF16), preferred_element_type=F32)
               + jnp.dot(s.astype(BF16), vh, preferred_element_type=F32))
        den = w_inter * jnp.sum(qh * nrow, axis=1, keepdims=True) + jnp.sum(s, axis=1, keepdims=True)
        hh = num / jnp.maximum(jnp.abs(den), jnp.exp(-m_s))
        b_last = bc[L - 1:L, :]
        upd = (b_last - bc) + igc
        m_new = jnp.maximum(b_last + m_h, jnp.max(upd, axis=0, keepdims=True))
        w_old = jnp.exp(b_last + m_h - m_new)
        w_r = jnp.exp(upd - m_new)
        vw = (vh.astype(F32) * w_r).astype(BF16)
        ct_sc[h] = w_old * ct + lax.dot_general(kb, vw, (((0,), (0,)), ((), ())), preferred_element_type=F32)
        n_sc[h:h + 1, :] = w_old * nrow + jnp.sum(kh * w_r, axis=0, keepdims=True)
        m_sc[h:h + 1, :] = jnp.broadcast_to(m_new, (1, LANES))
        hn = hh * lax.rsqrt(jnp.mean(hh * hh, axis=1, keepdims=True) + LN_EPS) * ng_ref[:, sl]
        outs.append(_sigmoid(xo_ref[0, :, sl].astype(F32)) * hn)

    @pl.when(c > 0)
    def _():
        o_ref[0] = jnp.concatenate(outs, axis=1).astype(BF16)


def _mlstm(zb, gates, zbm, gatesm, conv_w, conv_b, gate_b, ng):
    B, S, _ = zb.shape
    nc = S // BLOCK + 1

    def xc(c):
        return jnp.maximum(c - 1, 0)

    return pl.pallas_call(
        _mlstm_kernel,
        grid=(B, nc),
        in_specs=[
            pl.BlockSpec((1, BLOCK, 2 * ML_WIDTH), lambda b, c: (b, xc(c), C_MQK // (2 * ML_WIDTH))),
            pl.BlockSpec((1, BLOCK, ML_WIDTH), lambda b, c: (b, xc(c), C_MV // ML_WIDTH)),
            pl.BlockSpec((1, BLOCK, ML_WIDTH), lambda b, c: (b, xc(c), C_MO // ML_WIDTH)),
            pl.BlockSpec((1, BLOCK, GATE_W), lambda b, c: (b, xc(c), 0)),
            pl.BlockSpec((1, BLOCK, ZB_W), lambda b, c: (0, 0, 0)),
            pl.BlockSpec((1, BLOCK, GATE_W), lambda b, c: (0, 0, 0)),
            pl.BlockSpec((CONV_K, 2 * ML_WIDTH), lambda b, c: (0, 0)),
            pl.BlockSpec((1, 2 * ML_WIDTH), lambda b, c: (0, 0)),
            pl.BlockSpec((1, GATE_W), lambda b, c: (0, 0)),
            pl.BlockSpec((1, ML_WIDTH), lambda b, c: (0, 0)),
        ],
        out_specs=pl.BlockSpec((1, BLOCK, ML_WIDTH), lambda b, c: (b, xc(c), 0)),
        out_shape=jax.ShapeDtypeStruct((B, S, ML_WIDTH), BF16),
        scratch_shapes=[
            pltpu.VMEM((ML_HEADS, ML_DH, ML_DH), F32),
            pltpu.VMEM((8, ML_DH), F32),
            pltpu.VMEM((8, LANES), F32),
            pltpu.VMEM((8, 2 * ML_WIDTH), F32),
            pltpu.VMEM((8 + BLOCK, 2 * ML_WIDTH), F32),
        ],
        compiler_params=_cparams(2),
        name="mlstm",
    )(zb, zb, zb, gates, zbm, gatesm, conv_w, conv_b, gate_b, ng)


def _outproj_kernel(x_ref, ya_ref, ym_ref, ga_ref, gmm_ref, eg_ref, eb_ref, wa_ref, wm_ref, wo_ref,
                    g1_ref, b1_ref, wr_ref, br_ref,
                    h1_ref, eid_ref, gate_ref, rank_ref, cnt_ref, run_sc, *, tm):
    i = pl.program_id(0)

    @pl.when(i == 0)
    def _():
        run_sc[...] = jnp.zeros_like(run_sc)

    h0 = _ln(x_ref[...], eg_ref[...], eb_ref[...])
    pa = jnp.dot(ya_ref[...], wa_ref[...], preferred_element_type=F32)
    pm = jnp.dot(ym_ref[...], wm_ref[...], preferred_element_type=F32)
    merged = _sigmoid(ga_ref[...].astype(F32)) * pa + _sigmoid(gmm_ref[...].astype(F32)) * pm
    mix = jnp.dot(merged.astype(BF16), wo_ref[...], preferred_element_type=F32)
    h1 = _ln(DN_ALPHA * h0 + mix, g1_ref[...], b1_ref[...])
    h1_ref[...] = h1

    logits = jnp.dot(h1, wr_ref[...], preferred_element_type=F32, precision=lax.Precision.HIGHEST) + br_ref[...]
    lane = lax.broadcasted_iota(jnp.int32, (tm, LANES), 1)
    work = logits
    sel_e, sel_v = [], []
    for _ in range(TOP_K):
        mv = jnp.max(work, axis=1, keepdims=True)
        e = jnp.min(jnp.where(work == mv, lane, LANES), axis=1, keepdims=True)
        sel_e.append(e)
        sel_v.append(mv)
        work = jnp.where(lane == e, -jnp.inf, work)
    ex = [jnp.exp(v - sel_v[0]) for v in sel_v]
    den = ex[0] + ex[1] + ex[2] + ex[3]

    onehots = [lane == e for e in sel_e]
    oh = jnp.zeros((tm, LANES), F32)
    for o in onehots:
        oh = oh + o.astype(F32)
    r_i = lax.broadcasted_iota(jnp.int32, (tm, tm), 0)
    c_i = lax.broadcasted_iota(jnp.int32, (tm, tm), 1)
    strict = (c_i < r_i).astype(BF16)
    before = jnp.dot(strict, oh.astype(BF16), preferred_element_type=F32) + run_sc[...]
    eid = jnp.zeros((tm, LANES), jnp.int32)
    gate = jnp.zeros((tm, LANES), F32)
    rank = jnp.zeros((tm, LANES), F32)
    for kk in range(TOP_K):
        here = lane == kk
        eid = jnp.where(here, sel_e[kk], eid)
        gate = jnp.where(here, ex[kk] / den, gate)
        rk = jnp.sum(jnp.where(onehots[kk], before, 0.0), axis=1, keepdims=True)
        rank = jnp.where(here, rk, rank)
    eid_ref[...] = eid
    gate_ref[...] = gate
    rank_ref[...] = rank.astype(jnp.int32)
    run_sc[...] = run_sc[...] + jnp.sum(oh, axis=0, keepdims=True)
    cnt_ref[...] = run_sc[...]


def _outproj(x2, yatt, yml, zb2, eg, eb, wa, wm, wo, g1, b1, wr, br, *, tm):
    N, D = x2.shape
    vec = lambda w: pl.BlockSpec((1, w), lambda i: (0, 0))
    full = lambda a, b: pl.BlockSpec((a, b), lambda i: (0, 0))
    tile = lambda w: pl.BlockSpec((tm, w), lambda i: (i, 0))
    return pl.pallas_call(
        functools.partial(_outproj_kernel, tm=tm),
        grid=(N // tm,),
        in_specs=[
            tile(D), tile(ATT_HEADS * ATT_V_DIM), tile(ML_WIDTH),
            pl.BlockSpec((tm, D), lambda i: (i, C_GA // D)),
            pl.BlockSpec((tm, D), lambda i: (i, C_GM // D)),
            vec(D), vec(D),
            full(ATT_HEADS * ATT_V_DIM, D), full(ML_WIDTH, D), full(D, D),
            vec(D), vec(D), full(D, LANES), vec(LANES),
        ],
        out_specs=[tile(D), tile(LANES), tile(LANES), tile(LANES), vec(LANES)],
        out_shape=[
            jax.ShapeDtypeStruct((N, D), F32),
            jax.ShapeDtypeStruct((N, LANES), jnp.int32),
            jax.ShapeDtypeStruct((N, LANES), F32),
            jax.ShapeDtypeStruct((N, LANES), jnp.int32),
            jax.ShapeDtypeStruct((1, LANES), F32),
        ],
        scratch_shapes=[pltpu.VMEM((1, LANES), F32)],
        compiler_params=_cparams(1),
        name="outproj_router",
    )(x2, yatt, yml, zb2, zb2, eg, eb, wa, wm, wo, g1, b1, wr, br)


def _row_copies(src_row, dst_row, sem, n, issue):
    def body(r, carry):
        for kk in range(TOP_K):
            cp = pltpu.make_async_copy(src_row(r, kk), dst_row(r, kk), sem)
            if issue:
                cp.start()
            else:
                cp.wait()
        return carry
    lax.fori_loop(0, n, body, 0)


def _dispatch_kernel(d_ref, h_ref, xs_in_ref, xs_ref, sem, *, tm):
    del xs_in_ref
    src = lambda r, kk: h_ref.at[pl.ds(r, 1)]
    dst = lambda r, kk: xs_ref.at[pl.ds(d_ref[0, 0, r * TOP_K + kk], 1)]
    _row_copies(src, dst, sem.at[0], tm, True)
    _row_copies(src, dst, sem.at[0], tm, False)


def _dispatch(dest3, h1, xs0, *, tm):
    N, D = h1.shape
    return pl.pallas_call(
        functools.partial(_dispatch_kernel, tm=tm),
        grid=(N // tm,),
        in_specs=[
            pl.BlockSpec((1, 1, tm * TOP_K), lambda i: (i, 0, 0), memory_space=pltpu.SMEM),
            pl.BlockSpec((tm, D), lambda i: (i, 0)),
            pl.BlockSpec(memory_space=pl.ANY),
        ],
        out_specs=pl.BlockSpec(memory_space=pl.ANY),
        out_shape=jax.ShapeDtypeStruct(xs0.shape, F32),
        scratch_shapes=[pltpu.SemaphoreType.DMA((1,))],
        input_output_aliases={2: 0},
        compiler_params=_cparams(1),
        name="moe_dispatch",
    )(dest3, h1, xs0)


def _deinterleave_kernel(w_ref, p_ref, wg_ref, wl_ref):
    for c in range(w_ref.shape[2] // (2 * LANES)):
        blk = w_ref[0, :, c * 2 * LANES:(c + 1) * 2 * LANES].astype(BF16)
        r = jnp.dot(blk, p_ref[...], preferred_element_type=F32).astype(BF16)
        wg_ref[0, :, c * LANES:(c + 1) * LANES] = r[:, :LANES]
        wl_ref[0, :, c * LANES:(c + 1) * LANES] = r[:, LANES:]


def _deinterleave(w_gu):
    E, D, F2 = w_gu.shape
    F = F2 // 2
    j = np.arange(2 * LANES)
    sel = np.zeros((2 * LANES, 2 * LANES), np.float32)
    sel[j, (j % 2) * LANES + j // 2] = 1.0
    out = jax.ShapeDtypeStruct((E, D, F), BF16)
    return pl.pallas_call(
        _deinterleave_kernel,
        grid=(E,),
        in_specs=[
            pl.BlockSpec((1, D, F2), lambda e: (e, 0, 0)),
            pl.BlockSpec((2 * LANES, 2 * LANES), lambda e: (0, 0)),
        ],
        out_specs=[pl.BlockSpec((1, D, F), lambda e: (e, 0, 0))] * 2,
        out_shape=[out, out],
        compiler_params=_cparams(1),
        name="deinterleave_wgu",
    )(w_gu, jnp.asarray(sel, BF16))


def _expert_kernel(be_ref, nu_ref, x_ref, wg_ref, wl_ref, bg_ref, bl_ref, wd_ref, bd_ref, y_ref):
    del be_ref

    @pl.when(pl.program_id(0) < nu_ref[0])
    def _():
        xb = x_ref[...].astype(BF16)
        hg = jnp.dot(xb, wg_ref[0], preferred_element_type=F32) + bg_ref[0]
        hl = jnp.dot(xb, wl_ref[0], preferred_element_type=F32) + bl_ref[0]
        glu = jnp.minimum(hg, SWIGLU_LIMIT)
        lin = jnp.clip(hl, -SWIGLU_LIMIT, SWIGLU_LIMIT)
        act = glu * _sigmoid(SWIGLU_ALPHA * glu) * (lin + 1.0)
        y_ref[...] = jnp.dot(act.astype(BF16), wd_ref[0], preferred_element_type=F32) + bd_ref[0]


def _experts(blk_exp, n_used, xs, wg, wl, bg, bl, wd, bd, *, tm):
    MP, D = xs.shape
    F = wg.shape[2]
    row = lambda i, be, nu: (jnp.minimum(i, nu[0] - 1), 0)
    wsp = lambda a, b: pl.BlockSpec((1, a, b), lambda i, be, nu: (be[i], 0, 0))
    return pl.pallas_call(
        _expert_kernel,
        grid_spec=pltpu.PrefetchScalarGridSpec(
            num_scalar_prefetch=2,
            grid=(MP // tm,),
            in_specs=[pl.BlockSpec((tm, D), row), wsp(D, F), wsp(D, F), wsp(1, F), wsp(1, F), wsp(F, D), wsp(1, D)],
            out_specs=pl.BlockSpec((tm, D), row),
        ),
        out_shape=jax.ShapeDtypeStruct((MP, D), F32),
        compiler_params=_cparams(1),
        name="moe_experts",
    )(blk_exp, n_used, xs, wg, wl, bg, bl, wd, bd)


def _combine_kernel(dcur_ref, dnxt_ref, h1_ref, gate_ref, g2_ref, b2_ref, ys_ref, o_ref, ybuf, sem, *, tm, nsteps):
    i = pl.program_id(0)
    slot = i % 2

    def copies(d_ref, s, issue):
        src = lambda r, kk: ys_ref.at[pl.ds(d_ref[0, 0, r * TOP_K + kk], 1)]
        dst = lambda r, kk: ybuf.at[s, kk, pl.ds(r, 1)]
        _row_copies(src, dst, sem.at[s], tm, issue)

    @pl.when(i == 0)
    def _():
        copies(dcur_ref, 0, True)

    @pl.when(i + 1 < nsteps)
    def _():
        copies(dnxt_ref, 1 - slot, True)

    copies(dcur_ref, slot, False)
    acc = DN_ALPHA * h1_ref[...]
    for kk in range(TOP_K):
        acc = acc + gate_ref[:, kk:kk + 1] * ybuf[slot, kk]
    o_ref[...] = _ln(acc, g2_ref[...], b2_ref[...])


def _combine(dest3, h1, gate, g2, b2, ys, *, tm):
    N, D = h1.shape
    nsteps = N // tm
    dspec = lambda f: pl.BlockSpec((1, 1, tm * TOP_K), f, memory_space=pltpu.SMEM)
    return pl.pallas_call(
        functools.partial(_combine_kernel, tm=tm, nsteps=nsteps),
        grid=(nsteps,),
        in_specs=[
            dspec(lambda i: (i, 0, 0)),
            dspec(lambda i: (jnp.minimum(i + 1, nsteps - 1), 0, 0)),
            pl.BlockSpec((tm, D), lambda i: (i, 0)),
            pl.BlockSpec((tm, LANES), lambda i: (i, 0)),
            pl.BlockSpec((1, D), lambda i: (0, 0)),
            pl.BlockSpec((1, D), lambda i: (0, 0)),
            pl.BlockSpec(memory_space=pl.ANY),
        ],
        out_specs=pl.BlockSpec((tm, D), lambda i: (i, 0)),
        out_shape=jax.ShapeDtypeStruct((N, D), F32),
        scratch_shapes=[pltpu.VMEM((2, TOP_K, tm, D), F32), pltpu.SemaphoreType.DMA((2,))],
        compiler_params=_cparams(1),
        name="moe_combine",
    )(dest3, dest3, h1, gate, g2, b2, ys)


def _rotary_column_order():
    idx = np.empty((ATT_HEADS * LANES,), np.int32)
    for h in range(ATT_HEADS):
        for half in range(2):
            for sub in range(2):
                for dd in range(32):
                    idx[h * LANES + half * 64 + sub * 32 + dd] = (2 * h + sub) * ATT_QK_DIM + half * 32 + dd
    return idx


def _rotary_tables(pos):
    half = ATT_QK_DIM // 2
    inv_freq = ROPE_THETA ** (-jnp.arange(half, dtype=F32) / half)
    ang = pos.astype(F32)[:, None] * inv_freq[None, :]
    cos = jnp.tile(jnp.cos(ang), (1, 4))
    sin = jnp.tile(jnp.sin(ang), (1, 4))
    sign = jnp.where(jnp.arange(LANES) < 64, -1.0, 1.0).astype(F32)
    return cos, sin * sign[None, :]


def _pick(n, prefs):
    for t in prefs:
        if n % t == 0:
            return t
    raise ValueError(f"unsupported size {n}")


def kernel(x, meta, emb_ln_g, emb_ln_b, w_in, conv_w, conv_b, gate_bias, lam_q1, lam_k1, lam_q2, lam_k2,
           att_norm_g, ml_norm_g, w_att_out, w_ml_out, w_o, ln1_g, ln1_b, w_router, b_router,
           w_gu, b_gu, w_down, b_down, ln2_g, ln2_b):
    B, S, D = x.shape
    assert D == D_MODEL and S % 512 == 0 and w_in.shape[0] == DEPTH
    N = B * S
    row2 = lambda a: a.reshape(1, -1)

    w = w_in[0]
    o_aq, o_ak, o_av, o_mqk, o_mv, o_mo, o_gt, o_ga, o_gm = 0, 512, 1024, 1536, 2560, 3072, 3584, 3592, 4616
    perm = _rotary_column_order()
    gpad = jnp.zeros((D, LANES - ML_HEADS), F32)
    w_all = jnp.concatenate([
        w[:, o_ga:o_ga + D], w[:, o_gm:o_gm + D], w[:, o_mqk:o_mqk + 2 * ML_WIDTH],
        w[:, o_aq:o_aq + 512][:, perm], w[:, o_ak:o_ak + 512][:, perm], w[:, o_av:o_av + 512],
        w[:, o_mv:o_mv + ML_WIDTH], w[:, o_mo:o_mo + ML_WIDTH],
        w[:, o_gt:o_gt + ML_HEADS], gpad, w[:, o_gt + ML_HEADS:o_gt + 2 * ML_HEADS], gpad,
    ], axis=1).astype(BF16)
    gb = gate_bias[0]
    zpad = jnp.zeros((LANES - ML_HEADS,), F32)
    gate_b = jnp.concatenate([gb[:ML_HEADS], zpad, gb[ML_HEADS:], zpad]).reshape(1, GATE_W)

    cos_x, sin_x = _rotary_tables(N_META + jnp.arange(S))
    cos_m, sin_m = _rotary_tables(jnp.maximum(jnp.arange(BLOCK) - PAD, 0))

    eg, eb = row2(emb_ln_g), row2(emb_ln_b)

    tm_in = 512
    zb, gates = _inproj(x, eg, eb, w_all, cos_x, sin_x, tm=tm_in)
    xm = jnp.concatenate([jnp.zeros((PAD, D), x.dtype), meta.astype(x.dtype)], axis=0)[None]
    zbm, gatesm = _inproj(xm, eg, eb, w_all, cos_m, sin_m, tm=BLOCK, first_valid_row=PAD)

    tq = 512
    yatt = _attention(zb, zbm, row2(lam_q1[0]), row2(lam_k1[0]), row2(lam_q2[0]), row2(lam_k2[0]),
                      row2(att_norm_g[0]), tq=tq)
    yml = _mlstm(zb, gates, zbm, gatesm, conv_w[0], row2(conv_b[0]), gate_b, row2(ml_norm_g[0]))

    wr = jnp.concatenate([w_router[0], jnp.zeros((D, LANES - N_EXPERTS), F32)], axis=1)
    br = jnp.concatenate([b_router[0], jnp.full((LANES - N_EXPERTS,), NEG, F32)]).reshape(1, LANES)
    tm_out = 512
    h1, eid, gate, rank, cnt = _outproj(
        x.reshape(N, D), yatt.reshape(N, -1), yml.reshape(N, -1), zb.reshape(N, ZB_W), eg, eb,
        w_att_out[0].astype(BF16), w_ml_out[0].astype(BF16), w_o[0].astype(BF16),
        row2(ln1_g[0]), row2(ln1_b[0]), wr, br, tm=tm_out)

    tm_e = 512
    M = N * TOP_K
    nb = (M + N_EXPERTS * (tm_e - 1) + tm_e - 1) // tm_e
    counts = cnt[0, :N_EXPERTS].astype(jnp.int32)
    nblk = (counts + tm_e - 1) // tm_e
    cum = jnp.cumsum(nblk)
    pstart = (cum - nblk) * tm_e
    dest = pstart[eid[:, :TOP_K]] + rank[:, :TOP_K]
    n_used = cum[-1:].astype(jnp.int32)
    blk = jnp.minimum(jnp.arange(nb, dtype=jnp.int32), n_used[0] - 1)
    blk_exp = jnp.sum(blk[:, None] >= cum[None, :], axis=1).astype(jnp.int32)

    tm_d = 256
    dest3 = dest.reshape(N // tm_d, 1, tm_d * TOP_K)
    xs = _dispatch(dest3, h1, jnp.zeros((nb * tm_e, D), F32), tm=tm_d)

    wg, wl = _deinterleave(w_gu[0])
    bgu = b_gu[0]
    ys = _experts(blk_exp, n_used, xs, wg, wl,
                  bgu[:, None, 0::2], bgu[:, None, 1::2],
                  w_down[0].astype(BF16), b_down[0][:, None, :], tm=tm_e)

    out = _combine(dest3, h1, gate, row2(ln2_g[0]), row2(ln2_b[0]), ys, tm=tm_d)
    return out.reshape(B, S, D)
```

```python
import functools
import math

import jax
import jax.numpy as jnp
import numpy as np
from jax import lax
from jax.experimental import pallas as pl
from jax.experimental.pallas import tpu as pltpu

D_MODEL = 1024
N_META = 16
BLOCK = 128
PAD = BLOCK - N_META
NEG = -1e30
LN_EPS = 1e-5
ATT_HEADS = 4
ATT_QK_DIM = 64
ATT_V_DIM = 128
ROPE_THETA = 10000.0
ML_HEADS = 4
ML_DH = 128
ML_WIDTH = ML_HEADS * ML_DH
CONV_K = 4
N_EXPERTS = 32
TOP_K = 4
SWIGLU_LIMIT = 7.0
SWIGLU_ALPHA = 1.702
DEPTH = 1
DN_ALPHA = (2 * DEPTH) ** 0.25
LAMBDA_INIT = 0.8 - 0.6 * math.exp(-0.3 * 0)

LANES = 128
C_GA, C_GM, C_MQK, C_AQ, C_AK, C_AV, C_MV, C_MO = 0, 1024, 2048, 3072, 3584, 4096, 4608, 5120
ZB_W = 5632
GATE_W = 2 * LANES
W_ALL = ZB_W + GATE_W
CHUNK = 512

VMEM_LIMIT = 56 * 1024 * 1024

BF16 = jnp.bfloat16
F32 = jnp.float32


def _cparams(n_axes):
    return pltpu.CompilerParams(dimension_semantics=("arbitrary",) * n_axes, vmem_limit_bytes=VMEM_LIMIT)


def _ln(x, g, b):
    mu = jnp.mean(x, axis=-1, keepdims=True)
    xc = x - mu
    var = jnp.mean(xc * xc, axis=-1, keepdims=True)
    return xc * lax.rsqrt(var + LN_EPS) * g + b


def _sigmoid(x):
    return 1.0 / (1.0 + jnp.exp(-x))


def _inproj_kernel(x_ref, g_ref, b_ref, w_ref, cos_ref, sin_ref, zb_ref, gt_ref, *, first_valid_row):
    x = x_ref[0]
    tm = x.shape[0]
    hb = _ln(x, g_ref[...], b_ref[...]).astype(BF16)
    cos = cos_ref[...]
    sin = sin_ref[...]
    if first_valid_row:
        rowmask = lax.broadcasted_iota(jnp.int32, (tm, 1), 0) >= first_valid_row
    for c in range(ZB_W // CHUNK):
        z = jnp.dot(hb, w_ref[:, c * CHUNK:(c + 1) * CHUNK], preferred_element_type=F32)
        if c * CHUNK in (C_AQ, C_AK):
            parts = []
            for h in range(ATT_HEADS):
                zh = z[:, h * LANES:(h + 1) * LANES]
                parts.append(zh * cos + pltpu.roll(zh, 64, 1) * sin)
            z = jnp.concatenate(parts, axis=1)
            if c * CHUNK == C_AQ:
                z = z * (ATT_QK_DIM ** -0.5)
        if first_valid_row:
            z = jnp.where(rowmask, z, 0.0)
        zb_ref[0, :, c * CHUNK:(c + 1) * CHUNK] = z.astype(BF16)
    zg = jnp.dot(hb, w_ref[:, ZB_W:W_ALL], preferred_element_type=F32)
    if first_valid_row:
        zg = jnp.where(rowmask, zg, 0.0)
    gt_ref[0] = zg


def _inproj(x3, g, b, w, cos, sin, *, tm, first_valid_row=0):
    B, S, D = x3.shape
    nt = S // tm
    return pl.pallas_call(
        functools.partial(_inproj_kernel, first_valid_row=first_valid_row),
        grid=(B, nt),
        in_specs=[
            pl.BlockSpec((1, tm, D), lambda bi, i: (bi, i, 0)),
            pl.BlockSpec((1, D), lambda bi, i: (0, 0)),
            pl.BlockSpec((1, D), lambda bi, i: (0, 0)),
            pl.BlockSpec((D, W_ALL), lambda bi, i: (0, 0)),
            pl.BlockSpec((tm, LANES), lambda bi, i: (i, 0)),
            pl.BlockSpec((tm, LANES), lambda bi, i: (i, 0)),
        ],
        out_specs=[
            pl.BlockSpec((1, tm, ZB_W), lambda bi, i: (bi, i, 0)),
            pl.BlockSpec((1, tm, GATE_W), lambda bi, i: (bi, i, 0)),
        ],
        out_shape=[
            jax.ShapeDtypeStruct((B, S, ZB_W), BF16),
            jax.ShapeDtypeStruct((B, S, GATE_W), F32),
        ],
        compiler_params=_cparams(2),
        name="inproj",
    )(x3, g, b, w, cos, sin)


def _attn_kernel(q_ref, k_ref, v_ref, km_ref, vm_ref, lq1_ref, lk1_ref, lq2_ref, lk2_ref, ng_ref, o_ref,
                 sa_sc, sb_sc, pa_sc, pb_sc, aa_sc, ab_sc, m_sc, acc_sc, *, tq, tk, rc):
    assert tq == 2 * tk
    qi = pl.program_id(2)
    rows = 2 * tq
    q = q_ref[0]
    lane = lax.broadcasted_iota(jnp.int32, (tq, LANES), 1)
    is_map0 = (lane & 63) < 32
    zero = jnp.zeros_like(q)
    qq = jnp.concatenate([jnp.where(is_map0, q, zero), jnp.where(is_map0, zero, q)], axis=0)

    def qk_stage(kblk, s_ref):
        s_ref[:, 0:kblk.shape[0]] = lax.dot_general(qq, kblk, (((1,), (1,)), ((), ())),
                                                   preferred_element_type=F32)

    def pv_stage(p_ref, a_ref, vblk, first=False):
        width = vblk.shape[0]
        v1 = jnp.concatenate([vblk, jnp.ones_like(vblk)], axis=1)
        pv = jnp.dot(p_ref[:, 0:width], v1, preferred_element_type=F32)
        if first:
            acc_sc[...] = pv
        else:
            a = a_ref[...]
            acc_sc[...] = jnp.concatenate([a, a], axis=1) * acc_sc[...] + pv

    def sm_stage(s_ref, p_ref, a_ref, width, vis, first=False):
        nrep = width // LANES
        for r0 in range(0, rows, rc):
            kind = vis(r0)
            if kind == "none":
                continue
            s = s_ref[r0:r0 + rc, 0:width]
            if kind != "all":
                s = kind(s)
            m_cur = jnp.max(s, axis=1, keepdims=True)
            if first:
                m_new = jnp.broadcast_to(m_cur, (rc, LANES))
            else:
                m_prev = m_sc[r0:r0 + rc, :]
                m_new = jnp.maximum(m_prev, m_cur)
                a_ref[r0:r0 + rc, :] = jnp.exp(m_prev - m_new)
            m_sc[r0:r0 + rc, :] = m_new
        for r0 in range(0, rows, rc):
            kind = vis(r0)
            if kind == "none":
                p_ref[r0:r0 + rc, 0:width] = jnp.zeros((rc, width), BF16)
                a_ref[r0:r0 + rc, :] = jnp.ones((rc, LANES), F32)
                continue
            s = s_ref[r0:r0 + rc, 0:width]
            if kind != "all":
                s = kind(s)
            m_new = m_sc[r0:r0 + rc, :]
            m_rep = m_new if nrep == 1 else jnp.concatenate([m_new] * nrep, axis=1)
            p_ref[r0:r0 + rc, 0:width] = jnp.exp(s - m_rep).astype(BF16)

    def meta_vis(r0):
        def mask(s):
            col = lax.broadcasted_iota(jnp.int32, s.shape, 1)
            return jnp.where(col >= PAD, s, NEG)
        return mask

    def diag_vis(d):
        def vis(r0):
            q0 = r0 % tq
            if d * tk + tk - 1 <= q0:
                return "all"
            if d * tk > q0 + rc - 1:
                return "none"
            def mask(s):
                rowq = q0 + lax.broadcasted_iota(jnp.int32, s.shape, 0)
                col = d * tk + lax.broadcasted_iota(jnp.int32, s.shape, 1)
                return jnp.where(col <= rowq, s, NEG)
            return mask
        return vis

    all_vis = lambda r0: "all"
    kblk = lambda start: k_ref[0, pl.ds(start, tk), :]
    vblk = lambda start: v_ref[0, pl.ds(start, tk), :]

    qk_stage(km_ref[0], sa_sc)
    sm_stage(sa_sc, pa_sc, None, BLOCK, meta_vis, first=True)
    pv_stage(pa_sc, None, vm_ref[0], first=True)

    d0 = pl.multiple_of(qi * tq, tk)
    d1 = pl.multiple_of(qi * tq + tk, tk)
    qk_stage(kblk(d0), sa_sc)
    qk_stage(kblk(d1), sb_sc)
    sm_stage(sa_sc, pa_sc, aa_sc, tk, diag_vis(0))
    pv_stage(pa_sc, aa_sc, vblk(d0))
    sm_stage(sb_sc, pb_sc, ab_sc, tk, diag_vis(1))
    qk_stage(kblk(0), sa_sc)

    def pair(i, carry):
        a_start = pl.multiple_of(i * tq, tk)
        b_start = pl.multiple_of(i * tq + tk, tk)
        prev_b = pl.multiple_of(jnp.where(i == 0, d1, a_start - tk), tk)
        next_a = pl.multiple_of(jnp.minimum(a_start + tq, (qi - 1) * tq), tk)
        pv_stage(pb_sc, ab_sc, vblk(prev_b))
        sm_stage(sa_sc, pa_sc, aa_sc, tk, all_vis)
        qk_stage(kblk(b_start), sb_sc)
        pv_stage(pa_sc, aa_sc, vblk(a_start))
        sm_stage(sb_sc, pb_sc, ab_sc, tk, all_vis)
        qk_stage(kblk(next_a), sa_sc)
        return carry

    lax.fori_loop(0, qi, pair, 0)
    last_b = pl.multiple_of(jnp.where(qi == 0, d1, qi * tq - tk), tk)
    pv_stage(pb_sc, ab_sc, vblk(last_b))

    o = acc_sc[:, 0:ATT_V_DIM] / acc_sc[:, ATT_V_DIM:]
    s1 = jnp.sum(lq1_ref[...] * lk1_ref[...], axis=1, keepdims=True)
    s2 = jnp.sum(lq2_ref[...] * lk2_ref[...], axis=1, keepdims=True)
    lam = jnp.exp(s1) - jnp.exp(s2) + LAMBDA_INIT
    a = o[:tq] - lam * o[tq:]
    y = a * lax.rsqrt(jnp.mean(a * a, axis=1, keepdims=True) + LN_EPS) * ng_ref[...]
    o_ref[0] = (y * (1.0 - LAMBDA_INIT)).astype(BF16)


def _attention(zb, zbm, lq1, lk1, lq2, lk2, ng, *, tq):
    B, S, _ = zb.shape
    nq = S // tq
    tk = tq // 2
    qb, kb, vb = C_AQ // LANES, C_AK // LANES, C_AV // LANES
    lam_spec = pl.BlockSpec((1, ATT_QK_DIM), lambda b, h, i: (0, 0))
    return pl.pallas_call(
        functools.partial(_attn_kernel, tq=tq, tk=tk, rc=32),
        grid=(B, ATT_HEADS, nq),
        in_specs=[
            pl.BlockSpec((1, tq, LANES), lambda b, h, i: (b, i, qb + h)),
            pl.BlockSpec((1, S, LANES), lambda b, h, i: (b, 0, kb + h)),
            pl.BlockSpec((1, S, LANES), lambda b, h, i: (b, 0, vb + h)),
            pl.BlockSpec((1, BLOCK, LANES), lambda b, h, i: (0, 0, kb + h)),
            pl.BlockSpec((1, BLOCK, LANES), lambda b, h, i: (0, 0, vb + h)),
            lam_spec, lam_spec, lam_spec, lam_spec,
            pl.BlockSpec((1, ATT_V_DIM), lambda b, h, i: (0, 0)),
        ],
        out_specs=pl.BlockSpec((1, tq, LANES), lambda b, h, i: (b, i, h)),
        out_shape=jax.ShapeDtypeStruct((B, S, ATT_HEADS * ATT_V_DIM), BF16),
        scratch_shapes=[
            pltpu.VMEM((2 * tq, tk), F32), pltpu.VMEM((2 * tq, tk), F32),
            pltpu.VMEM((2 * tq, tk), BF16), pltpu.VMEM((2 * tq, tk), BF16),
            pltpu.VMEM((2 * tq, LANES), F32), pltpu.VMEM((2 * tq, LANES), F32),
            pltpu.VMEM((2 * tq, LANES), F32),
            pltpu.VMEM((2 * tq, 2 * ATT_V_DIM), F32),
        ],
        compiler_params=_cparams(3),
        name="diff_attention",
    )(zb, zb, zb, zbm, zbm, lq1, lk1, lq2, lk2, ng)


def _mlstm_kernel(xqk_ref, xv_ref, xo_ref, xg_ref, zm_ref, gm_ref, cw_ref, cb_ref, gb_ref, ng_ref, o_ref,
                  ct_sc, n_sc, m_sc, tail_sc, cbuf_sc):
    c = pl.program_id(1)
    L = BLOCK

    @pl.when(c == 0)
    def _():
        ct_sc[...] = jnp.zeros_like(ct_sc)
        n_sc[...] = jnp.zeros_like(n_sc)
        m_sc[...] = jnp.zeros_like(m_sc)
        tail_sc[...] = jnp.zeros_like(tail_sc)

    is_meta = c == 0
    row = lax.broadcasted_iota(jnp.int32, (L, 1), 0)
    valid = jnp.logical_or(c > 0, row >= PAD)

    qk_pre = jnp.where(is_meta, zm_ref[0, :, C_MQK:C_MQK + 2 * ML_WIDTH], xqk_ref[0]).astype(F32)
    v = jnp.where(is_meta, zm_ref[0, :, C_MV:C_MV + ML_WIDTH], xv_ref[0])
    gates = jnp.where(is_meta, gm_ref[0], xg_ref[0])

    cbuf_sc[0:8, :] = tail_sc[...]
    cbuf_sc[8:8 + L, :] = qk_pre
    tail_sc[...] = qk_pre[L - 8:L, :]
    acc = jnp.broadcast_to(cb_ref[...], (L, 2 * ML_WIDTH))
    for j in range(CONV_K):
        off = 8 - (CONV_K - 1) + j
        acc = acc + cw_ref[j:j + 1, :] * cbuf_sc[off:off + L, :]
    qk = acc * _sigmoid(acc)
    qk = jnp.where(valid, qk, 0.0)
    q = qk[:, :ML_WIDTH]
    k = qk[:, ML_WIDTH:] * (ML_DH ** -0.5)

    ig = gates[:, :LANES] + gb_ref[:, :LANES]
    fg = gates[:, LANES:] + gb_ref[:, LANES:]
    lf = jnp.minimum(fg, 0.0) - jnp.log1p(jnp.exp(-jnp.abs(fg)))
    ig = jnp.where(valid, ig, NEG)
    lf = jnp.where(valid, lf, 0.0)

    sidx = lax.broadcasted_iota(jnp.int32, (L, L), 0)
    ridx = lax.broadcasted_iota(jnp.int32, (L, L), 1)
    causal = ridx <= sidx
    tri = causal.astype(BF16)
    lf_hi = lf.astype(BF16)
    lf_lo = (lf - lf_hi.astype(F32)).astype(BF16)
    bcs = (jnp.dot(tri, lf_hi, preferred_element_type=F32)
           + jnp.dot(tri, lf_lo, preferred_element_type=F32))
    b_t = bcs.T
    ig_t = ig.T

    outs = []
    for h in range(ML_HEADS):
        sl = slice(h * ML_DH, (h + 1) * ML_DH)
        bc = bcs[:, h:h + 1]
        br = b_t[h:h + 1, :]
        igr = ig_t[h:h + 1, :]
        igc = ig[:, h:h + 1]
        m_h = m_sc[h:h + 1, 0:1]
        dlog = jnp.where(causal, (bc - br) + igr, NEG)
        inter = bc + m_h
        m_s = jnp.maximum(inter, jnp.max(dlog, axis=1, keepdims=True))
        w_intra = jnp.exp(dlog - m_s)
        w_inter = jnp.exp(inter - m_s)
        qh = q[:, sl]
        kh = k[:, sl]
        vh = v[:, sl]
        qb = qh.astype(BF16)
        kb = kh.astype(BF16)
        s = lax.dot_general(qb, kb, (((1,), (1,)), ((), ())), preferred_element_type=F32) * w_intra
        ct = ct_sc[h]
        nrow = n_sc[h:h + 1, :]
        num = (w_inter * jnp.dot(qb, ct.astype(BF16), preferred_element_type=F32)
               + jnp.dot(s.astype(BF16), vh, preferred_element_type=F32))
        den = w_inter * jnp.sum(qh * nrow, axis=1, keepdims=True) + jnp.sum(s, axis=1, keepdims=True)
        hh = num / jnp.maximum(jnp.abs(den), jnp.exp(-m_s))
        b_last = bc[L - 1:L, :]
        upd = (b_last - bc) + igc
        m_new = jnp.maximum(b_last + m_h, jnp.max(upd, axis=0, keepdims=True))
        w_old = jnp.exp(b_last + m_h - m_new)
        w_r = jnp.exp(upd - m_new)
        vw = (vh.astype(F32) * w_r).astype(BF16)
        ct_sc[h] = w_old * ct + lax.dot_general(kb, vw, (((0,), (0,)), ((), ())), preferred_element_type=F32)
        n_sc[h:h + 1, :] = w_old * nrow + jnp.sum(kh * w_r, axis=0, keepdims=True)
        m_sc[h:h + 1, :] = jnp.broadcast_to(m_new, (1, LANES))
        hn = hh * lax.rsqrt(jnp.mean(hh * hh, axis=1, keepdims=True) + LN_EPS) * ng_ref[:, sl]
        outs.append(_sigmoid(xo_ref[0, :, sl].astype(F32)) * hn)

    @pl.when(c > 0)
    def _():
        o_ref[0] = jnp.concatenate(outs, axis=1).astype(BF16)


def _mlstm(zb, gates, zbm, gatesm, conv_w, conv_b, gate_b, ng):
    B, S, _ = zb.shape
    nc = S // BLOCK + 1

    def xc(c):
        return jnp.maximum(c - 1, 0)

    return pl.pallas_call(
        _mlstm_kernel,
        grid=(B, nc),
        in_specs=[
            pl.BlockSpec((1, BLOCK, 2 * ML_WIDTH), lambda b, c: (b, xc(c), C_MQK // (2 * ML_WIDTH))),
            pl.BlockSpec((1, BLOCK, ML_WIDTH), lambda b, c: (b, xc(c), C_MV // ML_WIDTH)),
            pl.BlockSpec((1, BLOCK, ML_WIDTH), lambda b, c: (b, xc(c), C_MO // ML_WIDTH)),
            pl.BlockSpec((1, BLOCK, GATE_W), lambda b, c: (b, xc(c), 0)),
            pl.BlockSpec((1, BLOCK, ZB_W), lambda b, c: (0, 0, 0)),
            pl.BlockSpec((1, BLOCK, GATE_W), lambda b, c: (0, 0, 0)),
            pl.BlockSpec((CONV_K, 2 * ML_WIDTH), lambda b, c: (0, 0)),
            pl.BlockSpec((1, 2 * ML_WIDTH), lambda b, c: (0, 0)),
            pl.BlockSpec((1, GATE_W), lambda b, c: (0, 0)),
            pl.BlockSpec((1, ML_WIDTH), lambda b, c: (0, 0)),
        ],
        out_specs=pl.BlockSpec((1, BLOCK, ML_WIDTH), lambda b, c: (b, xc(c), 0)),
        out_shape=jax.ShapeDtypeStruct((B, S, ML_WIDTH), BF16),
        scratch_shapes=[
            pltpu.VMEM((ML_HEADS, ML_DH, ML_DH), F32),
            pltpu.VMEM((8, ML_DH), F32),
            pltpu.VMEM((8, LANES), F32),
            pltpu.VMEM((8, 2 * ML_WIDTH), F32),
            pltpu.VMEM((8 + BLOCK, 2 * ML_WIDTH), F32),
        ],
        compiler_params=_cparams(2),
        name="mlstm",
    )(zb, zb, zb, gates, zbm, gatesm, conv_w, conv_b, gate_b, ng)


def _outproj_kernel(x_ref, ya_ref, ym_ref, ga_ref, gmm_ref, eg_ref, eb_ref, wa_ref, wm_ref, wo_ref,
                    g1_ref, b1_ref, wr_ref, br_ref,
                    h1_ref, eid_ref, gate_ref, rank_ref, cnt_ref, run_sc, *, tm):
    i = pl.program_id(0)

    @pl.when(i == 0)
    def _():
        run_sc[...] = jnp.zeros_like(run_sc)

    h0 = _ln(x_ref[...], eg_ref[...], eb_ref[...])
    pa = jnp.dot(ya_ref[...], wa_ref[...], preferred_element_type=F32)
    pm = jnp.dot(ym_ref[...], wm_ref[...], preferred_element_type=F32)
    sig = lambda g: 0.5 * jnp.tanh(0.5 * g) + 0.5
    merged = sig(ga_ref[...]) * pa.astype(BF16) + sig(gmm_ref[...]) * pm.astype(BF16)
    mix = jnp.dot(merged, wo_ref[...], preferred_element_type=F32)
    h1 = _ln(DN_ALPHA * h0 + mix, g1_ref[...], b1_ref[...])
    h1_ref[...] = h1

    h_hi = h1.astype(BF16)
    h_mid = (h1 - h_hi.astype(F32)).astype(BF16)
    hh = jnp.dot(h_hi, wr_ref[...], preferred_element_type=F32)
    logits = (hh[:, :LANES] + hh[:, LANES:]
              + jnp.dot(h_mid, wr_ref[:, :LANES], preferred_element_type=F32) + br_ref[...])
    lane = lax.broadcasted_iota(jnp.int32, (tm, LANES), 1)
    work = logits
    sel_e, sel_v = [], []
    for _ in range(TOP_K):
        mv = jnp.max(work, axis=1, keepdims=True)
        e = jnp.min(jnp.where(work == mv, lane, LANES), axis=1, keepdims=True)
        sel_e.append(e)
        sel_v.append(mv)
        work = jnp.where(lane == e, -jnp.inf, work)
    ex = [jnp.exp(v - sel_v[0]) for v in sel_v]
    den = ex[0] + ex[1] + ex[2] + ex[3]

    onehots = [lane == e for e in sel_e]
    oh = jnp.zeros((tm, LANES), F32)
    for o in onehots:
        oh = oh + o.astype(F32)
    r_i = lax.broadcasted_iota(jnp.int32, (tm, tm), 0)
    c_i = lax.broadcasted_iota(jnp.int32, (tm, tm), 1)
    strict = (c_i < r_i).astype(BF16)
    before = jnp.dot(strict, oh.astype(BF16), preferred_element_type=F32) + run_sc[...]
    eid = jnp.zeros((tm, LANES), jnp.int32)
    gate = jnp.zeros((tm, LANES), F32)
    rank = jnp.zeros((tm, LANES), F32)
    for kk in range(TOP_K):
        here = lane == kk
        eid = jnp.where(here, sel_e[kk], eid)
        gate = jnp.where(here, ex[kk] / den, gate)
        rk = jnp.sum(jnp.where(onehots[kk], before, 0.0), axis=1, keepdims=True)
        rank = jnp.where(here, rk, rank)
    eid_ref[...] = eid
    gate_ref[...] = gate
    rank_ref[...] = rank.astype(jnp.int32)
    run_sc[...] = run_sc[...] + jnp.sum(oh, axis=0, keepdims=True)
    cnt_ref[...] = run_sc[...]


def _outproj(x2, yatt, yml, zb2, eg, eb, wa, wm, wo, g1, b1, wr, br, *, tm):
    N, D = x2.shape
    vec = lambda w: pl.BlockSpec((1, w), lambda i: (0, 0))
    full = lambda a, b: pl.BlockSpec((a, b), lambda i: (0, 0))
    tile = lambda w: pl.BlockSpec((tm, w), lambda i: (i, 0))
    return pl.pallas_call(
        functools.partial(_outproj_kernel, tm=tm),
        grid=(N // tm,),
        in_specs=[
            tile(D), tile(ATT_HEADS * ATT_V_DIM), tile(ML_WIDTH),
            pl.BlockSpec((tm, D), lambda i: (i, C_GA // D)),
            pl.BlockSpec((tm, D), lambda i: (i, C_GM // D)),
            vec(D), vec(D),
            full(ATT_HEADS * ATT_V_DIM, D), full(ML_WIDTH, D), full(D, D),
            vec(D), vec(D), full(D, 2 * LANES), vec(LANES),
        ],
        out_specs=[tile(D), tile(LANES), tile(LANES), tile(LANES), vec(LANES)],
        out_shape=[
            jax.ShapeDtypeStruct((N, D), F32),
            jax.ShapeDtypeStruct((N, LANES), jnp.int32),
            jax.ShapeDtypeStruct((N, LANES), F32),
            jax.ShapeDtypeStruct((N, LANES), jnp.int32),
            jax.ShapeDtypeStruct((1, LANES), F32),
        ],
        scratch_shapes=[pltpu.VMEM((1, LANES), F32)],
        compiler_params=_cparams(1),
        name="outproj_router",
    )(x2, yatt, yml, zb2, zb2, eg, eb, wa, wm, wo, g1, b1, wr, br)


def _row_copies(src_row, dst_row, sem, n, issue):
    def body(r, carry):
        for kk in range(TOP_K):
            cp = pltpu.make_async_copy(src_row(r, kk), dst_row(r, kk), sem)
            if issue:
                cp.start(priority=kk % 2)
            else:
                cp.wait()
        return carry
    lax.fori_loop(0, n, body, 0)


def _dispatch_kernel(pad_pos_ref, pad_len_ref, d_ref, h_ref, xs_ref, zero_sc, sem, *, tm):
    @pl.when(pl.program_id(0) == 0)
    def _():
        zero_sc[...] = jnp.zeros_like(zero_sc)

        def pad_copies(e, issue):
            pos = pad_pos_ref[e]

            def per_row(r, c):
                cp = pltpu.make_async_copy(zero_sc.at[pl.ds(0, 1)], xs_ref.at[pl.ds(pos + r, 1)],
                                           sem.at[1 + (e & 1)])
                if issue:
                    cp.start()
                else:
                    cp.wait()
                return c
            lax.fori_loop(0, pad_len_ref[e], per_row, 0)

        def per_expert(e, carry):
            @pl.when(e + 1 < N_EXPERTS)
            def _():
                pad_copies(e + 1, True)
            pad_copies(e, False)
            return carry

        pad_copies(0, True)
        lax.fori_loop(0, N_EXPERTS, per_expert, 0)

    src = lambda r, kk: h_ref.at[pl.ds(r, 1)]
    dst = lambda r, kk: xs_ref.at[pl.ds(d_ref[0, 0, r * TOP_K + kk], 1)]
    _row_copies(src, dst, sem.at[0], tm, True)
    _row_copies(src, dst, sem.at[0], tm, False)


def _dispatch(pad_pos, pad_len, dest3, h1, rows_out, *, tm):
    N, D = h1.shape
    return pl.pallas_call(
        functools.partial(_dispatch_kernel, tm=tm),
        grid_spec=pltpu.PrefetchScalarGridSpec(
            num_scalar_prefetch=2,
            grid=(N // tm,),
            in_specs=[
                pl.BlockSpec((1, 1, tm * TOP_K), lambda i, pp, pn: (i, 0, 0), memory_space=pltpu.SMEM),
                pl.BlockSpec((tm, D), lambda i, pp, pn: (i, 0)),
            ],
            out_specs=pl.BlockSpec(memory_space=pl.ANY),
            scratch_shapes=[pltpu.VMEM((8, D), F32), pltpu.SemaphoreType.DMA((3,))],
        ),
        out_shape=jax.ShapeDtypeStruct((rows_out, D), F32),
        compiler_params=_cparams(1),
        name="moe_dispatch",
    )(pad_pos, pad_len, dest3, h1)


def _deinterleave_kernel(w_ref, p_ref, wg_ref, wl_ref):
    for c in range(w_ref.shape[2] // (2 * LANES)):
        blk = w_ref[0, :, c * 2 * LANES:(c + 1) * 2 * LANES].astype(BF16)
        r = jnp.dot(blk, p_ref[...], preferred_element_type=F32).astype(BF16)
        wg_ref[0, :, c * LANES:(c + 1) * LANES] = r[:, :LANES]
        wl_ref[0, :, c * LANES:(c + 1) * LANES] = r[:, LANES:]


def _deinterleave(w_gu):
    E, D, F2 = w_gu.shape
    F = F2 // 2
    j = np.arange(2 * LANES)
    sel = np.zeros((2 * LANES, 2 * LANES), np.float32)
    sel[j, (j % 2) * LANES + j // 2] = 1.0
    out = jax.ShapeDtypeStruct((E, D, F), BF16)
    return pl.pallas_call(
        _deinterleave_kernel,
        grid=(E,),
        in_specs=[
            pl.BlockSpec((1, D, F2), lambda e: (e, 0, 0)),
            pl.BlockSpec((2 * LANES, 2 * LANES), lambda e: (0, 0)),
        ],
        out_specs=[pl.BlockSpec((1, D, F), lambda e: (e, 0, 0))] * 2,
        out_shape=[out, out],
        compiler_params=_cparams(1),
        name="deinterleave_wgu",
    )(w_gu, jnp.asarray(sel, BF16))


def _expert_kernel(be_ref, nu_ref, x_ref, wg_ref, wl_ref, bg_ref, bl_ref, wd_ref, bd_ref, y_ref):
    del be_ref

    @pl.when(pl.program_id(0) < nu_ref[0])
    def _():
        xb = x_ref[...].astype(BF16)
        hg = jnp.dot(xb, wg_ref[0], preferred_element_type=F32) + bg_ref[0]
        hl = jnp.dot(xb, wl_ref[0], preferred_element_type=F32) + bl_ref[0]
        glu = jnp.minimum(hg, SWIGLU_LIMIT)
        lin = jnp.clip(hl, -SWIGLU_LIMIT, SWIGLU_LIMIT)
        act = glu * _sigmoid(SWIGLU_ALPHA * glu) * (lin + 1.0)
        y_ref[...] = jnp.dot(act.astype(BF16), wd_ref[0], preferred_element_type=F32) + bd_ref[0]


def _experts(blk_exp, n_used, xs, wg, wl, bg, bl, wd, bd, *, tm):
    MP, D = xs.shape
    F = wg.shape[2]
    row = lambda i, be, nu: (jnp.minimum(i, nu[0] - 1), 0)
    wsp = lambda a, b: pl.BlockSpec((1, a, b), lambda i, be, nu: (be[i], 0, 0))
    return pl.pallas_call(
        _expert_kernel,
        grid_spec=pltpu.PrefetchScalarGridSpec(
            num_scalar_prefetch=2,
            grid=(MP // tm,),
            in_specs=[pl.BlockSpec((tm, D), row), wsp(D, F), wsp(D, F), wsp(1, F), wsp(1, F), wsp(F, D), wsp(1, D)],
            out_specs=pl.BlockSpec((tm, D), row),
        ),
        out_shape=jax.ShapeDtypeStruct((MP, D), F32),
        compiler_params=_cparams(1),
        name="moe_experts",
    )(blk_exp, n_used, xs, wg, wl, bg, bl, wd, bd)


def _combine_kernel(dcur_ref, dnxt_ref, h1_ref, gate_ref, g2_ref, b2_ref, ys_ref, o_ref, ybuf, sem, *, tm, nsteps):
    i = pl.program_id(0)
    slot = i % 2

    def copies(d_ref, s, issue):
        src = lambda r, kk: ys_ref.at[pl.ds(d_ref[0, 0, r * TOP_K + kk], 1)]
        dst = lambda r, kk: ybuf.at[s, kk, pl.ds(r, 1)]
        _row_copies(src, dst, sem.at[s], tm, issue)

    @pl.when(i == 0)
    def _():
        copies(dcur_ref, 0, True)

    @pl.when(i + 1 < nsteps)
    def _():
        copies(dnxt_ref, 1 - slot, True)

    copies(dcur_ref, slot, False)
    acc = DN_ALPHA * h1_ref[...]
    for kk in range(TOP_K):
        acc = acc + gate_ref[:, kk:kk + 1] * ybuf[slot, kk]
    o_ref[...] = _ln(acc, g2_ref[...], b2_ref[...])


def _combine(dest3, h1, gate, g2, b2, ys, *, tm):
    N, D = h1.shape
    nsteps = N // tm
    dspec = lambda f: pl.BlockSpec((1, 1, tm * TOP_K), f, memory_space=pltpu.SMEM)
    return pl.pallas_call(
        functools.partial(_combine_kernel, tm=tm, nsteps=nsteps),
        grid=(nsteps,),
        in_specs=[
            dspec(lambda i: (i, 0, 0)),
            dspec(lambda i: (jnp.minimum(i + 1, nsteps - 1), 0, 0)),
            pl.BlockSpec((tm, D), lambda i: (i, 0)),
            pl.BlockSpec((tm, LANES), lambda i: (i, 0)),
            pl.BlockSpec((1, D), lambda i: (0, 0)),
            pl.BlockSpec((1, D), lambda i: (0, 0)),
            pl.BlockSpec(memory_space=pl.ANY),
        ],
        out_specs=pl.BlockSpec((tm, D), lambda i: (i, 0)),
        out_shape=jax.ShapeDtypeStruct((N, D), F32),
        scratch_shapes=[pltpu.VMEM((2, TOP_K, tm, D), F32), pltpu.SemaphoreType.DMA((2,))],
        compiler_params=_cparams(1),
        name="moe_combine",
    )(dest3, dest3, h1, gate, g2, b2, ys)


def _rotary_column_order():
    idx = np.empty((ATT_HEADS * LANES,), np.int32)
    for h in range(ATT_HEADS):
        for half in range(2):
            for sub in range(2):
                for dd in range(32):
                    idx[h * LANES + half * 64 + sub * 32 + dd] = (2 * h + sub) * ATT_QK_DIM + half * 32 + dd
    return idx


def _rotary_tables(pos):
    half = ATT_QK_DIM // 2
    inv_freq = ROPE_THETA ** (-jnp.arange(half, dtype=F32) / half)
    ang = pos.astype(F32)[:, None] * inv_freq[None, :]
    cos = jnp.tile(jnp.cos(ang), (1, 4))
    sin = jnp.tile(jnp.sin(ang), (1, 4))
    sign = jnp.where(jnp.arange(LANES) < 64, -1.0, 1.0).astype(F32)
    return cos, sin * sign[None, :]


def _pick(n, prefs):
    for t in prefs:
        if n % t == 0:
            return t
    raise ValueError(f"unsupported size {n}")


def kernel(x, meta, emb_ln_g, emb_ln_b, w_in, conv_w, conv_b, gate_bias, lam_q1, lam_k1, lam_q2, lam_k2,
           att_norm_g, ml_norm_g, w_att_out, w_ml_out, w_o, ln1_g, ln1_b, w_router, b_router,
           w_gu, b_gu, w_down, b_down, ln2_g, ln2_b):
    B, S, D = x.shape
    assert D == D_MODEL and S % 512 == 0 and w_in.shape[0] == DEPTH
    N = B * S
    row2 = lambda a: a.reshape(1, -1)

    w = w_in[0]
    o_aq, o_ak, o_av, o_mqk, o_mv, o_mo, o_gt, o_ga, o_gm = 0, 512, 1024, 1536, 2560, 3072, 3584, 3592, 4616
    perm = _rotary_column_order()
    gpad = jnp.zeros((D, LANES - ML_HEADS), F32)
    w_all = jnp.concatenate([
        w[:, o_ga:o_ga + D], w[:, o_gm:o_gm + D], w[:, o_mqk:o_mqk + 2 * ML_WIDTH],
        w[:, o_aq:o_aq + 512][:, perm], w[:, o_ak:o_ak + 512][:, perm], w[:, o_av:o_av + 512],
        w[:, o_mv:o_mv + ML_WIDTH], w[:, o_mo:o_mo + ML_WIDTH],
        w[:, o_gt:o_gt + ML_HEADS], gpad, w[:, o_gt + ML_HEADS:o_gt + 2 * ML_HEADS], gpad,
    ], axis=1).astype(BF16)
    gb = gate_bias[0]
    zpad = jnp.zeros((LANES - ML_HEADS,), F32)
    gate_b = jnp.concatenate([gb[:ML_HEADS], zpad, gb[ML_HEADS:], zpad]).reshape(1, GATE_W)

    cos_x, sin_x = _rotary_tables(N_META + jnp.arange(S))
    cos_m, sin_m = _rotary_tables(jnp.maximum(jnp.arange(BLOCK) - PAD, 0))

    eg, eb = row2(emb_ln_g), row2(emb_ln_b)

    tm_in = 512
    zb, gates = _inproj(x, eg, eb, w_all, cos_x, sin_x, tm=tm_in)
    xm = jnp.concatenate([jnp.zeros((PAD, D), x.dtype), meta.astype(x.dtype)], axis=0)[None]
    zbm, gatesm = _inproj(xm, eg, eb, w_all, cos_m, sin_m, tm=BLOCK, first_valid_row=PAD)

    tq = 512
    yatt = _attention(zb, zbm, row2(lam_q1[0]), row2(lam_k1[0]), row2(lam_q2[0]), row2(lam_k2[0]),
                      row2(att_norm_g[0]), tq=tq)
    yml = _mlstm(zb, gates, zbm, gatesm, conv_w[0], row2(conv_b[0]), gate_b, row2(ml_norm_g[0]))

    wr32 = jnp.concatenate([w_router[0], jnp.zeros((D, LANES - N_EXPERTS), F32)], axis=1)
    wr_hi = wr32.astype(BF16)
    wr = jnp.concatenate([wr_hi, (wr32 - wr_hi.astype(F32)).astype(BF16)], axis=1)
    br = jnp.concatenate([b_router[0], jnp.full((LANES - N_EXPERTS,), NEG, F32)]).reshape(1, LANES)
    tm_out = 512
    h1, eid, gate, rank, cnt = _outproj(
        x.reshape(N, D), yatt.reshape(N, -1), yml.reshape(N, -1), zb.reshape(N, ZB_W), eg, eb,
        w_att_out[0].astype(BF16), w_ml_out[0].astype(BF16), w_o[0].astype(BF16),
        row2(ln1_g[0]), row2(ln1_b[0]), wr, br, tm=tm_out)

    tm_e = 512
    M = N * TOP_K
    nb = (M + N_EXPERTS * (tm_e - 1) + tm_e - 1) // tm_e
    counts = cnt[0, :N_EXPERTS].astype(jnp.int32)
    nblk = (counts + tm_e - 1) // tm_e
    cum = jnp.cumsum(nblk)
    pstart = (cum - nblk) * tm_e
    dest = pstart[eid[:, :TOP_K]] + rank[:, :TOP_K]
    n_used = cum[-1:].astype(jnp.int32)
    blk = jnp.minimum(jnp.arange(nb, dtype=jnp.int32), n_used[0] - 1)
    blk_exp = jnp.sum(blk[:, None] >= cum[None, :], axis=1).astype(jnp.int32)

    tm_d = 256
    dest3 = dest.reshape(N // tm_d, 1, tm_d * TOP_K)
    xs = _dispatch(pstart + counts, nblk * tm_e - counts, dest3, h1, nb * tm_e, tm=tm_d)

    wg, wl = _deinterleave(w_gu[0])
    bgu = b_gu[0]
    ys = _experts(blk_exp, n_used, xs, wg, wl,
                  bgu[:, None, 0::2], bgu[:, None, 1::2],
                  w_down[0].astype(BF16), b_down[0][:, None, :], tm=tm_e)

    out = _combine(dest3, h1, gate, row2(ln2_g[0]), row2(ln2_b[0]), ys, tm=tm_d)
    return out.reshape(B, S, D)
```

```python
import functools
import math

import jax
import jax.numpy as jnp
import numpy as np
from jax import lax
from jax.experimental import pallas as pl
from jax.experimental.pallas import tpu as pltpu

D_MODEL = 1024
N_META = 16
BLOCK = 128
PAD = BLOCK - N_META
NEG = -1e30
LN_EPS = 1e-5
ATT_HEADS = 4
ATT_QK_DIM = 64
ATT_V_DIM = 128
ROPE_THETA = 10000.0
ML_HEADS = 4
ML_DH = 128
ML_WIDTH = ML_HEADS * ML_DH
CONV_K = 4
N_EXPERTS = 32
TOP_K = 4
SWIGLU_LIMIT = 7.0
SWIGLU_ALPHA = 1.702
DEPTH = 1
DN_ALPHA = (2 * DEPTH) ** 0.25
LAMBDA_INIT = 0.8 - 0.6 * math.exp(-0.3 * 0)

LANES = 128
C_GA, C_GM, C_MQK, C_AQ, C_AK, C_AV, C_MV, C_MO = 0, 1024, 2048, 3072, 3584, 4096, 4608, 5120
ZB_W = 5632
GATE_W = 2 * LANES
W_ALL = ZB_W + GATE_W
CHUNK = 512

VMEM_LIMIT = 56 * 1024 * 1024

BF16 = jnp.bfloat16
F32 = jnp.float32


def _cparams(n_axes):
    return pltpu.CompilerParams(dimension_semantics=("arbitrary",) * n_axes, vmem_limit_bytes=VMEM_LIMIT)


def _ln(x, g, b):
    mu = jnp.mean(x, axis=-1, keepdims=True)
    xc = x - mu
    var = jnp.mean(xc * xc, axis=-1, keepdims=True)
    return xc * lax.rsqrt(var + LN_EPS) * g + b


def _sigmoid(x):
    return 1.0 / (1.0 + jnp.exp(-x))


ROW_TILE = D_MODEL // LANES


def _store_row_tiles(ref, val):
    tm = val.shape[0]
    for g in range(ROW_TILE):
        ref[pl.ds(g, tm, stride=ROW_TILE), :] = val[:, g * LANES:(g + 1) * LANES]


def _load_row_tiles(ref, tm):
    return jnp.concatenate([ref[pl.ds(g, tm, stride=ROW_TILE), :] for g in range(ROW_TILE)], axis=1)


def _row_tile(ref, r):
    return ref.at[pl.ds(pl.multiple_of(r * ROW_TILE, ROW_TILE), ROW_TILE)]


def _inproj_kernel(x_ref, g_ref, b_ref, w_ref, cos_ref, sin_ref, zb_ref, gt_ref, *, first_valid_row):
    x = x_ref[0]
    tm = x.shape[0]
    hb = _ln(x, g_ref[...], b_ref[...]).astype(BF16)
    cos = cos_ref[...]
    sin = sin_ref[...]
    if first_valid_row:
        rowmask = lax.broadcasted_iota(jnp.int32, (tm, 1), 0) >= first_valid_row
    for c in range(ZB_W // CHUNK):
        z = jnp.dot(hb, w_ref[:, c * CHUNK:(c + 1) * CHUNK], preferred_element_type=F32)
        if c * CHUNK in (C_AQ, C_AK):
            parts = []
            for h in range(ATT_HEADS):
                zh = z[:, h * LANES:(h + 1) * LANES]
                parts.append(zh * cos + pltpu.roll(zh, 64, 1) * sin)
            z = jnp.concatenate(parts, axis=1)
            if c * CHUNK == C_AQ:
                z = z * (ATT_QK_DIM ** -0.5)
        if first_valid_row:
            z = jnp.where(rowmask, z, 0.0)
        zb_ref[0, :, c * CHUNK:(c + 1) * CHUNK] = z.astype(BF16)
    zg = jnp.dot(hb, w_ref[:, ZB_W:W_ALL], preferred_element_type=F32)
    if first_valid_row:
        zg = jnp.where(rowmask, zg, 0.0)
    gt_ref[0] = zg


def _inproj(x3, g, b, w, cos, sin, *, tm, first_valid_row=0):
    B, S, D = x3.shape
    nt = S // tm
    return pl.pallas_call(
        functools.partial(_inproj_kernel, first_valid_row=first_valid_row),
        grid=(B, nt),
        in_specs=[
            pl.BlockSpec((1, tm, D), lambda bi, i: (bi, i, 0)),
            pl.BlockSpec((1, D), lambda bi, i: (0, 0)),
            pl.BlockSpec((1, D), lambda bi, i: (0, 0)),
            pl.BlockSpec((D, W_ALL), lambda bi, i: (0, 0)),
            pl.BlockSpec((tm, LANES), lambda bi, i: (i, 0)),
            pl.BlockSpec((tm, LANES), lambda bi, i: (i, 0)),
        ],
        out_specs=[
            pl.BlockSpec((1, tm, ZB_W), lambda bi, i: (bi, i, 0)),
            pl.BlockSpec((1, tm, GATE_W), lambda bi, i: (bi, i, 0)),
        ],
        out_shape=[
            jax.ShapeDtypeStruct((B, S, ZB_W), BF16),
            jax.ShapeDtypeStruct((B, S, GATE_W), F32),
        ],
        compiler_params=_cparams(2),
        name="inproj",
    )(x3, g, b, w, cos, sin)


def _attn_kernel(q_ref, k_ref, v_ref, km_ref, vm_ref, lq1_ref, lk1_ref, lq2_ref, lk2_ref, ng_ref, o_ref,
                 sa_sc, sb_sc, pa_sc, pb_sc, aa_sc, ab_sc, m_sc, acc_sc, *, tq, tk, rc):
    assert tq == 2 * tk
    qi = pl.program_id(2)
    rows = 2 * tq
    q = q_ref[0]
    lane = lax.broadcasted_iota(jnp.int32, (tq, LANES), 1)
    is_map0 = (lane & 63) < 32
    zero = jnp.zeros_like(q)
    qq = jnp.concatenate([jnp.where(is_map0, q, zero), jnp.where(is_map0, zero, q)], axis=0)

    def qk_stage(kblk, s_ref):
        s_ref[:, 0:kblk.shape[0]] = lax.dot_general(qq, kblk, (((1,), (1,)), ((), ())),
                                                   preferred_element_type=F32)

    def pv_stage(p_ref, a_ref, vblk, first=False):
        width = vblk.shape[0]
        v1 = jnp.concatenate([vblk, jnp.ones_like(vblk)], axis=1)
        pv = jnp.dot(p_ref[:, 0:width], v1, preferred_element_type=F32)
        if first:
            acc_sc[...] = pv
        else:
            a = a_ref[...]
            acc_sc[...] = jnp.concatenate([a, a], axis=1) * acc_sc[...] + pv

    def sm_stage(s_ref, p_ref, a_ref, width, vis, first=False):
        nrep = width // LANES
        for r0 in range(0, rows, rc):
            kind = vis(r0)
            if kind == "none":
                continue
            s = s_ref[r0:r0 + rc, 0:width]
            if kind != "all":
                s = kind(s)
            m_cur = jnp.max(s, axis=1, keepdims=True)
            if first:
                m_new = jnp.broadcast_to(m_cur, (rc, LANES))
            else:
                m_prev = m_sc[r0:r0 + rc, :]
                m_new = jnp.maximum(m_prev, m_cur)
                a_ref[r0:r0 + rc, :] = jnp.exp(m_prev - m_new)
            m_sc[r0:r0 + rc, :] = m_new
        for r0 in range(0, rows, rc):
            kind = vis(r0)
            if kind == "none":
                p_ref[r0:r0 + rc, 0:width] = jnp.zeros((rc, width), BF16)
                a_ref[r0:r0 + rc, :] = jnp.ones((rc, LANES), F32)
                continue
            s = s_ref[r0:r0 + rc, 0:width]
            if kind != "all":
                s = kind(s)
            m_new = m_sc[r0:r0 + rc, :]
            m_rep = m_new if nrep == 1 else jnp.concatenate([m_new] * nrep, axis=1)
            p_ref[r0:r0 + rc, 0:width] = jnp.exp(s - m_rep).astype(BF16)

    def meta_vis(r0):
        def mask(s):
            col = lax.broadcasted_iota(jnp.int32, s.shape, 1)
            return jnp.where(col >= PAD, s, NEG)
        return mask

    def diag_vis(d):
        def vis(r0):
            q0 = r0 % tq
            if d * tk + tk - 1 <= q0:
                return "all"
            if d * tk > q0 + rc - 1:
                return "none"
            def mask(s):
                rowq = q0 + lax.broadcasted_iota(jnp.int32, s.shape, 0)
                col = d * tk + lax.broadcasted_iota(jnp.int32, s.shape, 1)
                return jnp.where(col <= rowq, s, NEG)
            return mask
        return vis

    all_vis = lambda r0: "all"
    kblk = lambda start: k_ref[0, pl.ds(start, tk), :]
    vblk = lambda start: v_ref[0, pl.ds(start, tk), :]

    qk_stage(km_ref[0], sa_sc)
    sm_stage(sa_sc, pa_sc, None, BLOCK, meta_vis, first=True)
    pv_stage(pa_sc, None, vm_ref[0], first=True)

    d0 = pl.multiple_of(qi * tq, tk)
    d1 = pl.multiple_of(qi * tq + tk, tk)
    qk_stage(kblk(d0), sa_sc)
    qk_stage(kblk(d1), sb_sc)
    sm_stage(sa_sc, pa_sc, aa_sc, tk, diag_vis(0))
    pv_stage(pa_sc, aa_sc, vblk(d0))
    sm_stage(sb_sc, pb_sc, ab_sc, tk, diag_vis(1))
    qk_stage(kblk(0), sa_sc)

    def pair(i, carry):
        a_start = pl.multiple_of(i * tq, tk)
        b_start = pl.multiple_of(i * tq + tk, tk)
        prev_b = pl.multiple_of(jnp.where(i == 0, d1, a_start - tk), tk)
        next_a = pl.multiple_of(jnp.minimum(a_start + tq, (qi - 1) * tq), tk)
        pv_stage(pb_sc, ab_sc, vblk(prev_b))
        sm_stage(sa_sc, pa_sc, aa_sc, tk, all_vis)
        qk_stage(kblk(b_start), sb_sc)
        pv_stage(pa_sc, aa_sc, vblk(a_start))
        sm_stage(sb_sc, pb_sc, ab_sc, tk, all_vis)
        qk_stage(kblk(next_a), sa_sc)
        return carry

    lax.fori_loop(0, qi, pair, 0)
    last_b = pl.multiple_of(jnp.where(qi == 0, d1, qi * tq - tk), tk)
    pv_stage(pb_sc, ab_sc, vblk(last_b))

    o = acc_sc[:, 0:ATT_V_DIM] / acc_sc[:, ATT_V_DIM:]
    s1 = jnp.sum(lq1_ref[...] * lk1_ref[...], axis=1, keepdims=True)
    s2 = jnp.sum(lq2_ref[...] * lk2_ref[...], axis=1, keepdims=True)
    lam = jnp.exp(s1) - jnp.exp(s2) + LAMBDA_INIT
    a = o[:tq] - lam * o[tq:]
    y = a * lax.rsqrt(jnp.mean(a * a, axis=1, keepdims=True) + LN_EPS) * ng_ref[...]
    o_ref[0] = (y * (1.0 - LAMBDA_INIT)).astype(BF16)


def _attention(zb, zbm, lq1, lk1, lq2, lk2, ng, *, tq):
    B, S, _ = zb.shape
    nq = S // tq
    tk = tq // 2
    qb, kb, vb = C_AQ // LANES, C_AK // LANES, C_AV // LANES
    lam_spec = pl.BlockSpec((1, ATT_QK_DIM), lambda b, h, i: (0, 0))
    return pl.pallas_call(
        functools.partial(_attn_kernel, tq=tq, tk=tk, rc=32),
        grid=(B, ATT_HEADS, nq),
        in_specs=[
            pl.BlockSpec((1, tq, LANES), lambda b, h, i: (b, i, qb + h)),
            pl.BlockSpec((1, S, LANES), lambda b, h, i: (b, 0, kb + h)),
            pl.BlockSpec((1, S, LANES), lambda b, h, i: (b, 0, vb + h)),
            pl.BlockSpec((1, BLOCK, LANES), lambda b, h, i: (0, 0, kb + h)),
            pl.BlockSpec((1, BLOCK, LANES), lambda b, h, i: (0, 0, vb + h)),
            lam_spec, lam_spec, lam_spec, lam_spec,
            pl.BlockSpec((1, ATT_V_DIM), lambda b, h, i: (0, 0)),
        ],
        out_specs=pl.BlockSpec((1, tq, LANES), lambda b, h, i: (b, i, h)),
        out_shape=jax.ShapeDtypeStruct((B, S, ATT_HEADS * ATT_V_DIM), BF16),
        scratch_shapes=[
            pltpu.VMEM((2 * tq, tk), F32), pltpu.VMEM((2 * tq, tk), F32),
            pltpu.VMEM((2 * tq, tk), BF16), pltpu.VMEM((2 * tq, tk), BF16),
            pltpu.VMEM((2 * tq, LANES), F32), pltpu.VMEM((2 * tq, LANES), F32),
            pltpu.VMEM((2 * tq, LANES), F32),
            pltpu.VMEM((2 * tq, 2 * ATT_V_DIM), F32),
        ],
        compiler_params=_cparams(3),
        name="diff_attention",
    )(zb, zb, zb, zbm, zbm, lq1, lk1, lq2, lk2, ng)


def _mlstm_chunk(xqk_ref, xv_ref, xo_ref, xg_ref, zm_ref, gm_ref, cw_ref, cb_ref, gb_ref, ng_ref, o_ref,
                 ct_sc, n_sc, m_sc, tail_sc, cbuf_sc):
    c = pl.program_id(1)
    L = BLOCK

    @pl.when(c == 0)
    def _():
        ct_sc[...] = jnp.zeros_like(ct_sc)
        n_sc[...] = jnp.zeros_like(n_sc)
        m_sc[...] = jnp.zeros_like(m_sc)
        tail_sc[...] = jnp.zeros_like(tail_sc)

    is_meta = c == 0
    row = lax.broadcasted_iota(jnp.int32, (L, 1), 0)
    valid = jnp.logical_or(c > 0, row >= PAD)

    qk_pre = jnp.where(is_meta, zm_ref[0, :, C_MQK:C_MQK + 2 * ML_WIDTH], xqk_ref[...]).astype(F32)
    v = jnp.where(is_meta, zm_ref[0, :, C_MV:C_MV + ML_WIDTH], xv_ref[...])
    gates = jnp.where(is_meta, gm_ref[0], xg_ref[...])

    cbuf_sc[0:8, :] = tail_sc[...]
    cbuf_sc[8:8 + L, :] = qk_pre
    tail_sc[...] = qk_pre[L - 8:L, :]
    acc = jnp.broadcast_to(cb_ref[...], (L, 2 * ML_WIDTH))
    for j in range(CONV_K):
        off = 8 - (CONV_K - 1) + j
        acc = acc + cw_ref[j:j + 1, :] * cbuf_sc[off:off + L, :]
    qk = acc * _sigmoid(acc)
    qk = jnp.where(valid, qk, 0.0)
    q = qk[:, :ML_WIDTH]
    k = qk[:, ML_WIDTH:] * (ML_DH ** -0.5)

    ig = gates[:, :LANES] + gb_ref[:, :LANES]
    fg = gates[:, LANES:] + gb_ref[:, LANES:]
    lf = jnp.minimum(fg, 0.0) - jnp.log1p(jnp.exp(-jnp.abs(fg)))
    ig = jnp.where(valid, ig, NEG)
    lf = jnp.where(valid, lf, 0.0)

    sidx = lax.broadcasted_iota(jnp.int32, (L, L), 0)
    ridx = lax.broadcasted_iota(jnp.int32, (L, L), 1)
    causal = ridx <= sidx
    tri = causal.astype(BF16)
    lf_hi = lf.astype(BF16)
    lf_lo = (lf - lf_hi.astype(F32)).astype(BF16)
    bcs = (jnp.dot(tri, lf_hi, preferred_element_type=F32)
           + jnp.dot(tri, lf_lo, preferred_element_type=F32))
    b_t = bcs.T
    ig_t = ig.T

    outs = []
    for h in range(ML_HEADS):
        sl = slice(h * ML_DH, (h + 1) * ML_DH)
        bc = bcs[:, h:h + 1]
        br = b_t[h:h + 1, :]
        igr = ig_t[h:h + 1, :]
        igc = ig[:, h:h + 1]
        m_h = m_sc[h:h + 1, 0:1]
        dlog = jnp.where(causal, (bc - br) + igr, NEG)
        inter = bc + m_h
        m_s = jnp.maximum(inter, jnp.max(dlog, axis=1, keepdims=True))
        w_intra = jnp.exp(dlog - m_s)
        w_inter = jnp.exp(inter - m_s)
        qh = q[:, sl]
        kh = k[:, sl]
        vh = v[:, sl]
        qb = qh.astype(BF16)
        kb = kh.astype(BF16)
        s = lax.dot_general(qb, kb, (((1,), (1,)), ((), ())), preferred_element_type=F32) * w_intra
        ct = ct_sc[h]
        nrow = n_sc[h:h + 1, :]
        num = (w_inter * jnp.dot(qb, ct.astype(BF16), preferred_element_type=F32)
               + jnp.dot(s.astype(BF16), vh, preferred_element_type=F32))
        den = w_inter * jnp.sum(qh * nrow, axis=1, keepdims=True) + jnp.sum(s, axis=1, keepdims=True)
        hh = num / jnp.maximum(jnp.abs(den), jnp.exp(-m_s))
        b_last = bc[L - 1:L, :]
        upd = (b_last - bc) + igc
        m_new = jnp.maximum(b_last + m_h, jnp.max(upd, axis=0, keepdims=True))
        w_old = jnp.exp(b_last + m_h - m_new)
        w_r = jnp.exp(upd - m_new)
        vw = (vh.astype(F32) * w_r).astype(BF16)
        ct_sc[h] = w_old * ct + lax.dot_general(kb, vw, (((0,), (0,)), ((), ())), preferred_element_type=F32)
        n_sc[h:h + 1, :] = w_old * nrow + jnp.sum(kh * w_r, axis=0, keepdims=True)
        m_sc[h:h + 1, :] = jnp.broadcast_to(m_new, (1, LANES))
        hn = hh * lax.rsqrt(jnp.mean(hh * hh, axis=1, keepdims=True) + LN_EPS) * ng_ref[:, sl]
        outs.append(_sigmoid(xo_ref[:, sl].astype(F32)) * hn)

    @pl.when(c > 0)
    def _():
        o_ref[...] = jnp.concatenate(outs, axis=1).astype(BF16)


def _mlstm_kernel(xqk_ref, xv_ref, xo_ref, xg_ref, zm_ref, gm_ref, cw_ref, cb_ref, gb_ref, ng_ref, o_ref,
                  ct_sc, n_sc, m_sc, tail_sc, cbuf_sc, *, nbps):
    for bb in range(nbps):
        rows8 = pl.ds(bb * 8, 8)
        _mlstm_chunk(xqk_ref.at[bb], xv_ref.at[bb], xo_ref.at[bb], xg_ref.at[bb], zm_ref, gm_ref,
                     cw_ref, cb_ref, gb_ref, ng_ref, o_ref.at[bb],
                     ct_sc.at[pl.ds(bb * ML_HEADS, ML_HEADS)], n_sc.at[rows8], m_sc.at[rows8], tail_sc.at[rows8],
                     cbuf_sc.at[pl.ds(bb * (8 + BLOCK), 8 + BLOCK)])


def _mlstm(zb, gates, zbm, gatesm, conv_w, conv_b, gate_b, ng, *, nbps):
    B, S, _ = zb.shape
    nc = S // BLOCK + 1

    def xc(c):
        return jnp.maximum(c - 1, 0)

    return pl.pallas_call(
        functools.partial(_mlstm_kernel, nbps=nbps),
        grid=(B // nbps, nc),
        in_specs=[
            pl.BlockSpec((nbps, BLOCK, 2 * ML_WIDTH), lambda b, c: (b, xc(c), C_MQK // (2 * ML_WIDTH))),
            pl.BlockSpec((nbps, BLOCK, ML_WIDTH), lambda b, c: (b, xc(c), C_MV // ML_WIDTH)),
            pl.BlockSpec((nbps, BLOCK, ML_WIDTH), lambda b, c: (b, xc(c), C_MO // ML_WIDTH)),
            pl.BlockSpec((nbps, BLOCK, GATE_W), lambda b, c: (b, xc(c), 0)),
            pl.BlockSpec((1, BLOCK, ZB_W), lambda b, c: (0, 0, 0)),
            pl.BlockSpec((1, BLOCK, GATE_W), lambda b, c: (0, 0, 0)),
            pl.BlockSpec((CONV_K, 2 * ML_WIDTH), lambda b, c: (0, 0)),
            pl.BlockSpec((1, 2 * ML_WIDTH), lambda b, c: (0, 0)),
            pl.BlockSpec((1, GATE_W), lambda b, c: (0, 0)),
            pl.BlockSpec((1, ML_WIDTH), lambda b, c: (0, 0)),
        ],
        out_specs=pl.BlockSpec((nbps, BLOCK, ML_WIDTH), lambda b, c: (b, xc(c), 0)),
        out_shape=jax.ShapeDtypeStruct((B, S, ML_WIDTH), BF16),
        scratch_shapes=[
            pltpu.VMEM((nbps * ML_HEADS, ML_DH, ML_DH), F32),
            pltpu.VMEM((nbps * 8, ML_DH), F32),
            pltpu.VMEM((nbps * 8, LANES), F32),
            pltpu.VMEM((nbps * 8, 2 * ML_WIDTH), F32),
            pltpu.VMEM((nbps * (8 + BLOCK), 2 * ML_WIDTH), F32),
        ],
        compiler_params=_cparams(2),
        name="mlstm",
    )(zb, zb, zb, gates, zbm, gatesm, conv_w, conv_b, gate_b, ng)


def _outproj_kernel(x_ref, ya_ref, ym_ref, ga_ref, gmm_ref, eg_ref, eb_ref, wa_ref, wm_ref, wo_ref,
                    g1_ref, b1_ref, wr_ref, br_ref,
                    h1_ref, eid_ref, gate_ref, rank_ref, cnt_ref, run_sc, *, tm):
    i = pl.program_id(0)

    @pl.when(i == 0)
    def _():
        run_sc[...] = jnp.zeros_like(run_sc)

    h0 = _ln(x_ref[...], eg_ref[...], eb_ref[...])
    pa = jnp.dot(ya_ref[...], wa_ref[...], preferred_element_type=F32)
    pm = jnp.dot(ym_ref[...], wm_ref[...], preferred_element_type=F32)
    sig = lambda g: 0.5 * jnp.tanh(0.5 * g) + 0.5
    merged = sig(ga_ref[...]) * pa.astype(BF16) + sig(gmm_ref[...]) * pm.astype(BF16)
    mix = jnp.dot(merged, wo_ref[...], preferred_element_type=F32)
    h1 = _ln(DN_ALPHA * h0 + mix, g1_ref[...], b1_ref[...])
    _store_row_tiles(h1_ref, h1)

    h_hi = h1.astype(BF16)
    h_mid = (h1 - h_hi.astype(F32)).astype(BF16)
    hh = jnp.dot(h_hi, wr_ref[...], preferred_element_type=F32)
    logits = (hh[:, :LANES] + hh[:, LANES:]
              + jnp.dot(h_mid, wr_ref[:, :LANES], preferred_element_type=F32) + br_ref[...])
    lane = lax.broadcasted_iota(jnp.int32, (tm, LANES), 1)
    work = logits
    sel_e, sel_v = [], []
    for _ in range(TOP_K):
        mv = jnp.max(work, axis=1, keepdims=True)
        e = jnp.min(jnp.where(work == mv, lane, LANES), axis=1, keepdims=True)
        sel_e.append(e)
        sel_v.append(mv)
        work = jnp.where(lane == e, -jnp.inf, work)
    ex = [jnp.exp(v - sel_v[0]) for v in sel_v]
    den = ex[0] + ex[1] + ex[2] + ex[3]

    onehots = [lane == e for e in sel_e]
    oh = jnp.zeros((tm, LANES), F32)
    for o in onehots:
        oh = oh + o.astype(F32)
    r_i = lax.broadcasted_iota(jnp.int32, (tm, tm), 0)
    c_i = lax.broadcasted_iota(jnp.int32, (tm, tm), 1)
    strict = (c_i < r_i).astype(BF16)
    before = jnp.dot(strict, oh.astype(BF16), preferred_element_type=F32) + run_sc[...]
    eid = jnp.zeros((tm, LANES), jnp.int32)
    gate = jnp.zeros((tm, LANES), F32)
    rank = jnp.zeros((tm, LANES), F32)
    for kk in range(TOP_K):
        here = lane == kk
        eid = jnp.where(here, sel_e[kk], eid)
        gate = jnp.where(here, ex[kk] / den, gate)
        rk = jnp.sum(jnp.where(onehots[kk], before, 0.0), axis=1, keepdims=True)
        rank = jnp.where(here, rk, rank)
    eid_ref[...] = eid
    gate_ref[...] = gate
    rank_ref[...] = rank.astype(jnp.int32)
    run_sc[...] = run_sc[...] + jnp.sum(oh, axis=0, keepdims=True)
    cnt_ref[...] = run_sc[...]


def _outproj(x2, yatt, yml, zb2, eg, eb, wa, wm, wo, g1, b1, wr, br, *, tm):
    N, D = x2.shape
    vec = lambda w: pl.BlockSpec((1, w), lambda i: (0, 0))
    full = lambda a, b: pl.BlockSpec((a, b), lambda i: (0, 0))
    tile = lambda w: pl.BlockSpec((tm, w), lambda i: (i, 0))
    return pl.pallas_call(
        functools.partial(_outproj_kernel, tm=tm),
        grid=(N // tm,),
        in_specs=[
            tile(D), tile(ATT_HEADS * ATT_V_DIM), tile(ML_WIDTH),
            pl.BlockSpec((tm, D), lambda i: (i, C_GA // D)),
            pl.BlockSpec((tm, D), lambda i: (i, C_GM // D)),
            vec(D), vec(D),
            full(ATT_HEADS * ATT_V_DIM, D), full(ML_WIDTH, D), full(D, D),
            vec(D), vec(D), full(D, 2 * LANES), vec(LANES),
        ],
        out_specs=[pl.BlockSpec((tm * ROW_TILE, LANES), lambda i: (i, 0)),
                   tile(LANES), tile(LANES), tile(LANES), vec(LANES)],
        out_shape=[
            jax.ShapeDtypeStruct((N * ROW_TILE, LANES), F32),
            jax.ShapeDtypeStruct((N, LANES), jnp.int32),
            jax.ShapeDtypeStruct((N, LANES), F32),
            jax.ShapeDtypeStruct((N, LANES), jnp.int32),
            jax.ShapeDtypeStruct((1, LANES), F32),
        ],
        scratch_shapes=[pltpu.VMEM((1, LANES), F32)],
        compiler_params=_cparams(1),
        name="outproj_router",
    )(x2, yatt, yml, zb2, zb2, eg, eb, wa, wm, wo, g1, b1, wr, br)


def _row_copies(src_row, dst_row, sem, n, issue):
    def body(r, carry):
        for kk in range(TOP_K):
            cp = pltpu.make_async_copy(src_row(r, kk), dst_row(r, kk), sem)
            if issue:
                cp.start(priority=kk % 2)
            else:
                cp.wait()
        return carry
    lax.fori_loop(0, n, body, 0)


def _dispatch_kernel(pad_pos_ref, pad_len_ref, d_ref, h_ref, xs_ref, zero_sc, sem, *, tm):
    @pl.when(pl.program_id(0) == 0)
    def _():
        zero_sc[...] = jnp.zeros_like(zero_sc)

        def pad_copies(e, issue):
            pos = pad_pos_ref[e]

            def per_row(r, c):
                cp = pltpu.make_async_copy(zero_sc, _row_tile(xs_ref, pos + r), sem.at[1 + (e & 1)])
                if issue:
                    cp.start()
                else:
                    cp.wait()
                return c
            lax.fori_loop(0, pad_len_ref[e], per_row, 0)

        def per_expert(e, carry):
            @pl.when(e + 1 < N_EXPERTS)
            def _():
                pad_copies(e + 1, True)
            pad_copies(e, False)
            return carry

        pad_copies(0, True)
        lax.fori_loop(0, N_EXPERTS, per_expert, 0)

    src = lambda r, kk: _row_tile(h_ref, r)
    dst = lambda r, kk: _row_tile(xs_ref, d_ref[0, 0, r * TOP_K + kk])
    _row_copies(src, dst, sem.at[0], tm, True)
    _row_copies(src, dst, sem.at[0], tm, False)


def _dispatch(pad_pos, pad_len, dest3, h1t, rows_out, *, tm):
    N = h1t.shape[0] // ROW_TILE
    return pl.pallas_call(
        functools.partial(_dispatch_kernel, tm=tm),
        grid_spec=pltpu.PrefetchScalarGridSpec(
            num_scalar_prefetch=2,
            grid=(N // tm,),
            in_specs=[
                pl.BlockSpec((1, 1, tm * TOP_K), lambda i, pp, pn: (i, 0, 0), memory_space=pltpu.SMEM),
                pl.BlockSpec((tm * ROW_TILE, LANES), lambda i, pp, pn: (i, 0)),
            ],
            out_specs=pl.BlockSpec(memory_space=pl.ANY),
            scratch_shapes=[pltpu.VMEM((ROW_TILE, LANES), F32), pltpu.SemaphoreType.DMA((3,))],
        ),
        out_shape=jax.ShapeDtypeStruct((rows_out * ROW_TILE, LANES), F32),
        compiler_params=_cparams(1),
        name="moe_dispatch",
    )(pad_pos, pad_len, dest3, h1t)


def _deinterleave_kernel(w_ref, p_ref, wg_ref, wl_ref):
    for c in range(w_ref.shape[2] // (2 * LANES)):
        blk = w_ref[0, :, c * 2 * LANES:(c + 1) * 2 * LANES].astype(BF16)
        r = jnp.dot(blk, p_ref[...], preferred_element_type=F32).astype(BF16)
        wg_ref[0, :, c * LANES:(c + 1) * LANES] = r[:, :LANES]
        wl_ref[0, :, c * LANES:(c + 1) * LANES] = r[:, LANES:]


def _deinterleave(w_gu):
    E, D, F2 = w_gu.shape
    F = F2 // 2
    j = np.arange(2 * LANES)
    sel = np.zeros((2 * LANES, 2 * LANES), np.float32)
    sel[j, (j % 2) * LANES + j // 2] = 1.0
    out = jax.ShapeDtypeStruct((E, D, F), BF16)
    return pl.pallas_call(
        _deinterleave_kernel,
        grid=(E,),
        in_specs=[
            pl.BlockSpec((1, D, F2), lambda e: (e, 0, 0)),
            pl.BlockSpec((2 * LANES, 2 * LANES), lambda e: (0, 0)),
        ],
        out_specs=[pl.BlockSpec((1, D, F), lambda e: (e, 0, 0))] * 2,
        out_shape=[out, out],
        compiler_params=_cparams(1),
        name="deinterleave_wgu",
    )(w_gu, jnp.asarray(sel, BF16))


def _expert_kernel(be_ref, nu_ref, x_ref, wg_ref, wl_ref, bg_ref, bl_ref, wd_ref, bd_ref, y_ref, *, tm):
    del be_ref

    @pl.when(pl.program_id(0) < nu_ref[0])
    def _():
        xb = _load_row_tiles(x_ref, tm).astype(BF16)
        hg = jnp.dot(xb, wg_ref[0], preferred_element_type=F32) + bg_ref[0]
        hl = jnp.dot(xb, wl_ref[0], preferred_element_type=F32) + bl_ref[0]
        glu = jnp.minimum(hg, SWIGLU_LIMIT)
        lin = jnp.clip(hl, -SWIGLU_LIMIT, SWIGLU_LIMIT)
        act = glu * _sigmoid(SWIGLU_ALPHA * glu) * (lin + 1.0)
        _store_row_tiles(y_ref, jnp.dot(act.astype(BF16), wd_ref[0], preferred_element_type=F32) + bd_ref[0])


def _experts(blk_exp, n_used, xs, wg, wl, bg, bl, wd, bd, *, tm):
    MP = xs.shape[0] // ROW_TILE
    _, D, F = wg.shape
    rows = pl.BlockSpec((tm * ROW_TILE, LANES), lambda i, be, nu: (jnp.minimum(i, nu[0] - 1), 0))
    wsp = lambda a, b: pl.BlockSpec((1, a, b), lambda i, be, nu: (be[i], 0, 0))
    return pl.pallas_call(
        functools.partial(_expert_kernel, tm=tm),
        grid_spec=pltpu.PrefetchScalarGridSpec(
            num_scalar_prefetch=2,
            grid=(MP // tm,),
            in_specs=[rows, wsp(D, F), wsp(D, F), wsp(1, F), wsp(1, F), wsp(F, D), wsp(1, D)],
            out_specs=rows,
        ),
        out_shape=jax.ShapeDtypeStruct(xs.shape, F32),
        compiler_params=_cparams(1),
        name="moe_experts",
    )(blk_exp, n_used, xs, wg, wl, bg, bl, wd, bd)


def _combine_kernel(dcur_ref, dnxt_ref, h1_ref, gate_ref, g2_ref, b2_ref, ys_ref, o_ref, ybuf, sem, *, tm, nsteps):
    i = pl.program_id(0)
    slot = i % 2

    def copies(d_ref, s, issue):
        src = lambda r, kk: _row_tile(ys_ref, d_ref[0, 0, r * TOP_K + kk])
        dst = lambda r, kk: _row_tile(ybuf.at[s, kk], r)
        _row_copies(src, dst, sem.at[s], tm, issue)

    @pl.when(i == 0)
    def _():
        copies(dcur_ref, 0, True)

    @pl.when(i + 1 < nsteps)
    def _():
        copies(dnxt_ref, 1 - slot, True)

    copies(dcur_ref, slot, False)
    acc = DN_ALPHA * _load_row_tiles(h1_ref, tm)
    for kk in range(TOP_K):
        acc = acc + gate_ref[:, kk:kk + 1] * _load_row_tiles(ybuf.at[slot, kk], tm)
    o_ref[...] = _ln(acc, g2_ref[...], b2_ref[...])


def _combine(dest3, h1t, gate, g2, b2, ys, *, tm):
    N, D = h1t.shape[0] // ROW_TILE, D_MODEL
    nsteps = N // tm
    dspec = lambda f: pl.BlockSpec((1, 1, tm * TOP_K), f, memory_space=pltpu.SMEM)
    return pl.pallas_call(
        functools.partial(_combine_kernel, tm=tm, nsteps=nsteps),
        grid=(nsteps,),
        in_specs=[
            dspec(lambda i: (i, 0, 0)),
            dspec(lambda i: (jnp.minimum(i + 1, nsteps - 1), 0, 0)),
            pl.BlockSpec((tm * ROW_TILE, LANES), lambda i: (i, 0)),
            pl.BlockSpec((tm, LANES), lambda i: (i, 0)),
            pl.BlockSpec((1, D), lambda i: (0, 0)),
            pl.BlockSpec((1, D), lambda i: (0, 0)),
            pl.BlockSpec(memory_space=pl.ANY),
        ],
        out_specs=pl.BlockSpec((tm, D), lambda i: (i, 0)),
        out_shape=jax.ShapeDtypeStruct((N, D), F32),
        scratch_shapes=[pltpu.VMEM((2, TOP_K, tm * ROW_TILE, LANES), F32), pltpu.SemaphoreType.DMA((2,))],
        compiler_params=_cparams(1),
        name="moe_combine",
    )(dest3, dest3, h1t, gate, g2, b2, ys)


def _rotary_column_order():
    idx = np.empty((ATT_HEADS * LANES,), np.int32)
    for h in range(ATT_HEADS):
        for half in range(2):
            for sub in range(2):
                for dd in range(32):
                    idx[h * LANES + half * 64 + sub * 32 + dd] = (2 * h + sub) * ATT_QK_DIM + half * 32 + dd
    return idx


def _rotary_tables(pos):
    half = ATT_QK_DIM // 2
    inv_freq = ROPE_THETA ** (-jnp.arange(half, dtype=F32) / half)
    ang = pos.astype(F32)[:, None] * inv_freq[None, :]
    cos = jnp.tile(jnp.cos(ang), (1, 4))
    sin = jnp.tile(jnp.sin(ang), (1, 4))
    sign = jnp.where(jnp.arange(LANES) < 64, -1.0, 1.0).astype(F32)
    return cos, sin * sign[None, :]


def _pick(n, prefs):
    for t in prefs:
        if n % t == 0:
            return t
    raise ValueError(f"unsupported size {n}")


def kernel(x, meta, emb_ln_g, emb_ln_b, w_in, conv_w, conv_b, gate_bias, lam_q1, lam_k1, lam_q2, lam_k2,
           att_norm_g, ml_norm_g, w_att_out, w_ml_out, w_o, ln1_g, ln1_b, w_router, b_router,
           w_gu, b_gu, w_down, b_down, ln2_g, ln2_b):
    B, S, D = x.shape
    assert D == D_MODEL and S % 512 == 0 and w_in.shape[0] == DEPTH
    N = B * S
    row2 = lambda a: a.reshape(1, -1)

    w = w_in[0]
    o_aq, o_ak, o_av, o_mqk, o_mv, o_mo, o_gt, o_ga, o_gm = 0, 512, 1024, 1536, 2560, 3072, 3584, 3592, 4616
    perm = _rotary_column_order()
    gpad = jnp.zeros((D, LANES - ML_HEADS), F32)
    w_all = jnp.concatenate([
        w[:, o_ga:o_ga + D], w[:, o_gm:o_gm + D], w[:, o_mqk:o_mqk + 2 * ML_WIDTH],
        w[:, o_aq:o_aq + 512][:, perm], w[:, o_ak:o_ak + 512][:, perm], w[:, o_av:o_av + 512],
        w[:, o_mv:o_mv + ML_WIDTH], w[:, o_mo:o_mo + ML_WIDTH],
        w[:, o_gt:o_gt + ML_HEADS], gpad, w[:, o_gt + ML_HEADS:o_gt + 2 * ML_HEADS], gpad,
    ], axis=1).astype(BF16)
    gb = gate_bias[0]
    zpad = jnp.zeros((LANES - ML_HEADS,), F32)
    gate_b = jnp.concatenate([gb[:ML_HEADS], zpad, gb[ML_HEADS:], zpad]).reshape(1, GATE_W)

    cos_x, sin_x = _rotary_tables(N_META + jnp.arange(S))
    cos_m, sin_m = _rotary_tables(jnp.maximum(jnp.arange(BLOCK) - PAD, 0))

    eg, eb = row2(emb_ln_g), row2(emb_ln_b)

    tm_in = 512
    zb, gates = _inproj(x, eg, eb, w_all, cos_x, sin_x, tm=tm_in)
    xm = jnp.concatenate([jnp.zeros((PAD, D), x.dtype), meta.astype(x.dtype)], axis=0)[None]
    zbm, gatesm = _inproj(xm, eg, eb, w_all, cos_m, sin_m, tm=BLOCK, first_valid_row=PAD)

    tq = 512
    yatt = _attention(zb, zbm, row2(lam_q1[0]), row2(lam_k1[0]), row2(lam_q2[0]), row2(lam_k2[0]),
                      row2(att_norm_g[0]), tq=tq)
    yml = _mlstm(zb, gates, zbm, gatesm, conv_w[0], row2(conv_b[0]), gate_b, row2(ml_norm_g[0]),
                 nbps=2 if B % 2 == 0 else 1)

    wr32 = jnp.concatenate([w_router[0], jnp.zeros((D, LANES - N_EXPERTS), F32)], axis=1)
    wr_hi = wr32.astype(BF16)
    wr = jnp.concatenate([wr_hi, (wr32 - wr_hi.astype(F32)).astype(BF16)], axis=1)
    br = jnp.concatenate([b_router[0], jnp.full((LANES - N_EXPERTS,), NEG, F32)]).reshape(1, LANES)
    tm_out = 512
    h1, eid, gate, rank, cnt = _outproj(
        x.reshape(N, D), yatt.reshape(N, -1), yml.reshape(N, -1), zb.reshape(N, ZB_W), eg, eb,
        w_att_out[0].astype(BF16), w_ml_out[0].astype(BF16), w_o[0].astype(BF16),
        row2(ln1_g[0]), row2(ln1_b[0]), wr, br, tm=tm_out)

    tm_e = 512
    M = N * TOP_K
    nb = (M + N_EXPERTS * (tm_e - 1) + tm_e - 1) // tm_e
    counts = cnt[0, :N_EXPERTS].astype(jnp.int32)
    nblk = (counts + tm_e - 1) // tm_e
    cum = jnp.cumsum(nblk)
    pstart = (cum - nblk) * tm_e
    dest = pstart[eid[:, :TOP_K]] + rank[:, :TOP_K]
    n_used = cum[-1:].astype(jnp.int32)
    blk = jnp.minimum(jnp.arange(nb, dtype=jnp.int32), n_used[0] - 1)
    blk_exp = jnp.sum(blk[:, None] >= cum[None, :], axis=1).astype(jnp.int32)

    tm_d = 256
    dest3 = dest.reshape(N // tm_d, 1, tm_d * TOP_K)
    xs = _dispatch(pstart + counts, nblk * tm_e - counts, dest3, h1, nb * tm_e, tm=tm_d)

    wg, wl = _deinterleave(w_gu[0])
    bgu = b_gu[0]
    ys = _experts(blk_exp, n_used, xs, wg, wl,
                  bgu[:, None, 0::2], bgu[:, None, 1::2],
                  w_down[0].astype(BF16), b_down[0][:, None, :], tm=tm_e)

    out = _combine(dest3, h1, gate, row2(ln2_g[0]), row2(ln2_b[0]), ys, tm=tm_d)
    return out.reshape(B, S, D)
```

```python
import functools
import math

import jax
import jax.numpy as jnp
import numpy as np
from jax import lax
from jax.experimental import pallas as pl
from jax.experimental.pallas import tpu as pltpu

D_MODEL = 1024
N_META = 16
BLOCK = 128
PAD = BLOCK - N_META
NEG = -1e30
LN_EPS = 1e-5
ATT_HEADS = 4
ATT_QK_DIM = 64
ATT_V_DIM = 128
ROPE_THETA = 10000.0
ML_HEADS = 4
ML_DH = 128
ML_WIDTH = ML_HEADS * ML_DH
CONV_K = 4
N_EXPERTS = 32
TOP_K = 4
SWIGLU_LIMIT = 7.0
SWIGLU_ALPHA = 1.702
DEPTH = 1
DN_ALPHA = (2 * DEPTH) ** 0.25
LAMBDA_INIT = 0.8 - 0.6 * math.exp(-0.3 * 0)

LANES = 128
C_GA, C_GM, C_MQK, C_AQ, C_AK, C_AV, C_MV, C_MO = 0, 1024, 2048, 3072, 3584, 4096, 4608, 5120
ZB_W = 5632
GATE_W = 2 * LANES
W_ALL = ZB_W + GATE_W
CHUNK = 512

VMEM_LIMIT = 56 * 1024 * 1024

BF16 = jnp.bfloat16
F32 = jnp.float32


def _cparams(n_axes):
    return pltpu.CompilerParams(dimension_semantics=("arbitrary",) * n_axes, vmem_limit_bytes=VMEM_LIMIT)


def _ln(x, g, b):
    mu = jnp.mean(x, axis=-1, keepdims=True)
    xc = x - mu
    var = jnp.mean(xc * xc, axis=-1, keepdims=True)
    return xc * lax.rsqrt(var + LN_EPS) * g + b


def _sigmoid(x):
    return 1.0 / (1.0 + jnp.exp(-x))


ROW_TILE = D_MODEL // LANES


def _store_row_tiles(ref, val):
    tm = val.shape[0]
    for g in range(ROW_TILE):
        ref[pl.ds(g, tm, stride=ROW_TILE), :] = val[:, g * LANES:(g + 1) * LANES]


def _load_row_tiles(ref, tm):
    return jnp.concatenate([ref[pl.ds(g, tm, stride=ROW_TILE), :] for g in range(ROW_TILE)], axis=1)


def _row_tile(ref, r):
    return ref.at[pl.ds(pl.multiple_of(r * ROW_TILE, ROW_TILE), ROW_TILE)]


def _inproj_kernel(x_ref, g_ref, b_ref, w_ref, cos_ref, sin_ref, zb_ref, gt_ref, *, first_valid_row):
    x = x_ref[0]
    tm = x.shape[0]
    hb = _ln(x, g_ref[...], b_ref[...]).astype(BF16)
    cos = cos_ref[...]
    sin = sin_ref[...]
    if first_valid_row:
        rowmask = lax.broadcasted_iota(jnp.int32, (tm, 1), 0) >= first_valid_row
    for c in range(ZB_W // CHUNK):
        z = jnp.dot(hb, w_ref[:, c * CHUNK:(c + 1) * CHUNK], preferred_element_type=F32)
        if c * CHUNK in (C_AQ, C_AK):
            parts = []
            for h in range(ATT_HEADS):
                zh = z[:, h * LANES:(h + 1) * LANES]
                parts.append(zh * cos + pltpu.roll(zh, 64, 1) * sin)
            z = jnp.concatenate(parts, axis=1)
            if c * CHUNK == C_AQ:
                z = z * (ATT_QK_DIM ** -0.5 * math.log2(math.e))
        if first_valid_row:
            z = jnp.where(rowmask, z, 0.0)
        zb_ref[0, :, c * CHUNK:(c + 1) * CHUNK] = z.astype(BF16)
    zg = jnp.dot(hb, w_ref[:, ZB_W:W_ALL], preferred_element_type=F32)
    if first_valid_row:
        zg = jnp.where(rowmask, zg, 0.0)
    gt_ref[0] = zg


def _inproj(x3, g, b, w, cos, sin, *, tm, first_valid_row=0):
    B, S, D = x3.shape
    nt = S // tm
    return pl.pallas_call(
        functools.partial(_inproj_kernel, first_valid_row=first_valid_row),
        grid=(B, nt),
        in_specs=[
            pl.BlockSpec((1, tm, D), lambda bi, i: (bi, i, 0)),
            pl.BlockSpec((1, D), lambda bi, i: (0, 0)),
            pl.BlockSpec((1, D), lambda bi, i: (0, 0)),
            pl.BlockSpec((D, W_ALL), lambda bi, i: (0, 0)),
            pl.BlockSpec((tm, LANES), lambda bi, i: (i, 0)),
            pl.BlockSpec((tm, LANES), lambda bi, i: (i, 0)),
        ],
        out_specs=[
            pl.BlockSpec((1, tm, ZB_W), lambda bi, i: (bi, i, 0)),
            pl.BlockSpec((1, tm, GATE_W), lambda bi, i: (bi, i, 0)),
        ],
        out_shape=[
            jax.ShapeDtypeStruct((B, S, ZB_W), BF16),
            jax.ShapeDtypeStruct((B, S, GATE_W), F32),
        ],
        compiler_params=_cparams(2),
        name="inproj",
    )(x3, g, b, w, cos, sin)


def _attn_kernel(q_ref, k_ref, v_ref, km_ref, vm_ref, lq1_ref, lk1_ref, lq2_ref, lk2_ref, ng_ref, o_ref,
                 sa_sc, sb_sc, pa_sc, pb_sc, aa_sc, ab_sc, m_sc, acc_sc, *, tq, tk, rc):
    assert tq == 2 * tk
    qi = pl.program_id(2)
    rows = 2 * tq
    q = q_ref[0]
    lane = lax.broadcasted_iota(jnp.int32, (tq, LANES), 1)
    is_map0 = (lane & 63) < 32
    zero = jnp.zeros_like(q)
    qq = jnp.concatenate([jnp.where(is_map0, q, zero), jnp.where(is_map0, zero, q)], axis=0)

    def qk_stage(kblk, s_ref):
        s_ref[:, 0:kblk.shape[0]] = lax.dot_general(qq, kblk, (((1,), (1,)), ((), ())),
                                                   preferred_element_type=F32)

    def pv_stage(p_ref, a_ref, vblk, first=False):
        width = vblk.shape[0]
        v1 = jnp.concatenate([vblk, jnp.ones_like(vblk)], axis=1)
        pv = jnp.dot(p_ref[:, 0:width], v1, preferred_element_type=F32)
        if first:
            acc_sc[...] = pv
        else:
            a = a_ref[...]
            acc_sc[...] = jnp.concatenate([a, a], axis=1) * acc_sc[...] + pv

    def sm_stage(s_ref, p_ref, a_ref, width, vis, first=False):
        nrep = width // LANES
        for r0 in range(0, rows, rc):
            kind = vis(r0)
            if kind == "none":
                continue
            s = s_ref[r0:r0 + rc, 0:width]
            if kind != "all":
                s = kind(s)
            m_cur = jnp.max(s, axis=1, keepdims=True)
            if first:
                m_new = jnp.broadcast_to(m_cur, (rc, LANES))
            else:
                m_prev = m_sc[r0:r0 + rc, :]
                m_new = jnp.maximum(m_prev, m_cur)
                a_ref[r0:r0 + rc, :] = jnp.exp2(m_prev - m_new)
            m_sc[r0:r0 + rc, :] = m_new
        for r0 in range(0, rows, rc):
            kind = vis(r0)
            if kind == "none":
                p_ref[r0:r0 + rc, 0:width] = jnp.zeros((rc, width), BF16)
                a_ref[r0:r0 + rc, :] = jnp.ones((rc, LANES), F32)
                continue
            s = s_ref[r0:r0 + rc, 0:width]
            if kind != "all":
                s = kind(s)
            m_new = m_sc[r0:r0 + rc, :]
            m_rep = m_new if nrep == 1 else jnp.concatenate([m_new] * nrep, axis=1)
            p_ref[r0:r0 + rc, 0:width] = jnp.exp2(s - m_rep).astype(BF16)

    def meta_vis(r0):
        def mask(s):
            col = lax.broadcasted_iota(jnp.int32, s.shape, 1)
            return jnp.where(col >= PAD, s, NEG)
        return mask

    def diag_vis(d):
        def vis(r0):
            q0 = r0 % tq
            if d * tk + tk - 1 <= q0:
                return "all"
            if d * tk > q0 + rc - 1:
                return "none"
            def mask(s):
                rowq = q0 + lax.broadcasted_iota(jnp.int32, s.shape, 0)
                col = d * tk + lax.broadcasted_iota(jnp.int32, s.shape, 1)
                return jnp.where(col <= rowq, s, NEG)
            return mask
        return vis

    all_vis = lambda r0: "all"
    kblk = lambda start: k_ref[0, pl.ds(start, tk), :]
    vblk = lambda start: v_ref[0, pl.ds(start, tk), :]

    qk_stage(km_ref[0], sa_sc)
    sm_stage(sa_sc, pa_sc, None, BLOCK, meta_vis, first=True)
    pv_stage(pa_sc, None, vm_ref[0], first=True)

    d0 = pl.multiple_of(qi * tq, tk)
    d1 = pl.multiple_of(qi * tq + tk, tk)
    qk_stage(kblk(d0), sa_sc)
    qk_stage(kblk(d1), sb_sc)
    sm_stage(sa_sc, pa_sc, aa_sc, tk, diag_vis(0))
    pv_stage(pa_sc, aa_sc, vblk(d0))
    sm_stage(sb_sc, pb_sc, ab_sc, tk, diag_vis(1))
    qk_stage(kblk(0), sa_sc)

    def pair(i, carry):
        a_start = pl.multiple_of(i * tq, tk)
        b_start = pl.multiple_of(i * tq + tk, tk)
        prev_b = pl.multiple_of(jnp.where(i == 0, d1, a_start - tk), tk)
        next_a = pl.multiple_of(jnp.minimum(a_start + tq, (qi - 1) * tq), tk)
        pv_stage(pb_sc, ab_sc, vblk(prev_b))
        sm_stage(sa_sc, pa_sc, aa_sc, tk, all_vis)
        qk_stage(kblk(b_start), sb_sc)
        pv_stage(pa_sc, aa_sc, vblk(a_start))
        sm_stage(sb_sc, pb_sc, ab_sc, tk, all_vis)
        qk_stage(kblk(next_a), sa_sc)
        return carry

    lax.fori_loop(0, qi, pair, 0)
    last_b = pl.multiple_of(jnp.where(qi == 0, d1, qi * tq - tk), tk)
    pv_stage(pb_sc, ab_sc, vblk(last_b))

    o = acc_sc[:, 0:ATT_V_DIM] / acc_sc[:, ATT_V_DIM:]
    s1 = jnp.sum(lq1_ref[...] * lk1_ref[...], axis=1, keepdims=True)
    s2 = jnp.sum(lq2_ref[...] * lk2_ref[...], axis=1, keepdims=True)
    lam = jnp.exp(s1) - jnp.exp(s2) + LAMBDA_INIT
    a = o[:tq] - lam * o[tq:]
    y = a * lax.rsqrt(jnp.mean(a * a, axis=1, keepdims=True) + LN_EPS) * ng_ref[...]
    o_ref[0] = (y * (1.0 - LAMBDA_INIT)).astype(BF16)


def _attention(zb, zbm, lq1, lk1, lq2, lk2, ng, *, tq):
    B, S, _ = zb.shape
    nq = S // tq
    tk = tq // 2
    qb, kb, vb = C_AQ // LANES, C_AK // LANES, C_AV // LANES
    lam_spec = pl.BlockSpec((1, ATT_QK_DIM), lambda b, h, i: (0, 0))
    return pl.pallas_call(
        functools.partial(_attn_kernel, tq=tq, tk=tk, rc=32),
        grid=(B, ATT_HEADS, nq),
        in_specs=[
            pl.BlockSpec((1, tq, LANES), lambda b, h, i: (b, i, qb + h)),
            pl.BlockSpec((1, S, LANES), lambda b, h, i: (b, 0, kb + h)),
            pl.BlockSpec((1, S, LANES), lambda b, h, i: (b, 0, vb + h)),
            pl.BlockSpec((1, BLOCK, LANES), lambda b, h, i: (0, 0, kb + h)),
            pl.BlockSpec((1, BLOCK, LANES), lambda b, h, i: (0, 0, vb + h)),
            lam_spec, lam_spec, lam_spec, lam_spec,
            pl.BlockSpec((1, ATT_V_DIM), lambda b, h, i: (0, 0)),
        ],
        out_specs=pl.BlockSpec((1, tq, LANES), lambda b, h, i: (b, i, h)),
        out_shape=jax.ShapeDtypeStruct((B, S, ATT_HEADS * ATT_V_DIM), BF16),
        scratch_shapes=[
            pltpu.VMEM((2 * tq, tk), F32), pltpu.VMEM((2 * tq, tk), F32),
            pltpu.VMEM((2 * tq, tk), BF16), pltpu.VMEM((2 * tq, tk), BF16),
            pltpu.VMEM((2 * tq, LANES), F32), pltpu.VMEM((2 * tq, LANES), F32),
            pltpu.VMEM((2 * tq, LANES), F32),
            pltpu.VMEM((2 * tq, 2 * ATT_V_DIM), F32),
        ],
        compiler_params=_cparams(3),
        name="diff_attention",
    )(zb, zb, zb, zbm, zbm, lq1, lk1, lq2, lk2, ng)


def _mlstm_chunk(xqk_ref, xv_ref, xo_ref, xg_ref, zm_ref, gm_ref, cw_ref, cb_ref, gb_ref, ng_ref, o_ref,
                 ct_sc, n_sc, m_sc, tail_sc, cbuf_sc):
    c = pl.program_id(1)
    L = BLOCK

    @pl.when(c == 0)
    def _():
        ct_sc[...] = jnp.zeros_like(ct_sc)
        n_sc[...] = jnp.zeros_like(n_sc)
        m_sc[...] = jnp.zeros_like(m_sc)
        tail_sc[...] = jnp.zeros_like(tail_sc)

    is_meta = c == 0
    row = lax.broadcasted_iota(jnp.int32, (L, 1), 0)
    valid = jnp.logical_or(c > 0, row >= PAD)

    qk_pre = jnp.where(is_meta, zm_ref[0, :, C_MQK:C_MQK + 2 * ML_WIDTH], xqk_ref[...]).astype(F32)
    v = jnp.where(is_meta, zm_ref[0, :, C_MV:C_MV + ML_WIDTH], xv_ref[...])
    gates = jnp.where(is_meta, gm_ref[0], xg_ref[...])

    cbuf_sc[0:8, :] = tail_sc[...]
    cbuf_sc[8:8 + L, :] = qk_pre
    tail_sc[...] = qk_pre[L - 8:L, :]
    acc = jnp.broadcast_to(cb_ref[...], (L, 2 * ML_WIDTH))
    for j in range(CONV_K):
        off = 8 - (CONV_K - 1) + j
        acc = acc + cw_ref[j:j + 1, :] * cbuf_sc[off:off + L, :]
    qk = acc * _sigmoid(acc)
    qk = jnp.where(valid, qk, 0.0)
    q = qk[:, :ML_WIDTH]
    k = qk[:, ML_WIDTH:] * (ML_DH ** -0.5)

    ig = gates[:, :LANES] + gb_ref[:, :LANES]
    fg = gates[:, LANES:] + gb_ref[:, LANES:]
    lf = jnp.minimum(fg, 0.0) - jnp.log1p(jnp.exp(-jnp.abs(fg)))
    ig = jnp.where(valid, ig, NEG)
    lf = jnp.where(valid, lf, 0.0)

    sidx = lax.broadcasted_iota(jnp.int32, (L, L), 0)
    ridx = lax.broadcasted_iota(jnp.int32, (L, L), 1)
    causal = ridx <= sidx
    tri = causal.astype(BF16)
    lf_hi = lf.astype(BF16)
    lf_lo = (lf - lf_hi.astype(F32)).astype(BF16)
    bcs = (jnp.dot(tri, lf_hi, preferred_element_type=F32)
           + jnp.dot(tri, lf_lo, preferred_element_type=F32))
    b_t = bcs.T
    ig_t = ig.T

    outs = []
    for h in range(ML_HEADS):
        sl = slice(h * ML_DH, (h + 1) * ML_DH)
        bc = bcs[:, h:h + 1]
        br = b_t[h:h + 1, :]
        igr = ig_t[h:h + 1, :]
        igc = ig[:, h:h + 1]
        m_h = m_sc[h:h + 1, 0:1]
        dlog = jnp.where(causal, (bc - br) + igr, NEG)
        inter = bc + m_h
        m_s = jnp.maximum(inter, jnp.max(dlog, axis=1, keepdims=True))
        w_intra = jnp.exp(dlog - m_s)
        w_inter = jnp.exp(inter - m_s)
        qh = q[:, sl]
        kh = k[:, sl]
        vh = v[:, sl]
        qb = qh.astype(BF16)
        kb = kh.astype(BF16)
        s = lax.dot_general(qb, kb, (((1,), (1,)), ((), ())), preferred_element_type=F32) * w_intra
        ct = ct_sc[h]
        nrow = n_sc[h:h + 1, :]
        num = (w_inter * jnp.dot(qb, ct.astype(BF16), preferred_element_type=F32)
               + jnp.dot(s.astype(BF16), vh, preferred_element_type=F32))
        den = w_inter * jnp.sum(qh * nrow, axis=1, keepdims=True) + jnp.sum(s, axis=1, keepdims=True)
        hh = num / jnp.maximum(jnp.abs(den), jnp.exp(-m_s))
        b_last = bc[L - 1:L, :]
        upd = (b_last - bc) + igc
        m_new = jnp.maximum(b_last + m_h, jnp.max(upd, axis=0, keepdims=True))
        w_old = jnp.exp(b_last + m_h - m_new)
        w_r = jnp.exp(upd - m_new)
        vw = (vh.astype(F32) * w_r).astype(BF16)
        ct_sc[h] = w_old * ct + lax.dot_general(kb, vw, (((0,), (0,)), ((), ())), preferred_element_type=F32)
        n_sc[h:h + 1, :] = w_old * nrow + jnp.sum(kh * w_r, axis=0, keepdims=True)
        m_sc[h:h + 1, :] = jnp.broadcast_to(m_new, (1, LANES))
        hn = hh * lax.rsqrt(jnp.mean(hh * hh, axis=1, keepdims=True) + LN_EPS) * ng_ref[:, sl]
        outs.append(_sigmoid(xo_ref[:, sl].astype(F32)) * hn)

    @pl.when(c > 0)
    def _():
        o_ref[...] = jnp.concatenate(outs, axis=1).astype(BF16)


def _mlstm_kernel(xqk_ref, xv_ref, xo_ref, xg_ref, zm_ref, gm_ref, cw_ref, cb_ref, gb_ref, ng_ref, o_ref,
                  ct_sc, n_sc, m_sc, tail_sc, cbuf_sc, *, nbps):
    for bb in range(nbps):
        rows8 = pl.ds(bb * 8, 8)
        _mlstm_chunk(xqk_ref.at[bb], xv_ref.at[bb], xo_ref.at[bb], xg_ref.at[bb], zm_ref, gm_ref,
                     cw_ref, cb_ref, gb_ref, ng_ref, o_ref.at[bb],
                     ct_sc.at[pl.ds(bb * ML_HEADS, ML_HEADS)], n_sc.at[rows8], m_sc.at[rows8], tail_sc.at[rows8],
                     cbuf_sc.at[pl.ds(bb * (8 + BLOCK), 8 + BLOCK)])


def _mlstm(zb, gates, zbm, gatesm, conv_w, conv_b, gate_b, ng, *, nbps):
    B, S, _ = zb.shape
    nc = S // BLOCK + 1

    def xc(c):
        return jnp.maximum(c - 1, 0)

    return pl.pallas_call(
        functools.partial(_mlstm_kernel, nbps=nbps),
        grid=(B // nbps, nc),
        in_specs=[
            pl.BlockSpec((nbps, BLOCK, 2 * ML_WIDTH), lambda b, c: (b, xc(c), C_MQK // (2 * ML_WIDTH))),
            pl.BlockSpec((nbps, BLOCK, ML_WIDTH), lambda b, c: (b, xc(c), C_MV // ML_WIDTH)),
            pl.BlockSpec((nbps, BLOCK, ML_WIDTH), lambda b, c: (b, xc(c), C_MO // ML_WIDTH)),
            pl.BlockSpec((nbps, BLOCK, GATE_W), lambda b, c: (b, xc(c), 0)),
            pl.BlockSpec((1, BLOCK, ZB_W), lambda b, c: (0, 0, 0)),
            pl.BlockSpec((1, BLOCK, GATE_W), lambda b, c: (0, 0, 0)),
            pl.BlockSpec((CONV_K, 2 * ML_WIDTH), lambda b, c: (0, 0)),
            pl.BlockSpec((1, 2 * ML_WIDTH), lambda b, c: (0, 0)),
            pl.BlockSpec((1, GATE_W), lambda b, c: (0, 0)),
            pl.BlockSpec((1, ML_WIDTH), lambda b, c: (0, 0)),
        ],
        out_specs=pl.BlockSpec((nbps, BLOCK, ML_WIDTH), lambda b, c: (b, xc(c), 0)),
        out_shape=jax.ShapeDtypeStruct((B, S, ML_WIDTH), BF16),
        scratch_shapes=[
            pltpu.VMEM((nbps * ML_HEADS, ML_DH, ML_DH), F32),
            pltpu.VMEM((nbps * 8, ML_DH), F32),
            pltpu.VMEM((nbps * 8, LANES), F32),
            pltpu.VMEM((nbps * 8, 2 * ML_WIDTH), F32),
            pltpu.VMEM((nbps * (8 + BLOCK), 2 * ML_WIDTH), F32),
        ],
        compiler_params=_cparams(2),
        name="mlstm",
    )(zb, zb, zb, gates, zbm, gatesm, conv_w, conv_b, gate_b, ng)


def _outproj_kernel(x_ref, ya_ref, ym_ref, ga_ref, gmm_ref, eg_ref, eb_ref, wa_ref, wm_ref, wo_ref,
                    g1_ref, b1_ref, wr_ref, br_ref,
                    h1_ref, eid_ref, gate_ref, rank_ref, cnt_ref, run_sc, *, tm):
    i = pl.program_id(0)

    @pl.when(i == 0)
    def _():
        run_sc[...] = jnp.zeros_like(run_sc)

    h0 = _ln(x_ref[...], eg_ref[...], eb_ref[...])
    pa = jnp.dot(ya_ref[...], wa_ref[...], preferred_element_type=F32)
    pm = jnp.dot(ym_ref[...], wm_ref[...], preferred_element_type=F32)
    sig = lambda g: 0.5 * jnp.tanh(0.5 * g) + 0.5
    merged = sig(ga_ref[...]) * pa.astype(BF16) + sig(gmm_ref[...]) * pm.astype(BF16)
    mix = jnp.dot(merged, wo_ref[...], preferred_element_type=F32)
    h1 = _ln(DN_ALPHA * h0 + mix, g1_ref[...], b1_ref[...])
    _store_row_tiles(h1_ref, h1)

    h_hi = h1.astype(BF16)
    h_mid = (h1 - h_hi.astype(F32)).astype(BF16)
    hh = jnp.dot(h_hi, wr_ref[...], preferred_element_type=F32)
    logits = (hh[:, :LANES] + hh[:, LANES:]
              + jnp.dot(h_mid, wr_ref[:, :LANES], preferred_element_type=F32) + br_ref[...])
    lane = lax.broadcasted_iota(jnp.int32, (tm, LANES), 1)
    work = logits
    sel_e, sel_v = [], []
    for _ in range(TOP_K):
        mv = jnp.max(work, axis=1, keepdims=True)
        e = jnp.min(jnp.where(work == mv, lane, LANES), axis=1, keepdims=True)
        sel_e.append(e)
        sel_v.append(mv)
        work = jnp.where(lane == e, -jnp.inf, work)
    ex = [jnp.exp(v - sel_v[0]) for v in sel_v]
    den = ex[0] + ex[1] + ex[2] + ex[3]

    onehots = [lane == e for e in sel_e]
    oh = jnp.zeros((tm, LANES), F32)
    for o in onehots:
        oh = oh + o.astype(F32)
    r_i = lax.broadcasted_iota(jnp.int32, (tm, tm), 0)
    c_i = lax.broadcasted_iota(jnp.int32, (tm, tm), 1)
    strict = (c_i < r_i).astype(BF16)
    before = jnp.dot(strict, oh.astype(BF16), preferred_element_type=F32) + run_sc[...]
    eid = jnp.zeros((tm, LANES), F32)
    gate = jnp.zeros((tm, LANES), F32)
    rank = jnp.zeros((tm, LANES), F32)
    for kk in range(TOP_K):
        here = lane == kk
        eid = jnp.where(here, sel_e[kk].astype(F32), eid)
        gate = jnp.where(here, ex[kk] / den, gate)
        rk = jnp.sum(jnp.where(onehots[kk], before, 0.0), axis=1, keepdims=True)
        rank = jnp.where(here, rk, rank)
    gate_ref[...] = gate
    eid_ref[...] = eid.T[0:8, :]
    rank_ref[...] = rank.T[0:8, :]
    run_sc[...] = run_sc[...] + jnp.sum(oh, axis=0, keepdims=True)
    cnt_ref[...] = run_sc[...]


def _outproj(x2, yatt, yml, zb2, eg, eb, wa, wm, wo, g1, b1, wr, br, *, tm):
    N, D = x2.shape
    vec = lambda w: pl.BlockSpec((1, w), lambda i: (0, 0))
    full = lambda a, b: pl.BlockSpec((a, b), lambda i: (0, 0))
    tile = lambda w: pl.BlockSpec((tm, w), lambda i: (i, 0))
    return pl.pallas_call(
        functools.partial(_outproj_kernel, tm=tm),
        grid=(N // tm,),
        in_specs=[
            tile(D), tile(ATT_HEADS * ATT_V_DIM), tile(ML_WIDTH),
            pl.BlockSpec((tm, D), lambda i: (i, C_GA // D)),
            pl.BlockSpec((tm, D), lambda i: (i, C_GM // D)),
            vec(D), vec(D),
            full(ATT_HEADS * ATT_V_DIM, D), full(ML_WIDTH, D), full(D, D),
            vec(D), vec(D), full(D, 2 * LANES), vec(LANES),
        ],
        out_specs=[pl.BlockSpec((tm * ROW_TILE, LANES), lambda i: (i, 0)),
                   pl.BlockSpec((8, tm), lambda i: (0, i)), tile(LANES),
                   pl.BlockSpec((8, tm), lambda i: (0, i)), vec(LANES)],
        out_shape=[
            jax.ShapeDtypeStruct((N * ROW_TILE, LANES), F32),
            jax.ShapeDtypeStruct((8, N), F32),
            jax.ShapeDtypeStruct((N, LANES), F32),
            jax.ShapeDtypeStruct((8, N), F32),
            jax.ShapeDtypeStruct((1, LANES), F32),
        ],
        scratch_shapes=[pltpu.VMEM((1, LANES), F32)],
        compiler_params=_cparams(1),
        name="outproj_router",
    )(x2, yatt, yml, zb2, zb2, eg, eb, wa, wm, wo, g1, b1, wr, br)


def _row_copies(src_row, dst_row, sem, n, issue):
    def body(r, carry):
        for kk in range(TOP_K):
            cp = pltpu.make_async_copy(src_row(r, kk), dst_row(r, kk), sem)
            if issue:
                cp.start(priority=kk % 2)
            else:
                cp.wait()
        return carry
    lax.fori_loop(0, n, body, 0, unroll=8)


def _dispatch_kernel(pad_pos_ref, pad_len_ref, d_ref, h_ref, xs_ref, zero_sc, sem, *, tm):
    @pl.when(pl.program_id(0) == 0)
    def _():
        zero_sc[...] = jnp.zeros_like(zero_sc)

        def pad_copies(e, issue):
            pos = pad_pos_ref[e]

            def per_row(r, c):
                cp = pltpu.make_async_copy(zero_sc, _row_tile(xs_ref, pos + r), sem.at[1 + (e & 1)])
                if issue:
                    cp.start()
                else:
                    cp.wait()
                return c
            lax.fori_loop(0, pad_len_ref[e], per_row, 0)

        def per_expert(e, carry):
            @pl.when(e + 1 < N_EXPERTS)
            def _():
                pad_copies(e + 1, True)
            pad_copies(e, False)
            return carry

        pad_copies(0, True)
        lax.fori_loop(0, N_EXPERTS, per_expert, 0)

    src = lambda r, kk: _row_tile(h_ref, r)
    dst = lambda r, kk: _row_tile(xs_ref, d_ref[0, 0, kk * tm + r])
    _row_copies(src, dst, sem.at[0], tm, True)
    _row_copies(src, dst, sem.at[0], tm, False)


def _dispatch(pad_pos, pad_len, dest3, h1t, rows_out, *, tm):
    N = h1t.shape[0] // ROW_TILE
    return pl.pallas_call(
        functools.partial(_dispatch_kernel, tm=tm),
        grid_spec=pltpu.PrefetchScalarGridSpec(
            num_scalar_prefetch=2,
            grid=(N // tm,),
            in_specs=[
                pl.BlockSpec((1, 1, tm * TOP_K), lambda i, pp, pn: (i, 0, 0), memory_space=pltpu.SMEM),
                pl.BlockSpec((tm * ROW_TILE, LANES), lambda i, pp, pn: (i, 0)),
            ],
            out_specs=pl.BlockSpec(memory_space=pl.ANY),
            scratch_shapes=[pltpu.VMEM((ROW_TILE, LANES), F32), pltpu.SemaphoreType.DMA((3,))],
        ),
        out_shape=jax.ShapeDtypeStruct((rows_out * ROW_TILE, LANES), F32),
        compiler_params=_cparams(1),
        name="moe_dispatch",
    )(pad_pos, pad_len, dest3, h1t)


def _deinterleave_kernel(w_ref, p_ref, wg_ref, wl_ref):
    for c in range(w_ref.shape[2] // (2 * LANES)):
        blk = w_ref[0, :, c * 2 * LANES:(c + 1) * 2 * LANES].astype(BF16)
        r = jnp.dot(blk, p_ref[...], preferred_element_type=F32).astype(BF16)
        wg_ref[0, :, c * LANES:(c + 1) * LANES] = r[:, :LANES]
        wl_ref[0, :, c * LANES:(c + 1) * LANES] = r[:, LANES:]


def _deinterleave(w_gu):
    E, D, F2 = w_gu.shape
    F = F2 // 2
    j = np.arange(2 * LANES)
    sel = np.zeros((2 * LANES, 2 * LANES), np.float32)
    sel[j, (j % 2) * LANES + j // 2] = 1.0
    out = jax.ShapeDtypeStruct((E, D, F), BF16)
    return pl.pallas_call(
        _deinterleave_kernel,
        grid=(E,),
        in_specs=[
            pl.BlockSpec((1, D, F2), lambda e: (e, 0, 0)),
            pl.BlockSpec((2 * LANES, 2 * LANES), lambda e: (0, 0)),
        ],
        out_specs=[pl.BlockSpec((1, D, F), lambda e: (e, 0, 0))] * 2,
        out_shape=[out, out],
        compiler_params=_cparams(1),
        name="deinterleave_wgu",
    )(w_gu, jnp.asarray(sel, BF16))


def _expert_kernel(be_ref, nu_ref, x_ref, wg_ref, wl_ref, bg_ref, bl_ref, wd_ref, bd_ref, y_ref, *, tm):
    del be_ref

    @pl.when(pl.program_id(0) < nu_ref[0])
    def _():
        xb = _load_row_tiles(x_ref, tm).astype(BF16)
        hg = jnp.dot(xb, wg_ref[0], preferred_element_type=F32) + bg_ref[0]
        hl = jnp.dot(xb, wl_ref[0], preferred_element_type=F32) + bl_ref[0]
        glu = jnp.minimum(hg, SWIGLU_LIMIT)
        lin = jnp.clip(hl, -SWIGLU_LIMIT, SWIGLU_LIMIT)
        act = glu * _sigmoid(SWIGLU_ALPHA * glu) * (lin + 1.0)
        _store_row_tiles(y_ref, jnp.dot(act.astype(BF16), wd_ref[0], preferred_element_type=F32) + bd_ref[0])


def _experts(blk_exp, n_used, xs, wg, wl, bg, bl, wd, bd, *, tm):
    MP = xs.shape[0] // ROW_TILE
    _, D, F = wg.shape
    rows = pl.BlockSpec((tm * ROW_TILE, LANES), lambda i, be, nu: (jnp.minimum(i, nu[0] - 1), 0))
    wsp = lambda a, b: pl.BlockSpec((1, a, b), lambda i, be, nu: (be[i], 0, 0))
    return pl.pallas_call(
        functools.partial(_expert_kernel, tm=tm),
        grid_spec=pltpu.PrefetchScalarGridSpec(
            num_scalar_prefetch=2,
            grid=(MP // tm,),
            in_specs=[rows, wsp(D, F), wsp(D, F), wsp(1, F), wsp(1, F), wsp(F, D), wsp(1, D)],
            out_specs=rows,
        ),
        out_shape=jax.ShapeDtypeStruct(xs.shape, F32),
        compiler_params=_cparams(1),
        name="moe_experts",
    )(blk_exp, n_used, xs, wg, wl, bg, bl, wd, bd)


def _combine_kernel(dcur_ref, dnxt_ref, h1_ref, gate_ref, g2_ref, b2_ref, ys_ref, o_ref, ybuf, sem, *, tm, nsteps):
    i = pl.program_id(0)
    slot = i % 2

    def copies(d_ref, s, issue):
        src = lambda r, kk: _row_tile(ys_ref, d_ref[0, 0, kk * tm + r])
        dst = lambda r, kk: _row_tile(ybuf.at[s, kk], r)
        _row_copies(src, dst, sem.at[s], tm, issue)

    @pl.when(i == 0)
    def _():
        copies(dcur_ref, 0, True)

    @pl.when(i + 1 < nsteps)
    def _():
        copies(dnxt_ref, 1 - slot, True)

    copies(dcur_ref, slot, False)
    acc = DN_ALPHA * _load_row_tiles(h1_ref, tm)
    for kk in range(TOP_K):
        acc = acc + gate_ref[:, kk:kk + 1] * _load_row_tiles(ybuf.at[slot, kk], tm)
    o_ref[...] = _ln(acc, g2_ref[...], b2_ref[...])


def _combine(dest3, h1t, gate, g2, b2, ys, *, tm):
    N, D = h1t.shape[0] // ROW_TILE, D_MODEL
    nsteps = N // tm
    dspec = lambda f: pl.BlockSpec((1, 1, tm * TOP_K), f, memory_space=pltpu.SMEM)
    return pl.pallas_call(
        functools.partial(_combine_kernel, tm=tm, nsteps=nsteps),
        grid=(nsteps,),
        in_specs=[
            dspec(lambda i: (i, 0, 0)),
            dspec(lambda i: (jnp.minimum(i + 1, nsteps - 1), 0, 0)),
            pl.BlockSpec((tm * ROW_TILE, LANES), lambda i: (i, 0)),
            pl.BlockSpec((tm, LANES), lambda i: (i, 0)),
            pl.BlockSpec((1, D), lambda i: (0, 0)),
            pl.BlockSpec((1, D), lambda i: (0, 0)),
            pl.BlockSpec(memory_space=pl.ANY),
        ],
        out_specs=pl.BlockSpec((tm, D), lambda i: (i, 0)),
        out_shape=jax.ShapeDtypeStruct((N, D), F32),
        scratch_shapes=[pltpu.VMEM((2, TOP_K, tm * ROW_TILE, LANES), F32), pltpu.SemaphoreType.DMA((2,))],
        compiler_params=_cparams(1),
        name="moe_combine",
    )(dest3, dest3, h1t, gate, g2, b2, ys)


def _rotary_column_order():
    idx = np.empty((ATT_HEADS * LANES,), np.int32)
    for h in range(ATT_HEADS):
        for half in range(2):
            for sub in range(2):
                for dd in range(32):
                    idx[h * LANES + half * 64 + sub * 32 + dd] = (2 * h + sub) * ATT_QK_DIM + half * 32 + dd
    return idx


def _rotary_tables(pos):
    half = ATT_QK_DIM // 2
    inv_freq = ROPE_THETA ** (-jnp.arange(half, dtype=F32) / half)
    ang = pos.astype(F32)[:, None] * inv_freq[None, :]
    cos = jnp.tile(jnp.cos(ang), (1, 4))
    sin = jnp.tile(jnp.sin(ang), (1, 4))
    sign = jnp.where(jnp.arange(LANES) < 64, -1.0, 1.0).astype(F32)
    return cos, sin * sign[None, :]


def _pick(n, prefs):
    for t in prefs:
        if n % t == 0:
            return t
    raise ValueError(f"unsupported size {n}")


def kernel(x, meta, emb_ln_g, emb_ln_b, w_in, conv_w, conv_b, gate_bias, lam_q1, lam_k1, lam_q2, lam_k2,
           att_norm_g, ml_norm_g, w_att_out, w_ml_out, w_o, ln1_g, ln1_b, w_router, b_router,
           w_gu, b_gu, w_down, b_down, ln2_g, ln2_b):
    B, S, D = x.shape
    assert D == D_MODEL and S % 512 == 0 and w_in.shape[0] == DEPTH
    N = B * S
    row2 = lambda a: a.reshape(1, -1)

    w = w_in[0]
    o_aq, o_ak, o_av, o_mqk, o_mv, o_mo, o_gt, o_ga, o_gm = 0, 512, 1024, 1536, 2560, 3072, 3584, 3592, 4616
    perm = _rotary_column_order()
    gpad = jnp.zeros((D, LANES - ML_HEADS), F32)
    w_all = jnp.concatenate([
        w[:, o_ga:o_ga + D], w[:, o_gm:o_gm + D], w[:, o_mqk:o_mqk + 2 * ML_WIDTH],
        w[:, o_aq:o_aq + 512][:, perm], w[:, o_ak:o_ak + 512][:, perm], w[:, o_av:o_av + 512],
        w[:, o_mv:o_mv + ML_WIDTH], w[:, o_mo:o_mo + ML_WIDTH],
        w[:, o_gt:o_gt + ML_HEADS], gpad, w[:, o_gt + ML_HEADS:o_gt + 2 * ML_HEADS], gpad,
    ], axis=1).astype(BF16)
    gb = gate_bias[0]
    zpad = jnp.zeros((LANES - ML_HEADS,), F32)
    gate_b = jnp.concatenate([gb[:ML_HEADS], zpad, gb[ML_HEADS:], zpad]).reshape(1, GATE_W)

    cos_x, sin_x = _rotary_tables(N_META + jnp.arange(S))
    cos_m, sin_m = _rotary_tables(jnp.maximum(jnp.arange(BLOCK) - PAD, 0))

    eg, eb = row2(emb_ln_g), row2(emb_ln_b)

    tm_in = 512
    zb, gates = _inproj(x, eg, eb, w_all, cos_x, sin_x, tm=tm_in)
    xm = jnp.concatenate([jnp.zeros((PAD, D), x.dtype), meta.astype(x.dtype)], axis=0)[None]
    zbm, gatesm = _inproj(xm, eg, eb, w_all, cos_m, sin_m, tm=BLOCK, first_valid_row=PAD)

    tq = 512
    yatt = _attention(zb, zbm, row2(lam_q1[0]), row2(lam_k1[0]), row2(lam_q2[0]), row2(lam_k2[0]),
                      row2(att_norm_g[0]), tq=tq)
    yml = _mlstm(zb, gates, zbm, gatesm, conv_w[0], row2(conv_b[0]), gate_b, row2(ml_norm_g[0]),
                 nbps=2 if B % 2 == 0 else 1)

    wr32 = jnp.concatenate([w_router[0], jnp.zeros((D, LANES - N_EXPERTS), F32)], axis=1)
    wr_hi = wr32.astype(BF16)
    wr = jnp.concatenate([wr_hi, (wr32 - wr_hi.astype(F32)).astype(BF16)], axis=1)
    br = jnp.concatenate([b_router[0], jnp.full((LANES - N_EXPERTS,), NEG, F32)]).reshape(1, LANES)
    tm_out = 512
    h1, eid, gate, rank, cnt = _outproj(
        x.reshape(N, D), yatt.reshape(N, -1), yml.reshape(N, -1), zb.reshape(N, ZB_W), eg, eb,
        w_att_out[0].astype(BF16), w_ml_out[0].astype(BF16), w_o[0].astype(BF16),
        row2(ln1_g[0]), row2(ln1_b[0]), wr, br, tm=tm_out)

    tm_e = 512
    M = N * TOP_K
    nb = (M + N_EXPERTS * (tm_e - 1) + tm_e - 1) // tm_e
    counts = cnt[0, :N_EXPERTS].astype(jnp.int32)
    nblk = (counts + tm_e - 1) // tm_e
    cum = jnp.cumsum(nblk)
    pstart = (cum - nblk) * tm_e
    eid_t = eid[:TOP_K].astype(jnp.int32)
    base = jnp.sum(jnp.where(eid_t[None] == jnp.arange(N_EXPERTS)[:, None, None], pstart[:, None, None], 0), axis=0)
    dest = base + rank[:TOP_K].astype(jnp.int32)
    n_used = cum[-1:].astype(jnp.int32)
    blk = jnp.minimum(jnp.arange(nb, dtype=jnp.int32), n_used[0] - 1)
    blk_exp = jnp.sum(blk[:, None] >= cum[None, :], axis=1).astype(jnp.int32)

    tm_d = 256
    dest3 = dest.reshape(TOP_K, N // tm_d, tm_d).transpose(1, 0, 2).reshape(N // tm_d, 1, TOP_K * tm_d)
    xs = _dispatch(pstart + counts, nblk * tm_e - counts, dest3, h1, nb * tm_e, tm=tm_d)

    wg, wl = _deinterleave(w_gu[0])
    bgu = b_gu[0]
    ys = _experts(blk_exp, n_used, xs, wg, wl,
                  bgu[:, None, 0::2], bgu[:, None, 1::2],
                  w_down[0].astype(BF16), b_down[0][:, None, :], tm=tm_e)

    out = _combine(dest3, h1, gate, row2(ln2_g[0]), row2(ln2_b[0]), ys, tm=tm_d)
    return out.reshape(B, S, D)
```

```python
import functools
import math

import jax
import jax.numpy as jnp
import numpy as np
from jax import lax
from jax.experimental import pallas as pl
from jax.experimental.pallas import tpu as pltpu

D_MODEL = 1024
N_META = 16
BLOCK = 128
PAD = BLOCK - N_META
NEG = -1e30
LN_EPS = 1e-5
ATT_HEADS = 4
ATT_QK_DIM = 64
ATT_V_DIM = 128
ROPE_THETA = 10000.0
ML_HEADS = 4
ML_DH = 128
ML_WIDTH = ML_HEADS * ML_DH
CONV_K = 4
N_EXPERTS = 32
TOP_K = 4
SWIGLU_LIMIT = 7.0
SWIGLU_ALPHA = 1.702
DEPTH = 1
DN_ALPHA = (2 * DEPTH) ** 0.25
LAMBDA_INIT = 0.8 - 0.6 * math.exp(-0.3 * 0)

LANES = 128
C_GA, C_GM, C_MQK, C_AQ, C_AK, C_AV, C_MV, C_MO = 0, 1024, 2048, 3072, 3584, 4096, 4608, 5120
ZB_W = 5632
GATE_W = 2 * LANES
W_ALL = ZB_W + GATE_W
CHUNK = 512

VMEM_LIMIT = 56 * 1024 * 1024

BF16 = jnp.bfloat16
F32 = jnp.float32


def _cparams(n_axes):
    return pltpu.CompilerParams(dimension_semantics=("arbitrary",) * n_axes, vmem_limit_bytes=VMEM_LIMIT)


def _ln(x, g, b):
    mu = jnp.mean(x, axis=-1, keepdims=True)
    xc = x - mu
    var = jnp.mean(xc * xc, axis=-1, keepdims=True)
    return xc * lax.rsqrt(var + LN_EPS) * g + b


def _sigmoid(x):
    return 1.0 / (1.0 + jnp.exp(-x))


ROW_TILE = D_MODEL // LANES


def _store_row_tiles(ref, val):
    tm = val.shape[0]
    for g in range(ROW_TILE):
        ref[pl.ds(g, tm, stride=ROW_TILE), :] = val[:, g * LANES:(g + 1) * LANES]


def _load_row_tiles(ref, tm):
    return jnp.concatenate([ref[pl.ds(g, tm, stride=ROW_TILE), :] for g in range(ROW_TILE)], axis=1)


def _row_tile(ref, r):
    return ref.at[pl.ds(pl.multiple_of(r * ROW_TILE, ROW_TILE), ROW_TILE)]


def _inproj_kernel(x_ref, g_ref, b_ref, w_ref, cos_ref, sin_ref, zb_ref, gt_ref, *, first_valid_row):
    x = x_ref[0]
    tm = x.shape[0]
    hb = _ln(x, g_ref[...], b_ref[...]).astype(BF16)
    cos = cos_ref[...]
    sin = sin_ref[...]
    if first_valid_row:
        rowmask = lax.broadcasted_iota(jnp.int32, (tm, 1), 0) >= first_valid_row
    for c in range(ZB_W // CHUNK):
        z = jnp.dot(hb, w_ref[:, c * CHUNK:(c + 1) * CHUNK], preferred_element_type=F32)
        if c * CHUNK in (C_AQ, C_AK):
            parts = []
            for h in range(ATT_HEADS):
                zh = z[:, h * LANES:(h + 1) * LANES]
                parts.append(zh * cos + pltpu.roll(zh, 64, 1) * sin)
            z = jnp.concatenate(parts, axis=1)
            if c * CHUNK == C_AQ:
                z = z * (ATT_QK_DIM ** -0.5 * math.log2(math.e))
        if first_valid_row:
            z = jnp.where(rowmask, z, 0.0)
        zb_ref[0, :, c * CHUNK:(c + 1) * CHUNK] = z.astype(BF16)
    zg = jnp.dot(hb, w_ref[:, ZB_W:W_ALL], preferred_element_type=F32)
    if first_valid_row:
        zg = jnp.where(rowmask, zg, 0.0)
    gt_ref[0] = zg


def _inproj(x3, g, b, w, cos, sin, *, tm, first_valid_row=0):
    B, S, D = x3.shape
    nt = S // tm
    return pl.pallas_call(
        functools.partial(_inproj_kernel, first_valid_row=first_valid_row),
        grid=(B, nt),
        in_specs=[
            pl.BlockSpec((1, tm, D), lambda bi, i: (bi, i, 0)),
            pl.BlockSpec((1, D), lambda bi, i: (0, 0)),
            pl.BlockSpec((1, D), lambda bi, i: (0, 0)),
            pl.BlockSpec((D, W_ALL), lambda bi, i: (0, 0)),
            pl.BlockSpec((tm, LANES), lambda bi, i: (i, 0)),
            pl.BlockSpec((tm, LANES), lambda bi, i: (i, 0)),
        ],
        out_specs=[
            pl.BlockSpec((1, tm, ZB_W), lambda bi, i: (bi, i, 0)),
            pl.BlockSpec((1, tm, GATE_W), lambda bi, i: (bi, i, 0)),
        ],
        out_shape=[
            jax.ShapeDtypeStruct((B, S, ZB_W), BF16),
            jax.ShapeDtypeStruct((B, S, GATE_W), F32),
        ],
        compiler_params=_cparams(2),
        name="inproj",
    )(x3, g, b, w, cos, sin)


def _attn_kernel(q_ref, k_ref, v_ref, km_ref, vm_ref, lq1_ref, lk1_ref, lq2_ref, lk2_ref, ng_ref, o_ref,
                 sa_sc, sb_sc, pa_sc, pb_sc, aa_sc, ab_sc, m_sc, acc_sc, *, tq, tk, rc):
    assert tq == 2 * tk
    qi = pl.program_id(2)
    rows = 2 * tq
    q = q_ref[0]
    lane = lax.broadcasted_iota(jnp.int32, (tq, LANES), 1)
    is_map0 = (lane & 63) < 32
    zero = jnp.zeros_like(q)
    qq = jnp.concatenate([jnp.where(is_map0, q, zero), jnp.where(is_map0, zero, q)], axis=0)

    def qk_stage(kblk, s_ref, col0=0):
        s_ref[:, col0:col0 + kblk.shape[0]] = lax.dot_general(qq, kblk, (((1,), (1,)), ((), ())),
                                                              preferred_element_type=F32)

    def pv_stage(p_ref, a_ref, vblk, first=False):
        width = vblk.shape[0]
        v1 = jnp.concatenate([vblk, jnp.ones_like(vblk)], axis=1)
        pv = jnp.dot(p_ref[:, 0:width], v1, preferred_element_type=F32)
        if first:
            acc_sc[...] = pv
        else:
            a = a_ref[...]
            acc_sc[...] = jnp.concatenate([a, a], axis=1) * acc_sc[...] + pv

    def sm_stage(s_ref, p_ref, a_ref, width, vis, first=False):
        nrep = width // LANES
        for r0 in range(0, rows, rc):
            kind = vis(r0)
            if kind == "none":
                continue
            s = s_ref[r0:r0 + rc, 0:width]
            if kind != "all":
                s = kind(s)
            m_cur = jnp.max(s, axis=1, keepdims=True)
            if first:
                m_new = jnp.broadcast_to(m_cur, (rc, LANES))
            else:
                m_prev = m_sc[r0:r0 + rc, :]
                m_new = jnp.maximum(m_prev, m_cur)
                a_ref[r0:r0 + rc, :] = jnp.exp2(m_prev - m_new)
            m_sc[r0:r0 + rc, :] = m_new
        for r0 in range(0, rows, rc):
            kind = vis(r0)
            if kind == "none":
                p_ref[r0:r0 + rc, 0:width] = jnp.zeros((rc, width), BF16)
                a_ref[r0:r0 + rc, :] = jnp.ones((rc, LANES), F32)
                continue
            s = s_ref[r0:r0 + rc, 0:width]
            if kind != "all":
                s = kind(s)
            m_new = m_sc[r0:r0 + rc, :]
            m_rep = m_new if nrep == 1 else jnp.concatenate([m_new] * nrep, axis=1)
            p_ref[r0:r0 + rc, 0:width] = jnp.exp2(s - m_rep).astype(BF16)

    def first_vis(r0):
        q0 = r0 % tq
        def mask(s):
            rowq = q0 + lax.broadcasted_iota(jnp.int32, s.shape, 0)
            col = lax.broadcasted_iota(jnp.int32, s.shape, 1)
            keep = col >= tk + PAD
            if tk - 1 <= q0:
                keep = jnp.logical_or(keep, col < tk)
            else:
                keep = jnp.logical_or(keep, col <= rowq)
            return jnp.where(keep, s, NEG)
        return mask

    def diag_vis(d):
        def vis(r0):
            q0 = r0 % tq
            if d * tk + tk - 1 <= q0:
                return "all"
            if d * tk > q0 + rc - 1:
                return "none"
            def mask(s):
                rowq = q0 + lax.broadcasted_iota(jnp.int32, s.shape, 0)
                col = d * tk + lax.broadcasted_iota(jnp.int32, s.shape, 1)
                return jnp.where(col <= rowq, s, NEG)
            return mask
        return vis

    all_vis = lambda r0: "all"
    kblk = lambda start: k_ref[0, pl.ds(start, tk), :]
    vblk = lambda start: v_ref[0, pl.ds(start, tk), :]

    d0 = pl.multiple_of(qi * tq, tk)
    d1 = pl.multiple_of(qi * tq + tk, tk)
    qk_stage(kblk(d0), sa_sc)
    qk_stage(km_ref[0], sa_sc, col0=tk)
    qk_stage(kblk(d1), sb_sc)
    sm_stage(sa_sc, pa_sc, None, tk + BLOCK, first_vis, first=True)
    pv_stage(pa_sc, None, jnp.concatenate([vblk(d0), vm_ref[0]], axis=0), first=True)
    sm_stage(sb_sc, pb_sc, ab_sc, tk, diag_vis(1))
    qk_stage(kblk(0), sa_sc)

    def pair(i, carry):
        a_start = pl.multiple_of(i * tq, tk)
        b_start = pl.multiple_of(i * tq + tk, tk)
        prev_b = pl.multiple_of(jnp.where(i == 0, d1, a_start - tk), tk)
        next_a = pl.multiple_of(jnp.minimum(a_start + tq, (qi - 1) * tq), tk)
        qk_stage(kblk(b_start), sb_sc)
        pv_stage(pb_sc, ab_sc, vblk(prev_b))
        sm_stage(sa_sc, pa_sc, aa_sc, tk, all_vis)
        qk_stage(kblk(next_a), sa_sc)
        pv_stage(pa_sc, aa_sc, vblk(a_start))
        sm_stage(sb_sc, pb_sc, ab_sc, tk, all_vis)
        return carry

    lax.fori_loop(0, qi, pair, 0)
    last_b = pl.multiple_of(jnp.where(qi == 0, d1, qi * tq - tk), tk)
    pv_stage(pb_sc, ab_sc, vblk(last_b))

    o = acc_sc[:, 0:ATT_V_DIM] / acc_sc[:, ATT_V_DIM:]
    s1 = jnp.sum(lq1_ref[...] * lk1_ref[...], axis=1, keepdims=True)
    s2 = jnp.sum(lq2_ref[...] * lk2_ref[...], axis=1, keepdims=True)
    lam = jnp.exp(s1) - jnp.exp(s2) + LAMBDA_INIT
    a = o[:tq] - lam * o[tq:]
    y = a * lax.rsqrt(jnp.mean(a * a, axis=1, keepdims=True) + LN_EPS) * ng_ref[...]
    o_ref[0] = (y * (1.0 - LAMBDA_INIT)).astype(BF16)


def _attention(zb, zbm, lq1, lk1, lq2, lk2, ng, *, tq):
    B, S, _ = zb.shape
    nq = S // tq
    tk = tq // 2
    qb, kb, vb = C_AQ // LANES, C_AK // LANES, C_AV // LANES
    lam_spec = pl.BlockSpec((1, ATT_QK_DIM), lambda b, h, i: (0, 0))
    return pl.pallas_call(
        functools.partial(_attn_kernel, tq=tq, tk=tk, rc=32),
        grid=(B, ATT_HEADS, nq),
        in_specs=[
            pl.BlockSpec((1, tq, LANES), lambda b, h, i: (b, i, qb + h)),
            pl.BlockSpec((1, S, LANES), lambda b, h, i: (b, 0, kb + h)),
            pl.BlockSpec((1, S, LANES), lambda b, h, i: (b, 0, vb + h)),
            pl.BlockSpec((1, BLOCK, LANES), lambda b, h, i: (0, 0, kb + h)),
            pl.BlockSpec((1, BLOCK, LANES), lambda b, h, i: (0, 0, vb + h)),
            lam_spec, lam_spec, lam_spec, lam_spec,
            pl.BlockSpec((1, ATT_V_DIM), lambda b, h, i: (0, 0)),
        ],
        out_specs=pl.BlockSpec((1, tq, LANES), lambda b, h, i: (b, i, h)),
        out_shape=jax.ShapeDtypeStruct((B, S, ATT_HEADS * ATT_V_DIM), BF16),
        scratch_shapes=[
            pltpu.VMEM((2 * tq, tk + BLOCK), F32), pltpu.VMEM((2 * tq, tk), F32),
            pltpu.VMEM((2 * tq, tk + BLOCK), BF16), pltpu.VMEM((2 * tq, tk), BF16),
            pltpu.VMEM((2 * tq, LANES), F32), pltpu.VMEM((2 * tq, LANES), F32),
            pltpu.VMEM((2 * tq, LANES), F32),
            pltpu.VMEM((2 * tq, 2 * ATT_V_DIM), F32),
        ],
        compiler_params=_cparams(3),
        name="diff_attention",
    )(zb, zb, zb, zbm, zbm, lq1, lk1, lq2, lk2, ng)


def _mlstm_kernel(xqk_ref, xv_ref, xo_ref, xg_ref, zm_ref, gm_ref, cw_ref, cb_ref, gb_ref, ng_ref, sh_ref, o_ref,
                  ct_sc, n_sc, m_sc, prev_sc, *, nbps):
    c = pl.program_id(1)
    L = BLOCK

    @pl.when(c == 0)
    def _():
        ct_sc[...] = jnp.zeros_like(ct_sc)
        n_sc[...] = jnp.zeros_like(n_sc)
        m_sc[...] = jnp.zeros_like(m_sc)
        prev_sc[...] = jnp.zeros_like(prev_sc)

    is_meta = c == 0
    row = lax.broadcasted_iota(jnp.int32, (L, 1), 0)
    valid = jnp.logical_or(c > 0, row >= PAD)
    sidx = lax.broadcasted_iota(jnp.int32, (L, L), 0)
    ridx = lax.broadcasted_iota(jnp.int32, (L, L), 1)
    causal = ridx <= sidx
    tri = causal.astype(BF16)
    elems = range(nbps)
    pairs = [(bb, h) for bb in elems for h in range(ML_HEADS)]
    sl = lambda h: slice(h * ML_DH, (h + 1) * ML_DH)
    st = lambda bb, h: bb * 8 + h

    q, k, v, ig, bcs, b_t, ig_t = {}, {}, {}, {}, {}, {}, {}
    for bb in elems:
        qk_pre = jnp.where(is_meta, zm_ref[0, :, C_MQK:C_MQK + 2 * ML_WIDTH], xqk_ref[bb])
        v[bb] = jnp.where(is_meta, zm_ref[0, :, C_MV:C_MV + ML_WIDTH], xv_ref[bb])
        gates = jnp.where(is_meta, gm_ref[0], xg_ref[bb])
        prev = prev_sc[bb * L:(bb + 1) * L, :]
        ext = jnp.concatenate([qk_pre, prev], axis=0)
        prev_sc[bb * L:(bb + 1) * L, :] = qk_pre
        shifted = jnp.dot(sh_ref[...], ext, preferred_element_type=F32)
        acc = cb_ref[...] + cw_ref[CONV_K - 1:CONV_K, :] * qk_pre.astype(F32)
        for s in range(1, CONV_K):
            acc = acc + cw_ref[CONV_K - 1 - s:CONV_K - s, :] * shifted[(s - 1) * L:s * L, :]
        qk = acc * _sigmoid(acc)
        qk = jnp.where(valid, qk, 0.0)
        q[bb] = qk[:, :ML_WIDTH]
        k[bb] = qk[:, ML_WIDTH:] * (ML_DH ** -0.5)

        igv = gates[:, :LANES] + gb_ref[:, :LANES]
        fg = gates[:, LANES:] + gb_ref[:, LANES:]
        lf = jnp.minimum(fg, 0.0) - jnp.log1p(jnp.exp(-jnp.abs(fg)))
        igv = jnp.where(valid, igv, NEG)
        lf = jnp.where(valid, lf, 0.0)
        lf_hi = lf.astype(BF16)
        lf_lo = (lf - lf_hi.astype(F32)).astype(BF16)
        bcs[bb] = (jnp.dot(tri, lf_hi, preferred_element_type=F32)
                   + jnp.dot(tri, lf_lo, preferred_element_type=F32))
        ig[bb] = igv
        b_t[bb] = bcs[bb].T
        ig_t[bb] = igv.T

    bc, m_h, dlog, inter, m_s = {}, {}, {}, {}, {}
    for p in pairs:
        bb, h = p
        bc[p] = bcs[bb][:, h:h + 1]
        m_h[p] = m_sc[st(bb, h):st(bb, h) + 1, 0:1]
        dlog[p] = jnp.where(causal, (bc[p] - b_t[bb][h:h + 1, :]) + ig_t[bb][h:h + 1, :], NEG)
        inter[p] = bc[p] + m_h[p]
    for p in pairs:
        m_s[p] = jnp.maximum(inter[p], jnp.max(dlog[p], axis=1, keepdims=True))

    w_intra, w_inter, qb, kb, s, qc = {}, {}, {}, {}, {}, {}
    for p in pairs:
        bb, h = p
        w_intra[p] = jnp.exp(dlog[p] - m_s[p])
        w_inter[p] = jnp.exp(inter[p] - m_s[p])
        qb[p] = q[bb][:, sl(h)].astype(BF16)
        kb[p] = k[bb][:, sl(h)].astype(BF16)
    for p in pairs:
        bb, h = p
        s[p] = lax.dot_general(qb[p], kb[p], (((1,), (1,)), ((), ())), preferred_element_type=F32) * w_intra[p]
        qc[p] = jnp.dot(qb[p], ct_sc[bb * ML_HEADS + h].astype(BF16), preferred_element_type=F32)

    hh = {}
    for p in pairs:
        bb, h = p
        nrow = n_sc[st(bb, h):st(bb, h) + 1, :]
        num = w_inter[p] * qc[p] + jnp.dot(s[p].astype(BF16), v[bb][:, sl(h)], preferred_element_type=F32)
        den = (w_inter[p] * jnp.sum(q[bb][:, sl(h)] * nrow, axis=1, keepdims=True)
               + jnp.sum(s[p], axis=1, keepdims=True))
        hh[p] = num / jnp.maximum(jnp.abs(den), jnp.exp(-m_s[p]))

    for p in pairs:
        bb, h = p
        b_last = bc[p][L - 1:L, :]
        upd = (b_last - bc[p]) + ig[bb][:, h:h + 1]
        m_new = jnp.maximum(b_last + m_h[p], jnp.max(upd, axis=0, keepdims=True))
        w_old = jnp.exp(b_last + m_h[p] - m_new)
        w_r = jnp.exp(upd - m_new)
        vw = (v[bb][:, sl(h)].astype(F32) * w_r).astype(BF16)
        ci = bb * ML_HEADS + h
        ct_sc[ci] = w_old * ct_sc[ci] + lax.dot_general(kb[p], vw, (((0,), (0,)), ((), ())),
                                                         preferred_element_type=F32)
        r = st(bb, h)
        n_sc[r:r + 1, :] = w_old * n_sc[r:r + 1, :] + jnp.sum(k[bb][:, sl(h)] * w_r, axis=0, keepdims=True)
        m_sc[r:r + 1, :] = jnp.broadcast_to(m_new, (1, LANES))

    outs = {}
    for p in pairs:
        bb, h = p
        hn = hh[p] * lax.rsqrt(jnp.mean(hh[p] * hh[p], axis=1, keepdims=True) + LN_EPS) * ng_ref[:, sl(h)]
        outs[p] = _sigmoid(xo_ref[bb, :, sl(h)].astype(F32)) * hn

    @pl.when(c > 0)
    def _():
        for bb in elems:
            o_ref[bb] = jnp.concatenate([outs[(bb, h)] for h in range(ML_HEADS)], axis=1).astype(BF16)


def _mlstm(zb, gates, zbm, gatesm, conv_w, conv_b, gate_b, ng, *, nbps):
    B, S, _ = zb.shape
    nc = S // BLOCK + 1
    t = np.arange(BLOCK)
    shift = np.zeros(((CONV_K - 1) * BLOCK, 2 * BLOCK), np.float32)
    for s in range(1, CONV_K):
        shift[(s - 1) * BLOCK + t, np.where(t >= s, t - s, 2 * BLOCK + t - s)] = 1.0

    def xc(c):
        return jnp.maximum(c - 1, 0)

    return pl.pallas_call(
        functools.partial(_mlstm_kernel, nbps=nbps),
        grid=(B // nbps, nc),
        in_specs=[
            pl.BlockSpec((nbps, BLOCK, 2 * ML_WIDTH), lambda b, c: (b, xc(c), C_MQK // (2 * ML_WIDTH))),
            pl.BlockSpec((nbps, BLOCK, ML_WIDTH), lambda b, c: (b, xc(c), C_MV // ML_WIDTH)),
            pl.BlockSpec((nbps, BLOCK, ML_WIDTH), lambda b, c: (b, xc(c), C_MO // ML_WIDTH)),
            pl.BlockSpec((nbps, BLOCK, GATE_W), lambda b, c: (b, xc(c), 0)),
            pl.BlockSpec((1, BLOCK, ZB_W), lambda b, c: (0, 0, 0)),
            pl.BlockSpec((1, BLOCK, GATE_W), lambda b, c: (0, 0, 0)),
            pl.BlockSpec((CONV_K, 2 * ML_WIDTH), lambda b, c: (0, 0)),
            pl.BlockSpec((1, 2 * ML_WIDTH), lambda b, c: (0, 0)),
            pl.BlockSpec((1, GATE_W), lambda b, c: (0, 0)),
            pl.BlockSpec((1, ML_WIDTH), lambda b, c: (0, 0)),
            pl.BlockSpec(((CONV_K - 1) * BLOCK, 2 * BLOCK), lambda b, c: (0, 0)),
        ],
        out_specs=pl.BlockSpec((nbps, BLOCK, ML_WIDTH), lambda b, c: (b, xc(c), 0)),
        out_shape=jax.ShapeDtypeStruct((B, S, ML_WIDTH), BF16),
        scratch_shapes=[
            pltpu.VMEM((nbps * ML_HEADS, ML_DH, ML_DH), F32),
            pltpu.VMEM((nbps * 8, ML_DH), F32),
            pltpu.VMEM((nbps * 8, LANES), F32),
            pltpu.VMEM((nbps * BLOCK, 2 * ML_WIDTH), BF16),
        ],
        compiler_params=_cparams(2),
        name="mlstm",
    )(zb, zb, zb, gates, zbm, gatesm, conv_w, conv_b, gate_b, ng, jnp.asarray(shift, BF16))


def _outproj_kernel(x_ref, ya_ref, ym_ref, ga_ref, gmm_ref, eg_ref, eb_ref, wa_ref, wm_ref, wo_ref,
                    g1_ref, b1_ref, wr_ref, br_ref,
                    h1_ref, eid_ref, gate_ref, rank_ref, cnt_ref, run_sc, *, tm):
    i = pl.program_id(0)

    @pl.when(i == 0)
    def _():
        run_sc[...] = jnp.zeros_like(run_sc)

    h0 = _ln(x_ref[...], eg_ref[...], eb_ref[...])
    pa = jnp.dot(ya_ref[...], wa_ref[...], preferred_element_type=F32)
    pm = jnp.dot(ym_ref[...], wm_ref[...], preferred_element_type=F32)
    sig = lambda g: 0.5 * jnp.tanh(0.5 * g) + 0.5
    merged = sig(ga_ref[...]) * pa.astype(BF16) + sig(gmm_ref[...]) * pm.astype(BF16)
    mix = jnp.dot(merged, wo_ref[...], preferred_element_type=F32)
    h1 = _ln(DN_ALPHA * h0 + mix, g1_ref[...], b1_ref[...])
    _store_row_tiles(h1_ref, h1)

    h_hi = h1.astype(BF16)
    h_mid = (h1 - h_hi.astype(F32)).astype(BF16)
    hh = jnp.dot(h_hi, wr_ref[...], preferred_element_type=F32)
    logits = (hh[:, :LANES] + hh[:, LANES:]
              + jnp.dot(h_mid, wr_ref[:, :LANES], preferred_element_type=F32) + br_ref[...])
    lane = lax.broadcasted_iota(jnp.int32, (tm, LANES), 1)
    work = logits
    sel_e, sel_v = [], []
    for _ in range(TOP_K):
        mv = jnp.max(work, axis=1, keepdims=True)
        e = jnp.min(jnp.where(work == mv, lane, LANES), axis=1, keepdims=True)
        sel_e.append(e)
        sel_v.append(mv)
        work = jnp.where(lane == e, -jnp.inf, work)
    ex = [jnp.exp(v - sel_v[0]) for v in sel_v]
    den = ex[0] + ex[1] + ex[2] + ex[3]

    onehots = [lane == e for e in sel_e]
    oh = jnp.zeros((tm, LANES), F32)
    for o in onehots:
        oh = oh + o.astype(F32)
    r_i = lax.broadcasted_iota(jnp.int32, (tm, tm), 0)
    c_i = lax.broadcasted_iota(jnp.int32, (tm, tm), 1)
    strict = (c_i < r_i).astype(BF16)
    before = jnp.dot(strict, oh.astype(BF16), preferred_element_type=F32) + run_sc[...]
    eid = jnp.zeros((tm, LANES), F32)
    gate = jnp.zeros((tm, LANES), F32)
    rank = jnp.zeros((tm, LANES), F32)
    for kk in range(TOP_K):
        here = lane == kk
        eid = jnp.where(here, sel_e[kk].astype(F32), eid)
        gate = jnp.where(here, ex[kk] / den, gate)
        rk = jnp.sum(jnp.where(onehots[kk], before, 0.0), axis=1, keepdims=True)
        rank = jnp.where(here, rk, rank)
    gate_ref[...] = gate
    eid_ref[...] = eid.T[0:8, :]
    rank_ref[...] = rank.T[0:8, :]
    run_sc[...] = run_sc[...] + jnp.sum(oh, axis=0, keepdims=True)
    cnt_ref[...] = run_sc[...]


def _outproj(x2, yatt, yml, zb2, eg, eb, wa, wm, wo, g1, b1, wr, br, *, tm):
    N, D = x2.shape
    vec = lambda w: pl.BlockSpec((1, w), lambda i: (0, 0))
    full = lambda a, b: pl.BlockSpec((a, b), lambda i: (0, 0))
    tile = lambda w: pl.BlockSpec((tm, w), lambda i: (i, 0))
    return pl.pallas_call(
        functools.partial(_outproj_kernel, tm=tm),
        grid=(N // tm,),
        in_specs=[
            tile(D), tile(ATT_HEADS * ATT_V_DIM), tile(ML_WIDTH),
            pl.BlockSpec((tm, D), lambda i: (i, C_GA // D)),
            pl.BlockSpec((tm, D), lambda i: (i, C_GM // D)),
            vec(D), vec(D),
            full(ATT_HEADS * ATT_V_DIM, D), full(ML_WIDTH, D), full(D, D),
            vec(D), vec(D), full(D, 2 * LANES), vec(LANES),
        ],
        out_specs=[pl.BlockSpec((tm * ROW_TILE, LANES), lambda i: (i, 0)),
                   pl.BlockSpec((8, tm), lambda i: (0, i)), tile(LANES),
                   pl.BlockSpec((8, tm), lambda i: (0, i)), vec(LANES)],
        out_shape=[
            jax.ShapeDtypeStruct((N * ROW_TILE, LANES), F32),
            jax.ShapeDtypeStruct((8, N), F32),
            jax.ShapeDtypeStruct((N, LANES), F32),
            jax.ShapeDtypeStruct((8, N), F32),
            jax.ShapeDtypeStruct((1, LANES), F32),
        ],
        scratch_shapes=[pltpu.VMEM((1, LANES), F32)],
        compiler_params=_cparams(1),
        name="outproj_router",
    )(x2, yatt, yml, zb2, zb2, eg, eb, wa, wm, wo, g1, b1, wr, br)


def _row_copies(src_row, dst_row, sem, n, issue):
    def body(r, carry):
        for kk in range(TOP_K):
            cp = pltpu.make_async_copy(src_row(r, kk), dst_row(r, kk), sem)
            if issue:
                cp.start(priority=kk % 2)
            else:
                cp.wait()
        return carry
    lax.fori_loop(0, n, body, 0, unroll=8)


def _dispatch_kernel(pad_pos_ref, pad_len_ref, d_ref, h_ref, xs_ref, zero_sc, sem, *, tm, max_pad):
    @pl.when(pl.program_id(0) == 0)
    def _():
        zero_sc[...] = jnp.zeros_like(zero_sc)
        bits = [1 << b for b in reversed(range(max_pad.bit_length()))]

        def pad_copies(e, issue):
            pos = pad_pos_ref[e]
            length = pad_len_ref[e]
            for bit in bits:
                take = (length & bit) != 0

                @pl.when(take)
                def _(pos=pos, bit=bit):
                    dst = xs_ref.at[pl.ds(pl.multiple_of(pos * ROW_TILE, ROW_TILE), bit * ROW_TILE)]
                    cp = pltpu.make_async_copy(zero_sc.at[pl.ds(0, bit * ROW_TILE)], dst, sem.at[1])
                    if issue:
                        cp.start()
                    else:
                        cp.wait()
                pos = pos + jnp.where(take, bit, 0)

        def per_expert(e, carry):
            pad_copies(e, True)
            pad_copies(e, False)
            return carry

        lax.fori_loop(0, N_EXPERTS, per_expert, 0)

    src = lambda r, kk: _row_tile(h_ref, r)
    dst = lambda r, kk: _row_tile(xs_ref, d_ref[0, 0, kk * tm + r])
    _row_copies(src, dst, sem.at[0], tm, True)
    _row_copies(src, dst, sem.at[0], tm, False)


def _dispatch(pad_pos, pad_len, dest3, h1t, rows_out, *, tm, max_pad):
    N = h1t.shape[0] // ROW_TILE
    zero_rows = 1 << (max_pad.bit_length() - 1)
    return pl.pallas_call(
        functools.partial(_dispatch_kernel, tm=tm, max_pad=max_pad),
        grid_spec=pltpu.PrefetchScalarGridSpec(
            num_scalar_prefetch=2,
            grid=(N // tm,),
            in_specs=[
                pl.BlockSpec((1, 1, tm * TOP_K), lambda i, pp, pn: (i, 0, 0), memory_space=pltpu.SMEM),
                pl.BlockSpec((tm * ROW_TILE, LANES), lambda i, pp, pn: (i, 0)),
            ],
            out_specs=pl.BlockSpec(memory_space=pl.ANY),
            scratch_shapes=[pltpu.VMEM((zero_rows * ROW_TILE, LANES), F32), pltpu.SemaphoreType.DMA((2,))],
        ),
        out_shape=jax.ShapeDtypeStruct((rows_out * ROW_TILE, LANES), F32),
        compiler_params=_cparams(1),
        name="moe_dispatch",
    )(pad_pos, pad_len, dest3, h1t)


def _deinterleave_kernel(w_ref, p_ref, wg_ref, wl_ref):
    for c in range(w_ref.shape[2] // (2 * LANES)):
        blk = w_ref[0, :, c * 2 * LANES:(c + 1) * 2 * LANES].astype(BF16)
        r = jnp.dot(blk, p_ref[...], preferred_element_type=F32).astype(BF16)
        wg_ref[0, :, c * LANES:(c + 1) * LANES] = r[:, :LANES]
        wl_ref[0, :, c * LANES:(c + 1) * LANES] = r[:, LANES:]


def _deinterleave(w_gu):
    E, D, F2 = w_gu.shape
    F = F2 // 2
    j = np.arange(2 * LANES)
    sel = np.zeros((2 * LANES, 2 * LANES), np.float32)
    sel[j, (j % 2) * LANES + j // 2] = 1.0
    out = jax.ShapeDtypeStruct((E, D, F), BF16)
    return pl.pallas_call(
        _deinterleave_kernel,
        grid=(E,),
        in_specs=[
            pl.BlockSpec((1, D, F2), lambda e: (e, 0, 0)),
            pl.BlockSpec((2 * LANES, 2 * LANES), lambda e: (0, 0)),
        ],
        out_specs=[pl.BlockSpec((1, D, F), lambda e: (e, 0, 0))] * 2,
        out_shape=[out, out],
        compiler_params=_cparams(1),
        name="deinterleave_wgu",
    )(w_gu, jnp.asarray(sel, BF16))


def _expert_kernel(be_ref, nu_ref, x_ref, wg_ref, wl_ref, bg_ref, bl_ref, wd_ref, bd_ref, y_ref, *, tm):
    del be_ref

    @pl.when(pl.program_id(0) < nu_ref[0])
    def _():
        xb = _load_row_tiles(x_ref, tm).astype(BF16)
        hg = jnp.dot(xb, wg_ref[0], preferred_element_type=F32) + bg_ref[0]
        hl = jnp.dot(xb, wl_ref[0], preferred_element_type=F32) + bl_ref[0]
        glu = jnp.minimum(hg, SWIGLU_LIMIT)
        lin = jnp.clip(hl, -SWIGLU_LIMIT, SWIGLU_LIMIT)
        act = glu * _sigmoid(SWIGLU_ALPHA * glu) * (lin + 1.0)
        _store_row_tiles(y_ref, jnp.dot(act.astype(BF16), wd_ref[0], preferred_element_type=F32) + bd_ref[0])


def _experts(blk_exp, n_used, xs, wg, wl, bg, bl, wd, bd, *, tm):
    MP = xs.shape[0] // ROW_TILE
    _, D, F = wg.shape
    rows = pl.BlockSpec((tm * ROW_TILE, LANES), lambda i, be, nu: (jnp.minimum(i, nu[0] - 1), 0))
    wsp = lambda a, b: pl.BlockSpec((1, a, b), lambda i, be, nu: (be[i], 0, 0))
    return pl.pallas_call(
        functools.partial(_expert_kernel, tm=tm),
        grid_spec=pltpu.PrefetchScalarGridSpec(
            num_scalar_prefetch=2,
            grid=(MP // tm,),
            in_specs=[rows, wsp(D, F), wsp(D, F), wsp(1, F), wsp(1, F), wsp(F, D), wsp(1, D)],
            out_specs=rows,
        ),
        out_shape=jax.ShapeDtypeStruct(xs.shape, F32),
        compiler_params=_cparams(1),
        name="moe_experts",
    )(blk_exp, n_used, xs, wg, wl, bg, bl, wd, bd)


def _combine_kernel(dcur_ref, dnxt_ref, h1_ref, gate_ref, g2_ref, b2_ref, ys_ref, o_ref, ybuf, sem, *, tm, nsteps):
    i = pl.program_id(0)
    slot = i % 2

    def copies(d_ref, s, issue):
        src = lambda r, kk: _row_tile(ys_ref, d_ref[0, 0, kk * tm + r])
        dst = lambda r, kk: _row_tile(ybuf.at[s, kk], r)
        _row_copies(src, dst, sem.at[s], tm, issue)

    @pl.when(i == 0)
    def _():
        copies(dcur_ref, 0, True)

    @pl.when(i + 1 < nsteps)
    def _():
        copies(dnxt_ref, 1 - slot, True)

    copies(dcur_ref, slot, False)
    acc = DN_ALPHA * _load_row_tiles(h1_ref, tm)
    for kk in range(TOP_K):
        acc = acc + gate_ref[:, kk:kk + 1] * _load_row_tiles(ybuf.at[slot, kk], tm)
    o_ref[...] = _ln(acc, g2_ref[...], b2_ref[...])


def _combine(dest3, h1t, gate, g2, b2, ys, *, tm):
    N, D = h1t.shape[0] // ROW_TILE, D_MODEL
    nsteps = N // tm
    dspec = lambda f: pl.BlockSpec((1, 1, tm * TOP_K), f, memory_space=pltpu.SMEM)
    return pl.pallas_call(
        functools.partial(_combine_kernel, tm=tm, nsteps=nsteps),
        grid=(nsteps,),
        in_specs=[
            dspec(lambda i: (i, 0, 0)),
            dspec(lambda i: (jnp.minimum(i + 1, nsteps - 1), 0, 0)),
            pl.BlockSpec((tm * ROW_TILE, LANES), lambda i: (i, 0)),
            pl.BlockSpec((tm, LANES), lambda i: (i, 0)),
            pl.BlockSpec((1, D), lambda i: (0, 0)),
            pl.BlockSpec((1, D), lambda i: (0, 0)),
            pl.BlockSpec(memory_space=pl.ANY),
        ],
        out_specs=pl.BlockSpec((tm, D), lambda i: (i, 0)),
        out_shape=jax.ShapeDtypeStruct((N, D), F32),
        scratch_shapes=[pltpu.VMEM((2, TOP_K, tm * ROW_TILE, LANES), F32), pltpu.SemaphoreType.DMA((2,))],
        compiler_params=_cparams(1),
        name="moe_combine",
    )(dest3, dest3, h1t, gate, g2, b2, ys)


def _rotary_column_order():
    idx = np.empty((ATT_HEADS * LANES,), np.int32)
    for h in range(ATT_HEADS):
        for half in range(2):
            for sub in range(2):
                for dd in range(32):
                    idx[h * LANES + half * 64 + sub * 32 + dd] = (2 * h + sub) * ATT_QK_DIM + half * 32 + dd
    return idx


def _rotary_tables(pos):
    half = ATT_QK_DIM // 2
    inv_freq = ROPE_THETA ** (-jnp.arange(half, dtype=F32) / half)
    ang = pos.astype(F32)[:, None] * inv_freq[None, :]
    cos = jnp.tile(jnp.cos(ang), (1, 4))
    sin = jnp.tile(jnp.sin(ang), (1, 4))
    sign = jnp.where(jnp.arange(LANES) < 64, -1.0, 1.0).astype(F32)
    return cos, sin * sign[None, :]


def _pick(n, prefs):
    for t in prefs:
        if n % t == 0:
            return t
    raise ValueError(f"unsupported size {n}")


def kernel(x, meta, emb_ln_g, emb_ln_b, w_in, conv_w, conv_b, gate_bias, lam_q1, lam_k1, lam_q2, lam_k2,
           att_norm_g, ml_norm_g, w_att_out, w_ml_out, w_o, ln1_g, ln1_b, w_router, b_router,
           w_gu, b_gu, w_down, b_down, ln2_g, ln2_b):
    B, S, D = x.shape
    assert D == D_MODEL and S % 512 == 0 and w_in.shape[0] == DEPTH
    N = B * S
    row2 = lambda a: a.reshape(1, -1)

    w = w_in[0]
    o_aq, o_ak, o_av, o_mqk, o_mv, o_mo, o_gt, o_ga, o_gm = 0, 512, 1024, 1536, 2560, 3072, 3584, 3592, 4616
    perm = _rotary_column_order()
    gpad = jnp.zeros((D, LANES - ML_HEADS), F32)
    w_all = jnp.concatenate([
        w[:, o_ga:o_ga + D], w[:, o_gm:o_gm + D], w[:, o_mqk:o_mqk + 2 * ML_WIDTH],
        w[:, o_aq:o_aq + 512][:, perm], w[:, o_ak:o_ak + 512][:, perm], w[:, o_av:o_av + 512],
        w[:, o_mv:o_mv + ML_WIDTH], w[:, o_mo:o_mo + ML_WIDTH],
        w[:, o_gt:o_gt + ML_HEADS], gpad, w[:, o_gt + ML_HEADS:o_gt + 2 * ML_HEADS], gpad,
    ], axis=1).astype(BF16)
    gb = gate_bias[0]
    zpad = jnp.zeros((LANES - ML_HEADS,), F32)
    gate_b = jnp.concatenate([gb[:ML_HEADS], zpad, gb[ML_HEADS:], zpad]).reshape(1, GATE_W)

    cos_x, sin_x = _rotary_tables(N_META + jnp.arange(S))
    cos_m, sin_m = _rotary_tables(jnp.maximum(jnp.arange(BLOCK) - PAD, 0))

    eg, eb = row2(emb_ln_g), row2(emb_ln_b)

    tm_in = 512
    zb, gates = _inproj(x, eg, eb, w_all, cos_x, sin_x, tm=tm_in)
    xm = jnp.concatenate([jnp.zeros((PAD, D), x.dtype), meta.astype(x.dtype)], axis=0)[None]
    zbm, gatesm = _inproj(xm, eg, eb, w_all, cos_m, sin_m, tm=BLOCK, first_valid_row=PAD)

    tq = 512
    yatt = _attention(zb, zbm, row2(lam_q1[0]), row2(lam_k1[0]), row2(lam_q2[0]), row2(lam_k2[0]),
                      row2(att_norm_g[0]), tq=tq)
    yml = _mlstm(zb, gates, zbm, gatesm, conv_w[0], row2(conv_b[0]), gate_b, row2(ml_norm_g[0]),
                 nbps=2 if B % 2 == 0 else 1)

    wr32 = jnp.concatenate([w_router[0], jnp.zeros((D, LANES - N_EXPERTS), F32)], axis=1)
    wr_hi = wr32.astype(BF16)
    wr = jnp.concatenate([wr_hi, (wr32 - wr_hi.astype(F32)).astype(BF16)], axis=1)
    br = jnp.concatenate([b_router[0], jnp.full((LANES - N_EXPERTS,), NEG, F32)]).reshape(1, LANES)
    tm_out = 512
    h1, eid, gate, rank, cnt = _outproj(
        x.reshape(N, D), yatt.reshape(N, -1), yml.reshape(N, -1), zb.reshape(N, ZB_W), eg, eb,
        w_att_out[0].astype(BF16), w_ml_out[0].astype(BF16), w_o[0].astype(BF16),
        row2(ln1_g[0]), row2(ln1_b[0]), wr, br, tm=tm_out)

    tm_e = 512
    M = N * TOP_K
    nb = (M + N_EXPERTS * (tm_e - 1) + tm_e - 1) // tm_e
    counts = cnt[0, :N_EXPERTS].astype(jnp.int32)
    nblk = (counts + tm_e - 1) // tm_e
    cum = jnp.cumsum(nblk)
    pstart = (cum - nblk) * tm_e
    eid_t = eid[:TOP_K].astype(jnp.int32)
    base = jnp.sum(jnp.where(eid_t[None] == jnp.arange(N_EXPERTS)[:, None, None], pstart[:, None, None], 0), axis=0)
    dest = base + rank[:TOP_K].astype(jnp.int32)
    n_used = cum[-1:].astype(jnp.int32)
    blk = jnp.minimum(jnp.arange(nb, dtype=jnp.int32), n_used[0] - 1)
    blk_exp = jnp.sum(blk[:, None] >= cum[None, :], axis=1).astype(jnp.int32)

    tm_d = 256
    dest3 = dest.reshape(TOP_K, N // tm_d, tm_d).transpose(1, 0, 2).reshape(N // tm_d, 1, TOP_K * tm_d)
    xs = _dispatch(pstart + counts, nblk * tm_e - counts, dest3, h1, nb * tm_e, tm=tm_d, max_pad=tm_e - 1)

    wg, wl = _deinterleave(w_gu[0])
    bgu = b_gu[0]
    ys = _experts(blk_exp, n_used, xs, wg, wl,
                  bgu[:, None, 0::2], bgu[:, None, 1::2],
                  w_down[0].astype(BF16), b_down[0][:, None, :], tm=tm_e)

    out = _combine(dest3, h1, gate, row2(ln2_g[0]), row2(ln2_b[0]), ys, tm=tm_d)
    return out.reshape(B, S, D)
```

```python
import functools
import math

import jax
import jax.numpy as jnp
import numpy as np
from jax import lax
from jax.experimental import pallas as pl
from jax.experimental.pallas import tpu as pltpu

D_MODEL = 1024
N_META = 16
BLOCK = 128
PAD = BLOCK - N_META
NEG = -1e30
LN_EPS = 1e-5
ATT_HEADS = 4
ATT_QK_DIM = 64
ATT_V_DIM = 128
ROPE_THETA = 10000.0
ML_HEADS = 4
ML_DH = 128
ML_WIDTH = ML_HEADS * ML_DH
CONV_K = 4
N_EXPERTS = 32
TOP_K = 4
SWIGLU_LIMIT = 7.0
SWIGLU_ALPHA = 1.702
DEPTH = 1
DN_ALPHA = (2 * DEPTH) ** 0.25
LAMBDA_INIT = 0.8 - 0.6 * math.exp(-0.3 * 0)

LANES = 128
C_GA, C_GM, C_MQK, C_AQ, C_AK, C_AV, C_MV, C_MO = 0, 1024, 2048, 3072, 3584, 4096, 4608, 5120
ZB_W = 5632
GATE_W = 2 * LANES
W_ALL = ZB_W + GATE_W
CHUNK = 512

VMEM_LIMIT = 56 * 1024 * 1024

BF16 = jnp.bfloat16
F32 = jnp.float32


def _cparams(n_axes):
    return pltpu.CompilerParams(dimension_semantics=("arbitrary",) * n_axes, vmem_limit_bytes=VMEM_LIMIT)


def _ln(x, g, b):
    mu = jnp.mean(x, axis=-1, keepdims=True)
    xc = x - mu
    var = jnp.mean(xc * xc, axis=-1, keepdims=True)
    return xc * lax.rsqrt(var + LN_EPS) * g + b


def _sigmoid(x):
    return 1.0 / (1.0 + jnp.exp(-x))


ROW_TILE = D_MODEL // LANES


def _store_row_tiles(ref, val):
    tm = val.shape[0]
    for g in range(ROW_TILE):
        ref[pl.ds(g, tm, stride=ROW_TILE), :] = val[:, g * LANES:(g + 1) * LANES]


def _load_row_tiles(ref, tm):
    return jnp.concatenate([ref[pl.ds(g, tm, stride=ROW_TILE), :] for g in range(ROW_TILE)], axis=1)


def _row_tile(ref, r):
    return ref.at[pl.ds(pl.multiple_of(r * ROW_TILE, ROW_TILE), ROW_TILE)]


def _inproj_kernel(x_ref, g_ref, b_ref, w_ref, cos_ref, sin_ref, zb_ref, gt_ref, *, first_valid_row):
    x = x_ref[0]
    tm = x.shape[0]
    hb = _ln(x, g_ref[...], b_ref[...]).astype(BF16)
    cos = cos_ref[...]
    sin = sin_ref[...]
    if first_valid_row:
        rowmask = lax.broadcasted_iota(jnp.int32, (tm, 1), 0) >= first_valid_row
    for c in range(ZB_W // CHUNK):
        z = jnp.dot(hb, w_ref[:, c * CHUNK:(c + 1) * CHUNK], preferred_element_type=F32)
        if c * CHUNK in (C_AQ, C_AK):
            parts = []
            for h in range(ATT_HEADS):
                zh = z[:, h * LANES:(h + 1) * LANES]
                parts.append(zh * cos + pltpu.roll(zh, 64, 1) * sin)
            z = jnp.concatenate(parts, axis=1)
            if c * CHUNK == C_AQ:
                z = z * (ATT_QK_DIM ** -0.5 * math.log2(math.e))
        if first_valid_row:
            z = jnp.where(rowmask, z, 0.0)
        zb_ref[0, :, c * CHUNK:(c + 1) * CHUNK] = z.astype(BF16)
    zg = jnp.dot(hb, w_ref[:, ZB_W:W_ALL], preferred_element_type=F32)
    if first_valid_row:
        zg = jnp.where(rowmask, zg, 0.0)
    gt_ref[0] = zg


def _inproj(x3, g, b, w, cos, sin, *, tm, first_valid_row=0):
    B, S, D = x3.shape
    nt = S // tm
    return pl.pallas_call(
        functools.partial(_inproj_kernel, first_valid_row=first_valid_row),
        grid=(B, nt),
        in_specs=[
            pl.BlockSpec((1, tm, D), lambda bi, i: (bi, i, 0)),
            pl.BlockSpec((1, D), lambda bi, i: (0, 0)),
            pl.BlockSpec((1, D), lambda bi, i: (0, 0)),
            pl.BlockSpec((D, W_ALL), lambda bi, i: (0, 0)),
            pl.BlockSpec((tm, LANES), lambda bi, i: (i, 0)),
            pl.BlockSpec((tm, LANES), lambda bi, i: (i, 0)),
        ],
        out_specs=[
            pl.BlockSpec((1, tm, ZB_W), lambda bi, i: (bi, i, 0)),
            pl.BlockSpec((1, tm, GATE_W), lambda bi, i: (bi, i, 0)),
        ],
        out_shape=[
            jax.ShapeDtypeStruct((B, S, ZB_W), BF16),
            jax.ShapeDtypeStruct((B, S, GATE_W), F32),
        ],
        compiler_params=_cparams(2),
        name="inproj",
    )(x3, g, b, w, cos, sin)


def _attn_kernel(q_ref, k_ref, v_ref, km_ref, vm_ref, lq1_ref, lk1_ref, lq2_ref, lk2_ref, ng_ref, o_ref,
                 sa_sc, sb_sc, pa_sc, pb_sc, aa_sc, ab_sc, m_sc, acc_sc, *, tq, tk, rc):
    assert tq == 2 * tk
    qi = pl.program_id(2)
    rows = 2 * tq
    q = q_ref[0]
    lane = lax.broadcasted_iota(jnp.int32, (tq, LANES), 1)
    is_map0 = (lane & 63) < 32
    zero = jnp.zeros_like(q)
    qq = jnp.concatenate([jnp.where(is_map0, q, zero), jnp.where(is_map0, zero, q)], axis=0)

    def qk_stage(kblk, s_ref, col0=0):
        s_ref[:, col0:col0 + kblk.shape[0]] = lax.dot_general(qq, kblk, (((1,), (1,)), ((), ())),
                                                              preferred_element_type=F32)

    def pv_stage(p_ref, a_ref, vblk, first=False):
        width = vblk.shape[0]
        v1 = jnp.concatenate([vblk, jnp.ones_like(vblk)], axis=1)
        pv = jnp.dot(p_ref[:, 0:width], v1, preferred_element_type=F32)
        if first:
            acc_sc[...] = pv
        else:
            a = a_ref[...]
            acc_sc[...] = jnp.concatenate([a, a], axis=1) * acc_sc[...] + pv

    def sm_stage(s_ref, p_ref, a_ref, width, vis, first=False):
        nrep = width // LANES
        for r0 in range(0, rows, rc):
            kind = vis(r0)
            if kind == "none":
                continue
            s = s_ref[r0:r0 + rc, 0:width]
            if kind != "all":
                s = kind(s)
            m_cur = jnp.max(s, axis=1, keepdims=True)
            if first:
                m_new = jnp.broadcast_to(m_cur, (rc, LANES))
            else:
                m_prev = m_sc[r0:r0 + rc, :]
                m_new = jnp.maximum(m_prev, m_cur)
                a_ref[r0:r0 + rc, :] = jnp.exp2(m_prev - m_new)
            m_sc[r0:r0 + rc, :] = m_new
        for r0 in range(0, rows, rc):
            kind = vis(r0)
            if kind == "none":
                p_ref[r0:r0 + rc, 0:width] = jnp.zeros((rc, width), BF16)
                a_ref[r0:r0 + rc, :] = jnp.ones((rc, LANES), F32)
                continue
            s = s_ref[r0:r0 + rc, 0:width]
            if kind != "all":
                s = kind(s)
            m_new = m_sc[r0:r0 + rc, :]
            m_rep = m_new if nrep == 1 else jnp.concatenate([m_new] * nrep, axis=1)
            p_ref[r0:r0 + rc, 0:width] = jnp.exp2(s - m_rep).astype(BF16)

    def first_vis(r0):
        q0 = r0 % tq
        def mask(s):
            rowq = q0 + lax.broadcasted_iota(jnp.int32, s.shape, 0)
            col = lax.broadcasted_iota(jnp.int32, s.shape, 1)
            keep = col >= tk + PAD
            if tk - 1 <= q0:
                keep = jnp.logical_or(keep, col < tk)
            else:
                keep = jnp.logical_or(keep, col <= rowq)
            return jnp.where(keep, s, NEG)
        return mask

    def diag_vis(d):
        def vis(r0):
            q0 = r0 % tq
            if d * tk + tk - 1 <= q0:
                return "all"
            if d * tk > q0 + rc - 1:
                return "none"
            def mask(s):
                rowq = q0 + lax.broadcasted_iota(jnp.int32, s.shape, 0)
                col = d * tk + lax.broadcasted_iota(jnp.int32, s.shape, 1)
                return jnp.where(col <= rowq, s, NEG)
            return mask
        return vis

    all_vis = lambda r0: "all"
    kblk = lambda start: k_ref[0, pl.ds(start, tk), :]
    vblk = lambda start: v_ref[0, pl.ds(start, tk), :]

    d0 = pl.multiple_of(qi * tq, tk)
    d1 = pl.multiple_of(qi * tq + tk, tk)
    qk_stage(kblk(d0), sa_sc)
    qk_stage(km_ref[0], sa_sc, col0=tk)
    qk_stage(kblk(d1), sb_sc)
    sm_stage(sa_sc, pa_sc, None, tk + BLOCK, first_vis, first=True)
    pv_stage(pa_sc, None, jnp.concatenate([vblk(d0), vm_ref[0]], axis=0), first=True)
    sm_stage(sb_sc, pb_sc, ab_sc, tk, diag_vis(1))
    qk_stage(kblk(0), sa_sc)

    def pair(i, carry):
        a_start = pl.multiple_of(i * tq, tk)
        b_start = pl.multiple_of(i * tq + tk, tk)
        prev_b = pl.multiple_of(jnp.where(i == 0, d1, a_start - tk), tk)
        next_a = pl.multiple_of(jnp.minimum(a_start + tq, (qi - 1) * tq), tk)
        qk_stage(kblk(b_start), sb_sc)
        pv_stage(pb_sc, ab_sc, vblk(prev_b))
        sm_stage(sa_sc, pa_sc, aa_sc, tk, all_vis)
        qk_stage(kblk(next_a), sa_sc)
        pv_stage(pa_sc, aa_sc, vblk(a_start))
        sm_stage(sb_sc, pb_sc, ab_sc, tk, all_vis)
        return carry

    lax.fori_loop(0, qi, pair, 0)
    last_b = pl.multiple_of(jnp.where(qi == 0, d1, qi * tq - tk), tk)
    pv_stage(pb_sc, ab_sc, vblk(last_b))

    o = acc_sc[:, 0:ATT_V_DIM] / acc_sc[:, ATT_V_DIM:]
    s1 = jnp.sum(lq1_ref[...] * lk1_ref[...], axis=1, keepdims=True)
    s2 = jnp.sum(lq2_ref[...] * lk2_ref[...], axis=1, keepdims=True)
    lam = jnp.exp(s1) - jnp.exp(s2) + LAMBDA_INIT
    a = o[:tq] - lam * o[tq:]
    y = a * lax.rsqrt(jnp.mean(a * a, axis=1, keepdims=True) + LN_EPS) * ng_ref[...]
    o_ref[0] = (y * (1.0 - LAMBDA_INIT)).astype(BF16)


def _attention(zb, zbm, lq1, lk1, lq2, lk2, ng, *, tq):
    B, S, _ = zb.shape
    nq = S // tq
    tk = tq // 2
    qb, kb, vb = C_AQ // LANES, C_AK // LANES, C_AV // LANES
    lam_spec = pl.BlockSpec((1, ATT_QK_DIM), lambda b, h, i: (0, 0))
    return pl.pallas_call(
        functools.partial(_attn_kernel, tq=tq, tk=tk, rc=32),
        grid=(B, ATT_HEADS, nq),
        in_specs=[
            pl.BlockSpec((1, tq, LANES), lambda b, h, i: (b, i, qb + h)),
            pl.BlockSpec((1, S, LANES), lambda b, h, i: (b, 0, kb + h)),
            pl.BlockSpec((1, S, LANES), lambda b, h, i: (b, 0, vb + h)),
            pl.BlockSpec((1, BLOCK, LANES), lambda b, h, i: (0, 0, kb + h)),
            pl.BlockSpec((1, BLOCK, LANES), lambda b, h, i: (0, 0, vb + h)),
            lam_spec, lam_spec, lam_spec, lam_spec,
            pl.BlockSpec((1, ATT_V_DIM), lambda b, h, i: (0, 0)),
        ],
        out_specs=pl.BlockSpec((1, tq, LANES), lambda b, h, i: (b, i, h)),
        out_shape=jax.ShapeDtypeStruct((B, S, ATT_HEADS * ATT_V_DIM), BF16),
        scratch_shapes=[
            pltpu.VMEM((2 * tq, tk + BLOCK), F32), pltpu.VMEM((2 * tq, tk), F32),
            pltpu.VMEM((2 * tq, tk + BLOCK), BF16), pltpu.VMEM((2 * tq, tk), BF16),
            pltpu.VMEM((2 * tq, LANES), F32), pltpu.VMEM((2 * tq, LANES), F32),
            pltpu.VMEM((2 * tq, LANES), F32),
            pltpu.VMEM((2 * tq, 2 * ATT_V_DIM), F32),
        ],
        compiler_params=_cparams(3),
        name="diff_attention",
    )(zb, zb, zb, zbm, zbm, lq1, lk1, lq2, lk2, ng)


def _mlstm_kernel(xqk_ref, xv_ref, xo_ref, xg_ref, zm_ref, gm_ref, cw_ref, cb_ref, gb_ref, ng_ref, sh_ref, o_ref,
                  ct_sc, n_sc, m_sc, prev_sc, *, nbps):
    c = pl.program_id(1)
    L = BLOCK

    @pl.when(c == 0)
    def _():
        ct_sc[...] = jnp.zeros_like(ct_sc)
        n_sc[...] = jnp.zeros_like(n_sc)
        m_sc[...] = jnp.zeros_like(m_sc)
        prev_sc[...] = jnp.zeros_like(prev_sc)

    is_meta = c == 0
    row = lax.broadcasted_iota(jnp.int32, (L, 1), 0)
    valid = jnp.logical_or(c > 0, row >= PAD)
    sidx = lax.broadcasted_iota(jnp.int32, (L, L), 0)
    ridx = lax.broadcasted_iota(jnp.int32, (L, L), 1)
    causal = ridx <= sidx
    tri = causal.astype(BF16)
    elems = range(nbps)
    pairs = [(bb, h) for bb in elems for h in range(ML_HEADS)]
    sl = lambda h: slice(h * ML_DH, (h + 1) * ML_DH)
    st = lambda bb, h: bb * 8 + h

    q, k, v, ig, bcs, b_t, ig_t = {}, {}, {}, {}, {}, {}, {}
    for bb in elems:
        qk_pre = jnp.where(is_meta, zm_ref[0, :, C_MQK:C_MQK + 2 * ML_WIDTH], xqk_ref[bb])
        v[bb] = jnp.where(is_meta, zm_ref[0, :, C_MV:C_MV + ML_WIDTH], xv_ref[bb])
        gates = jnp.where(is_meta, gm_ref[0], xg_ref[bb])
        prev = prev_sc[bb * L:(bb + 1) * L, :]
        ext = jnp.concatenate([qk_pre, prev], axis=0)
        prev_sc[bb * L:(bb + 1) * L, :] = qk_pre
        shifted = jnp.dot(sh_ref[...], ext, preferred_element_type=F32)
        acc = cb_ref[...] + cw_ref[CONV_K - 1:CONV_K, :] * qk_pre.astype(F32)
        for s in range(1, CONV_K):
            acc = acc + cw_ref[CONV_K - 1 - s:CONV_K - s, :] * shifted[(s - 1) * L:s * L, :]
        qk = acc * _sigmoid(acc)
        qk = jnp.where(valid, qk, 0.0)
        q[bb] = qk[:, :ML_WIDTH]
        k[bb] = qk[:, ML_WIDTH:] * (ML_DH ** -0.5)

        igv = gates[:, :LANES] + gb_ref[:, :LANES]
        fg = gates[:, LANES:] + gb_ref[:, LANES:]
        lf = jnp.minimum(fg, 0.0) - jnp.log1p(jnp.exp(-jnp.abs(fg)))
        igv = jnp.where(valid, igv, NEG)
        lf = jnp.where(valid, lf, 0.0)
        lf_hi = lf.astype(BF16)
        lf_lo = (lf - lf_hi.astype(F32)).astype(BF16)
        bcs[bb] = (jnp.dot(tri, lf_hi, preferred_element_type=F32)
                   + jnp.dot(tri, lf_lo, preferred_element_type=F32))
        ig[bb] = igv
        b_t[bb] = bcs[bb].T
        ig_t[bb] = igv.T

    bc, m_h, dlog, inter, m_s = {}, {}, {}, {}, {}
    for p in pairs:
        bb, h = p
        bc[p] = bcs[bb][:, h:h + 1]
        m_h[p] = m_sc[st(bb, h):st(bb, h) + 1, 0:1]
        dlog[p] = jnp.where(causal, (bc[p] - b_t[bb][h:h + 1, :]) + ig_t[bb][h:h + 1, :], NEG)
        inter[p] = bc[p] + m_h[p]
    for p in pairs:
        m_s[p] = jnp.maximum(inter[p], jnp.max(dlog[p], axis=1, keepdims=True))

    w_intra, w_inter, qb, kb, s, qc = {}, {}, {}, {}, {}, {}
    for p in pairs:
        bb, h = p
        w_intra[p] = jnp.exp(dlog[p] - m_s[p])
        w_inter[p] = jnp.exp(inter[p] - m_s[p])
        qb[p] = q[bb][:, sl(h)].astype(BF16)
        kb[p] = k[bb][:, sl(h)].astype(BF16)
    for p in pairs:
        bb, h = p
        s[p] = lax.dot_general(qb[p], kb[p], (((1,), (1,)), ((), ())), preferred_element_type=F32) * w_intra[p]
        qc[p] = jnp.dot(qb[p], ct_sc[bb * ML_HEADS + h].astype(BF16), preferred_element_type=F32)

    hh = {}
    for p in pairs:
        bb, h = p
        nrow = n_sc[st(bb, h):st(bb, h) + 1, :]
        num = w_inter[p] * qc[p] + jnp.dot(s[p].astype(BF16), v[bb][:, sl(h)], preferred_element_type=F32)
        den = (w_inter[p] * jnp.sum(q[bb][:, sl(h)] * nrow, axis=1, keepdims=True)
               + jnp.sum(s[p], axis=1, keepdims=True))
        hh[p] = num / jnp.maximum(jnp.abs(den), jnp.exp(-m_s[p]))

    for p in pairs:
        bb, h = p
        b_last = bc[p][L - 1:L, :]
        upd = (b_last - bc[p]) + ig[bb][:, h:h + 1]
        m_new = jnp.maximum(b_last + m_h[p], jnp.max(upd, axis=0, keepdims=True))
        w_old = jnp.exp(b_last + m_h[p] - m_new)
        w_r = jnp.exp(upd - m_new)
        vw = (v[bb][:, sl(h)].astype(F32) * w_r).astype(BF16)
        ci = bb * ML_HEADS + h
        ct_sc[ci] = w_old * ct_sc[ci] + lax.dot_general(kb[p], vw, (((0,), (0,)), ((), ())),
                                                         preferred_element_type=F32)
        r = st(bb, h)
        n_sc[r:r + 1, :] = w_old * n_sc[r:r + 1, :] + jnp.sum(k[bb][:, sl(h)] * w_r, axis=0, keepdims=True)
        m_sc[r:r + 1, :] = jnp.broadcast_to(m_new, (1, LANES))

    outs = {}
    for p in pairs:
        bb, h = p
        hn = hh[p] * lax.rsqrt(jnp.mean(hh[p] * hh[p], axis=1, keepdims=True) + LN_EPS) * ng_ref[:, sl(h)]
        outs[p] = _sigmoid(xo_ref[bb, :, sl(h)].astype(F32)) * hn

    @pl.when(c > 0)
    def _():
        for bb in elems:
            o_ref[bb] = jnp.concatenate([outs[(bb, h)] for h in range(ML_HEADS)], axis=1).astype(BF16)


def _mlstm(zb, gates, zbm, gatesm, conv_w, conv_b, gate_b, ng, *, nbps):
    B, S, _ = zb.shape
    nc = S // BLOCK + 1
    t = np.arange(BLOCK)
    shift = np.zeros(((CONV_K - 1) * BLOCK, 2 * BLOCK), np.float32)
    for s in range(1, CONV_K):
        shift[(s - 1) * BLOCK + t, np.where(t >= s, t - s, 2 * BLOCK + t - s)] = 1.0

    def xc(c):
        return jnp.maximum(c - 1, 0)

    return pl.pallas_call(
        functools.partial(_mlstm_kernel, nbps=nbps),
        grid=(B // nbps, nc),
        in_specs=[
            pl.BlockSpec((nbps, BLOCK, 2 * ML_WIDTH), lambda b, c: (b, xc(c), C_MQK // (2 * ML_WIDTH))),
            pl.BlockSpec((nbps, BLOCK, ML_WIDTH), lambda b, c: (b, xc(c), C_MV // ML_WIDTH)),
            pl.BlockSpec((nbps, BLOCK, ML_WIDTH), lambda b, c: (b, xc(c), C_MO // ML_WIDTH)),
            pl.BlockSpec((nbps, BLOCK, GATE_W), lambda b, c: (b, xc(c), 0)),
            pl.BlockSpec((1, BLOCK, ZB_W), lambda b, c: (0, 0, 0)),
            pl.BlockSpec((1, BLOCK, GATE_W), lambda b, c: (0, 0, 0)),
            pl.BlockSpec((CONV_K, 2 * ML_WIDTH), lambda b, c: (0, 0)),
            pl.BlockSpec((1, 2 * ML_WIDTH), lambda b, c: (0, 0)),
            pl.BlockSpec((1, GATE_W), lambda b, c: (0, 0)),
            pl.BlockSpec((1, ML_WIDTH), lambda b, c: (0, 0)),
            pl.BlockSpec(((CONV_K - 1) * BLOCK, 2 * BLOCK), lambda b, c: (0, 0)),
        ],
        out_specs=pl.BlockSpec((nbps, BLOCK, ML_WIDTH), lambda b, c: (b, xc(c), 0)),
        out_shape=jax.ShapeDtypeStruct((B, S, ML_WIDTH), BF16),
        scratch_shapes=[
            pltpu.VMEM((nbps * ML_HEADS, ML_DH, ML_DH), F32),
            pltpu.VMEM((nbps * 8, ML_DH), F32),
            pltpu.VMEM((nbps * 8, LANES), F32),
            pltpu.VMEM((nbps * BLOCK, 2 * ML_WIDTH), BF16),
        ],
        compiler_params=_cparams(2),
        name="mlstm",
    )(zb, zb, zb, gates, zbm, gatesm, conv_w, conv_b, gate_b, ng, jnp.asarray(shift, BF16))


def _outproj_kernel(x_ref, ya_ref, ym_ref, ga_ref, gmm_ref, eg_ref, eb_ref, wa_ref, wm_ref, wo_ref,
                    g1_ref, b1_ref, wr_ref, br_ref,
                    h1_ref, eid_ref, gate_ref, rank_ref, cnt_ref, run_sc, *, tm, ts):
    i = pl.program_id(0)
    subs = range(tm // ts)
    rows = lambda j: slice(j * ts, (j + 1) * ts)

    @pl.when(i == 0)
    def _():
        run_sc[...] = jnp.zeros_like(run_sc)

    sig = lambda g: 0.5 * jnp.tanh(0.5 * g) + 0.5
    pa = {j: jnp.dot(ya_ref[rows(j), :], wa_ref[...], preferred_element_type=F32) for j in subs}
    pm = {j: jnp.dot(ym_ref[rows(j), :], wm_ref[...], preferred_element_type=F32) for j in subs}
    h0 = {j: _ln(x_ref[rows(j), :], eg_ref[...], eb_ref[...]) for j in subs}
    merged = {j: sig(ga_ref[rows(j), :]) * pa[j].astype(BF16) + sig(gmm_ref[rows(j), :]) * pm[j].astype(BF16)
              for j in subs}
    mix = {j: jnp.dot(merged[j], wo_ref[...], preferred_element_type=F32) for j in subs}
    h1 = {j: _ln(DN_ALPHA * h0[j] + mix[j], g1_ref[...], b1_ref[...]) for j in subs}
    for j in subs:
        _store_row_tiles(h1_ref.at[pl.ds(j * ts * ROW_TILE, ts * ROW_TILE)], h1[j])

    logits = {}
    for j in subs:
        h_hi = h1[j].astype(BF16)
        h_mid = (h1[j] - h_hi.astype(F32)).astype(BF16)
        hh = jnp.dot(h_hi, wr_ref[...], preferred_element_type=F32)
        logits[j] = (hh[:, :LANES] + hh[:, LANES:]
                     + jnp.dot(h_mid, wr_ref[:, :LANES], preferred_element_type=F32) + br_ref[...])

    lane = lax.broadcasted_iota(jnp.int32, (ts, LANES), 1)
    r_i = lax.broadcasted_iota(jnp.int32, (ts, ts), 0)
    c_i = lax.broadcasted_iota(jnp.int32, (ts, ts), 1)
    strict = (c_i < r_i).astype(BF16)
    sel_e, ex, den, onehots, oh = {}, {}, {}, {}, {}
    for j in subs:
        work = logits[j]
        es, vs = [], []
        for _ in range(TOP_K):
            mv = jnp.max(work, axis=1, keepdims=True)
            e = jnp.min(jnp.where(work == mv, lane, LANES), axis=1, keepdims=True)
            es.append(e)
            vs.append(mv)
            work = jnp.where(lane == e, -jnp.inf, work)
        sel_e[j] = es
        ex[j] = [jnp.exp(v - vs[0]) for v in vs]
        den[j] = ex[j][0] + ex[j][1] + ex[j][2] + ex[j][3]
        onehots[j] = [lane == e for e in es]
        acc = jnp.zeros((ts, LANES), F32)
        for o in onehots[j]:
            acc = acc + o.astype(F32)
        oh[j] = acc

    run = run_sc[...]
    for j in subs:
        before = jnp.dot(strict, oh[j].astype(BF16), preferred_element_type=F32) + run
        run = run + jnp.sum(oh[j], axis=0, keepdims=True)
        eid = jnp.zeros((ts, LANES), F32)
        gate = jnp.zeros((ts, LANES), F32)
        rank = jnp.zeros((ts, LANES), F32)
        for kk in range(TOP_K):
            here = lane == kk
            eid = jnp.where(here, sel_e[j][kk].astype(F32), eid)
            gate = jnp.where(here, ex[j][kk] / den[j], gate)
            rk = jnp.sum(jnp.where(onehots[j][kk], before, 0.0), axis=1, keepdims=True)
            rank = jnp.where(here, rk, rank)
        gate_ref[rows(j), :] = gate
        eid_ref[:, rows(j)] = eid.T[0:8, :]
        rank_ref[:, rows(j)] = rank.T[0:8, :]
    run_sc[...] = run
    cnt_ref[...] = run


def _outproj(x2, yatt, yml, zb2, eg, eb, wa, wm, wo, g1, b1, wr, br, *, tm, ts):
    N, D = x2.shape
    vec = lambda w: pl.BlockSpec((1, w), lambda i: (0, 0))
    full = lambda a, b: pl.BlockSpec((a, b), lambda i: (0, 0))
    tile = lambda w: pl.BlockSpec((tm, w), lambda i: (i, 0))
    return pl.pallas_call(
        functools.partial(_outproj_kernel, tm=tm, ts=ts),
        grid=(N // tm,),
        in_specs=[
            tile(D), tile(ATT_HEADS * ATT_V_DIM), tile(ML_WIDTH),
            pl.BlockSpec((tm, D), lambda i: (i, C_GA // D)),
            pl.BlockSpec((tm, D), lambda i: (i, C_GM // D)),
            vec(D), vec(D),
            full(ATT_HEADS * ATT_V_DIM, D), full(ML_WIDTH, D), full(D, D),
            vec(D), vec(D), full(D, 2 * LANES), vec(LANES),
        ],
        out_specs=[pl.BlockSpec((tm * ROW_TILE, LANES), lambda i: (i, 0)),
                   pl.BlockSpec((8, tm), lambda i: (0, i)), tile(LANES),
                   pl.BlockSpec((8, tm), lambda i: (0, i)), vec(LANES)],
        out_shape=[
            jax.ShapeDtypeStruct((N * ROW_TILE, LANES), F32),
            jax.ShapeDtypeStruct((8, N), F32),
            jax.ShapeDtypeStruct((N, LANES), F32),
            jax.ShapeDtypeStruct((8, N), F32),
            jax.ShapeDtypeStruct((1, LANES), F32),
        ],
        scratch_shapes=[pltpu.VMEM((1, LANES), F32)],
        compiler_params=_cparams(1),
        name="outproj_router",
    )(x2, yatt, yml, zb2, zb2, eg, eb, wa, wm, wo, g1, b1, wr, br)


def _row_copies(src_row, dst_row, sem, n, issue):
    def body(r, carry):
        for kk in range(TOP_K):
            cp = pltpu.make_async_copy(src_row(r, kk), dst_row(r, kk), sem)
            if issue:
                cp.start(priority=kk % 2)
            else:
                cp.wait()
        return carry
    lax.fori_loop(0, n, body, 0, unroll=8)


def _dispatch_kernel(pad_pos_ref, pad_len_ref, d_ref, h_ref, xs_ref, zero_sc, sem, *, tm, max_pad):
    @pl.when(pl.program_id(0) == 0)
    def _():
        zero_sc[...] = jnp.zeros_like(zero_sc)
        bits = [1 << b for b in reversed(range(max_pad.bit_length()))]

        def pad_copies(e, issue):
            pos = pad_pos_ref[e]
            length = pad_len_ref[e]
            for bit in bits:
                take = (length & bit) != 0

                @pl.when(take)
                def _(pos=pos, bit=bit):
                    dst = xs_ref.at[pl.ds(pl.multiple_of(pos * ROW_TILE, ROW_TILE), bit * ROW_TILE)]
                    cp = pltpu.make_async_copy(zero_sc.at[pl.ds(0, bit * ROW_TILE)], dst, sem.at[1])
                    if issue:
                        cp.start()
                    else:
                        cp.wait()
                pos = pos + jnp.where(take, bit, 0)

        def per_expert(e, carry):
            pad_copies(e, True)
            pad_copies(e, False)
            return carry

        lax.fori_loop(0, N_EXPERTS, per_expert, 0)

    src = lambda r, kk: _row_tile(h_ref, r)
    dst = lambda r, kk: _row_tile(xs_ref, d_ref[0, 0, kk * tm + r])
    _row_copies(src, dst, sem.at[0], tm, True)
    _row_copies(src, dst, sem.at[0], tm, False)


def _dispatch(pad_pos, pad_len, dest3, h1t, rows_out, *, tm, max_pad):
    N = h1t.shape[0] // ROW_TILE
    zero_rows = 1 << (max_pad.bit_length() - 1)
    return pl.pallas_call(
        functools.partial(_dispatch_kernel, tm=tm, max_pad=max_pad),
        grid_spec=pltpu.PrefetchScalarGridSpec(
            num_scalar_prefetch=2,
            grid=(N // tm,),
            in_specs=[
                pl.BlockSpec((1, 1, tm * TOP_K), lambda i, pp, pn: (i, 0, 0), memory_space=pltpu.SMEM),
                pl.BlockSpec((tm * ROW_TILE, LANES), lambda i, pp, pn: (i, 0)),
            ],
            out_specs=pl.BlockSpec(memory_space=pl.ANY),
            scratch_shapes=[pltpu.VMEM((zero_rows * ROW_TILE, LANES), F32), pltpu.SemaphoreType.DMA((2,))],
        ),
        out_shape=jax.ShapeDtypeStruct((rows_out * ROW_TILE, LANES), F32),
        compiler_params=_cparams(1),
        name="moe_dispatch",
    )(pad_pos, pad_len, dest3, h1t)


def _deinterleave_kernel(w_ref, p_ref, wg_ref, wl_ref):
    for c in range(w_ref.shape[2] // (2 * LANES)):
        blk = w_ref[0, :, c * 2 * LANES:(c + 1) * 2 * LANES].astype(BF16)
        r = jnp.dot(blk, p_ref[...], preferred_element_type=F32).astype(BF16)
        wg_ref[0, :, c * LANES:(c + 1) * LANES] = r[:, :LANES]
        wl_ref[0, :, c * LANES:(c + 1) * LANES] = r[:, LANES:]


def _deinterleave(w_gu):
    E, D, F2 = w_gu.shape
    F = F2 // 2
    j = np.arange(2 * LANES)
    sel = np.zeros((2 * LANES, 2 * LANES), np.float32)
    sel[j, (j % 2) * LANES + j // 2] = 1.0
    out = jax.ShapeDtypeStruct((E, D, F), BF16)
    return pl.pallas_call(
        _deinterleave_kernel,
        grid=(E,),
        in_specs=[
            pl.BlockSpec((1, D, F2), lambda e: (e, 0, 0)),
            pl.BlockSpec((2 * LANES, 2 * LANES), lambda e: (0, 0)),
        ],
        out_specs=[pl.BlockSpec((1, D, F), lambda e: (e, 0, 0))] * 2,
        out_shape=[out, out],
        compiler_params=_cparams(1),
        name="deinterleave_wgu",
    )(w_gu, jnp.asarray(sel, BF16))


def _expert_kernel(be_ref, nu_ref, x_ref, wg_ref, wl_ref, bg_ref, bl_ref, wd_ref, bd_ref, y_ref, *, tm):
    del be_ref

    @pl.when(pl.program_id(0) < nu_ref[0])
    def _():
        xb = _load_row_tiles(x_ref, tm).astype(BF16)
        hg = jnp.dot(xb, wg_ref[0], preferred_element_type=F32) + bg_ref[0]
        hl = jnp.dot(xb, wl_ref[0], preferred_element_type=F32) + bl_ref[0]
        glu = jnp.minimum(hg, SWIGLU_LIMIT)
        lin = jnp.clip(hl, -SWIGLU_LIMIT, SWIGLU_LIMIT)
        act = glu * _sigmoid(SWIGLU_ALPHA * glu) * (lin + 1.0)
        _store_row_tiles(y_ref, jnp.dot(act.astype(BF16), wd_ref[0], preferred_element_type=F32) + bd_ref[0])


def _experts(blk_exp, n_used, xs, wg, wl, bg, bl, wd, bd, *, tm):
    MP = xs.shape[0] // ROW_TILE
    _, D, F = wg.shape
    rows = pl.BlockSpec((tm * ROW_TILE, LANES), lambda i, be, nu: (jnp.minimum(i, nu[0] - 1), 0))
    wsp = lambda a, b: pl.BlockSpec((1, a, b), lambda i, be, nu: (be[i], 0, 0))
    return pl.pallas_call(
        functools.partial(_expert_kernel, tm=tm),
        grid_spec=pltpu.PrefetchScalarGridSpec(
            num_scalar_prefetch=2,
            grid=(MP // tm,),
            in_specs=[rows, wsp(D, F), wsp(D, F), wsp(1, F), wsp(1, F), wsp(F, D), wsp(1, D)],
            out_specs=rows,
        ),
        out_shape=jax.ShapeDtypeStruct(xs.shape, F32),
        compiler_params=_cparams(1),
        name="moe_experts",
    )(blk_exp, n_used, xs, wg, wl, bg, bl, wd, bd)


def _combine_kernel(dcur_ref, dnxt_ref, h1_ref, gate_ref, g2_ref, b2_ref, ys_ref, o_ref, ybuf, sem, *, tm, nsteps):
    i = pl.program_id(0)
    slot = i % 2

    def copies(d_ref, s, issue):
        src = lambda r, kk: _row_tile(ys_ref, d_ref[0, 0, kk * tm + r])
        dst = lambda r, kk: _row_tile(ybuf.at[s, kk], r)
        _row_copies(src, dst, sem.at[s], tm, issue)

    @pl.when(i == 0)
    def _():
        copies(dcur_ref, 0, True)

    @pl.when(i + 1 < nsteps)
    def _():
        copies(dnxt_ref, 1 - slot, True)

    copies(dcur_ref, slot, False)
    acc = DN_ALPHA * _load_row_tiles(h1_ref, tm)
    for kk in range(TOP_K):
        acc = acc + gate_ref[:, kk:kk + 1] * _load_row_tiles(ybuf.at[slot, kk], tm)
    o_ref[...] = _ln(acc, g2_ref[...], b2_ref[...])


def _combine(dest3, h1t, gate, g2, b2, ys, *, tm):
    N, D = h1t.shape[0] // ROW_TILE, D_MODEL
    nsteps = N // tm
    dspec = lambda f: pl.BlockSpec((1, 1, tm * TOP_K), f, memory_space=pltpu.SMEM)
    return pl.pallas_call(
        functools.partial(_combine_kernel, tm=tm, nsteps=nsteps),
        grid=(nsteps,),
        in_specs=[
            dspec(lambda i: (i, 0, 0)),
            dspec(lambda i: (jnp.minimum(i + 1, nsteps - 1), 0, 0)),
            pl.BlockSpec((tm * ROW_TILE, LANES), lambda i: (i, 0)),
            pl.BlockSpec((tm, LANES), lambda i: (i, 0)),
            pl.BlockSpec((1, D), lambda i: (0, 0)),
            pl.BlockSpec((1, D), lambda i: (0, 0)),
            pl.BlockSpec(memory_space=pl.ANY),
        ],
        out_specs=pl.BlockSpec((tm, D), lambda i: (i, 0)),
        out_shape=jax.ShapeDtypeStruct((N, D), F32),
        scratch_shapes=[pltpu.VMEM((2, TOP_K, tm * ROW_TILE, LANES), F32), pltpu.SemaphoreType.DMA((2,))],
        compiler_params=_cparams(1),
        name="moe_combine",
    )(dest3, dest3, h1t, gate, g2, b2, ys)


def _rotary_column_order():
    idx = np.empty((ATT_HEADS * LANES,), np.int32)
    for h in range(ATT_HEADS):
        for half in range(2):
            for sub in range(2):
                for dd in range(32):
                    idx[h * LANES + half * 64 + sub * 32 + dd] = (2 * h + sub) * ATT_QK_DIM + half * 32 + dd
    return idx


def _rotary_tables(pos):
    half = ATT_QK_DIM // 2
    inv_freq = ROPE_THETA ** (-jnp.arange(half, dtype=F32) / half)
    ang = pos.astype(F32)[:, None] * inv_freq[None, :]
    cos = jnp.tile(jnp.cos(ang), (1, 4))
    sin = jnp.tile(jnp.sin(ang), (1, 4))
    sign = jnp.where(jnp.arange(LANES) < 64, -1.0, 1.0).astype(F32)
    return cos, sin * sign[None, :]


def _pick(n, prefs):
    for t in prefs:
        if n % t == 0:
            return t
    raise ValueError(f"unsupported size {n}")


def kernel(x, meta, emb_ln_g, emb_ln_b, w_in, conv_w, conv_b, gate_bias, lam_q1, lam_k1, lam_q2, lam_k2,
           att_norm_g, ml_norm_g, w_att_out, w_ml_out, w_o, ln1_g, ln1_b, w_router, b_router,
           w_gu, b_gu, w_down, b_down, ln2_g, ln2_b):
    B, S, D = x.shape
    assert D == D_MODEL and S % 512 == 0 and w_in.shape[0] == DEPTH
    N = B * S
    row2 = lambda a: a.reshape(1, -1)

    w = w_in[0]
    o_aq, o_ak, o_av, o_mqk, o_mv, o_mo, o_gt, o_ga, o_gm = 0, 512, 1024, 1536, 2560, 3072, 3584, 3592, 4616
    perm = _rotary_column_order()
    gpad = jnp.zeros((D, LANES - ML_HEADS), F32)
    w_all = jnp.concatenate([
        w[:, o_ga:o_ga + D], w[:, o_gm:o_gm + D], w[:, o_mqk:o_mqk + 2 * ML_WIDTH],
        w[:, o_aq:o_aq + 512][:, perm], w[:, o_ak:o_ak + 512][:, perm], w[:, o_av:o_av + 512],
        w[:, o_mv:o_mv + ML_WIDTH], w[:, o_mo:o_mo + ML_WIDTH],
        w[:, o_gt:o_gt + ML_HEADS], gpad, w[:, o_gt + ML_HEADS:o_gt + 2 * ML_HEADS], gpad,
    ], axis=1).astype(BF16)
    gb = gate_bias[0]
    zpad = jnp.zeros((LANES - ML_HEADS,), F32)
    gate_b = jnp.concatenate([gb[:ML_HEADS], zpad, gb[ML_HEADS:], zpad]).reshape(1, GATE_W)

    cos_x, sin_x = _rotary_tables(N_META + jnp.arange(S))
    cos_m, sin_m = _rotary_tables(jnp.maximum(jnp.arange(BLOCK) - PAD, 0))

    eg, eb = row2(emb_ln_g), row2(emb_ln_b)

    tm_in = 512
    zb, gates = _inproj(x, eg, eb, w_all, cos_x, sin_x, tm=tm_in)
    xm = jnp.concatenate([jnp.zeros((PAD, D), x.dtype), meta.astype(x.dtype)], axis=0)[None]
    zbm, gatesm = _inproj(xm, eg, eb, w_all, cos_m, sin_m, tm=BLOCK, first_valid_row=PAD)

    tq = 512
    yatt = _attention(zb, zbm, row2(lam_q1[0]), row2(lam_k1[0]), row2(lam_q2[0]), row2(lam_k2[0]),
                      row2(att_norm_g[0]), tq=tq)
    yml = _mlstm(zb, gates, zbm, gatesm, conv_w[0], row2(conv_b[0]), gate_b, row2(ml_norm_g[0]),
                 nbps=2 if B % 2 == 0 else 1)

    wr32 = jnp.concatenate([w_router[0], jnp.zeros((D, LANES - N_EXPERTS), F32)], axis=1)
    wr_hi = wr32.astype(BF16)
    wr = jnp.concatenate([wr_hi, (wr32 - wr_hi.astype(F32)).astype(BF16)], axis=1)
    br = jnp.concatenate([b_router[0], jnp.full((LANES - N_EXPERTS,), NEG, F32)]).reshape(1, LANES)
    tm_out = 1024
    h1, eid, gate, rank, cnt = _outproj(
        x.reshape(N, D), yatt.reshape(N, -1), yml.reshape(N, -1), zb.reshape(N, ZB_W), eg, eb,
        w_att_out[0].astype(BF16), w_ml_out[0].astype(BF16), w_o[0].astype(BF16),
        row2(ln1_g[0]), row2(ln1_b[0]), wr, br, tm=tm_out, ts=512)

    tm_e = 512
    M = N * TOP_K
    nb = (M + N_EXPERTS * (tm_e - 1) + tm_e - 1) // tm_e
    counts = cnt[0, :N_EXPERTS].astype(jnp.int32)
    nblk = (counts + tm_e - 1) // tm_e
    cum = jnp.cumsum(nblk)
    pstart = (cum - nblk) * tm_e
    eid_t = eid[:TOP_K].astype(jnp.int32)
    base = jnp.sum(jnp.where(eid_t[None] == jnp.arange(N_EXPERTS)[:, None, None], pstart[:, None, None], 0), axis=0)
    dest = base + rank[:TOP_K].astype(jnp.int32)
    n_used = cum[-1:].astype(jnp.int32)
    blk = jnp.minimum(jnp.arange(nb, dtype=jnp.int32), n_used[0] - 1)
    blk_exp = jnp.sum(blk[:, None] >= cum[None, :], axis=1).astype(jnp.int32)

    tm_d = 256
    dest3 = dest.reshape(TOP_K, N // tm_d, tm_d).transpose(1, 0, 2).reshape(N // tm_d, 1, TOP_K * tm_d)
    xs = _dispatch(pstart + counts, nblk * tm_e - counts, dest3, h1, nb * tm_e, tm=tm_d, max_pad=tm_e - 1)

    wg, wl = _deinterleave(w_gu[0])
    bgu = b_gu[0]
    ys = _experts(blk_exp, n_used, xs, wg, wl,
                  bgu[:, None, 0::2], bgu[:, None, 1::2],
                  w_down[0].astype(BF16), b_down[0][:, None, :], tm=tm_e)

    out = _combine(dest3, h1, gate, row2(ln2_g[0]), row2(ln2_b[0]), ys, tm=tm_d)
    return out.reshape(B, S, D)
```

```python
import functools
import math

import jax
import jax.numpy as jnp
import numpy as np
from jax import lax
from jax.experimental import pallas as pl
from jax.experimental.pallas import tpu as pltpu

D_MODEL = 1024
N_META = 16
BLOCK = 128
PAD = BLOCK - N_META
NEG = -1e30
LN_EPS = 1e-5
ATT_HEADS = 4
ATT_QK_DIM = 64
ATT_V_DIM = 128
ROPE_THETA = 10000.0
ML_HEADS = 4
ML_DH = 128
ML_WIDTH = ML_HEADS * ML_DH
CONV_K = 4
N_EXPERTS = 32
TOP_K = 4
SWIGLU_LIMIT = 7.0
SWIGLU_ALPHA = 1.702
DEPTH = 1
DN_ALPHA = (2 * DEPTH) ** 0.25
LAMBDA_INIT = 0.8 - 0.6 * math.exp(-0.3 * 0)

LANES = 128
C_GA, C_GM, C_MQK, C_AQ, C_AK, C_AV, C_MV, C_MO = 0, 1024, 2048, 3072, 3584, 4096, 4608, 5120
ZB_W = 5632
GATE_W = 2 * LANES
W_ALL = ZB_W + GATE_W
CHUNK = 512

VMEM_LIMIT = 56 * 1024 * 1024

BF16 = jnp.bfloat16
F32 = jnp.float32


def _cparams(n_axes):
    return pltpu.CompilerParams(dimension_semantics=("arbitrary",) * n_axes, vmem_limit_bytes=VMEM_LIMIT)


def _ln(x, g, b):
    mu = jnp.mean(x, axis=-1, keepdims=True)
    xc = x - mu
    var = jnp.mean(xc * xc, axis=-1, keepdims=True)
    return xc * lax.rsqrt(var + LN_EPS) * g + b


def _sigmoid(x):
    return 1.0 / (1.0 + jnp.exp(-x))


ROW_TILE = D_MODEL // LANES


def _store_row_tiles(ref, val):
    tm = val.shape[0]
    for g in range(ROW_TILE):
        ref[pl.ds(g, tm, stride=ROW_TILE), :] = val[:, g * LANES:(g + 1) * LANES]


def _load_row_tiles(ref, tm):
    return jnp.concatenate([ref[pl.ds(g, tm, stride=ROW_TILE), :] for g in range(ROW_TILE)], axis=1)


def _row_tile(ref, r):
    return ref.at[pl.ds(pl.multiple_of(r * ROW_TILE, ROW_TILE), ROW_TILE)]


def _inproj_kernel(x_ref, g_ref, b_ref, w_ref, cos_ref, sin_ref, zb_ref, gt_ref, *, first_valid_row):
    x = x_ref[0]
    tm = x.shape[0]
    hb = _ln(x, g_ref[...], b_ref[...]).astype(BF16)
    cos = cos_ref[...]
    sin = sin_ref[...]
    if first_valid_row:
        rowmask = lax.broadcasted_iota(jnp.int32, (tm, 1), 0) >= first_valid_row
    for c in range(ZB_W // CHUNK):
        z = jnp.dot(hb, w_ref[:, c * CHUNK:(c + 1) * CHUNK], preferred_element_type=F32)
        if c * CHUNK in (C_AQ, C_AK):
            parts = []
            for h in range(ATT_HEADS):
                zh = z[:, h * LANES:(h + 1) * LANES]
                parts.append(zh * cos + pltpu.roll(zh, 64, 1) * sin)
            z = jnp.concatenate(parts, axis=1)
            if c * CHUNK == C_AQ:
                z = z * (ATT_QK_DIM ** -0.5 * math.log2(math.e))
        if first_valid_row:
            z = jnp.where(rowmask, z, 0.0)
        zb_ref[0, :, c * CHUNK:(c + 1) * CHUNK] = z.astype(BF16)
    zg = jnp.dot(hb, w_ref[:, ZB_W:W_ALL], preferred_element_type=F32)
    if first_valid_row:
        zg = jnp.where(rowmask, zg, 0.0)
    gt_ref[0] = zg


def _inproj(x3, g, b, w, cos, sin, *, tm, first_valid_row=0):
    B, S, D = x3.shape
    nt = S // tm
    return pl.pallas_call(
        functools.partial(_inproj_kernel, first_valid_row=first_valid_row),
        grid=(B, nt),
        in_specs=[
            pl.BlockSpec((1, tm, D), lambda bi, i: (bi, i, 0)),
            pl.BlockSpec((1, D), lambda bi, i: (0, 0)),
            pl.BlockSpec((1, D), lambda bi, i: (0, 0)),
            pl.BlockSpec((D, W_ALL), lambda bi, i: (0, 0)),
            pl.BlockSpec((tm, LANES), lambda bi, i: (i, 0)),
            pl.BlockSpec((tm, LANES), lambda bi, i: (i, 0)),
        ],
        out_specs=[
            pl.BlockSpec((1, tm, ZB_W), lambda bi, i: (bi, i, 0)),
            pl.BlockSpec((1, tm, GATE_W), lambda bi, i: (bi, i, 0)),
        ],
        out_shape=[
            jax.ShapeDtypeStruct((B, S, ZB_W), BF16),
            jax.ShapeDtypeStruct((B, S, GATE_W), F32),
        ],
        compiler_params=_cparams(2),
        name="inproj",
    )(x3, g, b, w, cos, sin)


def _attn_kernel(q_ref, k_ref, v_ref, km_ref, vm_ref, lq1_ref, lk1_ref, lq2_ref, lk2_ref, ng_ref, o_ref,
                 sa_sc, sb_sc, pa_sc, pb_sc, aa_sc, ab_sc, m_sc, acc_sc, *, tq, tk, rc, nh):
    assert tq == 2 * tk
    qi = pl.program_id(2)
    rows = 2 * tq
    heads = range(nh)
    cols = lambda hh: slice(hh * LANES, (hh + 1) * LANES)
    lane = lax.broadcasted_iota(jnp.int32, (tq, LANES), 1)
    is_map0 = (lane & 63) < 32
    qq = {}
    for hh in heads:
        q = q_ref[0, :, cols(hh)]
        zero = jnp.zeros_like(q)
        qq[hh] = jnp.concatenate([jnp.where(is_map0, q, zero), jnp.where(is_map0, zero, q)], axis=0)

    def qk_stage(hh, kblk, s_ref, col0=0):
        s_ref[hh, :, col0:col0 + kblk.shape[0]] = lax.dot_general(qq[hh], kblk, (((1,), (1,)), ((), ())),
                                                                  preferred_element_type=F32)

    def pv_stage(hh, p_ref, a_ref, vblk, first=False):
        width = vblk.shape[0]
        v1 = jnp.concatenate([vblk, jnp.ones_like(vblk)], axis=1)
        pv = jnp.dot(p_ref[hh, :, 0:width], v1, preferred_element_type=F32)
        if first:
            acc_sc[hh] = pv
        else:
            a = a_ref[hh]
            acc_sc[hh] = jnp.concatenate([a, a], axis=1) * acc_sc[hh] + pv

    def sm_stage(hh, s_ref, p_ref, a_ref, width, vis, first=False):
        nrep = width // LANES
        for r0 in range(0, rows, rc):
            kind = vis(r0)
            if kind == "none":
                continue
            s = s_ref[hh, r0:r0 + rc, 0:width]
            if kind != "all":
                s = kind(s)
            m_cur = jnp.max(s, axis=1, keepdims=True)
            if first:
                m_new = jnp.broadcast_to(m_cur, (rc, LANES))
            else:
                m_prev = m_sc[hh, r0:r0 + rc, :]
                m_new = jnp.maximum(m_prev, m_cur)
                a_ref[hh, r0:r0 + rc, :] = jnp.exp2(m_prev - m_new)
            m_sc[hh, r0:r0 + rc, :] = m_new
        for r0 in range(0, rows, rc):
            kind = vis(r0)
            if kind == "none":
                p_ref[hh, r0:r0 + rc, 0:width] = jnp.zeros((rc, width), BF16)
                a_ref[hh, r0:r0 + rc, :] = jnp.ones((rc, LANES), F32)
                continue
            s = s_ref[hh, r0:r0 + rc, 0:width]
            if kind != "all":
                s = kind(s)
            m_new = m_sc[hh, r0:r0 + rc, :]
            m_rep = m_new if nrep == 1 else jnp.concatenate([m_new] * nrep, axis=1)
            p_ref[hh, r0:r0 + rc, 0:width] = jnp.exp2(s - m_rep).astype(BF16)

    def first_vis(r0):
        q0 = r0 % tq
        def mask(s):
            rowq = q0 + lax.broadcasted_iota(jnp.int32, s.shape, 0)
            col = lax.broadcasted_iota(jnp.int32, s.shape, 1)
            keep = col >= tk + PAD
            if tk - 1 <= q0:
                keep = jnp.logical_or(keep, col < tk)
            else:
                keep = jnp.logical_or(keep, col <= rowq)
            return jnp.where(keep, s, NEG)
        return mask

    def diag_vis(d):
        def vis(r0):
            q0 = r0 % tq
            if d * tk + tk - 1 <= q0:
                return "all"
            if d * tk > q0 + rc - 1:
                return "none"
            def mask(s):
                rowq = q0 + lax.broadcasted_iota(jnp.int32, s.shape, 0)
                col = d * tk + lax.broadcasted_iota(jnp.int32, s.shape, 1)
                return jnp.where(col <= rowq, s, NEG)
            return mask
        return vis

    all_vis = lambda r0: "all"
    kblk = lambda hh, start: k_ref[0, pl.ds(start, tk), cols(hh)]
    vblk = lambda hh, start: v_ref[0, pl.ds(start, tk), cols(hh)]

    d0 = pl.multiple_of(qi * tq, tk)
    d1 = pl.multiple_of(qi * tq + tk, tk)
    for hh in heads:
        qk_stage(hh, kblk(hh, d0), sa_sc)
    for hh in heads:
        qk_stage(hh, km_ref[0, :, cols(hh)], sa_sc, col0=tk)
    for hh in heads:
        qk_stage(hh, kblk(hh, d1), sb_sc)
    for hh in heads:
        sm_stage(hh, sa_sc, pa_sc, None, tk + BLOCK, first_vis, first=True)
    for hh in heads:
        pv_stage(hh, pa_sc, None, jnp.concatenate([vblk(hh, d0), vm_ref[0, :, cols(hh)]], axis=0), first=True)
    for hh in heads:
        sm_stage(hh, sb_sc, pb_sc, ab_sc, tk, diag_vis(1))
    for hh in heads:
        qk_stage(hh, kblk(hh, 0), sa_sc)

    def pair(i, carry):
        a_start = pl.multiple_of(i * tq, tk)
        b_start = pl.multiple_of(i * tq + tk, tk)
        prev_b = pl.multiple_of(jnp.where(i == 0, d1, a_start - tk), tk)
        next_a = pl.multiple_of(jnp.minimum(a_start + tq, (qi - 1) * tq), tk)
        for hh in heads:
            qk_stage(hh, kblk(hh, b_start), sb_sc)
        for hh in heads:
            pv_stage(hh, pb_sc, ab_sc, vblk(hh, prev_b))
        for hh in heads:
            sm_stage(hh, sa_sc, pa_sc, aa_sc, tk, all_vis)
        for hh in heads:
            qk_stage(hh, kblk(hh, next_a), sa_sc)
        for hh in heads:
            pv_stage(hh, pa_sc, aa_sc, vblk(hh, a_start))
        for hh in heads:
            sm_stage(hh, sb_sc, pb_sc, ab_sc, tk, all_vis)
        return carry

    lax.fori_loop(0, qi, pair, 0)
    last_b = pl.multiple_of(jnp.where(qi == 0, d1, qi * tq - tk), tk)
    for hh in heads:
        pv_stage(hh, pb_sc, ab_sc, vblk(hh, last_b))

    s1 = jnp.sum(lq1_ref[...] * lk1_ref[...], axis=1, keepdims=True)
    s2 = jnp.sum(lq2_ref[...] * lk2_ref[...], axis=1, keepdims=True)
    lam = jnp.exp(s1) - jnp.exp(s2) + LAMBDA_INIT
    for hh in heads:
        o = acc_sc[hh, :, 0:ATT_V_DIM] / acc_sc[hh, :, ATT_V_DIM:]
        a = o[:tq] - lam * o[tq:]
        y = a * lax.rsqrt(jnp.mean(a * a, axis=1, keepdims=True) + LN_EPS) * ng_ref[...]
        o_ref[0, :, cols(hh)] = (y * (1.0 - LAMBDA_INIT)).astype(BF16)


def _attention(zb, zbm, lq1, lk1, lq2, lk2, ng, *, tq, nh):
    B, S, _ = zb.shape
    nq = S // tq
    tk = tq // 2
    w = nh * LANES
    qb, kb, vb = C_AQ // w, C_AK // w, C_AV // w
    lam_spec = pl.BlockSpec((1, ATT_QK_DIM), lambda b, h, i: (0, 0))
    return pl.pallas_call(
        functools.partial(_attn_kernel, tq=tq, tk=tk, rc=32, nh=nh),
        grid=(B, ATT_HEADS // nh, nq),
        in_specs=[
            pl.BlockSpec((1, tq, w), lambda b, h, i: (b, i, qb + h)),
            pl.BlockSpec((1, S, w), lambda b, h, i: (b, 0, kb + h)),
            pl.BlockSpec((1, S, w), lambda b, h, i: (b, 0, vb + h)),
            pl.BlockSpec((1, BLOCK, w), lambda b, h, i: (0, 0, kb + h)),
            pl.BlockSpec((1, BLOCK, w), lambda b, h, i: (0, 0, vb + h)),
            lam_spec, lam_spec, lam_spec, lam_spec,
            pl.BlockSpec((1, ATT_V_DIM), lambda b, h, i: (0, 0)),
        ],
        out_specs=pl.BlockSpec((1, tq, w), lambda b, h, i: (b, i, h)),
        out_shape=jax.ShapeDtypeStruct((B, S, ATT_HEADS * ATT_V_DIM), BF16),
        scratch_shapes=[
            pltpu.VMEM((nh, 2 * tq, tk + BLOCK), F32), pltpu.VMEM((nh, 2 * tq, tk), F32),
            pltpu.VMEM((nh, 2 * tq, tk + BLOCK), BF16), pltpu.VMEM((nh, 2 * tq, tk), BF16),
            pltpu.VMEM((nh, 2 * tq, LANES), F32), pltpu.VMEM((nh, 2 * tq, LANES), F32),
            pltpu.VMEM((nh, 2 * tq, LANES), F32),
            pltpu.VMEM((nh, 2 * tq, 2 * ATT_V_DIM), F32),
        ],
        compiler_params=_cparams(3),
        name="diff_attention",
    )(zb, zb, zb, zbm, zbm, lq1, lk1, lq2, lk2, ng)


def _mlstm_kernel(xqk_ref, xv_ref, xo_ref, xg_ref, zm_ref, gm_ref, cw_ref, cb_ref, gb_ref, ng_ref, sh_ref, o_ref,
                  ct_sc, n_sc, m_sc, prev_sc, *, nbps):
    c = pl.program_id(1)
    L = BLOCK

    @pl.when(c == 0)
    def _():
        ct_sc[...] = jnp.zeros_like(ct_sc)
        n_sc[...] = jnp.zeros_like(n_sc)
        m_sc[...] = jnp.zeros_like(m_sc)
        prev_sc[...] = jnp.zeros_like(prev_sc)

    is_meta = c == 0
    row = lax.broadcasted_iota(jnp.int32, (L, 1), 0)
    valid = jnp.logical_or(c > 0, row >= PAD)
    sidx = lax.broadcasted_iota(jnp.int32, (L, L), 0)
    ridx = lax.broadcasted_iota(jnp.int32, (L, L), 1)
    causal = ridx <= sidx
    tri = causal.astype(BF16)
    elems = range(nbps)
    pairs = [(bb, h) for bb in elems for h in range(ML_HEADS)]
    sl = lambda h: slice(h * ML_DH, (h + 1) * ML_DH)
    st = lambda bb, h: bb * 8 + h

    q, k, v, ig, bcs, b_t, ig_t = {}, {}, {}, {}, {}, {}, {}
    for bb in elems:
        qk_pre = jnp.where(is_meta, zm_ref[0, :, C_MQK:C_MQK + 2 * ML_WIDTH], xqk_ref[bb])
        v[bb] = jnp.where(is_meta, zm_ref[0, :, C_MV:C_MV + ML_WIDTH], xv_ref[bb])
        gates = jnp.where(is_meta, gm_ref[0], xg_ref[bb])
        prev = prev_sc[bb * L:(bb + 1) * L, :]
        ext = jnp.concatenate([qk_pre, prev], axis=0)
        prev_sc[bb * L:(bb + 1) * L, :] = qk_pre
        shifted = jnp.dot(sh_ref[...], ext, preferred_element_type=F32)
        acc = cb_ref[...] + cw_ref[CONV_K - 1:CONV_K, :] * qk_pre.astype(F32)
        for s in range(1, CONV_K):
            acc = acc + cw_ref[CONV_K - 1 - s:CONV_K - s, :] * shifted[(s - 1) * L:s * L, :]
        qk = acc * _sigmoid(acc)
        qk = jnp.where(valid, qk, 0.0)
        q[bb] = qk[:, :ML_WIDTH]
        k[bb] = qk[:, ML_WIDTH:] * (ML_DH ** -0.5)

        igv = gates[:, :LANES] + gb_ref[:, :LANES]
        fg = gates[:, LANES:] + gb_ref[:, LANES:]
        lf = jnp.minimum(fg, 0.0) - jnp.log1p(jnp.exp(-jnp.abs(fg)))
        igv = jnp.where(valid, igv, NEG)
        lf = jnp.where(valid, lf, 0.0)
        lf_hi = lf.astype(BF16)
        lf_lo = (lf - lf_hi.astype(F32)).astype(BF16)
        bcs[bb] = (jnp.dot(tri, lf_hi, preferred_element_type=F32)
                   + jnp.dot(tri, lf_lo, preferred_element_type=F32))
        ig[bb] = igv
        b_t[bb] = bcs[bb].T
        ig_t[bb] = igv.T

    bc, m_h, dlog, inter, m_s = {}, {}, {}, {}, {}
    for p in pairs:
        bb, h = p
        bc[p] = bcs[bb][:, h:h + 1]
        m_h[p] = m_sc[st(bb, h):st(bb, h) + 1, 0:1]
        dlog[p] = jnp.where(causal, (bc[p] - b_t[bb][h:h + 1, :]) + ig_t[bb][h:h + 1, :], NEG)
        inter[p] = bc[p] + m_h[p]
    for p in pairs:
        m_s[p] = jnp.maximum(inter[p], jnp.max(dlog[p], axis=1, keepdims=True))

    w_intra, w_inter, qb, kb, s, qc = {}, {}, {}, {}, {}, {}
    for p in pairs:
        bb, h = p
        w_intra[p] = jnp.exp(dlog[p] - m_s[p])
        w_inter[p] = jnp.exp(inter[p] - m_s[p])
        qb[p] = q[bb][:, sl(h)].astype(BF16)
        kb[p] = k[bb][:, sl(h)].astype(BF16)
    for p in pairs:
        bb, h = p
        s[p] = lax.dot_general(qb[p], kb[p], (((1,), (1,)), ((), ())), preferred_element_type=F32) * w_intra[p]
        qc[p] = jnp.dot(qb[p], ct_sc[bb * ML_HEADS + h].astype(BF16), preferred_element_type=F32)

    hh = {}
    for p in pairs:
        bb, h = p
        nrow = n_sc[st(bb, h):st(bb, h) + 1, :]
        num = w_inter[p] * qc[p] + jnp.dot(s[p].astype(BF16), v[bb][:, sl(h)], preferred_element_type=F32)
        den = (w_inter[p] * jnp.sum(q[bb][:, sl(h)] * nrow, axis=1, keepdims=True)
               + jnp.sum(s[p], axis=1, keepdims=True))
        hh[p] = num / jnp.maximum(jnp.abs(den), jnp.exp(-m_s[p]))

    for p in pairs:
        bb, h = p
        b_last = bc[p][L - 1:L, :]
        upd = (b_last - bc[p]) + ig[bb][:, h:h + 1]
        m_new = jnp.maximum(b_last + m_h[p], jnp.max(upd, axis=0, keepdims=True))
        w_old = jnp.exp(b_last + m_h[p] - m_new)
        w_r = jnp.exp(upd - m_new)
        vw = (v[bb][:, sl(h)].astype(F32) * w_r).astype(BF16)
        ci = bb * ML_HEADS + h
        ct_sc[ci] = w_old * ct_sc[ci] + lax.dot_general(kb[p], vw, (((0,), (0,)), ((), ())),
                                                         preferred_element_type=F32)
        r = st(bb, h)
        n_sc[r:r + 1, :] = w_old * n_sc[r:r + 1, :] + jnp.sum(k[bb][:, sl(h)] * w_r, axis=0, keepdims=True)
        m_sc[r:r + 1, :] = jnp.broadcast_to(m_new, (1, LANES))

    outs = {}
    for p in pairs:
        bb, h = p
        hn = hh[p] * lax.rsqrt(jnp.mean(hh[p] * hh[p], axis=1, keepdims=True) + LN_EPS) * ng_ref[:, sl(h)]
        outs[p] = _sigmoid(xo_ref[bb, :, sl(h)].astype(F32)) * hn

    @pl.when(c > 0)
    def _():
        for bb in elems:
            o_ref[bb] = jnp.concatenate([outs[(bb, h)] for h in range(ML_HEADS)], axis=1).astype(BF16)


def _mlstm(zb, gates, zbm, gatesm, conv_w, conv_b, gate_b, ng, *, nbps):
    B, S, _ = zb.shape
    nc = S // BLOCK + 1
    t = np.arange(BLOCK)
    shift = np.zeros(((CONV_K - 1) * BLOCK, 2 * BLOCK), np.float32)
    for s in range(1, CONV_K):
        shift[(s - 1) * BLOCK + t, np.where(t >= s, t - s, 2 * BLOCK + t - s)] = 1.0

    def xc(c):
        return jnp.maximum(c - 1, 0)

    return pl.pallas_call(
        functools.partial(_mlstm_kernel, nbps=nbps),
        grid=(B // nbps, nc),
        in_specs=[
            pl.BlockSpec((nbps, BLOCK, 2 * ML_WIDTH), lambda b, c: (b, xc(c), C_MQK // (2 * ML_WIDTH))),
            pl.BlockSpec((nbps, BLOCK, ML_WIDTH), lambda b, c: (b, xc(c), C_MV // ML_WIDTH)),
            pl.BlockSpec((nbps, BLOCK, ML_WIDTH), lambda b, c: (b, xc(c), C_MO // ML_WIDTH)),
            pl.BlockSpec((nbps, BLOCK, GATE_W), lambda b, c: (b, xc(c), 0)),
            pl.BlockSpec((1, BLOCK, ZB_W), lambda b, c: (0, 0, 0)),
            pl.BlockSpec((1, BLOCK, GATE_W), lambda b, c: (0, 0, 0)),
            pl.BlockSpec((CONV_K, 2 * ML_WIDTH), lambda b, c: (0, 0)),
            pl.BlockSpec((1, 2 * ML_WIDTH), lambda b, c: (0, 0)),
            pl.BlockSpec((1, GATE_W), lambda b, c: (0, 0)),
            pl.BlockSpec((1, ML_WIDTH), lambda b, c: (0, 0)),
            pl.BlockSpec(((CONV_K - 1) * BLOCK, 2 * BLOCK), lambda b, c: (0, 0)),
        ],
        out_specs=pl.BlockSpec((nbps, BLOCK, ML_WIDTH), lambda b, c: (b, xc(c), 0)),
        out_shape=jax.ShapeDtypeStruct((B, S, ML_WIDTH), BF16),
        scratch_shapes=[
            pltpu.VMEM((nbps * ML_HEADS, ML_DH, ML_DH), F32),
            pltpu.VMEM((nbps * 8, ML_DH), F32),
            pltpu.VMEM((nbps * 8, LANES), F32),
            pltpu.VMEM((nbps * BLOCK, 2 * ML_WIDTH), BF16),
        ],
        compiler_params=_cparams(2),
        name="mlstm",
    )(zb, zb, zb, gates, zbm, gatesm, conv_w, conv_b, gate_b, ng, jnp.asarray(shift, BF16))


def _outproj_kernel(x_ref, ya_ref, ym_ref, ga_ref, gmm_ref, eg_ref, eb_ref, wa_ref, wm_ref, wo_ref,
                    g1_ref, b1_ref, wr_ref, br_ref,
                    h1_ref, eid_ref, gate_ref, rank_ref, cnt_ref, run_sc, *, tm, ts):
    i = pl.program_id(0)
    subs = range(tm // ts)
    rows = lambda j: slice(j * ts, (j + 1) * ts)

    @pl.when(i == 0)
    def _():
        run_sc[...] = jnp.zeros_like(run_sc)

    sig = lambda g: 0.5 * jnp.tanh(0.5 * g) + 0.5
    pa = {j: jnp.dot(ya_ref[rows(j), :], wa_ref[...], preferred_element_type=F32) for j in subs}
    pm = {j: jnp.dot(ym_ref[rows(j), :], wm_ref[...], preferred_element_type=F32) for j in subs}
    h0 = {j: _ln(x_ref[rows(j), :], eg_ref[...], eb_ref[...]) for j in subs}
    merged = {j: sig(ga_ref[rows(j), :]) * pa[j].astype(BF16) + sig(gmm_ref[rows(j), :]) * pm[j].astype(BF16)
              for j in subs}
    mix = {j: jnp.dot(merged[j], wo_ref[...], preferred_element_type=F32) for j in subs}
    h1 = {j: _ln(DN_ALPHA * h0[j] + mix[j], g1_ref[...], b1_ref[...]) for j in subs}
    for j in subs:
        _store_row_tiles(h1_ref.at[pl.ds(j * ts * ROW_TILE, ts * ROW_TILE)], h1[j])

    logits = {}
    for j in subs:
        h_hi = h1[j].astype(BF16)
        h_mid = (h1[j] - h_hi.astype(F32)).astype(BF16)
        hh = jnp.dot(h_hi, wr_ref[...], preferred_element_type=F32)
        logits[j] = (hh[:, :LANES] + hh[:, LANES:]
                     + jnp.dot(h_mid, wr_ref[:, :LANES], preferred_element_type=F32) + br_ref[...])

    lane = lax.broadcasted_iota(jnp.int32, (ts, LANES), 1)
    r_i = lax.broadcasted_iota(jnp.int32, (ts, ts), 0)
    c_i = lax.broadcasted_iota(jnp.int32, (ts, ts), 1)
    strict = (c_i < r_i).astype(BF16)
    sel_e, ex, den, onehots, oh = {}, {}, {}, {}, {}
    for j in subs:
        work = logits[j]
        es, vs = [], []
        for _ in range(TOP_K):
            mv = jnp.max(work, axis=1, keepdims=True)
            e = jnp.min(jnp.where(work == mv, lane, LANES), axis=1, keepdims=True)
            es.append(e)
            vs.append(mv)
            work = jnp.where(lane == e, -jnp.inf, work)
        sel_e[j] = es
        ex[j] = [jnp.exp(v - vs[0]) for v in vs]
        den[j] = ex[j][0] + ex[j][1] + ex[j][2] + ex[j][3]
        onehots[j] = [lane == e for e in es]
        acc = jnp.zeros((ts, LANES), F32)
        for o in onehots[j]:
            acc = acc + o.astype(F32)
        oh[j] = acc

    run = run_sc[...]
    for j in subs:
        before = jnp.dot(strict, oh[j].astype(BF16), preferred_element_type=F32) + run
        run = run + jnp.sum(oh[j], axis=0, keepdims=True)
        eid = jnp.zeros((ts, LANES), F32)
        gate = jnp.zeros((ts, LANES), F32)
        rank = jnp.zeros((ts, LANES), F32)
        for kk in range(TOP_K):
            here = lane == kk
            eid = jnp.where(here, sel_e[j][kk].astype(F32), eid)
            gate = jnp.where(here, ex[j][kk] / den[j], gate)
            rk = jnp.sum(jnp.where(onehots[j][kk], before, 0.0), axis=1, keepdims=True)
            rank = jnp.where(here, rk, rank)
        gate_ref[rows(j), :] = gate
        eid_ref[:, rows(j)] = eid.T[0:8, :]
        rank_ref[:, rows(j)] = rank.T[0:8, :]
    run_sc[...] = run
    cnt_ref[...] = run


def _outproj(x2, yatt, yml, zb2, eg, eb, wa, wm, wo, g1, b1, wr, br, *, tm, ts):
    N, D = x2.shape
    vec = lambda w: pl.BlockSpec((1, w), lambda i: (0, 0))
    full = lambda a, b: pl.BlockSpec((a, b), lambda i: (0, 0))
    tile = lambda w: pl.BlockSpec((tm, w), lambda i: (i, 0))
    return pl.pallas_call(
        functools.partial(_outproj_kernel, tm=tm, ts=ts),
        grid=(N // tm,),
        in_specs=[
            tile(D), tile(ATT_HEADS * ATT_V_DIM), tile(ML_WIDTH),
            pl.BlockSpec((tm, D), lambda i: (i, C_GA // D)),
            pl.BlockSpec((tm, D), lambda i: (i, C_GM // D)),
            vec(D), vec(D),
            full(ATT_HEADS * ATT_V_DIM, D), full(ML_WIDTH, D), full(D, D),
            vec(D), vec(D), full(D, 2 * LANES), vec(LANES),
        ],
        out_specs=[pl.BlockSpec((tm * ROW_TILE, LANES), lambda i: (i, 0)),
                   pl.BlockSpec((8, tm), lambda i: (0, i)), tile(LANES),
                   pl.BlockSpec((8, tm), lambda i: (0, i)), vec(LANES)],
        out_shape=[
            jax.ShapeDtypeStruct((N * ROW_TILE, LANES), F32),
            jax.ShapeDtypeStruct((8, N), F32),
            jax.ShapeDtypeStruct((N, LANES), F32),
            jax.ShapeDtypeStruct((8, N), F32),
            jax.ShapeDtypeStruct((1, LANES), F32),
        ],
        scratch_shapes=[pltpu.VMEM((1, LANES), F32)],
        compiler_params=_cparams(1),
        name="outproj_router",
    )(x2, yatt, yml, zb2, zb2, eg, eb, wa, wm, wo, g1, b1, wr, br)


def _row_copies(src_row, dst_row, sem, n, issue):
    def body(r, carry):
        for kk in range(TOP_K):
            cp = pltpu.make_async_copy(src_row(r, kk), dst_row(r, kk), sem)
            if issue:
                cp.start(priority=kk % 2)
            else:
                cp.wait()
        return carry
    lax.fori_loop(0, n, body, 0, unroll=8)


def _dispatch_kernel(pad_pos_ref, pad_len_ref, d_ref, h_ref, xs_ref, zero_sc, sem, *, tm, max_pad):
    @pl.when(pl.program_id(0) == 0)
    def _():
        zero_sc[...] = jnp.zeros_like(zero_sc)
        bits = [1 << b for b in reversed(range(max_pad.bit_length()))]

        def pad_copies(e, issue):
            pos = pad_pos_ref[e]
            length = pad_len_ref[e]
            for bit in bits:
                take = (length & bit) != 0

                @pl.when(take)
                def _(pos=pos, bit=bit):
                    dst = xs_ref.at[pl.ds(pl.multiple_of(pos * ROW_TILE, ROW_TILE), bit * ROW_TILE)]
                    cp = pltpu.make_async_copy(zero_sc.at[pl.ds(0, bit * ROW_TILE)], dst, sem.at[1])
                    if issue:
                        cp.start()
                    else:
                        cp.wait()
                pos = pos + jnp.where(take, bit, 0)

        def per_expert(e, carry):
            pad_copies(e, True)
            pad_copies(e, False)
            return carry

        lax.fori_loop(0, N_EXPERTS, per_expert, 0)

    src = lambda r, kk: _row_tile(h_ref, r)
    dst = lambda r, kk: _row_tile(xs_ref, d_ref[0, 0, kk * tm + r])
    _row_copies(src, dst, sem.at[0], tm, True)
    _row_copies(src, dst, sem.at[0], tm, False)


def _dispatch(pad_pos, pad_len, dest3, h1t, rows_out, *, tm, max_pad):
    N = h1t.shape[0] // ROW_TILE
    zero_rows = 1 << (max_pad.bit_length() - 1)
    return pl.pallas_call(
        functools.partial(_dispatch_kernel, tm=tm, max_pad=max_pad),
        grid_spec=pltpu.PrefetchScalarGridSpec(
            num_scalar_prefetch=2,
            grid=(N // tm,),
            in_specs=[
                pl.BlockSpec((1, 1, tm * TOP_K), lambda i, pp, pn: (i, 0, 0), memory_space=pltpu.SMEM),
                pl.BlockSpec((tm * ROW_TILE, LANES), lambda i, pp, pn: (i, 0)),
            ],
            out_specs=pl.BlockSpec(memory_space=pl.ANY),
            scratch_shapes=[pltpu.VMEM((zero_rows * ROW_TILE, LANES), F32), pltpu.SemaphoreType.DMA((2,))],
        ),
        out_shape=jax.ShapeDtypeStruct((rows_out * ROW_TILE, LANES), F32),
        compiler_params=_cparams(1),
        name="moe_dispatch",
    )(pad_pos, pad_len, dest3, h1t)


def _select_matrix():
    j = np.arange(2 * LANES)
    sel = np.zeros((2 * LANES, 2 * LANES), np.float32)
    sel[j, (j % 2) * LANES + j // 2] = 1.0
    return jnp.asarray(sel, BF16)


def _expert_kernel(be_ref, nu_ref, x_ref, wgu_ref, bg_ref, bl_ref, wd_ref, bd_ref, sel_ref, y_ref,
                   wg_sc, wl_sc, wd_sc, *, tm):
    i = pl.program_id(0)
    active = i < nu_ref[0]
    new_expert = jnp.logical_or(i == 0, be_ref[i] != be_ref[jnp.maximum(i - 1, 0)])

    @pl.when(jnp.logical_and(active, new_expert))
    def _():
        for c in range(wgu_ref.shape[2] // (2 * LANES)):
            blk = wgu_ref[0, :, c * 2 * LANES:(c + 1) * 2 * LANES].astype(BF16)
            r = jnp.dot(blk, sel_ref[...], preferred_element_type=F32).astype(BF16)
            wg_sc[:, c * LANES:(c + 1) * LANES] = r[:, :LANES]
            wl_sc[:, c * LANES:(c + 1) * LANES] = r[:, LANES:]
        wd_sc[...] = wd_ref[0].astype(BF16)

    @pl.when(active)
    def _():
        xb = _load_row_tiles(x_ref, tm).astype(BF16)
        hg = jnp.dot(xb, wg_sc[...], preferred_element_type=F32) + bg_ref[0]
        hl = jnp.dot(xb, wl_sc[...], preferred_element_type=F32) + bl_ref[0]
        glu = jnp.minimum(hg, SWIGLU_LIMIT)
        lin = jnp.clip(hl, -SWIGLU_LIMIT, SWIGLU_LIMIT)
        act = glu * _sigmoid(SWIGLU_ALPHA * glu) * (lin + 1.0)
        _store_row_tiles(y_ref, jnp.dot(act.astype(BF16), wd_sc[...], preferred_element_type=F32) + bd_ref[0])


def _experts(blk_exp, n_used, xs, wgu, bg, bl, wd, bd, *, tm):
    MP = xs.shape[0] // ROW_TILE
    _, D, F2 = wgu.shape
    F = F2 // 2
    rows = pl.BlockSpec((tm * ROW_TILE, LANES), lambda i, be, nu: (jnp.minimum(i, nu[0] - 1), 0))
    wsp = lambda a, b: pl.BlockSpec((1, a, b), lambda i, be, nu: (be[i], 0, 0))
    return pl.pallas_call(
        functools.partial(_expert_kernel, tm=tm),
        grid_spec=pltpu.PrefetchScalarGridSpec(
            num_scalar_prefetch=2,
            grid=(MP // tm,),
            in_specs=[rows, wsp(D, F2), wsp(1, F), wsp(1, F), wsp(F, D), wsp(1, D),
                      pl.BlockSpec((2 * LANES, 2 * LANES), lambda i, be, nu: (0, 0))],
            out_specs=rows,
            scratch_shapes=[pltpu.VMEM((D, F), BF16), pltpu.VMEM((D, F), BF16), pltpu.VMEM((F, D), BF16)],
        ),
        out_shape=jax.ShapeDtypeStruct(xs.shape, F32),
        compiler_params=_cparams(1),
        name="moe_experts",
    )(blk_exp, n_used, xs, wgu, bg, bl, wd, bd, _select_matrix())


def _combine_kernel(dcur_ref, dnxt_ref, h1_ref, gate_ref, g2_ref, b2_ref, ys_ref, o_ref, ybuf, sem, *, tm, nsteps):
    i = pl.program_id(0)
    slot = i % 2

    def copies(d_ref, s, issue):
        src = lambda r, kk: _row_tile(ys_ref, d_ref[0, 0, kk * tm + r])
        dst = lambda r, kk: _row_tile(ybuf.at[s, kk], r)
        _row_copies(src, dst, sem.at[s], tm, issue)

    @pl.when(i == 0)
    def _():
        copies(dcur_ref, 0, True)

    @pl.when(i + 1 < nsteps)
    def _():
        copies(dnxt_ref, 1 - slot, True)

    copies(dcur_ref, slot, False)
    acc = DN_ALPHA * _load_row_tiles(h1_ref, tm)
    for kk in range(TOP_K):
        acc = acc + gate_ref[:, kk:kk + 1] * _load_row_tiles(ybuf.at[slot, kk], tm)
    o_ref[...] = _ln(acc, g2_ref[...], b2_ref[...])


def _combine(dest3, h1t, gate, g2, b2, ys, *, tm):
    N, D = h1t.shape[0] // ROW_TILE, D_MODEL
    nsteps = N // tm
    dspec = lambda f: pl.BlockSpec((1, 1, tm * TOP_K), f, memory_space=pltpu.SMEM)
    return pl.pallas_call(
        functools.partial(_combine_kernel, tm=tm, nsteps=nsteps),
        grid=(nsteps,),
        in_specs=[
            dspec(lambda i: (i, 0, 0)),
            dspec(lambda i: (jnp.minimum(i + 1, nsteps - 1), 0, 0)),
            pl.BlockSpec((tm * ROW_TILE, LANES), lambda i: (i, 0)),
            pl.BlockSpec((tm, LANES), lambda i: (i, 0)),
            pl.BlockSpec((1, D), lambda i: (0, 0)),
            pl.BlockSpec((1, D), lambda i: (0, 0)),
            pl.BlockSpec(memory_space=pl.ANY),
        ],
        out_specs=pl.BlockSpec((tm, D), lambda i: (i, 0)),
        out_shape=jax.ShapeDtypeStruct((N, D), F32),
        scratch_shapes=[pltpu.VMEM((2, TOP_K, tm * ROW_TILE, LANES), F32), pltpu.SemaphoreType.DMA((2,))],
        compiler_params=_cparams(1),
        name="moe_combine",
    )(dest3, dest3, h1t, gate, g2, b2, ys)


def _rotary_column_order():
    idx = np.empty((ATT_HEADS * LANES,), np.int32)
    for h in range(ATT_HEADS):
        for half in range(2):
            for sub in range(2):
                for dd in range(32):
                    idx[h * LANES + half * 64 + sub * 32 + dd] = (2 * h + sub) * ATT_QK_DIM + half * 32 + dd
    return idx


def _rotary_tables(pos):
    half = ATT_QK_DIM // 2
    inv_freq = ROPE_THETA ** (-jnp.arange(half, dtype=F32) / half)
    ang = pos.astype(F32)[:, None] * inv_freq[None, :]
    cos = jnp.tile(jnp.cos(ang), (1, 4))
    sin = jnp.tile(jnp.sin(ang), (1, 4))
    sign = jnp.where(jnp.arange(LANES) < 64, -1.0, 1.0).astype(F32)
    return cos, sin * sign[None, :]


def _pick(n, prefs):
    for t in prefs:
        if n % t == 0:
            return t
    raise ValueError(f"unsupported size {n}")


def kernel(x, meta, emb_ln_g, emb_ln_b, w_in, conv_w, conv_b, gate_bias, lam_q1, lam_k1, lam_q2, lam_k2,
           att_norm_g, ml_norm_g, w_att_out, w_ml_out, w_o, ln1_g, ln1_b, w_router, b_router,
           w_gu, b_gu, w_down, b_down, ln2_g, ln2_b):
    B, S, D = x.shape
    assert D == D_MODEL and S % 512 == 0 and w_in.shape[0] == DEPTH
    N = B * S
    row2 = lambda a: a.reshape(1, -1)

    w = w_in[0]
    o_aq, o_ak, o_av, o_mqk, o_mv, o_mo, o_gt, o_ga, o_gm = 0, 512, 1024, 1536, 2560, 3072, 3584, 3592, 4616
    perm = _rotary_column_order()
    gpad = jnp.zeros((D, LANES - ML_HEADS), F32)
    w_all = jnp.concatenate([
        w[:, o_ga:o_ga + D], w[:, o_gm:o_gm + D], w[:, o_mqk:o_mqk + 2 * ML_WIDTH],
        w[:, o_aq:o_aq + 512][:, perm], w[:, o_ak:o_ak + 512][:, perm], w[:, o_av:o_av + 512],
        w[:, o_mv:o_mv + ML_WIDTH], w[:, o_mo:o_mo + ML_WIDTH],
        w[:, o_gt:o_gt + ML_HEADS], gpad, w[:, o_gt + ML_HEADS:o_gt + 2 * ML_HEADS], gpad,
    ], axis=1).astype(BF16)
    gb = gate_bias[0]
    zpad = jnp.zeros((LANES - ML_HEADS,), F32)
    gate_b = jnp.concatenate([gb[:ML_HEADS], zpad, gb[ML_HEADS:], zpad]).reshape(1, GATE_W)

    cos_x, sin_x = _rotary_tables(N_META + jnp.arange(S))
    cos_m, sin_m = _rotary_tables(jnp.maximum(jnp.arange(BLOCK) - PAD, 0))

    eg, eb = row2(emb_ln_g), row2(emb_ln_b)

    tm_in = 512
    zb, gates = _inproj(x, eg, eb, w_all, cos_x, sin_x, tm=tm_in)
    xm = jnp.concatenate([jnp.zeros((PAD, D), x.dtype), meta.astype(x.dtype)], axis=0)[None]
    zbm, gatesm = _inproj(xm, eg, eb, w_all, cos_m, sin_m, tm=BLOCK, first_valid_row=PAD)

    tq = 512
    yatt = _attention(zb, zbm, row2(lam_q1[0]), row2(lam_k1[0]), row2(lam_q2[0]), row2(lam_k2[0]),
                      row2(att_norm_g[0]), tq=tq, nh=1)
    yml = _mlstm(zb, gates, zbm, gatesm, conv_w[0], row2(conv_b[0]), gate_b, row2(ml_norm_g[0]),
                 nbps=2 if B % 2 == 0 else 1)

    wr32 = jnp.concatenate([w_router[0], jnp.zeros((D, LANES - N_EXPERTS), F32)], axis=1)
    wr_hi = wr32.astype(BF16)
    wr = jnp.concatenate([wr_hi, (wr32 - wr_hi.astype(F32)).astype(BF16)], axis=1)
    br = jnp.concatenate([b_router[0], jnp.full((LANES - N_EXPERTS,), NEG, F32)]).reshape(1, LANES)
    tm_out = 1024
    h1, eid, gate, rank, cnt = _outproj(
        x.reshape(N, D), yatt.reshape(N, -1), yml.reshape(N, -1), zb.reshape(N, ZB_W), eg, eb,
        w_att_out[0].astype(BF16), w_ml_out[0].astype(BF16), w_o[0].astype(BF16),
        row2(ln1_g[0]), row2(ln1_b[0]), wr, br, tm=tm_out, ts=512)

    tm_e = 512
    M = N * TOP_K
    nb = (M + N_EXPERTS * (tm_e - 1) + tm_e - 1) // tm_e
    counts = cnt[0, :N_EXPERTS].astype(jnp.int32)
    nblk = (counts + tm_e - 1) // tm_e
    cum = jnp.cumsum(nblk)
    pstart = (cum - nblk) * tm_e
    eid_t = eid[:TOP_K].astype(jnp.int32)
    base = jnp.sum(jnp.where(eid_t[None] == jnp.arange(N_EXPERTS)[:, None, None], pstart[:, None, None], 0), axis=0)
    dest = base + rank[:TOP_K].astype(jnp.int32)
    n_used = cum[-1:].astype(jnp.int32)
    blk = jnp.minimum(jnp.arange(nb, dtype=jnp.int32), n_used[0] - 1)
    blk_exp = jnp.sum(blk[:, None] >= cum[None, :], axis=1).astype(jnp.int32)

    tm_d = 256
    dest3 = dest.reshape(TOP_K, N // tm_d, tm_d).transpose(1, 0, 2).reshape(N // tm_d, 1, TOP_K * tm_d)
    xs = _dispatch(pstart + counts, nblk * tm_e - counts, dest3, h1, nb * tm_e, tm=tm_d, max_pad=tm_e - 1)

    bgu = b_gu[0]
    ys = _experts(blk_exp, n_used, xs, w_gu[0], bgu[:, None, 0::2], bgu[:, None, 1::2],
                  w_down[0], b_down[0][:, None, :], tm=tm_e)

    out = _combine(dest3, h1, gate, row2(ln2_g[0]), row2(ln2_b[0]), ys, tm=tm_d)
    return out.reshape(B, S, D)
```

```python
import functools
import math

import jax
import jax.numpy as jnp
import numpy as np
from jax import lax
from jax.experimental import pallas as pl
from jax.experimental.pallas import tpu as pltpu

D_MODEL = 1024
N_META = 16
BLOCK = 128
PAD = BLOCK - N_META
NEG = -1e30
LN_EPS = 1e-5
ATT_HEADS = 4
ATT_QK_DIM = 64
ATT_V_DIM = 128
ROPE_THETA = 10000.0
ML_HEADS = 4
ML_DH = 128
ML_WIDTH = ML_HEADS * ML_DH
CONV_K = 4
N_EXPERTS = 32
TOP_K = 4
SWIGLU_LIMIT = 7.0
SWIGLU_ALPHA = 1.702
DEPTH = 1
DN_ALPHA = (2 * DEPTH) ** 0.25
LAMBDA_INIT = 0.8 - 0.6 * math.exp(-0.3 * 0)

LANES = 128
C_GA, C_GM, C_MQK, C_AQ, C_AK, C_AV, C_MV, C_MO = 0, 1024, 2048, 3072, 3584, 4096, 4608, 5120
ZB_W = 5632
GATE_W = 2 * LANES
W_ALL = ZB_W + GATE_W
CHUNK = 512

VMEM_LIMIT = 56 * 1024 * 1024

BF16 = jnp.bfloat16
F32 = jnp.float32


def _cparams(n_axes):
    return pltpu.CompilerParams(dimension_semantics=("arbitrary",) * n_axes, vmem_limit_bytes=VMEM_LIMIT)


def _ln(x, g, b):
    mu = jnp.mean(x, axis=-1, keepdims=True)
    xc = x - mu
    var = jnp.mean(xc * xc, axis=-1, keepdims=True)
    return xc * lax.rsqrt(var + LN_EPS) * g + b


def _sigmoid(x):
    return 1.0 / (1.0 + jnp.exp(-x))


ROW_TILE = D_MODEL // LANES


def _store_row_tiles(ref, val):
    tm = val.shape[0]
    for g in range(ROW_TILE):
        ref[pl.ds(g, tm, stride=ROW_TILE), :] = val[:, g * LANES:(g + 1) * LANES]


def _load_row_tiles(ref, tm):
    return jnp.concatenate([ref[pl.ds(g, tm, stride=ROW_TILE), :] for g in range(ROW_TILE)], axis=1)


def _row_tile(ref, r):
    return ref.at[pl.ds(pl.multiple_of(r * ROW_TILE, ROW_TILE), ROW_TILE)]


def _inproj_kernel(x_ref, g_ref, b_ref, w_ref, cos_ref, sin_ref, zb_ref, gt_ref, *, first_valid_row):
    x = x_ref[0]
    tm = x.shape[0]
    hb = _ln(x, g_ref[...], b_ref[...]).astype(BF16)
    cos = cos_ref[...]
    sin = sin_ref[...]
    if first_valid_row:
        rowmask = lax.broadcasted_iota(jnp.int32, (tm, 1), 0) >= first_valid_row
    for c in range(ZB_W // CHUNK):
        z = jnp.dot(hb, w_ref[:, c * CHUNK:(c + 1) * CHUNK], preferred_element_type=F32)
        if c * CHUNK in (C_AQ, C_AK):
            parts = []
            for h in range(ATT_HEADS):
                zh = z[:, h * LANES:(h + 1) * LANES]
                parts.append(zh * cos + pltpu.roll(zh, 64, 1) * sin)
            z = jnp.concatenate(parts, axis=1)
            if c * CHUNK == C_AQ:
                z = z * (ATT_QK_DIM ** -0.5 * math.log2(math.e))
        if first_valid_row:
            z = jnp.where(rowmask, z, 0.0)
        zb_ref[0, :, c * CHUNK:(c + 1) * CHUNK] = z.astype(BF16)
    zg = jnp.dot(hb, w_ref[:, ZB_W:W_ALL], preferred_element_type=F32)
    if first_valid_row:
        zg = jnp.where(rowmask, zg, 0.0)
    gt_ref[0] = zg


def _inproj(x3, g, b, w, cos, sin, *, tm, first_valid_row=0):
    B, S, D = x3.shape
    nt = S // tm
    return pl.pallas_call(
        functools.partial(_inproj_kernel, first_valid_row=first_valid_row),
        grid=(B, nt),
        in_specs=[
            pl.BlockSpec((1, tm, D), lambda bi, i: (bi, i, 0)),
            pl.BlockSpec((1, D), lambda bi, i: (0, 0)),
            pl.BlockSpec((1, D), lambda bi, i: (0, 0)),
            pl.BlockSpec((D, W_ALL), lambda bi, i: (0, 0)),
            pl.BlockSpec((tm, LANES), lambda bi, i: (i, 0)),
            pl.BlockSpec((tm, LANES), lambda bi, i: (i, 0)),
        ],
        out_specs=[
            pl.BlockSpec((1, tm, ZB_W), lambda bi, i: (bi, i, 0)),
            pl.BlockSpec((1, tm, GATE_W), lambda bi, i: (bi, i, 0)),
        ],
        out_shape=[
            jax.ShapeDtypeStruct((B, S, ZB_W), BF16),
            jax.ShapeDtypeStruct((B, S, GATE_W), F32),
        ],
        compiler_params=_cparams(2),
        name="inproj",
    )(x3, g, b, w, cos, sin)


def _attn_kernel(q_ref, k_ref, v_ref, km_ref, vm_ref, lq1_ref, lk1_ref, lq2_ref, lk2_ref, ng_ref, o_ref,
                 sa_sc, sb_sc, pa_sc, pb_sc, aa_sc, ab_sc, m_sc, acc_sc, *, tq, tk, rc, nh):
    assert tq == 2 * tk
    qi = pl.program_id(2)
    rows = 2 * tq
    heads = range(nh)
    cols = lambda hh: slice(hh * LANES, (hh + 1) * LANES)
    lane = lax.broadcasted_iota(jnp.int32, (tq, LANES), 1)
    is_map0 = (lane & 63) < 32
    qq = {}
    for hh in heads:
        q = q_ref[0, :, cols(hh)]
        zero = jnp.zeros_like(q)
        qq[hh] = jnp.concatenate([jnp.where(is_map0, q, zero), jnp.where(is_map0, zero, q)], axis=0)

    def qk_stage(hh, kblk, s_ref, col0=0):
        s_ref[hh, :, col0:col0 + kblk.shape[0]] = lax.dot_general(qq[hh], kblk, (((1,), (1,)), ((), ())),
                                                                  preferred_element_type=F32)

    def pv_stage(hh, p_ref, a_ref, vblk, first=False):
        width = vblk.shape[0]
        v1 = jnp.concatenate([vblk, jnp.ones_like(vblk)], axis=1)
        pv = jnp.dot(p_ref[hh, :, 0:width], v1, preferred_element_type=F32)
        if first:
            acc_sc[hh] = pv
        else:
            a = a_ref[hh]
            acc_sc[hh] = jnp.concatenate([a, a], axis=1) * acc_sc[hh] + pv

    def sm_stage(hh, s_ref, p_ref, a_ref, width, vis, first=False):
        nrep = width // LANES
        for r0 in range(0, rows, rc):
            kind = vis(r0)
            if kind == "none":
                continue
            s = s_ref[hh, r0:r0 + rc, 0:width]
            if kind != "all":
                s = kind(s)
            m_cur = jnp.max(s, axis=1, keepdims=True)
            if first:
                m_new = jnp.broadcast_to(m_cur, (rc, LANES))
            else:
                m_prev = m_sc[hh, r0:r0 + rc, :]
                m_new = jnp.maximum(m_prev, m_cur)
                a_ref[hh, r0:r0 + rc, :] = jnp.exp2(m_prev - m_new)
            m_sc[hh, r0:r0 + rc, :] = m_new
        for r0 in range(0, rows, rc):
            kind = vis(r0)
            if kind == "none":
                p_ref[hh, r0:r0 + rc, 0:width] = jnp.zeros((rc, width), BF16)
                a_ref[hh, r0:r0 + rc, :] = jnp.ones((rc, LANES), F32)
                continue
            s = s_ref[hh, r0:r0 + rc, 0:width]
            if kind != "all":
                s = kind(s)
            m_new = m_sc[hh, r0:r0 + rc, :]
            m_rep = m_new if nrep == 1 else jnp.concatenate([m_new] * nrep, axis=1)
            p_ref[hh, r0:r0 + rc, 0:width] = jnp.exp2(s - m_rep).astype(BF16)

    def first_vis(r0):
        q0 = r0 % tq
        def mask(s):
            rowq = q0 + lax.broadcasted_iota(jnp.int32, s.shape, 0)
            col = lax.broadcasted_iota(jnp.int32, s.shape, 1)
            keep = col >= tk + PAD
            if tk - 1 <= q0:
                keep = jnp.logical_or(keep, col < tk)
            else:
                keep = jnp.logical_or(keep, col <= rowq)
            return jnp.where(keep, s, NEG)
        return mask

    def diag_vis(d):
        def vis(r0):
            q0 = r0 % tq
            if d * tk + tk - 1 <= q0:
                return "all"
            if d * tk > q0 + rc - 1:
                return "none"
            def mask(s):
                rowq = q0 + lax.broadcasted_iota(jnp.int32, s.shape, 0)
                col = d * tk + lax.broadcasted_iota(jnp.int32, s.shape, 1)
                return jnp.where(col <= rowq, s, NEG)
            return mask
        return vis

    all_vis = lambda r0: "all"
    kblk = lambda hh, start: k_ref[0, pl.ds(start, tk), cols(hh)]
    vblk = lambda hh, start: v_ref[0, pl.ds(start, tk), cols(hh)]

    d0 = pl.multiple_of(qi * tq, tk)
    d1 = pl.multiple_of(qi * tq + tk, tk)
    for hh in heads:
        qk_stage(hh, kblk(hh, d0), sa_sc)
    for hh in heads:
        qk_stage(hh, km_ref[0, :, cols(hh)], sa_sc, col0=tk)
    for hh in heads:
        qk_stage(hh, kblk(hh, d1), sb_sc)
    for hh in heads:
        sm_stage(hh, sa_sc, pa_sc, None, tk + BLOCK, first_vis, first=True)
    for hh in heads:
        pv_stage(hh, pa_sc, None, jnp.concatenate([vblk(hh, d0), vm_ref[0, :, cols(hh)]], axis=0), first=True)
    for hh in heads:
        sm_stage(hh, sb_sc, pb_sc, ab_sc, tk, diag_vis(1))
    for hh in heads:
        qk_stage(hh, kblk(hh, 0), sa_sc)

    def pair(i, carry):
        a_start = pl.multiple_of(i * tq, tk)
        b_start = pl.multiple_of(i * tq + tk, tk)
        prev_b = pl.multiple_of(jnp.where(i == 0, d1, a_start - tk), tk)
        next_a = pl.multiple_of(jnp.minimum(a_start + tq, (qi - 1) * tq), tk)
        for hh in heads:
            qk_stage(hh, kblk(hh, b_start), sb_sc)
        for hh in heads:
            pv_stage(hh, pb_sc, ab_sc, vblk(hh, prev_b))
        for hh in heads:
            sm_stage(hh, sa_sc, pa_sc, aa_sc, tk, all_vis)
        for hh in heads:
            qk_stage(hh, kblk(hh, next_a), sa_sc)
        for hh in heads:
            pv_stage(hh, pa_sc, aa_sc, vblk(hh, a_start))
        for hh in heads:
            sm_stage(hh, sb_sc, pb_sc, ab_sc, tk, all_vis)
        return carry

    lax.fori_loop(0, qi, pair, 0)
    last_b = pl.multiple_of(jnp.where(qi == 0, d1, qi * tq - tk), tk)
    for hh in heads:
        pv_stage(hh, pb_sc, ab_sc, vblk(hh, last_b))

    s1 = jnp.sum(lq1_ref[...] * lk1_ref[...], axis=1, keepdims=True)
    s2 = jnp.sum(lq2_ref[...] * lk2_ref[...], axis=1, keepdims=True)
    lam = jnp.exp(s1) - jnp.exp(s2) + LAMBDA_INIT
    for hh in heads:
        o = acc_sc[hh, :, 0:ATT_V_DIM] / acc_sc[hh, :, ATT_V_DIM:]
        a = o[:tq] - lam * o[tq:]
        y = a * lax.rsqrt(jnp.mean(a * a, axis=1, keepdims=True) + LN_EPS) * ng_ref[...]
        o_ref[0, :, cols(hh)] = (y * (1.0 - LAMBDA_INIT)).astype(BF16)


def _attention(zb, zbm, lq1, lk1, lq2, lk2, ng, *, tq, nh):
    B, S, _ = zb.shape
    nq = S // tq
    tk = tq // 2
    w = nh * LANES
    qb, kb, vb = C_AQ // w, C_AK // w, C_AV // w
    lam_spec = pl.BlockSpec((1, ATT_QK_DIM), lambda b, h, i: (0, 0))
    return pl.pallas_call(
        functools.partial(_attn_kernel, tq=tq, tk=tk, rc=32, nh=nh),
        grid=(B, ATT_HEADS // nh, nq),
        in_specs=[
            pl.BlockSpec((1, tq, w), lambda b, h, i: (b, i, qb + h)),
            pl.BlockSpec((1, S, w), lambda b, h, i: (b, 0, kb + h)),
            pl.BlockSpec((1, S, w), lambda b, h, i: (b, 0, vb + h)),
            pl.BlockSpec((1, BLOCK, w), lambda b, h, i: (0, 0, kb + h)),
            pl.BlockSpec((1, BLOCK, w), lambda b, h, i: (0, 0, vb + h)),
            lam_spec, lam_spec, lam_spec, lam_spec,
            pl.BlockSpec((1, ATT_V_DIM), lambda b, h, i: (0, 0)),
        ],
        out_specs=pl.BlockSpec((1, tq, w), lambda b, h, i: (b, i, h)),
        out_shape=jax.ShapeDtypeStruct((B, S, ATT_HEADS * ATT_V_DIM), BF16),
        scratch_shapes=[
            pltpu.VMEM((nh, 2 * tq, tk + BLOCK), F32), pltpu.VMEM((nh, 2 * tq, tk), F32),
            pltpu.VMEM((nh, 2 * tq, tk + BLOCK), BF16), pltpu.VMEM((nh, 2 * tq, tk), BF16),
            pltpu.VMEM((nh, 2 * tq, LANES), F32), pltpu.VMEM((nh, 2 * tq, LANES), F32),
            pltpu.VMEM((nh, 2 * tq, LANES), F32),
            pltpu.VMEM((nh, 2 * tq, 2 * ATT_V_DIM), F32),
        ],
        compiler_params=_cparams(3),
        name="diff_attention",
    )(zb, zb, zb, zbm, zbm, lq1, lk1, lq2, lk2, ng)


def _mlstm_kernel(xqk_ref, xv_ref, xo_ref, xg_ref, zm_ref, gm_ref, cw_ref, cb_ref, gb_ref, ng_ref, sh_ref, o_ref,
                  ct_sc, n_sc, m_sc, prev_sc, *, nbps):
    c = pl.program_id(1)
    L = BLOCK

    @pl.when(c == 0)
    def _():
        ct_sc[...] = jnp.zeros_like(ct_sc)
        n_sc[...] = jnp.zeros_like(n_sc)
        m_sc[...] = jnp.zeros_like(m_sc)
        prev_sc[...] = jnp.zeros_like(prev_sc)

    is_meta = c == 0
    row = lax.broadcasted_iota(jnp.int32, (L, 1), 0)
    valid = jnp.logical_or(c > 0, row >= PAD)
    sidx = lax.broadcasted_iota(jnp.int32, (L, L), 0)
    ridx = lax.broadcasted_iota(jnp.int32, (L, L), 1)
    causal = ridx <= sidx
    tri = causal.astype(BF16)
    elems = range(nbps)
    pairs = [(bb, h) for bb in elems for h in range(ML_HEADS)]
    sl = lambda h: slice(h * ML_DH, (h + 1) * ML_DH)
    st = lambda bb, h: bb * 8 + h

    q, k, v, ig, bcs, b_t, ig_t = {}, {}, {}, {}, {}, {}, {}
    for bb in elems:
        qk_pre = jnp.where(is_meta, zm_ref[0, :, C_MQK:C_MQK + 2 * ML_WIDTH], xqk_ref[bb])
        v[bb] = jnp.where(is_meta, zm_ref[0, :, C_MV:C_MV + ML_WIDTH], xv_ref[bb])
        gates = jnp.where(is_meta, gm_ref[0], xg_ref[bb])
        prev = prev_sc[bb * L:(bb + 1) * L, :]
        ext = jnp.concatenate([qk_pre, prev], axis=0)
        prev_sc[bb * L:(bb + 1) * L, :] = qk_pre
        shifted = jnp.dot(sh_ref[...], ext, preferred_element_type=F32)
        acc = cb_ref[...] + cw_ref[CONV_K - 1:CONV_K, :] * qk_pre.astype(F32)
        for s in range(1, CONV_K):
            acc = acc + cw_ref[CONV_K - 1 - s:CONV_K - s, :] * shifted[(s - 1) * L:s * L, :]
        qk = acc * _sigmoid(acc)
        qk = jnp.where(valid, qk, 0.0)
        q[bb] = qk[:, :ML_WIDTH]
        k[bb] = qk[:, ML_WIDTH:] * (ML_DH ** -0.5)

        igv = gates[:, :LANES] + gb_ref[:, :LANES]
        fg = gates[:, LANES:] + gb_ref[:, LANES:]
        lf = jnp.minimum(fg, 0.0) - jnp.log1p(jnp.exp(-jnp.abs(fg)))
        igv = jnp.where(valid, igv, NEG)
        lf = jnp.where(valid, lf, 0.0)
        lf_hi = lf.astype(BF16)
        lf_lo = (lf - lf_hi.astype(F32)).astype(BF16)
        bcs[bb] = (jnp.dot(tri, lf_hi, preferred_element_type=F32)
                   + jnp.dot(tri, lf_lo, preferred_element_type=F32))
        ig[bb] = igv
        b_t[bb] = bcs[bb].T
        ig_t[bb] = igv.T

    bc, m_h, dlog, inter, m_s = {}, {}, {}, {}, {}
    for p in pairs:
        bb, h = p
        bc[p] = bcs[bb][:, h:h + 1]
        m_h[p] = m_sc[st(bb, h):st(bb, h) + 1, 0:1]
        dlog[p] = jnp.where(causal, (bc[p] - b_t[bb][h:h + 1, :]) + ig_t[bb][h:h + 1, :], NEG)
        inter[p] = bc[p] + m_h[p]
    for p in pairs:
        m_s[p] = jnp.maximum(inter[p], jnp.max(dlog[p], axis=1, keepdims=True))

    w_intra, w_inter, qb, kb, s, qc = {}, {}, {}, {}, {}, {}
    for p in pairs:
        bb, h = p
        w_intra[p] = jnp.exp(dlog[p] - m_s[p])
        w_inter[p] = jnp.exp(inter[p] - m_s[p])
        qb[p] = q[bb][:, sl(h)].astype(BF16)
        kb[p] = k[bb][:, sl(h)].astype(BF16)
    for p in pairs:
        bb, h = p
        s[p] = lax.dot_general(qb[p], kb[p], (((1,), (1,)), ((), ())), preferred_element_type=F32) * w_intra[p]
        qc[p] = jnp.dot(qb[p], ct_sc[bb * ML_HEADS + h].astype(BF16), preferred_element_type=F32)

    hh = {}
    for p in pairs:
        bb, h = p
        nrow = n_sc[st(bb, h):st(bb, h) + 1, :]
        num = w_inter[p] * qc[p] + jnp.dot(s[p].astype(BF16), v[bb][:, sl(h)], preferred_element_type=F32)
        den = (w_inter[p] * jnp.sum(q[bb][:, sl(h)] * nrow, axis=1, keepdims=True)
               + jnp.sum(s[p], axis=1, keepdims=True))
        hh[p] = num / jnp.maximum(jnp.abs(den), jnp.exp(-m_s[p]))

    for p in pairs:
        bb, h = p
        b_last = bc[p][L - 1:L, :]
        upd = (b_last - bc[p]) + ig[bb][:, h:h + 1]
        m_new = jnp.maximum(b_last + m_h[p], jnp.max(upd, axis=0, keepdims=True))
        w_old = jnp.exp(b_last + m_h[p] - m_new)
        w_r = jnp.exp(upd - m_new)
        vw = (v[bb][:, sl(h)].astype(F32) * w_r).astype(BF16)
        ci = bb * ML_HEADS + h
        ct_sc[ci] = w_old * ct_sc[ci] + lax.dot_general(kb[p], vw, (((0,), (0,)), ((), ())),
                                                         preferred_element_type=F32)
        r = st(bb, h)
        n_sc[r:r + 1, :] = w_old * n_sc[r:r + 1, :] + jnp.sum(k[bb][:, sl(h)] * w_r, axis=0, keepdims=True)
        m_sc[r:r + 1, :] = jnp.broadcast_to(m_new, (1, LANES))

    outs = {}
    for p in pairs:
        bb, h = p
        hn = hh[p] * lax.rsqrt(jnp.mean(hh[p] * hh[p], axis=1, keepdims=True) + LN_EPS) * ng_ref[:, sl(h)]
        outs[p] = _sigmoid(xo_ref[bb, :, sl(h)].astype(F32)) * hn

    @pl.when(c > 0)
    def _():
        for bb in elems:
            o_ref[bb] = jnp.concatenate([outs[(bb, h)] for h in range(ML_HEADS)], axis=1).astype(BF16)


def _mlstm(zb, gates, zbm, gatesm, conv_w, conv_b, gate_b, ng, *, nbps):
    B, S, _ = zb.shape
    nc = S // BLOCK + 1
    t = np.arange(BLOCK)
    shift = np.zeros(((CONV_K - 1) * BLOCK, 2 * BLOCK), np.float32)
    for s in range(1, CONV_K):
        shift[(s - 1) * BLOCK + t, np.where(t >= s, t - s, 2 * BLOCK + t - s)] = 1.0

    def xc(c):
        return jnp.maximum(c - 1, 0)

    return pl.pallas_call(
        functools.partial(_mlstm_kernel, nbps=nbps),
        grid=(B // nbps, nc),
        in_specs=[
            pl.BlockSpec((nbps, BLOCK, 2 * ML_WIDTH), lambda b, c: (b, xc(c), C_MQK // (2 * ML_WIDTH))),
            pl.BlockSpec((nbps, BLOCK, ML_WIDTH), lambda b, c: (b, xc(c), C_MV // ML_WIDTH)),
            pl.BlockSpec((nbps, BLOCK, ML_WIDTH), lambda b, c: (b, xc(c), C_MO // ML_WIDTH)),
            pl.BlockSpec((nbps, BLOCK, GATE_W), lambda b, c: (b, xc(c), 0)),
            pl.BlockSpec((1, BLOCK, ZB_W), lambda b, c: (0, 0, 0)),
            pl.BlockSpec((1, BLOCK, GATE_W), lambda b, c: (0, 0, 0)),
            pl.BlockSpec((CONV_K, 2 * ML_WIDTH), lambda b, c: (0, 0)),
            pl.BlockSpec((1, 2 * ML_WIDTH), lambda b, c: (0, 0)),
            pl.BlockSpec((1, GATE_W), lambda b, c: (0, 0)),
            pl.BlockSpec((1, ML_WIDTH), lambda b, c: (0, 0)),
            pl.BlockSpec(((CONV_K - 1) * BLOCK, 2 * BLOCK), lambda b, c: (0, 0)),
        ],
        out_specs=pl.BlockSpec((nbps, BLOCK, ML_WIDTH), lambda b, c: (b, xc(c), 0)),
        out_shape=jax.ShapeDtypeStruct((B, S, ML_WIDTH), BF16),
        scratch_shapes=[
            pltpu.VMEM((nbps * ML_HEADS, ML_DH, ML_DH), F32),
            pltpu.VMEM((nbps * 8, ML_DH), F32),
            pltpu.VMEM((nbps * 8, LANES), F32),
            pltpu.VMEM((nbps * BLOCK, 2 * ML_WIDTH), BF16),
        ],
        compiler_params=_cparams(2),
        name="mlstm",
    )(zb, zb, zb, gates, zbm, gatesm, conv_w, conv_b, gate_b, ng, jnp.asarray(shift, BF16))


def _outproj_kernel(x_ref, ya_ref, ym_ref, ga_ref, gmm_ref, eg_ref, eb_ref, wa_ref, wm_ref, wo_ref,
                    g1_ref, b1_ref, wr_ref, br_ref,
                    h1_ref, eid_ref, gate_ref, rank_ref, cnt_ref, run_sc, *, tm, ts):
    i = pl.program_id(0)
    subs = range(tm // ts)
    rows = lambda j: slice(j * ts, (j + 1) * ts)

    @pl.when(i == 0)
    def _():
        run_sc[...] = jnp.zeros_like(run_sc)

    sig = lambda g: 0.5 * jnp.tanh(0.5 * g) + 0.5
    pa = {j: jnp.dot(ya_ref[rows(j), :], wa_ref[...], preferred_element_type=F32) for j in subs}
    pm = {j: jnp.dot(ym_ref[rows(j), :], wm_ref[...], preferred_element_type=F32) for j in subs}
    h0 = {j: _ln(x_ref[rows(j), :], eg_ref[...], eb_ref[...]) for j in subs}
    merged = {j: sig(ga_ref[rows(j), :]) * pa[j].astype(BF16) + sig(gmm_ref[rows(j), :]) * pm[j].astype(BF16)
              for j in subs}
    mix = {j: jnp.dot(merged[j], wo_ref[...], preferred_element_type=F32) for j in subs}
    h1 = {j: _ln(DN_ALPHA * h0[j] + mix[j], g1_ref[...], b1_ref[...]) for j in subs}
    for j in subs:
        _store_row_tiles(h1_ref.at[pl.ds(j * ts * ROW_TILE, ts * ROW_TILE)], h1[j])

    logits = {}
    for j in subs:
        h_hi = h1[j].astype(BF16)
        h_mid = (h1[j] - h_hi.astype(F32)).astype(BF16)
        hh = jnp.dot(h_hi, wr_ref[...], preferred_element_type=F32)
        logits[j] = (hh[:, :LANES] + hh[:, LANES:]
                     + jnp.dot(h_mid, wr_ref[:, :LANES], preferred_element_type=F32) + br_ref[...])

    lane = lax.broadcasted_iota(jnp.int32, (ts, LANES), 1)
    r_i = lax.broadcasted_iota(jnp.int32, (ts, ts), 0)
    c_i = lax.broadcasted_iota(jnp.int32, (ts, ts), 1)
    strict = (c_i < r_i).astype(BF16)
    sel_e, ex, den, onehots, oh = {}, {}, {}, {}, {}
    for j in subs:
        work = logits[j]
        es, vs = [], []
        for _ in range(TOP_K):
            mv = jnp.max(work, axis=1, keepdims=True)
            e = jnp.min(jnp.where(work == mv, lane, LANES), axis=1, keepdims=True)
            es.append(e)
            vs.append(mv)
            work = jnp.where(lane == e, -jnp.inf, work)
        sel_e[j] = es
        ex[j] = [jnp.exp(v - vs[0]) for v in vs]
        den[j] = ex[j][0] + ex[j][1] + ex[j][2] + ex[j][3]
        onehots[j] = [lane == e for e in es]
        acc = jnp.zeros((ts, LANES), F32)
        for o in onehots[j]:
            acc = acc + o.astype(F32)
        oh[j] = acc

    run = run_sc[...]
    for j in subs:
        before = jnp.dot(strict, oh[j].astype(BF16), preferred_element_type=F32) + run
        run = run + jnp.sum(oh[j], axis=0, keepdims=True)
        eid = jnp.zeros((ts, LANES), F32)
        gate = jnp.zeros((ts, LANES), F32)
        rank = jnp.zeros((ts, LANES), F32)
        for kk in range(TOP_K):
            here = lane == kk
            eid = jnp.where(here, sel_e[j][kk].astype(F32), eid)
            gate = jnp.where(here, ex[j][kk] / den[j], gate)
            rk = jnp.sum(jnp.where(onehots[j][kk], before, 0.0), axis=1, keepdims=True)
            rank = jnp.where(here, rk, rank)
        gate_ref[rows(j), :] = gate
        eid_ref[:, rows(j)] = eid.T[0:8, :]
        rank_ref[:, rows(j)] = rank.T[0:8, :]
    run_sc[...] = run
    cnt_ref[...] = run


def _outproj(x2, yatt, yml, zb2, eg, eb, wa, wm, wo, g1, b1, wr, br, *, tm, ts):
    N, D = x2.shape
    vec = lambda w: pl.BlockSpec((1, w), lambda i: (0, 0))
    full = lambda a, b: pl.BlockSpec((a, b), lambda i: (0, 0))
    tile = lambda w: pl.BlockSpec((tm, w), lambda i: (i, 0))
    return pl.pallas_call(
        functools.partial(_outproj_kernel, tm=tm, ts=ts),
        grid=(N // tm,),
        in_specs=[
            tile(D), tile(ATT_HEADS * ATT_V_DIM), tile(ML_WIDTH),
            pl.BlockSpec((tm, D), lambda i: (i, C_GA // D)),
            pl.BlockSpec((tm, D), lambda i: (i, C_GM // D)),
            vec(D), vec(D),
            full(ATT_HEADS * ATT_V_DIM, D), full(ML_WIDTH, D), full(D, D),
            vec(D), vec(D), full(D, 2 * LANES), vec(LANES),
        ],
        out_specs=[pl.BlockSpec((tm * ROW_TILE, LANES), lambda i: (i, 0)),
                   pl.BlockSpec((8, tm), lambda i: (0, i)), tile(LANES),
                   pl.BlockSpec((8, tm), lambda i: (0, i)), vec(LANES)],
        out_shape=[
            jax.ShapeDtypeStruct((N * ROW_TILE, LANES), F32),
            jax.ShapeDtypeStruct((8, N), F32),
            jax.ShapeDtypeStruct((N, LANES), F32),
            jax.ShapeDtypeStruct((8, N), F32),
            jax.ShapeDtypeStruct((1, LANES), F32),
        ],
        scratch_shapes=[pltpu.VMEM((1, LANES), F32)],
        compiler_params=_cparams(1),
        name="outproj_router",
    )(x2, yatt, yml, zb2, zb2, eg, eb, wa, wm, wo, g1, b1, wr, br)


def _row_copies(src_row, dst_row, sem, n, issue):
    def body(r, carry):
        for kk in range(TOP_K):
            cp = pltpu.make_async_copy(src_row(r, kk), dst_row(r, kk), sem)
            if issue:
                cp.start(priority=kk % 2)
            else:
                cp.wait()
        return carry
    lax.fori_loop(0, n, body, 0, unroll=8)


def _dispatch_kernel(pad_pos_ref, pad_len_ref, d_ref, h_ref, xs_ref, zero_sc, sem, *, tm, max_pad):
    @pl.when(pl.program_id(0) == 0)
    def _():
        zero_sc[...] = jnp.zeros_like(zero_sc)
        bits = [1 << b for b in reversed(range(max_pad.bit_length()))]

        def pad_copies(e, issue):
            pos = pad_pos_ref[e]
            length = pad_len_ref[e]
            for bit in bits:
                take = (length & bit) != 0

                @pl.when(take)
                def _(pos=pos, bit=bit):
                    dst = xs_ref.at[pl.ds(pl.multiple_of(pos * ROW_TILE, ROW_TILE), bit * ROW_TILE)]
                    cp = pltpu.make_async_copy(zero_sc.at[pl.ds(0, bit * ROW_TILE)], dst, sem.at[1])
                    if issue:
                        cp.start()
                    else:
                        cp.wait()
                pos = pos + jnp.where(take, bit, 0)

        def per_expert(e, carry):
            pad_copies(e, True)
            pad_copies(e, False)
            return carry

        lax.fori_loop(0, N_EXPERTS, per_expert, 0)

    src = lambda r, kk: _row_tile(h_ref, r)
    dst = lambda r, kk: _row_tile(xs_ref, d_ref[0, 0, kk * tm + r])
    _row_copies(src, dst, sem.at[0], tm, True)
    _row_copies(src, dst, sem.at[0], tm, False)


def _dispatch(pad_pos, pad_len, dest3, h1t, rows_out, *, tm, max_pad):
    N = h1t.shape[0] // ROW_TILE
    zero_rows = 1 << (max_pad.bit_length() - 1)
    return pl.pallas_call(
        functools.partial(_dispatch_kernel, tm=tm, max_pad=max_pad),
        grid_spec=pltpu.PrefetchScalarGridSpec(
            num_scalar_prefetch=2,
            grid=(N // tm,),
            in_specs=[
                pl.BlockSpec((1, 1, tm * TOP_K), lambda i, pp, pn: (i, 0, 0), memory_space=pltpu.SMEM),
                pl.BlockSpec((tm * ROW_TILE, LANES), lambda i, pp, pn: (i, 0)),
            ],
            out_specs=pl.BlockSpec(memory_space=pl.ANY),
            scratch_shapes=[pltpu.VMEM((zero_rows * ROW_TILE, LANES), F32), pltpu.SemaphoreType.DMA((2,))],
        ),
        out_shape=jax.ShapeDtypeStruct((rows_out * ROW_TILE, LANES), F32),
        compiler_params=_cparams(1),
        name="moe_dispatch",
    )(pad_pos, pad_len, dest3, h1t)


def _select_matrix():
    j = np.arange(2 * LANES)
    sel = np.zeros((2 * LANES, 2 * LANES), np.float32)
    sel[j, (j % 2) * LANES + j // 2] = 1.0
    return jnp.asarray(sel, BF16)


def _expert_kernel(be_ref, nu_ref, x_ref, wgu_ref, bg_ref, bl_ref, wd_ref, bd_ref, sel_ref, y_ref,
                   wg_sc, wl_sc, wd_sc, *, tm):
    i = pl.program_id(0)
    active = i < nu_ref[0]
    new_expert = jnp.logical_or(i == 0, be_ref[i] != be_ref[jnp.maximum(i - 1, 0)])

    @pl.when(jnp.logical_and(active, new_expert))
    def _():
        for c in range(wgu_ref.shape[2] // (2 * LANES)):
            blk = wgu_ref[0, :, c * 2 * LANES:(c + 1) * 2 * LANES].astype(BF16)
            r = jnp.dot(blk, sel_ref[...], preferred_element_type=F32).astype(BF16)
            wg_sc[:, c * LANES:(c + 1) * LANES] = r[:, :LANES]
            wl_sc[:, c * LANES:(c + 1) * LANES] = r[:, LANES:]
        wd_sc[...] = wd_ref[0].astype(BF16)

    @pl.when(active)
    def _():
        xb = _load_row_tiles(x_ref, tm).astype(BF16)
        hg = jnp.dot(xb, wg_sc[...], preferred_element_type=F32) + bg_ref[0]
        hl = jnp.dot(xb, wl_sc[...], preferred_element_type=F32) + bl_ref[0]
        glu = jnp.minimum(hg, SWIGLU_LIMIT)
        lin = jnp.clip(hl, -SWIGLU_LIMIT, SWIGLU_LIMIT)
        act = glu * _sigmoid(SWIGLU_ALPHA * glu) * (lin + 1.0)
        _store_row_tiles(y_ref, jnp.dot(act.astype(BF16), wd_sc[...], preferred_element_type=F32) + bd_ref[0])


def _experts(blk_exp, n_used, xs, wgu, bg, bl, wd, bd, *, tm):
    MP = xs.shape[0] // ROW_TILE
    _, D, F2 = wgu.shape
    F = F2 // 2
    rows = pl.BlockSpec((tm * ROW_TILE, LANES), lambda i, be, nu: (jnp.minimum(i, nu[0] - 1), 0))
    wsp = lambda a, b: pl.BlockSpec((1, a, b), lambda i, be, nu: (be[i], 0, 0))
    return pl.pallas_call(
        functools.partial(_expert_kernel, tm=tm),
        grid_spec=pltpu.PrefetchScalarGridSpec(
            num_scalar_prefetch=2,
            grid=(MP // tm,),
            in_specs=[rows, wsp(D, F2), wsp(1, F), wsp(1, F), wsp(F, D), wsp(1, D),
                      pl.BlockSpec((2 * LANES, 2 * LANES), lambda i, be, nu: (0, 0))],
            out_specs=rows,
            scratch_shapes=[pltpu.VMEM((D, F), BF16), pltpu.VMEM((D, F), BF16), pltpu.VMEM((F, D), BF16)],
        ),
        out_shape=jax.ShapeDtypeStruct(xs.shape, F32),
        compiler_params=_cparams(1),
        name="moe_experts",
    )(blk_exp, n_used, xs, wgu, bg, bl, wd, bd, _select_matrix())


def _combine_kernel(dcur_ref, dnxt_ref, h1_ref, gate_ref, g2_ref, b2_ref, ys_ref, o_ref, ybuf, sem, *, tm, nsteps):
    i = pl.program_id(0)
    slot = i % 2

    def copies(d_ref, s, issue):
        src = lambda r, kk: _row_tile(ys_ref, d_ref[0, 0, kk * tm + r])
        dst = lambda r, kk: _row_tile(ybuf.at[s, kk], r)
        _row_copies(src, dst, sem.at[s], tm, issue)

    @pl.when(i == 0)
    def _():
        copies(dcur_ref, 0, True)

    @pl.when(i + 1 < nsteps)
    def _():
        copies(dnxt_ref, 1 - slot, True)

    copies(dcur_ref, slot, False)
    acc = DN_ALPHA * _load_row_tiles(h1_ref, tm)
    for kk in range(TOP_K):
        acc = acc + gate_ref[:, kk:kk + 1] * _load_row_tiles(ybuf.at[slot, kk], tm)
    o_ref[...] = _ln(acc, g2_ref[...], b2_ref[...])


def _combine(dest3, h1t, gate, g2, b2, ys, *, tm):
    N, D = h1t.shape[0] // ROW_TILE, D_MODEL
    nsteps = N // tm
    dspec = lambda f: pl.BlockSpec((1, 1, tm * TOP_K), f, memory_space=pltpu.SMEM)
    return pl.pallas_call(
        functools.partial(_combine_kernel, tm=tm, nsteps=nsteps),
        grid=(nsteps,),
        in_specs=[
            dspec(lambda i: (i, 0, 0)),
            dspec(lambda i: (jnp.minimum(i + 1, nsteps - 1), 0, 0)),
            pl.BlockSpec((tm * ROW_TILE, LANES), lambda i: (i, 0)),
            pl.BlockSpec((tm, LANES), lambda i: (i, 0)),
            pl.BlockSpec((1, D), lambda i: (0, 0)),
            pl.BlockSpec((1, D), lambda i: (0, 0)),
            pl.BlockSpec(memory_space=pl.ANY),
        ],
        out_specs=pl.BlockSpec((tm, D), lambda i: (i, 0)),
        out_shape=jax.ShapeDtypeStruct((N, D), F32),
        scratch_shapes=[pltpu.VMEM((2, TOP_K, tm * ROW_TILE, LANES), F32), pltpu.SemaphoreType.DMA((2,))],
        compiler_params=_cparams(1),
        name="moe_combine",
    )(dest3, dest3, h1t, gate, g2, b2, ys)


def _rotary_column_order():
    idx = np.empty((ATT_HEADS * LANES,), np.int32)
    for h in range(ATT_HEADS):
        for half in range(2):
            for sub in range(2):
                for dd in range(32):
                    idx[h * LANES + half * 64 + sub * 32 + dd] = (2 * h + sub) * ATT_QK_DIM + half * 32 + dd
    return idx


def _rotary_tables(pos):
    half = ATT_QK_DIM // 2
    inv_freq = ROPE_THETA ** (-jnp.arange(half, dtype=F32) / half)
    ang = pos.astype(F32)[:, None] * inv_freq[None, :]
    cos = jnp.tile(jnp.cos(ang), (1, 4))
    sin = jnp.tile(jnp.sin(ang), (1, 4))
    sign = jnp.where(jnp.arange(LANES) < 64, -1.0, 1.0).astype(F32)
    return cos, sin * sign[None, :]


def _pick(n, prefs):
    for t in prefs:
        if n % t == 0:
            return t
    raise ValueError(f"unsupported size {n}")


def kernel(x, meta, emb_ln_g, emb_ln_b, w_in, conv_w, conv_b, gate_bias, lam_q1, lam_k1, lam_q2, lam_k2,
           att_norm_g, ml_norm_g, w_att_out, w_ml_out, w_o, ln1_g, ln1_b, w_router, b_router,
           w_gu, b_gu, w_down, b_down, ln2_g, ln2_b):
    B, S, D = x.shape
    assert D == D_MODEL and S % 512 == 0 and w_in.shape[0] == DEPTH
    N = B * S
    row2 = lambda a: a.reshape(1, -1)

    w = w_in[0]
    o_aq, o_ak, o_av, o_mqk, o_mv, o_mo, o_gt, o_ga, o_gm = 0, 512, 1024, 1536, 2560, 3072, 3584, 3592, 4616
    perm = _rotary_column_order()
    gpad = jnp.zeros((D, LANES - ML_HEADS), F32)
    w_all = jnp.concatenate([
        w[:, o_ga:o_ga + D], w[:, o_gm:o_gm + D], w[:, o_mqk:o_mqk + 2 * ML_WIDTH],
        w[:, o_aq:o_aq + 512][:, perm], w[:, o_ak:o_ak + 512][:, perm], w[:, o_av:o_av + 512],
        w[:, o_mv:o_mv + ML_WIDTH], w[:, o_mo:o_mo + ML_WIDTH],
        w[:, o_gt:o_gt + ML_HEADS], gpad, w[:, o_gt + ML_HEADS:o_gt + 2 * ML_HEADS], gpad,
    ], axis=1).astype(BF16)
    gb = gate_bias[0]
    zpad = jnp.zeros((LANES - ML_HEADS,), F32)
    gate_b = jnp.concatenate([gb[:ML_HEADS], zpad, gb[ML_HEADS:], zpad]).reshape(1, GATE_W)

    cos_x, sin_x = _rotary_tables(N_META + jnp.arange(S))
    cos_m, sin_m = _rotary_tables(jnp.maximum(jnp.arange(BLOCK) - PAD, 0))

    eg, eb = row2(emb_ln_g), row2(emb_ln_b)

    tm_in = 512
    zb, gates = _inproj(x, eg, eb, w_all, cos_x, sin_x, tm=tm_in)
    xm = jnp.concatenate([jnp.zeros((PAD, D), x.dtype), meta.astype(x.dtype)], axis=0)[None]
    zbm, gatesm = _inproj(xm, eg, eb, w_all, cos_m, sin_m, tm=BLOCK, first_valid_row=PAD)

    tq = 512
    yatt = _attention(zb, zbm, row2(lam_q1[0]), row2(lam_k1[0]), row2(lam_q2[0]), row2(lam_k2[0]),
                      row2(att_norm_g[0]), tq=tq, nh=1)
    yml = _mlstm(zb, gates, zbm, gatesm, conv_w[0], row2(conv_b[0]), gate_b, row2(ml_norm_g[0]),
                 nbps=4 if B % 4 == 0 else (2 if B % 2 == 0 else 1))

    wr32 = jnp.concatenate([w_router[0], jnp.zeros((D, LANES - N_EXPERTS), F32)], axis=1)
    wr_hi = wr32.astype(BF16)
    wr = jnp.concatenate([wr_hi, (wr32 - wr_hi.astype(F32)).astype(BF16)], axis=1)
    br = jnp.concatenate([b_router[0], jnp.full((LANES - N_EXPERTS,), NEG, F32)]).reshape(1, LANES)
    tm_out = 1024
    h1, eid, gate, rank, cnt = _outproj(
        x.reshape(N, D), yatt.reshape(N, -1), yml.reshape(N, -1), zb.reshape(N, ZB_W), eg, eb,
        w_att_out[0].astype(BF16), w_ml_out[0].astype(BF16), w_o[0].astype(BF16),
        row2(ln1_g[0]), row2(ln1_b[0]), wr, br, tm=tm_out, ts=512)

    tm_e = 512
    M = N * TOP_K
    nb = (M + N_EXPERTS * (tm_e - 1) + tm_e - 1) // tm_e
    counts = cnt[0, :N_EXPERTS].astype(jnp.int32)
    nblk = (counts + tm_e - 1) // tm_e
    cum = jnp.cumsum(nblk)
    pstart = (cum - nblk) * tm_e
    eid_t = eid[:TOP_K].astype(jnp.int32)
    base = jnp.sum(jnp.where(eid_t[None] == jnp.arange(N_EXPERTS)[:, None, None], pstart[:, None, None], 0), axis=0)
    dest = base + rank[:TOP_K].astype(jnp.int32)
    n_used = cum[-1:].astype(jnp.int32)
    blk = jnp.minimum(jnp.arange(nb, dtype=jnp.int32), n_used[0] - 1)
    blk_exp = jnp.sum(blk[:, None] >= cum[None, :], axis=1).astype(jnp.int32)

    tm_d = 512
    dest3 = dest.reshape(TOP_K, N // tm_d, tm_d).transpose(1, 0, 2).reshape(N // tm_d, 1, TOP_K * tm_d)
    xs = _dispatch(pstart + counts, nblk * tm_e - counts, dest3, h1, nb * tm_e, tm=tm_d, max_pad=tm_e - 1)

    bgu = b_gu[0]
    ys = _experts(blk_exp, n_used, xs, w_gu[0], bgu[:, None, 0::2], bgu[:, None, 1::2],
                  w_down[0], b_down[0][:, None, :], tm=tm_e)

    out = _combine(dest3, h1, gate, row2(ln2_g[0]), row2(ln2_b[0]), ys, tm=tm_d)
    return out.reshape(B, S, D)
```

```python
import functools
import math

import jax
import jax.numpy as jnp
import numpy as np
from jax import lax
from jax.experimental import pallas as pl
from jax.experimental.pallas import tpu as pltpu

D_MODEL = 1024
N_META = 16
BLOCK = 128
PAD = BLOCK - N_META
NEG = -1e30
LN_EPS = 1e-5
ATT_HEADS = 4
ATT_QK_DIM = 64
ATT_V_DIM = 128
ROPE_THETA = 10000.0
ML_HEADS = 4
ML_DH = 128
ML_WIDTH = ML_HEADS * ML_DH
CONV_K = 4
N_EXPERTS = 32
TOP_K = 4
SWIGLU_LIMIT = 7.0
SWIGLU_ALPHA = 1.702
DEPTH = 1
DN_ALPHA = (2 * DEPTH) ** 0.25
LAMBDA_INIT = 0.8 - 0.6 * math.exp(-0.3 * 0)

LANES = 128
C_GA, C_GM, C_MQK, C_AQ, C_AK, C_AV, C_MV, C_MO = 0, 1024, 2048, 3072, 3584, 4096, 4608, 5120
ZB_W = 5632
GATE_W = 2 * LANES
W_ALL = ZB_W + GATE_W
CHUNK = 512

VMEM_LIMIT = 56 * 1024 * 1024

BF16 = jnp.bfloat16
F32 = jnp.float32


def _cparams(n_axes):
    return pltpu.CompilerParams(dimension_semantics=("arbitrary",) * n_axes, vmem_limit_bytes=VMEM_LIMIT)


def _ln(x, g, b):
    mu = jnp.mean(x, axis=-1, keepdims=True)
    xc = x - mu
    var = jnp.mean(xc * xc, axis=-1, keepdims=True)
    return xc * lax.rsqrt(var + LN_EPS) * g + b


def _sigmoid(x):
    return 1.0 / (1.0 + jnp.exp(-x))


ROW_TILE = D_MODEL // LANES


def _store_row_tiles(ref, val):
    tm = val.shape[0]
    for g in range(ROW_TILE):
        ref[pl.ds(g, tm, stride=ROW_TILE), :] = val[:, g * LANES:(g + 1) * LANES]


def _load_row_tiles(ref, tm):
    return jnp.concatenate([ref[pl.ds(g, tm, stride=ROW_TILE), :] for g in range(ROW_TILE)], axis=1)


def _row_tile(ref, r):
    return ref.at[pl.ds(pl.multiple_of(r * ROW_TILE, ROW_TILE), ROW_TILE)]


def _inproj_kernel(x_ref, g_ref, b_ref, w_ref, cos_ref, sin_ref, zb_ref, gt_ref, *, first_valid_row):
    x = x_ref[0]
    tm = x.shape[0]
    hb = _ln(x, g_ref[...], b_ref[...]).astype(BF16)
    cos = cos_ref[...]
    sin = sin_ref[...]
    if first_valid_row:
        rowmask = lax.broadcasted_iota(jnp.int32, (tm, 1), 0) >= first_valid_row
    for c in range(ZB_W // CHUNK):
        z = jnp.dot(hb, w_ref[:, c * CHUNK:(c + 1) * CHUNK], preferred_element_type=F32)
        if c * CHUNK in (C_AQ, C_AK):
            parts = []
            for h in range(ATT_HEADS):
                zh = z[:, h * LANES:(h + 1) * LANES]
                parts.append(zh * cos + pltpu.roll(zh, 64, 1) * sin)
            z = jnp.concatenate(parts, axis=1)
            if c * CHUNK == C_AQ:
                z = z * (ATT_QK_DIM ** -0.5 * math.log2(math.e))
        if first_valid_row:
            z = jnp.where(rowmask, z, 0.0)
        zb_ref[0, :, c * CHUNK:(c + 1) * CHUNK] = z.astype(BF16)
    zg = jnp.dot(hb, w_ref[:, ZB_W:W_ALL], preferred_element_type=F32)
    if first_valid_row:
        zg = jnp.where(rowmask, zg, 0.0)
    gt_ref[0] = zg


def _inproj(x3, g, b, w, cos, sin, *, tm, first_valid_row=0):
    B, S, D = x3.shape
    nt = S // tm
    return pl.pallas_call(
        functools.partial(_inproj_kernel, first_valid_row=first_valid_row),
        grid=(B, nt),
        in_specs=[
            pl.BlockSpec((1, tm, D), lambda bi, i: (bi, i, 0)),
            pl.BlockSpec((1, D), lambda bi, i: (0, 0)),
            pl.BlockSpec((1, D), lambda bi, i: (0, 0)),
            pl.BlockSpec((D, W_ALL), lambda bi, i: (0, 0)),
            pl.BlockSpec((tm, LANES), lambda bi, i: (i, 0)),
            pl.BlockSpec((tm, LANES), lambda bi, i: (i, 0)),
        ],
        out_specs=[
            pl.BlockSpec((1, tm, ZB_W), lambda bi, i: (bi, i, 0)),
            pl.BlockSpec((1, tm, GATE_W), lambda bi, i: (bi, i, 0)),
        ],
        out_shape=[
            jax.ShapeDtypeStruct((B, S, ZB_W), BF16),
            jax.ShapeDtypeStruct((B, S, GATE_W), F32),
        ],
        compiler_params=_cparams(2),
        name="inproj",
    )(x3, g, b, w, cos, sin)


def _attn_kernel(q_ref, k_ref, v_ref, km_ref, vm_ref, lq1_ref, lk1_ref, lq2_ref, lk2_ref, ng_ref, o_ref,
                 sa_sc, sb_sc, pa_sc, pb_sc, aa_sc, ab_sc, m_sc, acc_sc, *, tq, tk, rc, nh):
    assert tq == 2 * tk
    qi = pl.program_id(2)
    rows = 2 * tq
    heads = range(nh)
    cols = lambda hh: slice(hh * LANES, (hh + 1) * LANES)
    lane = lax.broadcasted_iota(jnp.int32, (tq, LANES), 1)
    is_map0 = (lane & 63) < 32
    qq = {}
    for hh in heads:
        q = q_ref[0, :, cols(hh)]
        zero = jnp.zeros_like(q)
        qq[hh] = jnp.concatenate([jnp.where(is_map0, q, zero), jnp.where(is_map0, zero, q)], axis=0)

    def qk_stage(hh, kblk, s_ref, col0=0):
        s_ref[hh, :, col0:col0 + kblk.shape[0]] = lax.dot_general(qq[hh], kblk, (((1,), (1,)), ((), ())),
                                                                  preferred_element_type=F32)

    def pv_stage(hh, p_ref, a_ref, vblk, first=False):
        width = vblk.shape[0]
        v1 = jnp.concatenate([vblk, jnp.ones_like(vblk)], axis=1)
        pv = jnp.dot(p_ref[hh, :, 0:width], v1, preferred_element_type=F32)
        if first:
            acc_sc[hh] = pv
        else:
            a = a_ref[hh]
            acc_sc[hh] = jnp.concatenate([a, a], axis=1) * acc_sc[hh] + pv

    def sm_stage(hh, s_ref, p_ref, a_ref, width, vis, first=False):
        nrep = width // LANES
        for r0 in range(0, rows, rc):
            kind = vis(r0)
            if kind == "none":
                continue
            s = s_ref[hh, r0:r0 + rc, 0:width]
            if kind != "all":
                s = kind(s)
            m_cur = jnp.max(s, axis=1, keepdims=True)
            if first:
                m_new = jnp.broadcast_to(m_cur, (rc, LANES))
            else:
                m_prev = m_sc[hh, r0:r0 + rc, :]
                m_new = jnp.maximum(m_prev, m_cur)
                a_ref[hh, r0:r0 + rc, :] = jnp.exp2(m_prev - m_new)
            m_sc[hh, r0:r0 + rc, :] = m_new
        for r0 in range(0, rows, rc):
            kind = vis(r0)
            if kind == "none":
                p_ref[hh, r0:r0 + rc, 0:width] = jnp.zeros((rc, width), BF16)
                a_ref[hh, r0:r0 + rc, :] = jnp.ones((rc, LANES), F32)
                continue
            s = s_ref[hh, r0:r0 + rc, 0:width]
            if kind != "all":
                s = kind(s)
            m_new = m_sc[hh, r0:r0 + rc, :]
            m_rep = m_new if nrep == 1 else jnp.concatenate([m_new] * nrep, axis=1)
            p_ref[hh, r0:r0 + rc, 0:width] = jnp.exp2(s - m_rep).astype(BF16)

    def first_vis(r0):
        q0 = r0 % tq
        def mask(s):
            rowq = q0 + lax.broadcasted_iota(jnp.int32, s.shape, 0)
            col = lax.broadcasted_iota(jnp.int32, s.shape, 1)
            keep = col >= tk + PAD
            if tk - 1 <= q0:
                keep = jnp.logical_or(keep, col < tk)
            else:
                keep = jnp.logical_or(keep, col <= rowq)
            return jnp.where(keep, s, NEG)
        return mask

    def diag_vis(d):
        def vis(r0):
            q0 = r0 % tq
            if d * tk + tk - 1 <= q0:
                return "all"
            if d * tk > q0 + rc - 1:
                return "none"
            def mask(s):
                rowq = q0 + lax.broadcasted_iota(jnp.int32, s.shape, 0)
                col = d * tk + lax.broadcasted_iota(jnp.int32, s.shape, 1)
                return jnp.where(col <= rowq, s, NEG)
            return mask
        return vis

    all_vis = lambda r0: "all"
    kblk = lambda hh, start: k_ref[0, pl.ds(start, tk), cols(hh)]
    vblk = lambda hh, start: v_ref[0, pl.ds(start, tk), cols(hh)]

    d0 = pl.multiple_of(qi * tq, tk)
    d1 = pl.multiple_of(qi * tq + tk, tk)
    for hh in heads:
        qk_stage(hh, kblk(hh, d0), sa_sc)
    for hh in heads:
        qk_stage(hh, km_ref[0, :, cols(hh)], sa_sc, col0=tk)
    for hh in heads:
        qk_stage(hh, kblk(hh, d1), sb_sc)
    for hh in heads:
        sm_stage(hh, sa_sc, pa_sc, None, tk + BLOCK, first_vis, first=True)
    for hh in heads:
        pv_stage(hh, pa_sc, None, jnp.concatenate([vblk(hh, d0), vm_ref[0, :, cols(hh)]], axis=0), first=True)
    for hh in heads:
        sm_stage(hh, sb_sc, pb_sc, ab_sc, tk, diag_vis(1))
    for hh in heads:
        qk_stage(hh, kblk(hh, 0), sa_sc)

    def pair(i, carry):
        a_start = pl.multiple_of(i * tq, tk)
        b_start = pl.multiple_of(i * tq + tk, tk)
        prev_b = pl.multiple_of(jnp.where(i == 0, d1, a_start - tk), tk)
        next_a = pl.multiple_of(jnp.minimum(a_start + tq, (qi - 1) * tq), tk)
        for hh in heads:
            qk_stage(hh, kblk(hh, b_start), sb_sc)
        for hh in heads:
            pv_stage(hh, pb_sc, ab_sc, vblk(hh, prev_b))
        for hh in heads:
            sm_stage(hh, sa_sc, pa_sc, aa_sc, tk, all_vis)
        for hh in heads:
            qk_stage(hh, kblk(hh, next_a), sa_sc)
        for hh in heads:
            pv_stage(hh, pa_sc, aa_sc, vblk(hh, a_start))
        for hh in heads:
            sm_stage(hh, sb_sc, pb_sc, ab_sc, tk, all_vis)
        return carry

    lax.fori_loop(0, qi, pair, 0)
    last_b = pl.multiple_of(jnp.where(qi == 0, d1, qi * tq - tk), tk)
    for hh in heads:
        pv_stage(hh, pb_sc, ab_sc, vblk(hh, last_b))

    s1 = jnp.sum(lq1_ref[...] * lk1_ref[...], axis=1, keepdims=True)
    s2 = jnp.sum(lq2_ref[...] * lk2_ref[...], axis=1, keepdims=True)
    lam = jnp.exp(s1) - jnp.exp(s2) + LAMBDA_INIT
    for hh in heads:
        o = acc_sc[hh, :, 0:ATT_V_DIM] / acc_sc[hh, :, ATT_V_DIM:]
        a = o[:tq] - lam * o[tq:]
        y = a * lax.rsqrt(jnp.mean(a * a, axis=1, keepdims=True) + LN_EPS) * ng_ref[...]
        o_ref[0, :, cols(hh)] = (y * (1.0 - LAMBDA_INIT)).astype(BF16)


def _attention(zb, zbm, lq1, lk1, lq2, lk2, ng, *, tq, nh):
    B, S, _ = zb.shape
    nq = S // tq
    tk = tq // 2
    w = nh * LANES
    qb, kb, vb = C_AQ // w, C_AK // w, C_AV // w
    lam_spec = pl.BlockSpec((1, ATT_QK_DIM), lambda b, h, i: (0, 0))
    return pl.pallas_call(
        functools.partial(_attn_kernel, tq=tq, tk=tk, rc=32, nh=nh),
        grid=(B, ATT_HEADS // nh, nq),
        in_specs=[
            pl.BlockSpec((1, tq, w), lambda b, h, i: (b, i, qb + h)),
            pl.BlockSpec((1, S, w), lambda b, h, i: (b, 0, kb + h)),
            pl.BlockSpec((1, S, w), lambda b, h, i: (b, 0, vb + h)),
            pl.BlockSpec((1, BLOCK, w), lambda b, h, i: (0, 0, kb + h)),
            pl.BlockSpec((1, BLOCK, w), lambda b, h, i: (0, 0, vb + h)),
            lam_spec, lam_spec, lam_spec, lam_spec,
            pl.BlockSpec((1, ATT_V_DIM), lambda b, h, i: (0, 0)),
        ],
        out_specs=pl.BlockSpec((1, tq, w), lambda b, h, i: (b, i, h)),
        out_shape=jax.ShapeDtypeStruct((B, S, ATT_HEADS * ATT_V_DIM), BF16),
        scratch_shapes=[
            pltpu.VMEM((nh, 2 * tq, tk + BLOCK), F32), pltpu.VMEM((nh, 2 * tq, tk), F32),
            pltpu.VMEM((nh, 2 * tq, tk + BLOCK), BF16), pltpu.VMEM((nh, 2 * tq, tk), BF16),
            pltpu.VMEM((nh, 2 * tq, LANES), F32), pltpu.VMEM((nh, 2 * tq, LANES), F32),
            pltpu.VMEM((nh, 2 * tq, LANES), F32),
            pltpu.VMEM((nh, 2 * tq, 2 * ATT_V_DIM), F32),
        ],
        compiler_params=_cparams(3),
        name="diff_attention",
    )(zb, zb, zb, zbm, zbm, lq1, lk1, lq2, lk2, ng)


def _mlstm_kernel(xqk_ref, xv_ref, xo_ref, xg_ref, zm_ref, gm_ref, cw_ref, cb_ref, gb_ref, ng_ref, sh_ref, o_ref,
                  ct_sc, n_sc, m_sc, prev_sc, *, nbps):
    c = pl.program_id(1)
    L = BLOCK

    @pl.when(c == 0)
    def _():
        ct_sc[...] = jnp.zeros_like(ct_sc)
        n_sc[...] = jnp.zeros_like(n_sc)
        m_sc[...] = jnp.zeros_like(m_sc)
        prev_sc[...] = jnp.zeros_like(prev_sc)

    is_meta = c == 0
    row = lax.broadcasted_iota(jnp.int32, (L, 1), 0)
    valid = jnp.logical_or(c > 0, row >= PAD)
    sidx = lax.broadcasted_iota(jnp.int32, (L, L), 0)
    ridx = lax.broadcasted_iota(jnp.int32, (L, L), 1)
    causal = ridx <= sidx
    tri = causal.astype(BF16)
    elems = range(nbps)
    pairs = [(bb, h) for bb in elems for h in range(ML_HEADS)]
    sl = lambda h: slice(h * ML_DH, (h + 1) * ML_DH)
    st = lambda bb, h: bb * 8 + h

    q, k, v, ig, bcs, b_t, ig_t = {}, {}, {}, {}, {}, {}, {}
    for bb in elems:
        qk_pre = jnp.where(is_meta, zm_ref[0, :, C_MQK:C_MQK + 2 * ML_WIDTH], xqk_ref[bb])
        v[bb] = jnp.where(is_meta, zm_ref[0, :, C_MV:C_MV + ML_WIDTH], xv_ref[bb])
        gates = jnp.where(is_meta, gm_ref[0], xg_ref[bb])
        prev = prev_sc[bb * L:(bb + 1) * L, :]
        ext = jnp.concatenate([qk_pre, prev], axis=0)
        prev_sc[bb * L:(bb + 1) * L, :] = qk_pre
        shifted = jnp.dot(sh_ref[...], ext, preferred_element_type=F32)
        acc = cb_ref[...] + cw_ref[CONV_K - 1:CONV_K, :] * qk_pre.astype(F32)
        for s in range(1, CONV_K):
            acc = acc + cw_ref[CONV_K - 1 - s:CONV_K - s, :] * shifted[(s - 1) * L:s * L, :]
        qk = acc * _sigmoid(acc)
        qk = jnp.where(valid, qk, 0.0)
        q[bb] = qk[:, :ML_WIDTH]
        k[bb] = qk[:, ML_WIDTH:] * (ML_DH ** -0.5)

        igv = gates[:, :LANES] + gb_ref[:, :LANES]
        fg = gates[:, LANES:] + gb_ref[:, LANES:]
        lf = jnp.minimum(fg, 0.0) - jnp.log1p(jnp.exp(-jnp.abs(fg)))
        igv = jnp.where(valid, igv, NEG)
        lf = jnp.where(valid, lf, 0.0)
        lf_hi = lf.astype(BF16)
        lf_lo = (lf - lf_hi.astype(F32)).astype(BF16)
        bcs[bb] = (jnp.dot(tri, lf_hi, preferred_element_type=F32)
                   + jnp.dot(tri, lf_lo, preferred_element_type=F32))
        ig[bb] = igv
        b_t[bb] = bcs[bb].T
        ig_t[bb] = igv.T

    bc, m_h, dlog, inter, m_s = {}, {}, {}, {}, {}
    for p in pairs:
        bb, h = p
        bc[p] = bcs[bb][:, h:h + 1]
        m_h[p] = m_sc[st(bb, h):st(bb, h) + 1, 0:1]
        dlog[p] = jnp.where(causal, (bc[p] - b_t[bb][h:h + 1, :]) + ig_t[bb][h:h + 1, :], NEG)
        inter[p] = bc[p] + m_h[p]
    for p in pairs:
        m_s[p] = jnp.maximum(inter[p], jnp.max(dlog[p], axis=1, keepdims=True))

    w_intra, w_inter, qb, kb, s, qc = {}, {}, {}, {}, {}, {}
    for p in pairs:
        bb, h = p
        w_intra[p] = jnp.exp(dlog[p] - m_s[p])
        w_inter[p] = jnp.exp(inter[p] - m_s[p])
        qb[p] = q[bb][:, sl(h)].astype(BF16)
        kb[p] = k[bb][:, sl(h)].astype(BF16)
    for p in pairs:
        bb, h = p
        s[p] = lax.dot_general(qb[p], kb[p], (((1,), (1,)), ((), ())), preferred_element_type=F32) * w_intra[p]
        qc[p] = jnp.dot(qb[p], ct_sc[bb * ML_HEADS + h].astype(BF16), preferred_element_type=F32)

    hh = {}
    for p in pairs:
        bb, h = p
        nrow = n_sc[st(bb, h):st(bb, h) + 1, :]
        num = w_inter[p] * qc[p] + jnp.dot(s[p].astype(BF16), v[bb][:, sl(h)], preferred_element_type=F32)
        den = (w_inter[p] * jnp.sum(q[bb][:, sl(h)] * nrow, axis=1, keepdims=True)
               + jnp.sum(s[p], axis=1, keepdims=True))
        hh[p] = num / jnp.maximum(jnp.abs(den), jnp.exp(-m_s[p]))

    for p in pairs:
        bb, h = p
        b_last = bc[p][L - 1:L, :]
        upd = (b_last - bc[p]) + ig[bb][:, h:h + 1]
        m_new = jnp.maximum(b_last + m_h[p], jnp.max(upd, axis=0, keepdims=True))
        w_old = jnp.exp(b_last + m_h[p] - m_new)
        w_r = jnp.exp(upd - m_new)
        vw = (v[bb][:, sl(h)].astype(F32) * w_r).astype(BF16)
        ci = bb * ML_HEADS + h
        ct_sc[ci] = w_old * ct_sc[ci] + lax.dot_general(kb[p], vw, (((0,), (0,)), ((), ())),
                                                         preferred_element_type=F32)
        r = st(bb, h)
        n_sc[r:r + 1, :] = w_old * n_sc[r:r + 1, :] + jnp.sum(k[bb][:, sl(h)] * w_r, axis=0, keepdims=True)
        m_sc[r:r + 1, :] = jnp.broadcast_to(m_new, (1, LANES))

    outs = {}
    for p in pairs:
        bb, h = p
        hn = hh[p] * lax.rsqrt(jnp.mean(hh[p] * hh[p], axis=1, keepdims=True) + LN_EPS) * ng_ref[:, sl(h)]
        outs[p] = _sigmoid(xo_ref[bb, :, sl(h)].astype(F32)) * hn

    @pl.when(c > 0)
    def _():
        for bb in elems:
            o_ref[bb] = jnp.concatenate([outs[(bb, h)] for h in range(ML_HEADS)], axis=1).astype(BF16)


def _mlstm(zb, gates, zbm, gatesm, conv_w, conv_b, gate_b, ng, *, nbps):
    B, S, _ = zb.shape
    nc = S // BLOCK + 1
    t = np.arange(BLOCK)
    shift = np.zeros(((CONV_K - 1) * BLOCK, 2 * BLOCK), np.float32)
    for s in range(1, CONV_K):
        shift[(s - 1) * BLOCK + t, np.where(t >= s, t - s, 2 * BLOCK + t - s)] = 1.0

    def xc(c):
        return jnp.maximum(c - 1, 0)

    return pl.pallas_call(
        functools.partial(_mlstm_kernel, nbps=nbps),
        grid=(B // nbps, nc),
        in_specs=[
            pl.BlockSpec((nbps, BLOCK, 2 * ML_WIDTH), lambda b, c: (b, xc(c), C_MQK // (2 * ML_WIDTH))),
            pl.BlockSpec((nbps, BLOCK, ML_WIDTH), lambda b, c: (b, xc(c), C_MV // ML_WIDTH)),
            pl.BlockSpec((nbps, BLOCK, ML_WIDTH), lambda b, c: (b, xc(c), C_MO // ML_WIDTH)),
            pl.BlockSpec((nbps, BLOCK, GATE_W), lambda b, c: (b, xc(c), 0)),
            pl.BlockSpec((1, BLOCK, ZB_W), lambda b, c: (0, 0, 0)),
            pl.BlockSpec((1, BLOCK, GATE_W), lambda b, c: (0, 0, 0)),
            pl.BlockSpec((CONV_K, 2 * ML_WIDTH), lambda b, c: (0, 0)),
            pl.BlockSpec((1, 2 * ML_WIDTH), lambda b, c: (0, 0)),
            pl.BlockSpec((1, GATE_W), lambda b, c: (0, 0)),
            pl.BlockSpec((1, ML_WIDTH), lambda b, c: (0, 0)),
            pl.BlockSpec(((CONV_K - 1) * BLOCK, 2 * BLOCK), lambda b, c: (0, 0)),
        ],
        out_specs=pl.BlockSpec((nbps, BLOCK, ML_WIDTH), lambda b, c: (b, xc(c), 0)),
        out_shape=jax.ShapeDtypeStruct((B, S, ML_WIDTH), BF16),
        scratch_shapes=[
            pltpu.VMEM((nbps * ML_HEADS, ML_DH, ML_DH), F32),
            pltpu.VMEM((nbps * 8, ML_DH), F32),
            pltpu.VMEM((nbps * 8, LANES), F32),
            pltpu.VMEM((nbps * BLOCK, 2 * ML_WIDTH), BF16),
        ],
        compiler_params=_cparams(2),
        name="mlstm",
    )(zb, zb, zb, gates, zbm, gatesm, conv_w, conv_b, gate_b, ng, jnp.asarray(shift, BF16))


def _outproj_kernel(x_ref, ya_ref, ym_ref, ga_ref, gmm_ref, eg_ref, eb_ref, wa_ref, wm_ref, wo_ref,
                    g1_ref, b1_ref, wr_ref, br_ref,
                    h1_ref, eid_ref, gate_ref, rank_ref, cnt_ref, run_sc, *, tm, ts):
    i = pl.program_id(0)
    subs = range(tm // ts)
    rows = lambda j: slice(j * ts, (j + 1) * ts)

    @pl.when(i == 0)
    def _():
        run_sc[...] = jnp.zeros_like(run_sc)

    sig = lambda g: 0.5 * jnp.tanh(0.5 * g) + 0.5
    pa = {j: jnp.dot(ya_ref[rows(j), :], wa_ref[...], preferred_element_type=F32) for j in subs}
    pm = {j: jnp.dot(ym_ref[rows(j), :], wm_ref[...], preferred_element_type=F32) for j in subs}
    h0 = {j: _ln(x_ref[rows(j), :], eg_ref[...], eb_ref[...]) for j in subs}
    merged = {j: sig(ga_ref[rows(j), :]) * pa[j].astype(BF16) + sig(gmm_ref[rows(j), :]) * pm[j].astype(BF16)
              for j in subs}
    mix = {j: jnp.dot(merged[j], wo_ref[...], preferred_element_type=F32) for j in subs}
    h1 = {j: _ln(DN_ALPHA * h0[j] + mix[j], g1_ref[...], b1_ref[...]) for j in subs}
    for j in subs:
        _store_row_tiles(h1_ref.at[pl.ds(j * ts * ROW_TILE, ts * ROW_TILE)], h1[j])

    logits = {}
    for j in subs:
        h_hi = h1[j].astype(BF16)
        h_mid = (h1[j] - h_hi.astype(F32)).astype(BF16)
        hh = jnp.dot(h_hi, wr_ref[...], preferred_element_type=F32)
        logits[j] = (hh[:, :LANES] + hh[:, LANES:]
                     + jnp.dot(h_mid, wr_ref[:, :LANES], preferred_element_type=F32) + br_ref[...])

    lane = lax.broadcasted_iota(jnp.int32, (ts, LANES), 1)
    r_i = lax.broadcasted_iota(jnp.int32, (ts, ts), 0)
    c_i = lax.broadcasted_iota(jnp.int32, (ts, ts), 1)
    strict = (c_i < r_i).astype(BF16)
    sel_e, ex, den, onehots, oh = {}, {}, {}, {}, {}
    for j in subs:
        work = logits[j]
        es, vs = [], []
        for _ in range(TOP_K):
            mv = jnp.max(work, axis=1, keepdims=True)
            e = jnp.min(jnp.where(work == mv, lane, LANES), axis=1, keepdims=True)
            es.append(e)
            vs.append(mv)
            work = jnp.where(lane == e, -jnp.inf, work)
        sel_e[j] = es
        ex[j] = [jnp.exp(v - vs[0]) for v in vs]
        den[j] = ex[j][0] + ex[j][1] + ex[j][2] + ex[j][3]
        onehots[j] = [lane == e for e in es]
        acc = jnp.zeros((ts, LANES), F32)
        for o in onehots[j]:
            acc = acc + o.astype(F32)
        oh[j] = acc

    run = run_sc[...]
    for j in subs:
        before = jnp.dot(strict, oh[j].astype(BF16), preferred_element_type=F32) + run
        run = run + jnp.sum(oh[j], axis=0, keepdims=True)
        eid = jnp.zeros((ts, LANES), F32)
        gate = jnp.zeros((ts, LANES), F32)
        rank = jnp.zeros((ts, LANES), F32)
        for kk in range(TOP_K):
            here = lane == kk
            eid = jnp.where(here, sel_e[j][kk].astype(F32), eid)
            gate = jnp.where(here, ex[j][kk] / den[j], gate)
            rk = jnp.sum(jnp.where(onehots[j][kk], before, 0.0), axis=1, keepdims=True)
            rank = jnp.where(here, rk, rank)
        gate_ref[rows(j), :] = gate
        eid_ref[:, rows(j)] = eid.T[0:8, :]
        rank_ref[:, rows(j)] = rank.T[0:8, :]
    run_sc[...] = run
    cnt_ref[...] = run


def _outproj(x2, yatt, yml, zb2, eg, eb, wa, wm, wo, g1, b1, wr, br, *, tm, ts):
    N, D = x2.shape
    vec = lambda w: pl.BlockSpec((1, w), lambda i: (0, 0))
    full = lambda a, b: pl.BlockSpec((a, b), lambda i: (0, 0))
    tile = lambda w: pl.BlockSpec((tm, w), lambda i: (i, 0))
    return pl.pallas_call(
        functools.partial(_outproj_kernel, tm=tm, ts=ts),
        grid=(N // tm,),
        in_specs=[
            tile(D), tile(ATT_HEADS * ATT_V_DIM), tile(ML_WIDTH),
            pl.BlockSpec((tm, D), lambda i: (i, C_GA // D)),
            pl.BlockSpec((tm, D), lambda i: (i, C_GM // D)),
            vec(D), vec(D),
            full(ATT_HEADS * ATT_V_DIM, D), full(ML_WIDTH, D), full(D, D),
            vec(D), vec(D), full(D, 2 * LANES), vec(LANES),
        ],
        out_specs=[pl.BlockSpec((tm * ROW_TILE, LANES), lambda i: (i, 0)),
                   pl.BlockSpec((8, tm), lambda i: (0, i)), tile(LANES),
                   pl.BlockSpec((8, tm), lambda i: (0, i)), vec(LANES)],
        out_shape=[
            jax.ShapeDtypeStruct((N * ROW_TILE, LANES), F32),
            jax.ShapeDtypeStruct((8, N), F32),
            jax.ShapeDtypeStruct((N, LANES), F32),
            jax.ShapeDtypeStruct((8, N), F32),
            jax.ShapeDtypeStruct((1, LANES), F32),
        ],
        scratch_shapes=[pltpu.VMEM((1, LANES), F32)],
        compiler_params=_cparams(1),
        name="outproj_router",
    )(x2, yatt, yml, zb2, zb2, eg, eb, wa, wm, wo, g1, b1, wr, br)


def _row_copies(src_row, dst_row, sem, n, issue):
    def body(r, carry):
        for kk in range(TOP_K):
            cp = pltpu.make_async_copy(src_row(r, kk), dst_row(r, kk), sem)
            if issue:
                cp.start(priority=kk % 2)
            else:
                cp.wait()
        return carry
    lax.fori_loop(0, n, body, 0, unroll=8)


def _dispatch_kernel(pad_pos_ref, pad_len_ref, d_ref, h_ref, xs_ref, zero_sc, sem, *, tm, max_pad):
    @pl.when(pl.program_id(0) == 0)
    def _():
        zero_sc[...] = jnp.zeros_like(zero_sc)
        bits = [1 << b for b in reversed(range(max_pad.bit_length()))]

        def pad_copies(e, issue):
            pos = pad_pos_ref[e]
            length = pad_len_ref[e]
            for bit in bits:
                take = (length & bit) != 0

                @pl.when(take)
                def _(pos=pos, bit=bit):
                    dst = xs_ref.at[pl.ds(pl.multiple_of(pos * ROW_TILE, ROW_TILE), bit * ROW_TILE)]
                    cp = pltpu.make_async_copy(zero_sc.at[pl.ds(0, bit * ROW_TILE)], dst, sem.at[1])
                    if issue:
                        cp.start()
                    else:
                        cp.wait()
                pos = pos + jnp.where(take, bit, 0)

        def per_expert(e, carry):
            pad_copies(e, True)
            pad_copies(e, False)
            return carry

        lax.fori_loop(0, N_EXPERTS, per_expert, 0)

    src = lambda r, kk: _row_tile(h_ref, r)
    dst = lambda r, kk: _row_tile(xs_ref, d_ref[0, 0, kk * tm + r])
    _row_copies(src, dst, sem.at[0], tm, True)
    _row_copies(src, dst, sem.at[0], tm, False)


def _dispatch(pad_pos, pad_len, dest3, h1t, rows_out, *, tm, max_pad):
    N = h1t.shape[0] // ROW_TILE
    zero_rows = 1 << (max_pad.bit_length() - 1)
    return pl.pallas_call(
        functools.partial(_dispatch_kernel, tm=tm, max_pad=max_pad),
        grid_spec=pltpu.PrefetchScalarGridSpec(
            num_scalar_prefetch=2,
            grid=(N // tm,),
            in_specs=[
                pl.BlockSpec((1, 1, tm * TOP_K), lambda i, pp, pn: (i, 0, 0), memory_space=pltpu.SMEM),
                pl.BlockSpec((tm * ROW_TILE, LANES), lambda i, pp, pn: (i, 0)),
            ],
            out_specs=pl.BlockSpec(memory_space=pl.ANY),
            scratch_shapes=[pltpu.VMEM((zero_rows * ROW_TILE, LANES), F32), pltpu.SemaphoreType.DMA((2,))],
        ),
        out_shape=jax.ShapeDtypeStruct((rows_out * ROW_TILE, LANES), F32),
        compiler_params=_cparams(1),
        name="moe_dispatch",
    )(pad_pos, pad_len, dest3, h1t)


def _select_matrix():
    j = np.arange(2 * LANES)
    sel = np.zeros((2 * LANES, 2 * LANES), np.float32)
    sel[j, (j % 2) * LANES + j // 2] = 1.0
    return jnp.asarray(sel, BF16)


def _expert_kernel(be_ref, nu_ref, x_ref, wgu_ref, bg_ref, bl_ref, wd_ref, bd_ref, sel_ref, y_ref,
                   wg_sc, wl_sc, wd_sc, *, tm):
    i = pl.program_id(0)
    active = i < nu_ref[0]
    new_expert = jnp.logical_or(i == 0, be_ref[i] != be_ref[jnp.maximum(i - 1, 0)])

    @pl.when(jnp.logical_and(active, new_expert))
    def _():
        for c in range(wgu_ref.shape[2] // (2 * LANES)):
            blk = wgu_ref[0, :, c * 2 * LANES:(c + 1) * 2 * LANES].astype(BF16)
            r = jnp.dot(blk, sel_ref[...], preferred_element_type=F32).astype(BF16)
            wg_sc[:, c * LANES:(c + 1) * LANES] = r[:, :LANES]
            wl_sc[:, c * LANES:(c + 1) * LANES] = r[:, LANES:]
        wd_sc[...] = wd_ref[0].astype(BF16)

    @pl.when(active)
    def _():
        xb = _load_row_tiles(x_ref, tm).astype(BF16)
        hg = jnp.dot(xb, wg_sc[...], preferred_element_type=F32) + bg_ref[0]
        hl = jnp.dot(xb, wl_sc[...], preferred_element_type=F32) + bl_ref[0]
        glu = jnp.minimum(hg, SWIGLU_LIMIT)
        lin = jnp.clip(hl, -SWIGLU_LIMIT, SWIGLU_LIMIT)
        act = glu * _sigmoid(SWIGLU_ALPHA * glu) * (lin + 1.0)
        _store_row_tiles(y_ref, jnp.dot(act.astype(BF16), wd_sc[...], preferred_element_type=F32) + bd_ref[0])


def _experts(blk_exp, n_used, xs, wgu, bg, bl, wd, bd, *, tm):
    MP = xs.shape[0] // ROW_TILE
    _, D, F2 = wgu.shape
    F = F2 // 2
    rows = pl.BlockSpec((tm * ROW_TILE, LANES), lambda i, be, nu: (jnp.minimum(i, nu[0] - 1), 0))
    wsp = lambda a, b: pl.BlockSpec((1, a, b), lambda i, be, nu: (be[i], 0, 0))
    return pl.pallas_call(
        functools.partial(_expert_kernel, tm=tm),
        grid_spec=pltpu.PrefetchScalarGridSpec(
            num_scalar_prefetch=2,
            grid=(MP // tm,),
            in_specs=[rows, wsp(D, F2), wsp(1, F), wsp(1, F), wsp(F, D), wsp(1, D),
                      pl.BlockSpec((2 * LANES, 2 * LANES), lambda i, be, nu: (0, 0))],
            out_specs=rows,
            scratch_shapes=[pltpu.VMEM((D, F), BF16), pltpu.VMEM((D, F), BF16), pltpu.VMEM((F, D), BF16)],
        ),
        out_shape=jax.ShapeDtypeStruct(xs.shape, F32),
        compiler_params=_cparams(1),
        name="moe_experts",
    )(blk_exp, n_used, xs, wgu, bg, bl, wd, bd, _select_matrix())


def _combine_kernel(dcur_ref, dnxt_ref, h1_ref, gate_ref, g2_ref, b2_ref, ys_ref, o_ref, ybuf, sem, *, tm, nsteps):
    i = pl.program_id(0)
    slot = i % 2

    def copies(d_ref, s, issue):
        src = lambda r, kk: _row_tile(ys_ref, d_ref[0, 0, kk * tm + r])
        dst = lambda r, kk: _row_tile(ybuf.at[s, kk], r)
        _row_copies(src, dst, sem.at[s], tm, issue)

    @pl.when(i == 0)
    def _():
        copies(dcur_ref, 0, True)

    @pl.when(i + 1 < nsteps)
    def _():
        copies(dnxt_ref, 1 - slot, True)

    copies(dcur_ref, slot, False)
    acc = DN_ALPHA * _load_row_tiles(h1_ref, tm)
    for kk in range(TOP_K):
        acc = acc + gate_ref[:, kk:kk + 1] * _load_row_tiles(ybuf.at[slot, kk], tm)
    o_ref[...] = _ln(acc, g2_ref[...], b2_ref[...])


def _combine(dest3, h1t, gate, g2, b2, ys, *, tm):
    N, D = h1t.shape[0] // ROW_TILE, D_MODEL
    nsteps = N // tm
    dspec = lambda f: pl.BlockSpec((1, 1, tm * TOP_K), f, memory_space=pltpu.SMEM)
    return pl.pallas_call(
        functools.partial(_combine_kernel, tm=tm, nsteps=nsteps),
        grid=(nsteps,),
        in_specs=[
            dspec(lambda i: (i, 0, 0)),
            dspec(lambda i: (jnp.minimum(i + 1, nsteps - 1), 0, 0)),
            pl.BlockSpec((tm * ROW_TILE, LANES), lambda i: (i, 0)),
            pl.BlockSpec((tm, LANES), lambda i: (i, 0)),
            pl.BlockSpec((1, D), lambda i: (0, 0)),
            pl.BlockSpec((1, D), lambda i: (0, 0)),
            pl.BlockSpec(memory_space=pl.ANY),
        ],
        out_specs=pl.BlockSpec((tm, D), lambda i: (i, 0)),
        out_shape=jax.ShapeDtypeStruct((N, D), F32),
        scratch_shapes=[pltpu.VMEM((2, TOP_K, tm * ROW_TILE, LANES), F32), pltpu.SemaphoreType.DMA((2,))],
        compiler_params=_cparams(1),
        name="moe_combine",
    )(dest3, dest3, h1t, gate, g2, b2, ys)


def _rotary_column_order():
    idx = np.empty((ATT_HEADS * LANES,), np.int32)
    for h in range(ATT_HEADS):
        for half in range(2):
            for sub in range(2):
                for dd in range(32):
                    idx[h * LANES + half * 64 + sub * 32 + dd] = (2 * h + sub) * ATT_QK_DIM + half * 32 + dd
    return idx


def _rotary_tables(pos):
    half = ATT_QK_DIM // 2
    inv_freq = ROPE_THETA ** (-jnp.arange(half, dtype=F32) / half)
    ang = pos.astype(F32)[:, None] * inv_freq[None, :]
    cos = jnp.tile(jnp.cos(ang), (1, 4))
    sin = jnp.tile(jnp.sin(ang), (1, 4))
    sign = jnp.where(jnp.arange(LANES) < 64, -1.0, 1.0).astype(F32)
    return cos, sin * sign[None, :]


def _pick(n, prefs):
    for t in prefs:
        if n % t == 0:
            return t
    raise ValueError(f"unsupported size {n}")


def kernel(x, meta, emb_ln_g, emb_ln_b, w_in, conv_w, conv_b, gate_bias, lam_q1, lam_k1, lam_q2, lam_k2,
           att_norm_g, ml_norm_g, w_att_out, w_ml_out, w_o, ln1_g, ln1_b, w_router, b_router,
           w_gu, b_gu, w_down, b_down, ln2_g, ln2_b):
    B, S, D = x.shape
    assert D == D_MODEL and S % 512 == 0 and w_in.shape[0] == DEPTH
    N = B * S
    row2 = lambda a: a.reshape(1, -1)

    w = w_in[0]
    o_aq, o_ak, o_av, o_mqk, o_mv, o_mo, o_gt, o_ga, o_gm = 0, 512, 1024, 1536, 2560, 3072, 3584, 3592, 4616
    perm = _rotary_column_order()
    gpad = jnp.zeros((D, LANES - ML_HEADS), F32)
    w_all = jnp.concatenate([
        w[:, o_ga:o_ga + D], w[:, o_gm:o_gm + D], w[:, o_mqk:o_mqk + 2 * ML_WIDTH],
        w[:, o_aq:o_aq + 512][:, perm], w[:, o_ak:o_ak + 512][:, perm], w[:, o_av:o_av + 512],
        w[:, o_mv:o_mv + ML_WIDTH], w[:, o_mo:o_mo + ML_WIDTH],
        w[:, o_gt:o_gt + ML_HEADS], gpad, w[:, o_gt + ML_HEADS:o_gt + 2 * ML_HEADS], gpad,
    ], axis=1).astype(BF16)
    gb = gate_bias[0]
    zpad = jnp.zeros((LANES - ML_HEADS,), F32)
    gate_b = jnp.concatenate([gb[:ML_HEADS], zpad, gb[ML_HEADS:], zpad]).reshape(1, GATE_W)

    cos_x, sin_x = _rotary_tables(N_META + jnp.arange(S))
    cos_m, sin_m = _rotary_tables(jnp.maximum(jnp.arange(BLOCK) - PAD, 0))

    eg, eb = row2(emb_ln_g), row2(emb_ln_b)

    tm_in = 512
    zb, gates = _inproj(x, eg, eb, w_all, cos_x, sin_x, tm=tm_in)
    xm = jnp.concatenate([jnp.zeros((PAD, D), x.dtype), meta.astype(x.dtype)], axis=0)[None]
    zbm, gatesm = _inproj(xm, eg, eb, w_all, cos_m, sin_m, tm=BLOCK, first_valid_row=PAD)

    tq = 512
    yatt = _attention(zb, zbm, row2(lam_q1[0]), row2(lam_k1[0]), row2(lam_q2[0]), row2(lam_k2[0]),
                      row2(att_norm_g[0]), tq=tq, nh=1)
    yml = _mlstm(zb, gates, zbm, gatesm, conv_w[0], row2(conv_b[0]), gate_b, row2(ml_norm_g[0]),
                 nbps=4 if B % 4 == 0 else (2 if B % 2 == 0 else 1))

    wr32 = jnp.concatenate([w_router[0], jnp.zeros((D, LANES - N_EXPERTS), F32)], axis=1)
    wr_hi = wr32.astype(BF16)
    wr = jnp.concatenate([wr_hi, (wr32 - wr_hi.astype(F32)).astype(BF16)], axis=1)
    br = jnp.concatenate([b_router[0], jnp.full((LANES - N_EXPERTS,), NEG, F32)]).reshape(1, LANES)
    tm_out = 1024
    h1, eid, gate, rank, cnt = _outproj(
        x.reshape(N, D), yatt.reshape(N, -1), yml.reshape(N, -1), zb.reshape(N, ZB_W), eg, eb,
        w_att_out[0].astype(BF16), w_ml_out[0].astype(BF16), w_o[0].astype(BF16),
        row2(ln1_g[0]), row2(ln1_b[0]), wr, br, tm=tm_out, ts=512)

    tm_e = 512
    M = N * TOP_K
    nb = (M + N_EXPERTS * (tm_e - 1) + tm_e - 1) // tm_e
    counts = cnt[0, :N_EXPERTS].astype(jnp.int32)
    nblk = (counts + tm_e - 1) // tm_e
    cum = jnp.cumsum(nblk)
    pstart = (cum - nblk) * tm_e
    eid_t = eid[:TOP_K].astype(jnp.int32)
    base = jnp.sum(jnp.where(eid_t[None] == jnp.arange(N_EXPERTS)[:, None, None], pstart[:, None, None], 0), axis=0)
    dest = base + rank[:TOP_K].astype(jnp.int32)
    n_used = cum[-1:].astype(jnp.int32)
    blk = jnp.minimum(jnp.arange(nb, dtype=jnp.int32), n_used[0] - 1)
    blk_exp = jnp.sum(blk[:, None] >= cum[None, :], axis=1).astype(jnp.int32)

    def dest_tiles(tm):
        return dest.reshape(TOP_K, N // tm, tm).transpose(1, 0, 2).reshape(N // tm, 1, TOP_K * tm)

    tm_d, tm_c = _pick(N, (1024, 512, 256)), 256
    xs = _dispatch(pstart + counts, nblk * tm_e - counts, dest_tiles(tm_d), h1, nb * tm_e, tm=tm_d,
                   max_pad=tm_e - 1)

    bgu = b_gu[0]
    ys = _experts(blk_exp, n_used, xs, w_gu[0], bgu[:, None, 0::2], bgu[:, None, 1::2],
                  w_down[0], b_down[0][:, None, :], tm=tm_e)

    out = _combine(dest_tiles(tm_c), h1, gate, row2(ln2_g[0]), row2(ln2_b[0]), ys, tm=tm_c)
    return out.reshape(B, S, D)
```

```python
import functools
import math

import jax
import jax.numpy as jnp
import numpy as np
from jax import lax
from jax.experimental import pallas as pl
from jax.experimental.pallas import tpu as pltpu

D_MODEL = 1024
N_META = 16
BLOCK = 128
PAD = BLOCK - N_META
NEG = -1e30
LN_EPS = 1e-5
ATT_HEADS = 4
ATT_QK_DIM = 64
ATT_V_DIM = 128
ROPE_THETA = 10000.0
ML_HEADS = 4
ML_DH = 128
ML_WIDTH = ML_HEADS * ML_DH
CONV_K = 4
N_EXPERTS = 32
TOP_K = 4
SWIGLU_LIMIT = 7.0
SWIGLU_ALPHA = 1.702
DEPTH = 1
DN_ALPHA = (2 * DEPTH) ** 0.25
LAMBDA_INIT = 0.8 - 0.6 * math.exp(-0.3 * 0)

LANES = 128
C_GA, C_GM, C_MQK, C_AQ, C_AK, C_AV, C_MV, C_MO = 0, 1024, 2048, 3072, 3584, 4096, 4608, 5120
ZB_W = 5632
GATE_W = 2 * LANES
W_ALL = ZB_W + GATE_W
CHUNK = 512

VMEM_LIMIT = 56 * 1024 * 1024

BF16 = jnp.bfloat16
F32 = jnp.float32


def _cparams(n_axes):
    return pltpu.CompilerParams(dimension_semantics=("arbitrary",) * n_axes, vmem_limit_bytes=VMEM_LIMIT)


def _ln(x, g, b):
    mu = jnp.mean(x, axis=-1, keepdims=True)
    xc = x - mu
    var = jnp.mean(xc * xc, axis=-1, keepdims=True)
    return xc * lax.rsqrt(var + LN_EPS) * g + b


def _sigmoid(x):
    return 1.0 / (1.0 + jnp.exp(-x))


ROW_TILE = D_MODEL // LANES


def _store_row_tiles(ref, val):
    tm = val.shape[0]
    for g in range(ROW_TILE):
        ref[pl.ds(g, tm, stride=ROW_TILE), :] = val[:, g * LANES:(g + 1) * LANES]


def _load_row_tiles(ref, tm):
    return jnp.concatenate([ref[pl.ds(g, tm, stride=ROW_TILE), :] for g in range(ROW_TILE)], axis=1)


def _row_tile(ref, r):
    return ref.at[pl.ds(pl.multiple_of(r * ROW_TILE, ROW_TILE), ROW_TILE)]


def _inproj_kernel(x_ref, g_ref, b_ref, w_ref, cos_ref, sin_ref, zb_ref, gt_ref, *, first_valid_row):
    x = x_ref[0]
    tm = x.shape[0]
    hb = _ln(x, g_ref[...], b_ref[...]).astype(BF16)
    cos = cos_ref[...]
    sin = sin_ref[...]
    if first_valid_row:
        rowmask = lax.broadcasted_iota(jnp.int32, (tm, 1), 0) >= first_valid_row
    for c in range(ZB_W // CHUNK):
        z = jnp.dot(hb, w_ref[:, c * CHUNK:(c + 1) * CHUNK], preferred_element_type=F32)
        if c * CHUNK in (C_AQ, C_AK):
            parts = []
            for h in range(ATT_HEADS):
                zh = z[:, h * LANES:(h + 1) * LANES]
                parts.append(zh * cos + pltpu.roll(zh, 64, 1) * sin)
            z = jnp.concatenate(parts, axis=1)
            if c * CHUNK == C_AQ:
                z = z * (ATT_QK_DIM ** -0.5 * math.log2(math.e))
        if first_valid_row:
            z = jnp.where(rowmask, z, 0.0)
        zb_ref[0, :, c * CHUNK:(c + 1) * CHUNK] = z.astype(BF16)
    zg = jnp.dot(hb, w_ref[:, ZB_W:W_ALL], preferred_element_type=F32)
    if first_valid_row:
        zg = jnp.where(rowmask, zg, 0.0)
    gt_ref[0] = zg


def _inproj(x3, g, b, w, cos, sin, *, tm, first_valid_row=0):
    B, S, D = x3.shape
    nt = S // tm
    return pl.pallas_call(
        functools.partial(_inproj_kernel, first_valid_row=first_valid_row),
        grid=(B, nt),
        in_specs=[
            pl.BlockSpec((1, tm, D), lambda bi, i: (bi, i, 0)),
            pl.BlockSpec((1, D), lambda bi, i: (0, 0)),
            pl.BlockSpec((1, D), lambda bi, i: (0, 0)),
            pl.BlockSpec((D, W_ALL), lambda bi, i: (0, 0)),
            pl.BlockSpec((tm, LANES), lambda bi, i: (i, 0)),
            pl.BlockSpec((tm, LANES), lambda bi, i: (i, 0)),
        ],
        out_specs=[
            pl.BlockSpec((1, tm, ZB_W), lambda bi, i: (bi, i, 0)),
            pl.BlockSpec((1, tm, GATE_W), lambda bi, i: (bi, i, 0)),
        ],
        out_shape=[
            jax.ShapeDtypeStruct((B, S, ZB_W), BF16),
            jax.ShapeDtypeStruct((B, S, GATE_W), F32),
        ],
        compiler_params=_cparams(2),
        name="inproj",
    )(x3, g, b, w, cos, sin)


def _attn_kernel(q_ref, k_ref, v_ref, km_ref, vm_ref, lq1_ref, lk1_ref, lq2_ref, lk2_ref, ng_ref, o_ref,
                 sa_sc, sb_sc, pa_sc, pb_sc, aa_sc, ab_sc, m_sc, acc_sc, *, tq, tk, rc, nh):
    assert tq == 2 * tk
    qi = pl.program_id(2)
    rows = 2 * tq
    heads = range(nh)
    cols = lambda hh: slice(hh * LANES, (hh + 1) * LANES)
    lane = lax.broadcasted_iota(jnp.int32, (tq, LANES), 1)
    is_map0 = (lane & 63) < 32
    qq = {}
    for hh in heads:
        q = q_ref[0, :, cols(hh)]
        zero = jnp.zeros_like(q)
        qq[hh] = jnp.concatenate([jnp.where(is_map0, q, zero), jnp.where(is_map0, zero, q)], axis=0)

    def qk_stage(hh, kblk, s_ref, col0=0):
        s_ref[hh, :, col0:col0 + kblk.shape[0]] = lax.dot_general(qq[hh], kblk, (((1,), (1,)), ((), ())),
                                                                  preferred_element_type=F32)

    def pv_stage(hh, p_ref, a_ref, vblk, first=False):
        width = vblk.shape[0]
        v1 = jnp.concatenate([vblk, jnp.ones_like(vblk)], axis=1)
        pv = jnp.dot(p_ref[hh, :, 0:width], v1, preferred_element_type=F32)
        if first:
            acc_sc[hh] = pv
        else:
            a = a_ref[hh]
            acc_sc[hh] = jnp.concatenate([a, a], axis=1) * acc_sc[hh] + pv

    def sm_stage(hh, s_ref, p_ref, a_ref, width, vis, first=False):
        nrep = width // LANES
        for r0 in range(0, rows, rc):
            kind = vis(r0)
            if kind == "none":
                continue
            s = s_ref[hh, r0:r0 + rc, 0:width]
            if kind != "all":
                s = kind(s)
            m_cur = jnp.max(s, axis=1, keepdims=True)
            if first:
                m_new = jnp.broadcast_to(m_cur, (rc, LANES))
            else:
                m_prev = m_sc[hh, r0:r0 + rc, :]
                m_new = jnp.maximum(m_prev, m_cur)
                a_ref[hh, r0:r0 + rc, :] = jnp.exp2(m_prev - m_new)
            m_sc[hh, r0:r0 + rc, :] = m_new
        for r0 in range(0, rows, rc):
            kind = vis(r0)
            if kind == "none":
                p_ref[hh, r0:r0 + rc, 0:width] = jnp.zeros((rc, width), BF16)
                a_ref[hh, r0:r0 + rc, :] = jnp.ones((rc, LANES), F32)
                continue
            s = s_ref[hh, r0:r0 + rc, 0:width]
            if kind != "all":
                s = kind(s)
            m_new = m_sc[hh, r0:r0 + rc, :]
            m_rep = m_new if nrep == 1 else jnp.concatenate([m_new] * nrep, axis=1)
            p_ref[hh, r0:r0 + rc, 0:width] = jnp.exp2(s - m_rep).astype(BF16)

    def first_vis(r0):
        q0 = r0 % tq
        def mask(s):
            rowq = q0 + lax.broadcasted_iota(jnp.int32, s.shape, 0)
            col = lax.broadcasted_iota(jnp.int32, s.shape, 1)
            keep = col >= tk + PAD
            if tk - 1 <= q0:
                keep = jnp.logical_or(keep, col < tk)
            else:
                keep = jnp.logical_or(keep, col <= rowq)
            return jnp.where(keep, s, NEG)
        return mask

    def diag_vis(d):
        def vis(r0):
            q0 = r0 % tq
            if d * tk + tk - 1 <= q0:
                return "all"
            if d * tk > q0 + rc - 1:
                return "none"
            def mask(s):
                rowq = q0 + lax.broadcasted_iota(jnp.int32, s.shape, 0)
                col = d * tk + lax.broadcasted_iota(jnp.int32, s.shape, 1)
                return jnp.where(col <= rowq, s, NEG)
            return mask
        return vis

    all_vis = lambda r0: "all"
    kblk = lambda hh, start: k_ref[0, pl.ds(start, tk), cols(hh)]
    vblk = lambda hh, start: v_ref[0, pl.ds(start, tk), cols(hh)]

    d0 = pl.multiple_of(qi * tq, tk)
    d1 = pl.multiple_of(qi * tq + tk, tk)
    for hh in heads:
        qk_stage(hh, kblk(hh, d0), sa_sc)
    for hh in heads:
        qk_stage(hh, km_ref[0, :, cols(hh)], sa_sc, col0=tk)
    for hh in heads:
        qk_stage(hh, kblk(hh, d1), sb_sc)
    for hh in heads:
        sm_stage(hh, sa_sc, pa_sc, None, tk + BLOCK, first_vis, first=True)
    for hh in heads:
        pv_stage(hh, pa_sc, None, jnp.concatenate([vblk(hh, d0), vm_ref[0, :, cols(hh)]], axis=0), first=True)
    for hh in heads:
        sm_stage(hh, sb_sc, pb_sc, ab_sc, tk, diag_vis(1))
    for hh in heads:
        qk_stage(hh, kblk(hh, 0), sa_sc)

    def pair(i, carry):
        a_start = pl.multiple_of(i * tq, tk)
        b_start = pl.multiple_of(i * tq + tk, tk)
        prev_b = pl.multiple_of(jnp.where(i == 0, d1, a_start - tk), tk)
        next_a = pl.multiple_of(jnp.minimum(a_start + tq, (qi - 1) * tq), tk)
        for hh in heads:
            qk_stage(hh, kblk(hh, b_start), sb_sc)
        for hh in heads:
            pv_stage(hh, pb_sc, ab_sc, vblk(hh, prev_b))
        for hh in heads:
            sm_stage(hh, sa_sc, pa_sc, aa_sc, tk, all_vis)
        for hh in heads:
            qk_stage(hh, kblk(hh, next_a), sa_sc)
        for hh in heads:
            pv_stage(hh, pa_sc, aa_sc, vblk(hh, a_start))
        for hh in heads:
            sm_stage(hh, sb_sc, pb_sc, ab_sc, tk, all_vis)
        return carry

    lax.fori_loop(0, qi, pair, 0)
    last_b = pl.multiple_of(jnp.where(qi == 0, d1, qi * tq - tk), tk)
    for hh in heads:
        pv_stage(hh, pb_sc, ab_sc, vblk(hh, last_b))

    s1 = jnp.sum(lq1_ref[...] * lk1_ref[...], axis=1, keepdims=True)
    s2 = jnp.sum(lq2_ref[...] * lk2_ref[...], axis=1, keepdims=True)
    lam = jnp.exp(s1) - jnp.exp(s2) + LAMBDA_INIT
    for hh in heads:
        o = acc_sc[hh, :, 0:ATT_V_DIM] / acc_sc[hh, :, ATT_V_DIM:]
        a = o[:tq] - lam * o[tq:]
        y = a * lax.rsqrt(jnp.mean(a * a, axis=1, keepdims=True) + LN_EPS) * ng_ref[...]
        o_ref[0, :, cols(hh)] = (y * (1.0 - LAMBDA_INIT)).astype(BF16)


def _attention(zb, zbm, lq1, lk1, lq2, lk2, ng, *, tq, nh):
    B, S, _ = zb.shape
    nq = S // tq
    tk = tq // 2
    w = nh * LANES
    qb, kb, vb = C_AQ // w, C_AK // w, C_AV // w
    lam_spec = pl.BlockSpec((1, ATT_QK_DIM), lambda b, h, i: (0, 0))
    return pl.pallas_call(
        functools.partial(_attn_kernel, tq=tq, tk=tk, rc=32, nh=nh),
        grid=(B, ATT_HEADS // nh, nq),
        in_specs=[
            pl.BlockSpec((1, tq, w), lambda b, h, i: (b, i, qb + h)),
            pl.BlockSpec((1, S, w), lambda b, h, i: (b, 0, kb + h)),
            pl.BlockSpec((1, S, w), lambda b, h, i: (b, 0, vb + h)),
            pl.BlockSpec((1, BLOCK, w), lambda b, h, i: (0, 0, kb + h)),
            pl.BlockSpec((1, BLOCK, w), lambda b, h, i: (0, 0, vb + h)),
            lam_spec, lam_spec, lam_spec, lam_spec,
            pl.BlockSpec((1, ATT_V_DIM), lambda b, h, i: (0, 0)),
        ],
        out_specs=pl.BlockSpec((1, tq, w), lambda b, h, i: (b, i, h)),
        out_shape=jax.ShapeDtypeStruct((B, S, ATT_HEADS * ATT_V_DIM), BF16),
        scratch_shapes=[
            pltpu.VMEM((nh, 2 * tq, tk + BLOCK), F32), pltpu.VMEM((nh, 2 * tq, tk), F32),
            pltpu.VMEM((nh, 2 * tq, tk + BLOCK), BF16), pltpu.VMEM((nh, 2 * tq, tk), BF16),
            pltpu.VMEM((nh, 2 * tq, LANES), F32), pltpu.VMEM((nh, 2 * tq, LANES), F32),
            pltpu.VMEM((nh, 2 * tq, LANES), F32),
            pltpu.VMEM((nh, 2 * tq, 2 * ATT_V_DIM), F32),
        ],
        compiler_params=_cparams(3),
        name="diff_attention",
    )(zb, zb, zb, zbm, zbm, lq1, lk1, lq2, lk2, ng)


def _mlstm_kernel(xqk_ref, xv_ref, xo_ref, xg_ref, zm_ref, gm_ref, cw_ref, cb_ref, gb_ref, ng_ref, sh_ref, o_ref,
                  ct_sc, n_sc, m_sc, prev_sc, *, nbps):
    c = pl.program_id(1)
    L = BLOCK

    @pl.when(c == 0)
    def _():
        ct_sc[...] = jnp.zeros_like(ct_sc)
        n_sc[...] = jnp.zeros_like(n_sc)
        m_sc[...] = jnp.zeros_like(m_sc)
        prev_sc[...] = jnp.zeros_like(prev_sc)

    is_meta = c == 0
    row = lax.broadcasted_iota(jnp.int32, (L, 1), 0)
    valid = jnp.logical_or(c > 0, row >= PAD)
    sidx = lax.broadcasted_iota(jnp.int32, (L, L), 0)
    ridx = lax.broadcasted_iota(jnp.int32, (L, L), 1)
    causal = ridx <= sidx
    tri = causal.astype(BF16)
    elems = range(nbps)
    pairs = [(bb, h) for bb in elems for h in range(ML_HEADS)]
    sl = lambda h: slice(h * ML_DH, (h + 1) * ML_DH)
    st = lambda bb, h: bb * 8 + h

    q, k, v, ig, bcs, b_t, ig_t = {}, {}, {}, {}, {}, {}, {}
    for bb in elems:
        qk_pre = jnp.where(is_meta, zm_ref[0, :, C_MQK:C_MQK + 2 * ML_WIDTH], xqk_ref[bb])
        v[bb] = jnp.where(is_meta, zm_ref[0, :, C_MV:C_MV + ML_WIDTH], xv_ref[bb])
        gates = jnp.where(is_meta, gm_ref[0], xg_ref[bb])
        prev = prev_sc[bb * L:(bb + 1) * L, :]
        ext = jnp.concatenate([qk_pre, prev], axis=0)
        prev_sc[bb * L:(bb + 1) * L, :] = qk_pre
        shifted = jnp.dot(sh_ref[...], ext, preferred_element_type=F32)
        acc = cb_ref[...] + cw_ref[CONV_K - 1:CONV_K, :] * qk_pre.astype(F32)
        for s in range(1, CONV_K):
            acc = acc + cw_ref[CONV_K - 1 - s:CONV_K - s, :] * shifted[(s - 1) * L:s * L, :]
        qk = acc * _sigmoid(acc)
        qk = jnp.where(valid, qk, 0.0)
        q[bb] = qk[:, :ML_WIDTH]
        k[bb] = qk[:, ML_WIDTH:] * (ML_DH ** -0.5)

        igv = gates[:, :LANES] + gb_ref[:, :LANES]
        fg = gates[:, LANES:] + gb_ref[:, LANES:]
        lf = jnp.minimum(fg, 0.0) - jnp.log1p(jnp.exp(-jnp.abs(fg)))
        igv = jnp.where(valid, igv, NEG)
        lf = jnp.where(valid, lf, 0.0)
        lf_hi = lf.astype(BF16)
        lf_lo = (lf - lf_hi.astype(F32)).astype(BF16)
        bcs[bb] = (jnp.dot(tri, lf_hi, preferred_element_type=F32)
                   + jnp.dot(tri, lf_lo, preferred_element_type=F32))
        ig[bb] = igv
        b_t[bb] = bcs[bb].T
        ig_t[bb] = igv.T

    bc, m_h, dlog, inter, m_s = {}, {}, {}, {}, {}
    for p in pairs:
        bb, h = p
        bc[p] = bcs[bb][:, h:h + 1]
        m_h[p] = m_sc[st(bb, h):st(bb, h) + 1, 0:1]
        dlog[p] = jnp.where(causal, (bc[p] - b_t[bb][h:h + 1, :]) + ig_t[bb][h:h + 1, :], NEG)
        inter[p] = bc[p] + m_h[p]
    for p in pairs:
        m_s[p] = jnp.maximum(inter[p], jnp.max(dlog[p], axis=1, keepdims=True))

    w_intra, w_inter, qb, kb, s, qc = {}, {}, {}, {}, {}, {}
    for p in pairs:
        bb, h = p
        w_intra[p] = jnp.exp(dlog[p] - m_s[p])
        w_inter[p] = jnp.exp(inter[p] - m_s[p])
        qb[p] = q[bb][:, sl(h)].astype(BF16)
        kb[p] = k[bb][:, sl(h)].astype(BF16)
    for p in pairs:
        bb, h = p
        s[p] = lax.dot_general(qb[p], kb[p], (((1,), (1,)), ((), ())), preferred_element_type=F32) * w_intra[p]
        qc[p] = jnp.dot(qb[p], ct_sc[bb * ML_HEADS + h].astype(BF16), preferred_element_type=F32)

    hh = {}
    for p in pairs:
        bb, h = p
        nrow = n_sc[st(bb, h):st(bb, h) + 1, :]
        num = w_inter[p] * qc[p] + jnp.dot(s[p].astype(BF16), v[bb][:, sl(h)], preferred_element_type=F32)
        den = (w_inter[p] * jnp.sum(q[bb][:, sl(h)] * nrow, axis=1, keepdims=True)
               + jnp.sum(s[p], axis=1, keepdims=True))
        hh[p] = num / jnp.maximum(jnp.abs(den), jnp.exp(-m_s[p]))

    for p in pairs:
        bb, h = p
        b_last = bc[p][L - 1:L, :]
        upd = (b_last - bc[p]) + ig[bb][:, h:h + 1]
        m_new = jnp.maximum(b_last + m_h[p], jnp.max(upd, axis=0, keepdims=True))
        w_old = jnp.exp(b_last + m_h[p] - m_new)
        w_r = jnp.exp(upd - m_new)
        vw = (v[bb][:, sl(h)].astype(F32) * w_r).astype(BF16)
        ci = bb * ML_HEADS + h
        ct_sc[ci] = w_old * ct_sc[ci] + lax.dot_general(kb[p], vw, (((0,), (0,)), ((), ())),
                                                         preferred_element_type=F32)
        r = st(bb, h)
        n_sc[r:r + 1, :] = w_old * n_sc[r:r + 1, :] + jnp.sum(k[bb][:, sl(h)] * w_r, axis=0, keepdims=True)
        m_sc[r:r + 1, :] = jnp.broadcast_to(m_new, (1, LANES))

    outs = {}
    for p in pairs:
        bb, h = p
        hn = hh[p] * lax.rsqrt(jnp.mean(hh[p] * hh[p], axis=1, keepdims=True) + LN_EPS) * ng_ref[:, sl(h)]
        outs[p] = _sigmoid(xo_ref[bb, :, sl(h)].astype(F32)) * hn

    @pl.when(c > 0)
    def _():
        for bb in elems:
            o_ref[bb] = jnp.concatenate([outs[(bb, h)] for h in range(ML_HEADS)], axis=1).astype(BF16)


def _mlstm(zb, gates, zbm, gatesm, conv_w, conv_b, gate_b, ng, *, nbps):
    B, S, _ = zb.shape
    nc = S // BLOCK + 1
    t = np.arange(BLOCK)
    shift = np.zeros(((CONV_K - 1) * BLOCK, 2 * BLOCK), np.float32)
    for s in range(1, CONV_K):
        shift[(s - 1) * BLOCK + t, np.where(t >= s, t - s, 2 * BLOCK + t - s)] = 1.0

    def xc(c):
        return jnp.maximum(c - 1, 0)

    return pl.pallas_call(
        functools.partial(_mlstm_kernel, nbps=nbps),
        grid=(B // nbps, nc),
        in_specs=[
            pl.BlockSpec((nbps, BLOCK, 2 * ML_WIDTH), lambda b, c: (b, xc(c), C_MQK // (2 * ML_WIDTH))),
            pl.BlockSpec((nbps, BLOCK, ML_WIDTH), lambda b, c: (b, xc(c), C_MV // ML_WIDTH)),
            pl.BlockSpec((nbps, BLOCK, ML_WIDTH), lambda b, c: (b, xc(c), C_MO // ML_WIDTH)),
            pl.BlockSpec((nbps, BLOCK, GATE_W), lambda b, c: (b, xc(c), 0)),
            pl.BlockSpec((1, BLOCK, ZB_W), lambda b, c: (0, 0, 0)),
            pl.BlockSpec((1, BLOCK, GATE_W), lambda b, c: (0, 0, 0)),
            pl.BlockSpec((CONV_K, 2 * ML_WIDTH), lambda b, c: (0, 0)),
            pl.BlockSpec((1, 2 * ML_WIDTH), lambda b, c: (0, 0)),
            pl.BlockSpec((1, GATE_W), lambda b, c: (0, 0)),
            pl.BlockSpec((1, ML_WIDTH), lambda b, c: (0, 0)),
            pl.BlockSpec(((CONV_K - 1) * BLOCK, 2 * BLOCK), lambda b, c: (0, 0)),
        ],
        out_specs=pl.BlockSpec((nbps, BLOCK, ML_WIDTH), lambda b, c: (b, xc(c), 0)),
        out_shape=jax.ShapeDtypeStruct((B, S, ML_WIDTH), BF16),
        scratch_shapes=[
            pltpu.VMEM((nbps * ML_HEADS, ML_DH, ML_DH), F32),
            pltpu.VMEM((nbps * 8, ML_DH), F32),
            pltpu.VMEM((nbps * 8, LANES), F32),
            pltpu.VMEM((nbps * BLOCK, 2 * ML_WIDTH), BF16),
        ],
        compiler_params=_cparams(2),
        name="mlstm",
    )(zb, zb, zb, gates, zbm, gatesm, conv_w, conv_b, gate_b, ng, jnp.asarray(shift, BF16))


def _outproj_kernel(x_ref, ya_ref, ym_ref, ga_ref, gmm_ref, eg_ref, eb_ref, wa_ref, wm_ref, wo_ref,
                    g1_ref, b1_ref, wr_ref, br_ref,
                    h1_ref, eid_ref, gate_ref, rank_ref, cnt_ref, run_sc, *, tm, ts):
    i = pl.program_id(0)
    subs = range(tm // ts)
    rows = lambda j: slice(j * ts, (j + 1) * ts)

    @pl.when(i == 0)
    def _():
        run_sc[...] = jnp.zeros_like(run_sc)

    sig = lambda g: 0.5 * jnp.tanh(0.5 * g) + 0.5
    pa = {j: jnp.dot(ya_ref[rows(j), :], wa_ref[...], preferred_element_type=F32) for j in subs}
    pm = {j: jnp.dot(ym_ref[rows(j), :], wm_ref[...], preferred_element_type=F32) for j in subs}
    h0 = {j: _ln(x_ref[rows(j), :], eg_ref[...], eb_ref[...]) for j in subs}
    merged = {j: sig(ga_ref[rows(j), :]) * pa[j].astype(BF16) + sig(gmm_ref[rows(j), :]) * pm[j].astype(BF16)
              for j in subs}
    mix = {j: jnp.dot(merged[j], wo_ref[...], preferred_element_type=F32) for j in subs}
    h1 = {j: _ln(DN_ALPHA * h0[j] + mix[j], g1_ref[...], b1_ref[...]) for j in subs}
    for j in subs:
        _store_row_tiles(h1_ref.at[pl.ds(j * ts * ROW_TILE, ts * ROW_TILE)], h1[j])

    logits = {}
    for j in subs:
        h_hi = h1[j].astype(BF16)
        h_mid = (h1[j] - h_hi.astype(F32)).astype(BF16)
        hh = jnp.dot(h_hi, wr_ref[...], preferred_element_type=F32)
        logits[j] = (hh[:, :LANES] + hh[:, LANES:]
                     + jnp.dot(h_mid, wr_ref[:, :LANES], preferred_element_type=F32) + br_ref[...])

    lane = lax.broadcasted_iota(jnp.int32, (ts, LANES), 1)
    r_i = lax.broadcasted_iota(jnp.int32, (ts, ts), 0)
    c_i = lax.broadcasted_iota(jnp.int32, (ts, ts), 1)
    strict = (c_i < r_i).astype(BF16)
    sel_e, ex, den, onehots, oh = {}, {}, {}, {}, {}
    for j in subs:
        work = logits[j]
        es, vs = [], []
        for _ in range(TOP_K):
            mv = jnp.max(work, axis=1, keepdims=True)
            e = jnp.min(jnp.where(work == mv, lane, LANES), axis=1, keepdims=True)
            es.append(e)
            vs.append(mv)
            work = jnp.where(lane == e, -jnp.inf, work)
        sel_e[j] = es
        ex[j] = [jnp.exp(v - vs[0]) for v in vs]
        den[j] = ex[j][0] + ex[j][1] + ex[j][2] + ex[j][3]
        onehots[j] = [lane == e for e in es]
        acc = jnp.zeros((ts, LANES), F32)
        for o in onehots[j]:
            acc = acc + o.astype(F32)
        oh[j] = acc

    run = run_sc[...]
    for j in subs:
        before = jnp.dot(strict, oh[j].astype(BF16), preferred_element_type=F32) + run
        run = run + jnp.sum(oh[j], axis=0, keepdims=True)
        eid = jnp.zeros((ts, LANES), F32)
        gate = jnp.zeros((ts, LANES), F32)
        rank = jnp.zeros((ts, LANES), F32)
        for kk in range(TOP_K):
            here = lane == kk
            eid = jnp.where(here, sel_e[j][kk].astype(F32), eid)
            gate = jnp.where(here, ex[j][kk] / den[j], gate)
            rk = jnp.sum(jnp.where(onehots[j][kk], before, 0.0), axis=1, keepdims=True)
            rank = jnp.where(here, rk, rank)
        gate_ref[rows(j), :] = gate
        eid_ref[:, rows(j)] = eid.T[0:8, :]
        rank_ref[:, rows(j)] = rank.T[0:8, :]
    run_sc[...] = run
    cnt_ref[...] = run


def _outproj(x2, yatt, yml, zb2, eg, eb, wa, wm, wo, g1, b1, wr, br, *, tm, ts):
    N, D = x2.shape
    vec = lambda w: pl.BlockSpec((1, w), lambda i: (0, 0))
    full = lambda a, b: pl.BlockSpec((a, b), lambda i: (0, 0))
    tile = lambda w: pl.BlockSpec((tm, w), lambda i: (i, 0))
    return pl.pallas_call(
        functools.partial(_outproj_kernel, tm=tm, ts=ts),
        grid=(N // tm,),
        in_specs=[
            tile(D), tile(ATT_HEADS * ATT_V_DIM), tile(ML_WIDTH),
            pl.BlockSpec((tm, D), lambda i: (i, C_GA // D)),
            pl.BlockSpec((tm, D), lambda i: (i, C_GM // D)),
            vec(D), vec(D),
            full(ATT_HEADS * ATT_V_DIM, D), full(ML_WIDTH, D), full(D, D),
            vec(D), vec(D), full(D, 2 * LANES), vec(LANES),
        ],
        out_specs=[pl.BlockSpec((tm * ROW_TILE, LANES), lambda i: (i, 0)),
                   pl.BlockSpec((8, tm), lambda i: (0, i)), tile(LANES),
                   pl.BlockSpec((8, tm), lambda i: (0, i)), vec(LANES)],
        out_shape=[
            jax.ShapeDtypeStruct((N * ROW_TILE, LANES), F32),
            jax.ShapeDtypeStruct((8, N), F32),
            jax.ShapeDtypeStruct((N, LANES), F32),
            jax.ShapeDtypeStruct((8, N), F32),
            jax.ShapeDtypeStruct((1, LANES), F32),
        ],
        scratch_shapes=[pltpu.VMEM((1, LANES), F32)],
        compiler_params=_cparams(1),
        name="outproj_router",
    )(x2, yatt, yml, zb2, zb2, eg, eb, wa, wm, wo, g1, b1, wr, br)


def _row_copies(src_row, dst_row, sem, n, issue):
    def body(r, carry):
        for kk in range(TOP_K):
            cp = pltpu.make_async_copy(src_row(r, kk), dst_row(r, kk), sem)
            if issue:
                cp.start(priority=kk % 2)
            else:
                cp.wait()
        return carry
    lax.fori_loop(0, n, body, 0, unroll=8)


def _dispatch_kernel(pad_pos_ref, pad_len_ref, d_ref, h_ref, xs_ref, zero_sc, sem, *, tm, max_pad):
    @pl.when(pl.program_id(0) == 0)
    def _():
        zero_sc[...] = jnp.zeros_like(zero_sc)
        bits = [1 << b for b in reversed(range(max_pad.bit_length()))]

        def pad_copies(e, issue):
            pos = pad_pos_ref[e]
            length = pad_len_ref[e]
            for bit in bits:
                take = (length & bit) != 0

                @pl.when(take)
                def _(pos=pos, bit=bit):
                    dst = xs_ref.at[pl.ds(pl.multiple_of(pos * ROW_TILE, ROW_TILE), bit * ROW_TILE)]
                    cp = pltpu.make_async_copy(zero_sc.at[pl.ds(0, bit * ROW_TILE)], dst, sem.at[1])
                    if issue:
                        cp.start()
                    else:
                        cp.wait()
                pos = pos + jnp.where(take, bit, 0)

        def per_expert(e, carry):
            pad_copies(e, True)
            pad_copies(e, False)
            return carry

        lax.fori_loop(0, N_EXPERTS, per_expert, 0)

    src = lambda r, kk: _row_tile(h_ref, r)
    dst = lambda r, kk: _row_tile(xs_ref, d_ref[0, 0, kk * tm + r])
    _row_copies(src, dst, sem.at[0], tm, True)
    _row_copies(src, dst, sem.at[0], tm, False)


def _dispatch(pad_pos, pad_len, dest3, h1t, rows_out, *, tm, max_pad):
    N = h1t.shape[0] // ROW_TILE
    zero_rows = 1 << (max_pad.bit_length() - 1)
    return pl.pallas_call(
        functools.partial(_dispatch_kernel, tm=tm, max_pad=max_pad),
        grid_spec=pltpu.PrefetchScalarGridSpec(
            num_scalar_prefetch=2,
            grid=(N // tm,),
            in_specs=[
                pl.BlockSpec((1, 1, tm * TOP_K), lambda i, pp, pn: (i, 0, 0), memory_space=pltpu.SMEM),
                pl.BlockSpec((tm * ROW_TILE, LANES), lambda i, pp, pn: (i, 0)),
            ],
            out_specs=pl.BlockSpec(memory_space=pl.ANY),
            scratch_shapes=[pltpu.VMEM((zero_rows * ROW_TILE, LANES), F32), pltpu.SemaphoreType.DMA((2,))],
        ),
        out_shape=jax.ShapeDtypeStruct((rows_out * ROW_TILE, LANES), F32),
        compiler_params=_cparams(1),
        name="moe_dispatch",
    )(pad_pos, pad_len, dest3, h1t)


def _select_matrix():
    j = np.arange(2 * LANES)
    sel = np.zeros((2 * LANES, 2 * LANES), np.float32)
    sel[j, (j % 2) * LANES + j // 2] = 1.0
    return jnp.asarray(sel, BF16)


def _expert_kernel(be_ref, nu_ref, x_ref, wgu_ref, bg_ref, bl_ref, wd_ref, bd_ref, sel_ref, y_ref,
                   wg_sc, wl_sc, wd_sc, *, tm):
    i = pl.program_id(0)
    active = i < nu_ref[0]
    new_expert = jnp.logical_or(i == 0, be_ref[i] != be_ref[jnp.maximum(i - 1, 0)])

    @pl.when(jnp.logical_and(active, new_expert))
    def _():
        for c in range(wgu_ref.shape[2] // (2 * LANES)):
            blk = wgu_ref[0, :, c * 2 * LANES:(c + 1) * 2 * LANES].astype(BF16)
            r = jnp.dot(blk, sel_ref[...], preferred_element_type=F32).astype(BF16)
            wg_sc[:, c * LANES:(c + 1) * LANES] = r[:, :LANES]
            wl_sc[:, c * LANES:(c + 1) * LANES] = r[:, LANES:]
        wd_sc[...] = wd_ref[0].astype(BF16)

    @pl.when(active)
    def _():
        xb = _load_row_tiles(x_ref, tm).astype(BF16)
        hg = jnp.dot(xb, wg_sc[...], preferred_element_type=F32) + bg_ref[0]
        hl = jnp.dot(xb, wl_sc[...], preferred_element_type=F32) + bl_ref[0]
        glu = jnp.minimum(hg, SWIGLU_LIMIT)
        lin = jnp.clip(hl, -SWIGLU_LIMIT, SWIGLU_LIMIT)
        act = glu * _sigmoid(SWIGLU_ALPHA * glu) * (lin + 1.0)
        _store_row_tiles(y_ref, jnp.dot(act.astype(BF16), wd_sc[...], preferred_element_type=F32) + bd_ref[0])


def _experts(blk_exp, n_used, xs, wgu, bg, bl, wd, bd, *, tm):
    MP = xs.shape[0] // ROW_TILE
    _, D, F2 = wgu.shape
    F = F2 // 2
    rows = pl.BlockSpec((tm * ROW_TILE, LANES), lambda i, be, nu: (jnp.minimum(i, nu[0] - 1), 0))
    wsp = lambda a, b: pl.BlockSpec((1, a, b), lambda i, be, nu: (be[i], 0, 0))
    return pl.pallas_call(
        functools.partial(_expert_kernel, tm=tm),
        grid_spec=pltpu.PrefetchScalarGridSpec(
            num_scalar_prefetch=2,
            grid=(MP // tm,),
            in_specs=[rows, wsp(D, F2), wsp(1, F), wsp(1, F), wsp(F, D), wsp(1, D),
                      pl.BlockSpec((2 * LANES, 2 * LANES), lambda i, be, nu: (0, 0))],
            out_specs=rows,
            scratch_shapes=[pltpu.VMEM((D, F), BF16), pltpu.VMEM((D, F), BF16), pltpu.VMEM((F, D), BF16)],
        ),
        out_shape=jax.ShapeDtypeStruct(xs.shape, F32),
        compiler_params=_cparams(1),
        name="moe_experts",
    )(blk_exp, n_used, xs, wgu, bg, bl, wd, bd, _select_matrix())


def _combine_kernel(dcur_ref, dnxt_ref, h1_ref, gate_ref, g2_ref, b2_ref, ys_ref, o_ref, ybuf, sem, *, tm, nsteps):
    i = pl.program_id(0)
    slot = i % 2

    def copies(d_ref, s, issue):
        src = lambda r, kk: _row_tile(ys_ref, d_ref[0, 0, kk * tm + r])
        dst = lambda r, kk: _row_tile(ybuf.at[s, kk], r)
        _row_copies(src, dst, sem.at[s], tm, issue)

    @pl.when(i == 0)
    def _():
        copies(dcur_ref, 0, True)

    @pl.when(i + 1 < nsteps)
    def _():
        copies(dnxt_ref, 1 - slot, True)

    copies(dcur_ref, slot, False)
    acc = DN_ALPHA * _load_row_tiles(h1_ref, tm)
    for kk in range(TOP_K):
        acc = acc + gate_ref[:, kk:kk + 1] * _load_row_tiles(ybuf.at[slot, kk], tm)
    o_ref[...] = _ln(acc, g2_ref[...], b2_ref[...])


def _combine(dest3, h1t, gate, g2, b2, ys, *, tm):
    N, D = h1t.shape[0] // ROW_TILE, D_MODEL
    nsteps = N // tm
    dspec = lambda f: pl.BlockSpec((1, 1, tm * TOP_K), f, memory_space=pltpu.SMEM)
    return pl.pallas_call(
        functools.partial(_combine_kernel, tm=tm, nsteps=nsteps),
        grid=(nsteps,),
        in_specs=[
            dspec(lambda i: (i, 0, 0)),
            dspec(lambda i: (jnp.minimum(i + 1, nsteps - 1), 0, 0)),
            pl.BlockSpec((tm * ROW_TILE, LANES), lambda i: (i, 0)),
            pl.BlockSpec((tm, LANES), lambda i: (i, 0)),
            pl.BlockSpec((1, D), lambda i: (0, 0)),
            pl.BlockSpec((1, D), lambda i: (0, 0)),
            pl.BlockSpec(memory_space=pl.ANY),
        ],
        out_specs=pl.BlockSpec((tm, D), lambda i: (i, 0)),
        out_shape=jax.ShapeDtypeStruct((N, D), F32),
        scratch_shapes=[pltpu.VMEM((2, TOP_K, tm * ROW_TILE, LANES), F32), pltpu.SemaphoreType.DMA((2,))],
        compiler_params=_cparams(1),
        name="moe_combine",
    )(dest3, dest3, h1t, gate, g2, b2, ys)


def _rotary_column_order():
    idx = np.empty((ATT_HEADS * LANES,), np.int32)
    for h in range(ATT_HEADS):
        for half in range(2):
            for sub in range(2):
                for dd in range(32):
                    idx[h * LANES + half * 64 + sub * 32 + dd] = (2 * h + sub) * ATT_QK_DIM + half * 32 + dd
    return idx


def _rotary_tables(pos):
    half = ATT_QK_DIM // 2
    inv_freq = ROPE_THETA ** (-jnp.arange(half, dtype=F32) / half)
    ang = pos.astype(F32)[:, None] * inv_freq[None, :]
    cos = jnp.tile(jnp.cos(ang), (1, 4))
    sin = jnp.tile(jnp.sin(ang), (1, 4))
    sign = jnp.where(jnp.arange(LANES) < 64, -1.0, 1.0).astype(F32)
    return cos, sin * sign[None, :]


def _pick(n, prefs):
    for t in prefs:
        if n % t == 0:
            return t
    raise ValueError(f"unsupported size {n}")


def kernel(x, meta, emb_ln_g, emb_ln_b, w_in, conv_w, conv_b, gate_bias, lam_q1, lam_k1, lam_q2, lam_k2,
           att_norm_g, ml_norm_g, w_att_out, w_ml_out, w_o, ln1_g, ln1_b, w_router, b_router,
           w_gu, b_gu, w_down, b_down, ln2_g, ln2_b):
    B, S, D = x.shape
    assert D == D_MODEL and S % 512 == 0 and w_in.shape[0] == DEPTH
    N = B * S
    row2 = lambda a: a.reshape(1, -1)

    w = w_in[0]
    o_aq, o_ak, o_av, o_mqk, o_mv, o_mo, o_gt, o_ga, o_gm = 0, 512, 1024, 1536, 2560, 3072, 3584, 3592, 4616
    perm = _rotary_column_order()
    gpad = jnp.zeros((D, LANES - ML_HEADS), F32)
    w_all = jnp.concatenate([
        w[:, o_ga:o_ga + D], w[:, o_gm:o_gm + D], w[:, o_mqk:o_mqk + 2 * ML_WIDTH],
        w[:, o_aq:o_aq + 512][:, perm], w[:, o_ak:o_ak + 512][:, perm], w[:, o_av:o_av + 512],
        w[:, o_mv:o_mv + ML_WIDTH], w[:, o_mo:o_mo + ML_WIDTH],
        w[:, o_gt:o_gt + ML_HEADS], gpad, w[:, o_gt + ML_HEADS:o_gt + 2 * ML_HEADS], gpad,
    ], axis=1).astype(BF16)
    gb = gate_bias[0]
    zpad = jnp.zeros((LANES - ML_HEADS,), F32)
    gate_b = jnp.concatenate([gb[:ML_HEADS], zpad, gb[ML_HEADS:], zpad]).reshape(1, GATE_W)

    cos_x, sin_x = _rotary_tables(N_META + jnp.arange(S))
    cos_m, sin_m = _rotary_tables(jnp.maximum(jnp.arange(BLOCK) - PAD, 0))

    eg, eb = row2(emb_ln_g), row2(emb_ln_b)

    tm_in = 512
    zb, gates = _inproj(x, eg, eb, w_all, cos_x, sin_x, tm=tm_in)
    xm = jnp.concatenate([jnp.zeros((PAD, D), x.dtype), meta.astype(x.dtype)], axis=0)[None]
    zbm, gatesm = _inproj(xm, eg, eb, w_all, cos_m, sin_m, tm=BLOCK, first_valid_row=PAD)

    tq = 512
    yatt = _attention(zb, zbm, row2(lam_q1[0]), row2(lam_k1[0]), row2(lam_q2[0]), row2(lam_k2[0]),
                      row2(att_norm_g[0]), tq=tq, nh=1)
    yml = _mlstm(zb, gates, zbm, gatesm, conv_w[0], row2(conv_b[0]), gate_b, row2(ml_norm_g[0]),
                 nbps=4 if B % 4 == 0 else (2 if B % 2 == 0 else 1))

    wr32 = jnp.concatenate([w_router[0], jnp.zeros((D, LANES - N_EXPERTS), F32)], axis=1)
    wr_hi = wr32.astype(BF16)
    wr = jnp.concatenate([wr_hi, (wr32 - wr_hi.astype(F32)).astype(BF16)], axis=1)
    br = jnp.concatenate([b_router[0], jnp.full((LANES - N_EXPERTS,), NEG, F32)]).reshape(1, LANES)
    tm_out = 1024
    h1, eid, gate, rank, cnt = _outproj(
        x.reshape(N, D), yatt.reshape(N, -1), yml.reshape(N, -1), zb.reshape(N, ZB_W), eg, eb,
        w_att_out[0].astype(BF16), w_ml_out[0].astype(BF16), w_o[0].astype(BF16),
        row2(ln1_g[0]), row2(ln1_b[0]), wr, br, tm=tm_out, ts=512)

    tm_e = 512
    M = N * TOP_K
    nb = (M + N_EXPERTS * (tm_e - 1) + tm_e - 1) // tm_e
    counts = cnt[0, :N_EXPERTS].astype(jnp.int32)
    nblk = (counts + tm_e - 1) // tm_e
    cum = jnp.cumsum(nblk)
    pstart = (cum - nblk) * tm_e
    eid_t = eid[:TOP_K].astype(jnp.int32)
    base = jnp.sum(jnp.where(eid_t[None] == jnp.arange(N_EXPERTS)[:, None, None], pstart[:, None, None], 0), axis=0)
    dest = base + rank[:TOP_K].astype(jnp.int32)
    n_used = cum[-1:].astype(jnp.int32)
    blk = jnp.minimum(jnp.arange(nb, dtype=jnp.int32), n_used[0] - 1)
    blk_exp = jnp.sum(blk[:, None] >= cum[None, :], axis=1).astype(jnp.int32)

    def dest_tiles(tm):
        return dest.reshape(TOP_K, N // tm, tm).transpose(1, 0, 2).reshape(N // tm, 1, TOP_K * tm)

    tm_d, tm_c = _pick(N, (2048, 1024, 512, 256)), 128
    xs = _dispatch(pstart + counts, nblk * tm_e - counts, dest_tiles(tm_d), h1, nb * tm_e, tm=tm_d,
                   max_pad=tm_e - 1)

    bgu = b_gu[0]
    ys = _experts(blk_exp, n_used, xs, w_gu[0], bgu[:, None, 0::2], bgu[:, None, 1::2],
                  w_down[0], b_down[0][:, None, :], tm=tm_e)

    out = _combine(dest_tiles(tm_c), h1, gate, row2(ln2_g[0]), row2(ln2_b[0]), ys, tm=tm_c)
    return out.reshape(B, S, D)
```

```python
import functools
import math

import jax
import jax.numpy as jnp
import numpy as np
from jax import lax
from jax.experimental import pallas as pl
from jax.experimental.pallas import tpu as pltpu

D_MODEL = 1024
N_META = 16
BLOCK = 128
PAD = BLOCK - N_META
NEG = -1e30
LN_EPS = 1e-5
ATT_HEADS = 4
ATT_QK_DIM = 64
ATT_V_DIM = 128
ROPE_THETA = 10000.0
ML_HEADS = 4
ML_DH = 128
ML_WIDTH = ML_HEADS * ML_DH
CONV_K = 4
N_EXPERTS = 32
TOP_K = 4
SWIGLU_LIMIT = 7.0
SWIGLU_ALPHA = 1.702
DEPTH = 1
DN_ALPHA = (2 * DEPTH) ** 0.25
LAMBDA_INIT = 0.8 - 0.6 * math.exp(-0.3 * 0)

LANES = 128
C_GA, C_GM, C_MQK, C_AQ, C_AK, C_AV, C_MV, C_MO = 0, 1024, 2048, 3072, 3584, 4096, 4608, 5120
ZB_W = 5632
GATE_W = 2 * LANES
W_ALL = ZB_W + GATE_W
CHUNK = 512

VMEM_LIMIT = 56 * 1024 * 1024

BF16 = jnp.bfloat16
F32 = jnp.float32


def _cparams(n_axes):
    return pltpu.CompilerParams(dimension_semantics=("arbitrary",) * n_axes, vmem_limit_bytes=VMEM_LIMIT)


def _ln(x, g, b):
    mu = jnp.mean(x, axis=-1, keepdims=True)
    xc = x - mu
    var = jnp.mean(xc * xc, axis=-1, keepdims=True)
    return xc * lax.rsqrt(var + LN_EPS) * g + b


def _sigmoid(x):
    return 1.0 / (1.0 + jnp.exp(-x))


ROW_TILE = D_MODEL // LANES


def _store_row_tiles(ref, val):
    tm = val.shape[0]
    for g in range(ROW_TILE):
        ref[pl.ds(g, tm, stride=ROW_TILE), :] = val[:, g * LANES:(g + 1) * LANES]


def _load_row_tiles(ref, tm):
    return jnp.concatenate([ref[pl.ds(g, tm, stride=ROW_TILE), :] for g in range(ROW_TILE)], axis=1)


def _row_tile(ref, r):
    return ref.at[pl.ds(pl.multiple_of(r * ROW_TILE, ROW_TILE), ROW_TILE)]


def _inproj_kernel(x_ref, g_ref, b_ref, w_ref, cos_ref, sin_ref, zb_ref, gt_ref, *, first_valid_row):
    x = x_ref[0]
    tm = x.shape[0]
    hb = _ln(x, g_ref[...], b_ref[...]).astype(BF16)
    cos = cos_ref[...]
    sin = sin_ref[...]
    if first_valid_row:
        rowmask = lax.broadcasted_iota(jnp.int32, (tm, 1), 0) >= first_valid_row
    for c in range(ZB_W // CHUNK):
        z = jnp.dot(hb, w_ref[:, c * CHUNK:(c + 1) * CHUNK], preferred_element_type=F32)
        if c * CHUNK in (C_AQ, C_AK):
            parts = []
            for h in range(ATT_HEADS):
                zh = z[:, h * LANES:(h + 1) * LANES]
                parts.append(zh * cos + pltpu.roll(zh, 64, 1) * sin)
            z = jnp.concatenate(parts, axis=1)
            if c * CHUNK == C_AQ:
                z = z * (ATT_QK_DIM ** -0.5 * math.log2(math.e))
        if first_valid_row:
            z = jnp.where(rowmask, z, 0.0)
        zb_ref[0, :, c * CHUNK:(c + 1) * CHUNK] = z.astype(BF16)
    zg = jnp.dot(hb, w_ref[:, ZB_W:W_ALL], preferred_element_type=F32)
    if first_valid_row:
        zg = jnp.where(rowmask, zg, 0.0)
    gt_ref[0] = zg


def _inproj(x3, g, b, w, cos, sin, *, tm, first_valid_row=0):
    B, S, D = x3.shape
    nt = S // tm
    return pl.pallas_call(
        functools.partial(_inproj_kernel, first_valid_row=first_valid_row),
        grid=(B, nt),
        in_specs=[
            pl.BlockSpec((1, tm, D), lambda bi, i: (bi, i, 0)),
            pl.BlockSpec((1, D), lambda bi, i: (0, 0)),
            pl.BlockSpec((1, D), lambda bi, i: (0, 0)),
            pl.BlockSpec((D, W_ALL), lambda bi, i: (0, 0)),
            pl.BlockSpec((tm, LANES), lambda bi, i: (i, 0)),
            pl.BlockSpec((tm, LANES), lambda bi, i: (i, 0)),
        ],
        out_specs=[
            pl.BlockSpec((1, tm, ZB_W), lambda bi, i: (bi, i, 0)),
            pl.BlockSpec((1, tm, GATE_W), lambda bi, i: (bi, i, 0)),
        ],
        out_shape=[
            jax.ShapeDtypeStruct((B, S, ZB_W), BF16),
            jax.ShapeDtypeStruct((B, S, GATE_W), F32),
        ],
        compiler_params=_cparams(2),
        name="inproj",
    )(x3, g, b, w, cos, sin)


def _attn_kernel(q_ref, k_ref, v_ref, km_ref, vm_ref, lq1_ref, lk1_ref, lq2_ref, lk2_ref, ng_ref, o_ref,
                 sa_sc, sb_sc, pa_sc, pb_sc, aa_sc, ab_sc, m_sc, acc_sc, *, tq, tk, rc, nh):
    assert tq == 2 * tk
    qi = pl.program_id(2)
    rows = 2 * tq
    heads = range(nh)
    cols = lambda hh: slice(hh * LANES, (hh + 1) * LANES)
    lane = lax.broadcasted_iota(jnp.int32, (tq, LANES), 1)
    is_map0 = (lane & 63) < 32
    qq = {}
    for hh in heads:
        q = q_ref[0, :, cols(hh)]
        zero = jnp.zeros_like(q)
        qq[hh] = jnp.concatenate([jnp.where(is_map0, q, zero), jnp.where(is_map0, zero, q)], axis=0)

    def qk_stage(hh, kblk, s_ref, col0=0):
        s_ref[hh, :, col0:col0 + kblk.shape[0]] = lax.dot_general(qq[hh], kblk, (((1,), (1,)), ((), ())),
                                                                  preferred_element_type=F32)

    def pv_stage(hh, p_ref, a_ref, vblk, first=False):
        width = vblk.shape[0]
        v1 = jnp.concatenate([vblk, jnp.ones_like(vblk)], axis=1)
        pv = jnp.dot(p_ref[hh, :, 0:width], v1, preferred_element_type=F32)
        if first:
            acc_sc[hh] = pv
        else:
            a = a_ref[hh]
            acc_sc[hh] = jnp.concatenate([a, a], axis=1) * acc_sc[hh] + pv

    def sm_stage(hh, s_ref, p_ref, a_ref, width, vis, first=False):
        nrep = width // LANES
        for r0 in range(0, rows, rc):
            kind = vis(r0)
            if kind == "none":
                continue
            s = s_ref[hh, r0:r0 + rc, 0:width]
            if kind != "all":
                s = kind(s)
            m_cur = jnp.max(s, axis=1, keepdims=True)
            if first:
                m_new = jnp.broadcast_to(m_cur, (rc, LANES))
            else:
                m_prev = m_sc[hh, r0:r0 + rc, :]
                m_new = jnp.maximum(m_prev, m_cur)
                a_ref[hh, r0:r0 + rc, :] = jnp.exp2(m_prev - m_new)
            m_sc[hh, r0:r0 + rc, :] = m_new
        for r0 in range(0, rows, rc):
            kind = vis(r0)
            if kind == "none":
                p_ref[hh, r0:r0 + rc, 0:width] = jnp.zeros((rc, width), BF16)
                a_ref[hh, r0:r0 + rc, :] = jnp.ones((rc, LANES), F32)
                continue
            s = s_ref[hh, r0:r0 + rc, 0:width]
            if kind != "all":
                s = kind(s)
            m_new = m_sc[hh, r0:r0 + rc, :]
            m_rep = m_new if nrep == 1 else jnp.concatenate([m_new] * nrep, axis=1)
            p_ref[hh, r0:r0 + rc, 0:width] = jnp.exp2(s - m_rep).astype(BF16)

    def first_vis(r0):
        q0 = r0 % tq
        def mask(s):
            rowq = q0 + lax.broadcasted_iota(jnp.int32, s.shape, 0)
            col = lax.broadcasted_iota(jnp.int32, s.shape, 1)
            keep = col >= tk + PAD
            if tk - 1 <= q0:
                keep = jnp.logical_or(keep, col < tk)
            else:
                keep = jnp.logical_or(keep, col <= rowq)
            return jnp.where(keep, s, NEG)
        return mask

    def diag_vis(d):
        def vis(r0):
            q0 = r0 % tq
            if d * tk + tk - 1 <= q0:
                return "all"
            if d * tk > q0 + rc - 1:
                return "none"
            def mask(s):
                rowq = q0 + lax.broadcasted_iota(jnp.int32, s.shape, 0)
                col = d * tk + lax.broadcasted_iota(jnp.int32, s.shape, 1)
                return jnp.where(col <= rowq, s, NEG)
            return mask
        return vis

    all_vis = lambda r0: "all"
    kblk = lambda hh, start: k_ref[0, pl.ds(start, tk), cols(hh)]
    vblk = lambda hh, start: v_ref[0, pl.ds(start, tk), cols(hh)]

    d0 = pl.multiple_of(qi * tq, tk)
    d1 = pl.multiple_of(qi * tq + tk, tk)
    for hh in heads:
        qk_stage(hh, kblk(hh, d0), sa_sc)
    for hh in heads:
        qk_stage(hh, km_ref[0, :, cols(hh)], sa_sc, col0=tk)
    for hh in heads:
        qk_stage(hh, kblk(hh, d1), sb_sc)
    for hh in heads:
        sm_stage(hh, sa_sc, pa_sc, None, tk + BLOCK, first_vis, first=True)
    for hh in heads:
        pv_stage(hh, pa_sc, None, jnp.concatenate([vblk(hh, d0), vm_ref[0, :, cols(hh)]], axis=0), first=True)
    for hh in heads:
        sm_stage(hh, sb_sc, pb_sc, ab_sc, tk, diag_vis(1))
    for hh in heads:
        qk_stage(hh, kblk(hh, 0), sa_sc)

    def pair(i, carry):
        a_start = pl.multiple_of(i * tq, tk)
        b_start = pl.multiple_of(i * tq + tk, tk)
        prev_b = pl.multiple_of(jnp.where(i == 0, d1, a_start - tk), tk)
        next_a = pl.multiple_of(jnp.minimum(a_start + tq, (qi - 1) * tq), tk)
        for hh in heads:
            qk_stage(hh, kblk(hh, b_start), sb_sc)
        for hh in heads:
            pv_stage(hh, pb_sc, ab_sc, vblk(hh, prev_b))
        for hh in heads:
            sm_stage(hh, sa_sc, pa_sc, aa_sc, tk, all_vis)
        for hh in heads:
            qk_stage(hh, kblk(hh, next_a), sa_sc)
        for hh in heads:
            pv_stage(hh, pa_sc, aa_sc, vblk(hh, a_start))
        for hh in heads:
            sm_stage(hh, sb_sc, pb_sc, ab_sc, tk, all_vis)
        return carry

    lax.fori_loop(0, qi, pair, 0)
    last_b = pl.multiple_of(jnp.where(qi == 0, d1, qi * tq - tk), tk)
    for hh in heads:
        pv_stage(hh, pb_sc, ab_sc, vblk(hh, last_b))

    s1 = jnp.sum(lq1_ref[...] * lk1_ref[...], axis=1, keepdims=True)
    s2 = jnp.sum(lq2_ref[...] * lk2_ref[...], axis=1, keepdims=True)
    lam = jnp.exp(s1) - jnp.exp(s2) + LAMBDA_INIT
    for hh in heads:
        o = acc_sc[hh, :, 0:ATT_V_DIM] / acc_sc[hh, :, ATT_V_DIM:]
        a = o[:tq] - lam * o[tq:]
        y = a * lax.rsqrt(jnp.mean(a * a, axis=1, keepdims=True) + LN_EPS) * ng_ref[...]
        o_ref[0, :, cols(hh)] = (y * (1.0 - LAMBDA_INIT)).astype(BF16)


def _attention(zb, zbm, lq1, lk1, lq2, lk2, ng, *, tq, nh):
    B, S, _ = zb.shape
    nq = S // tq
    tk = tq // 2
    w = nh * LANES
    qb, kb, vb = C_AQ // w, C_AK // w, C_AV // w
    lam_spec = pl.BlockSpec((1, ATT_QK_DIM), lambda b, h, i: (0, 0))
    return pl.pallas_call(
        functools.partial(_attn_kernel, tq=tq, tk=tk, rc=32, nh=nh),
        grid=(B, ATT_HEADS // nh, nq),
        in_specs=[
            pl.BlockSpec((1, tq, w), lambda b, h, i: (b, i, qb + h)),
            pl.BlockSpec((1, S, w), lambda b, h, i: (b, 0, kb + h)),
            pl.BlockSpec((1, S, w), lambda b, h, i: (b, 0, vb + h)),
            pl.BlockSpec((1, BLOCK, w), lambda b, h, i: (0, 0, kb + h)),
            pl.BlockSpec((1, BLOCK, w), lambda b, h, i: (0, 0, vb + h)),
            lam_spec, lam_spec, lam_spec, lam_spec,
            pl.BlockSpec((1, ATT_V_DIM), lambda b, h, i: (0, 0)),
        ],
        out_specs=pl.BlockSpec((1, tq, w), lambda b, h, i: (b, i, h)),
        out_shape=jax.ShapeDtypeStruct((B, S, ATT_HEADS * ATT_V_DIM), BF16),
        scratch_shapes=[
            pltpu.VMEM((nh, 2 * tq, tk + BLOCK), F32), pltpu.VMEM((nh, 2 * tq, tk), F32),
            pltpu.VMEM((nh, 2 * tq, tk + BLOCK), BF16), pltpu.VMEM((nh, 2 * tq, tk), BF16),
            pltpu.VMEM((nh, 2 * tq, LANES), F32), pltpu.VMEM((nh, 2 * tq, LANES), F32),
            pltpu.VMEM((nh, 2 * tq, LANES), F32),
            pltpu.VMEM((nh, 2 * tq, 2 * ATT_V_DIM), F32),
        ],
        compiler_params=_cparams(3),
        name="diff_attention",
    )(zb, zb, zb, zbm, zbm, lq1, lk1, lq2, lk2, ng)


def _mlstm_kernel(xqk_ref, xv_ref, xo_ref, xg_ref, zm_ref, gm_ref, cw_ref, cb_ref, gb_ref, ng_ref, sh_ref, o_ref,
                  ct_sc, n_sc, m_sc, prev_sc, *, nbps):
    c = pl.program_id(1)
    L = BLOCK

    @pl.when(c == 0)
    def _():
        ct_sc[...] = jnp.zeros_like(ct_sc)
        n_sc[...] = jnp.zeros_like(n_sc)
        m_sc[...] = jnp.zeros_like(m_sc)
        prev_sc[...] = jnp.zeros_like(prev_sc)

    is_meta = c == 0
    row = lax.broadcasted_iota(jnp.int32, (L, 1), 0)
    valid = jnp.logical_or(c > 0, row >= PAD)
    sidx = lax.broadcasted_iota(jnp.int32, (L, L), 0)
    ridx = lax.broadcasted_iota(jnp.int32, (L, L), 1)
    causal = ridx <= sidx
    tri = causal.astype(BF16)
    elems = range(nbps)
    pairs = [(bb, h) for bb in elems for h in range(ML_HEADS)]
    sl = lambda h: slice(h * ML_DH, (h + 1) * ML_DH)
    st = lambda bb, h: bb * 8 + h

    q, k, v, ig, bcs, b_t, ig_t = {}, {}, {}, {}, {}, {}, {}
    for bb in elems:
        qk_pre = jnp.where(is_meta, zm_ref[0, :, C_MQK:C_MQK + 2 * ML_WIDTH], xqk_ref[bb])
        v[bb] = jnp.where(is_meta, zm_ref[0, :, C_MV:C_MV + ML_WIDTH], xv_ref[bb])
        gates = jnp.where(is_meta, gm_ref[0], xg_ref[bb])
        prev = prev_sc[bb * L:(bb + 1) * L, :]
        ext = jnp.concatenate([qk_pre, prev], axis=0)
        prev_sc[bb * L:(bb + 1) * L, :] = qk_pre
        shifted = jnp.dot(sh_ref[...], ext, preferred_element_type=F32)
        acc = cb_ref[...] + cw_ref[CONV_K - 1:CONV_K, :] * qk_pre.astype(F32)
        for s in range(1, CONV_K):
            acc = acc + cw_ref[CONV_K - 1 - s:CONV_K - s, :] * shifted[(s - 1) * L:s * L, :]
        qk = acc * _sigmoid(acc)
        qk = jnp.where(valid, qk, 0.0)
        q[bb] = qk[:, :ML_WIDTH]
        k[bb] = qk[:, ML_WIDTH:] * (ML_DH ** -0.5)

        igv = gates[:, :LANES] + gb_ref[:, :LANES]
        fg = gates[:, LANES:] + gb_ref[:, LANES:]
        lf = jnp.minimum(fg, 0.0) - jnp.log1p(jnp.exp(-jnp.abs(fg)))
        igv = jnp.where(valid, igv, NEG)
        lf = jnp.where(valid, lf, 0.0)
        lf_hi = lf.astype(BF16)
        lf_lo = (lf - lf_hi.astype(F32)).astype(BF16)
        bcs[bb] = (jnp.dot(tri, lf_hi, preferred_element_type=F32)
                   + jnp.dot(tri, lf_lo, preferred_element_type=F32))
        ig[bb] = igv
        b_t[bb] = bcs[bb].T
        ig_t[bb] = igv.T

    bc, m_h, dlog, inter, m_s = {}, {}, {}, {}, {}
    for p in pairs:
        bb, h = p
        bc[p] = bcs[bb][:, h:h + 1]
        m_h[p] = m_sc[st(bb, h):st(bb, h) + 1, 0:1]
        dlog[p] = jnp.where(causal, (bc[p] - b_t[bb][h:h + 1, :]) + ig_t[bb][h:h + 1, :], NEG)
        inter[p] = bc[p] + m_h[p]
    for p in pairs:
        m_s[p] = jnp.maximum(inter[p], jnp.max(dlog[p], axis=1, keepdims=True))

    w_intra, w_inter, qb, kb, s, qc = {}, {}, {}, {}, {}, {}
    for p in pairs:
        bb, h = p
        w_intra[p] = jnp.exp(dlog[p] - m_s[p])
        w_inter[p] = jnp.exp(inter[p] - m_s[p])
        qb[p] = q[bb][:, sl(h)].astype(BF16)
        kb[p] = k[bb][:, sl(h)].astype(BF16)
    for p in pairs:
        bb, h = p
        s[p] = lax.dot_general(qb[p], kb[p], (((1,), (1,)), ((), ())), preferred_element_type=F32) * w_intra[p]
        qc[p] = jnp.dot(qb[p], ct_sc[bb * ML_HEADS + h].astype(BF16), preferred_element_type=F32)

    hh = {}
    for p in pairs:
        bb, h = p
        nrow = n_sc[st(bb, h):st(bb, h) + 1, :]
        num = w_inter[p] * qc[p] + jnp.dot(s[p].astype(BF16), v[bb][:, sl(h)], preferred_element_type=F32)
        den = (w_inter[p] * jnp.sum(q[bb][:, sl(h)] * nrow, axis=1, keepdims=True)
               + jnp.sum(s[p], axis=1, keepdims=True))
        hh[p] = num / jnp.maximum(jnp.abs(den), jnp.exp(-m_s[p]))

    for p in pairs:
        bb, h = p
        b_last = bc[p][L - 1:L, :]
        upd = (b_last - bc[p]) + ig[bb][:, h:h + 1]
        m_new = jnp.maximum(b_last + m_h[p], jnp.max(upd, axis=0, keepdims=True))
        w_old = jnp.exp(b_last + m_h[p] - m_new)
        w_r = jnp.exp(upd - m_new)
        vw = (v[bb][:, sl(h)].astype(F32) * w_r).astype(BF16)
        ci = bb * ML_HEADS + h
        ct_sc[ci] = w_old * ct_sc[ci] + lax.dot_general(kb[p], vw, (((0,), (0,)), ((), ())),
                                                         preferred_element_type=F32)
        r = st(bb, h)
        n_sc[r:r + 1, :] = w_old * n_sc[r:r + 1, :] + jnp.sum(k[bb][:, sl(h)] * w_r, axis=0, keepdims=True)
        m_sc[r:r + 1, :] = jnp.broadcast_to(m_new, (1, LANES))

    outs = {}
    for p in pairs:
        bb, h = p
        hn = hh[p] * lax.rsqrt(jnp.mean(hh[p] * hh[p], axis=1, keepdims=True) + LN_EPS) * ng_ref[:, sl(h)]
        outs[p] = _sigmoid(xo_ref[bb, :, sl(h)].astype(F32)) * hn

    @pl.when(c > 0)
    def _():
        for bb in elems:
            o_ref[bb] = jnp.concatenate([outs[(bb, h)] for h in range(ML_HEADS)], axis=1).astype(BF16)


def _mlstm(zb, gates, zbm, gatesm, conv_w, conv_b, gate_b, ng, *, nbps):
    B, S, _ = zb.shape
    nc = S // BLOCK + 1
    t = np.arange(BLOCK)
    shift = np.zeros(((CONV_K - 1) * BLOCK, 2 * BLOCK), np.float32)
    for s in range(1, CONV_K):
        shift[(s - 1) * BLOCK + t, np.where(t >= s, t - s, 2 * BLOCK + t - s)] = 1.0

    def xc(c):
        return jnp.maximum(c - 1, 0)

    return pl.pallas_call(
        functools.partial(_mlstm_kernel, nbps=nbps),
        grid=(B // nbps, nc),
        in_specs=[
            pl.BlockSpec((nbps, BLOCK, 2 * ML_WIDTH), lambda b, c: (b, xc(c), C_MQK // (2 * ML_WIDTH))),
            pl.BlockSpec((nbps, BLOCK, ML_WIDTH), lambda b, c: (b, xc(c), C_MV // ML_WIDTH)),
            pl.BlockSpec((nbps, BLOCK, ML_WIDTH), lambda b, c: (b, xc(c), C_MO // ML_WIDTH)),
            pl.BlockSpec((nbps, BLOCK, GATE_W), lambda b, c: (b, xc(c), 0)),
            pl.BlockSpec((1, BLOCK, ZB_W), lambda b, c: (0, 0, 0)),
            pl.BlockSpec((1, BLOCK, GATE_W), lambda b, c: (0, 0, 0)),
            pl.BlockSpec((CONV_K, 2 * ML_WIDTH), lambda b, c: (0, 0)),
            pl.BlockSpec((1, 2 * ML_WIDTH), lambda b, c: (0, 0)),
            pl.BlockSpec((1, GATE_W), lambda b, c: (0, 0)),
            pl.BlockSpec((1, ML_WIDTH), lambda b, c: (0, 0)),
            pl.BlockSpec(((CONV_K - 1) * BLOCK, 2 * BLOCK), lambda b, c: (0, 0)),
        ],
        out_specs=pl.BlockSpec((nbps, BLOCK, ML_WIDTH), lambda b, c: (b, xc(c), 0)),
        out_shape=jax.ShapeDtypeStruct((B, S, ML_WIDTH), BF16),
        scratch_shapes=[
            pltpu.VMEM((nbps * ML_HEADS, ML_DH, ML_DH), F32),
            pltpu.VMEM((nbps * 8, ML_DH), F32),
            pltpu.VMEM((nbps * 8, LANES), F32),
            pltpu.VMEM((nbps * BLOCK, 2 * ML_WIDTH), BF16),
        ],
        compiler_params=_cparams(2),
        name="mlstm",
    )(zb, zb, zb, gates, zbm, gatesm, conv_w, conv_b, gate_b, ng, jnp.asarray(shift, BF16))


def _outproj_kernel(x_ref, ya_ref, ym_ref, ga_ref, gmm_ref, eg_ref, eb_ref, wa_ref, wm_ref, wo_ref,
                    g1_ref, b1_ref, wr_ref, br_ref,
                    h1_ref, eid_ref, gate_ref, rank_ref, cnt_ref, run_sc, *, tm, ts):
    i = pl.program_id(0)
    subs = range(tm // ts)
    rows = lambda j: slice(j * ts, (j + 1) * ts)

    @pl.when(i == 0)
    def _():
        run_sc[...] = jnp.zeros_like(run_sc)

    sig = lambda g: 0.5 * jnp.tanh(0.5 * g) + 0.5
    pa = {j: jnp.dot(ya_ref[rows(j), :], wa_ref[...], preferred_element_type=F32) for j in subs}
    pm = {j: jnp.dot(ym_ref[rows(j), :], wm_ref[...], preferred_element_type=F32) for j in subs}
    h0 = {j: _ln(x_ref[rows(j), :], eg_ref[...], eb_ref[...]) for j in subs}
    merged = {j: sig(ga_ref[rows(j), :]) * pa[j].astype(BF16) + sig(gmm_ref[rows(j), :]) * pm[j].astype(BF16)
              for j in subs}
    mix = {j: jnp.dot(merged[j], wo_ref[...], preferred_element_type=F32) for j in subs}
    h1 = {j: _ln(DN_ALPHA * h0[j] + mix[j], g1_ref[...], b1_ref[...]) for j in subs}
    for j in subs:
        _store_row_tiles(h1_ref.at[pl.ds(j * ts * ROW_TILE, ts * ROW_TILE)], h1[j])

    logits = {}
    for j in subs:
        h_hi = h1[j].astype(BF16)
        h_mid = (h1[j] - h_hi.astype(F32)).astype(BF16)
        hh = jnp.dot(h_hi, wr_ref[...], preferred_element_type=F32)
        logits[j] = (hh[:, :LANES] + hh[:, LANES:]
                     + jnp.dot(h_mid, wr_ref[:, :LANES], preferred_element_type=F32) + br_ref[...])

    lane = lax.broadcasted_iota(jnp.int32, (ts, LANES), 1)
    r_i = lax.broadcasted_iota(jnp.int32, (ts, ts), 0)
    c_i = lax.broadcasted_iota(jnp.int32, (ts, ts), 1)
    strict = (c_i < r_i).astype(BF16)
    sel_e, ex, den, onehots, oh = {}, {}, {}, {}, {}
    for j in subs:
        work = logits[j]
        es, vs = [], []
        for _ in range(TOP_K):
            mv = jnp.max(work, axis=1, keepdims=True)
            e = jnp.min(jnp.where(work == mv, lane, LANES), axis=1, keepdims=True)
            es.append(e)
            vs.append(mv)
            work = jnp.where(lane == e, -jnp.inf, work)
        sel_e[j] = es
        ex[j] = [jnp.exp(v - vs[0]) for v in vs]
        den[j] = ex[j][0] + ex[j][1] + ex[j][2] + ex[j][3]
        onehots[j] = [lane == e for e in es]
        acc = jnp.zeros((ts, LANES), F32)
        for o in onehots[j]:
            acc = acc + o.astype(F32)
        oh[j] = acc

    run = run_sc[...]
    for j in subs:
        before = jnp.dot(strict, oh[j].astype(BF16), preferred_element_type=F32) + run
        run = run + jnp.sum(oh[j], axis=0, keepdims=True)
        eid = jnp.zeros((ts, LANES), F32)
        gate = jnp.zeros((ts, LANES), F32)
        rank = jnp.zeros((ts, LANES), F32)
        for kk in range(TOP_K):
            here = lane == kk
            eid = jnp.where(here, sel_e[j][kk].astype(F32), eid)
            gate = jnp.where(here, ex[j][kk] / den[j], gate)
            rk = jnp.sum(jnp.where(onehots[j][kk], before, 0.0), axis=1, keepdims=True)
            rank = jnp.where(here, rk, rank)
        gate_ref[rows(j), :] = gate
        eid_ref[:, rows(j)] = eid.T[0:8, :]
        rank_ref[:, rows(j)] = rank.T[0:8, :]
    run_sc[...] = run
    cnt_ref[...] = run


def _outproj(x2, yatt, yml, zb2, eg, eb, wa, wm, wo, g1, b1, wr, br, *, tm, ts):
    N, D = x2.shape
    vec = lambda w: pl.BlockSpec((1, w), lambda i: (0, 0))
    full = lambda a, b: pl.BlockSpec((a, b), lambda i: (0, 0))
    tile = lambda w: pl.BlockSpec((tm, w), lambda i: (i, 0))
    return pl.pallas_call(
        functools.partial(_outproj_kernel, tm=tm, ts=ts),
        grid=(N // tm,),
        in_specs=[
            tile(D), tile(ATT_HEADS * ATT_V_DIM), tile(ML_WIDTH),
            pl.BlockSpec((tm, D), lambda i: (i, C_GA // D)),
            pl.BlockSpec((tm, D), lambda i: (i, C_GM // D)),
            vec(D), vec(D),
            full(ATT_HEADS * ATT_V_DIM, D), full(ML_WIDTH, D), full(D, D),
            vec(D), vec(D), full(D, 2 * LANES), vec(LANES),
        ],
        out_specs=[pl.BlockSpec((tm * ROW_TILE, LANES), lambda i: (i, 0)),
                   pl.BlockSpec((8, tm), lambda i: (0, i)), tile(LANES),
                   pl.BlockSpec((8, tm), lambda i: (0, i)), vec(LANES)],
        out_shape=[
            jax.ShapeDtypeStruct((N * ROW_TILE, LANES), F32),
            jax.ShapeDtypeStruct((8, N), F32),
            jax.ShapeDtypeStruct((N, LANES), F32),
            jax.ShapeDtypeStruct((8, N), F32),
            jax.ShapeDtypeStruct((1, LANES), F32),
        ],
        scratch_shapes=[pltpu.VMEM((1, LANES), F32)],
        compiler_params=_cparams(1),
        name="outproj_router",
    )(x2, yatt, yml, zb2, zb2, eg, eb, wa, wm, wo, g1, b1, wr, br)


def _row_copies(src_row, dst_row, sem, n, issue):
    def body(r, carry):
        for kk in range(TOP_K):
            cp = pltpu.make_async_copy(src_row(r, kk), dst_row(r, kk), sem)
            if issue:
                cp.start(priority=kk % 2)
            else:
                cp.wait()
        return carry
    lax.fori_loop(0, n, body, 0, unroll=8)


def _dispatch_kernel(pad_pos_ref, pad_len_ref, d_ref, h_ref, xs_ref, zero_sc, sem, *, tm, max_pad):
    @pl.when(pl.program_id(0) == 0)
    def _():
        zero_sc[...] = jnp.zeros_like(zero_sc)
        bits = [1 << b for b in reversed(range(max_pad.bit_length()))]

        def pad_copies(e, issue):
            pos = pad_pos_ref[e]
            length = pad_len_ref[e]
            for bit in bits:
                take = (length & bit) != 0

                @pl.when(take)
                def _(pos=pos, bit=bit):
                    dst = xs_ref.at[pl.ds(pl.multiple_of(pos * ROW_TILE, ROW_TILE), bit * ROW_TILE)]
                    cp = pltpu.make_async_copy(zero_sc.at[pl.ds(0, bit * ROW_TILE)], dst, sem.at[1])
                    if issue:
                        cp.start()
                    else:
                        cp.wait()
                pos = pos + jnp.where(take, bit, 0)

        def per_expert(e, carry):
            pad_copies(e, True)
            pad_copies(e, False)
            return carry

        lax.fori_loop(0, N_EXPERTS, per_expert, 0)

    src = lambda r, kk: _row_tile(h_ref, r)
    dst = lambda r, kk: _row_tile(xs_ref, d_ref[0, 0, kk * tm + r])
    _row_copies(src, dst, sem.at[0], tm, True)
    _row_copies(src, dst, sem.at[0], tm, False)


def _dispatch(pad_pos, pad_len, dest3, h1t, rows_out, *, tm, max_pad):
    N = h1t.shape[0] // ROW_TILE
    zero_rows = 1 << (max_pad.bit_length() - 1)
    return pl.pallas_call(
        functools.partial(_dispatch_kernel, tm=tm, max_pad=max_pad),
        grid_spec=pltpu.PrefetchScalarGridSpec(
            num_scalar_prefetch=2,
            grid=(N // tm,),
            in_specs=[
                pl.BlockSpec((1, 1, tm * TOP_K), lambda i, pp, pn: (i, 0, 0), memory_space=pltpu.SMEM),
                pl.BlockSpec((tm * ROW_TILE, LANES), lambda i, pp, pn: (i, 0)),
            ],
            out_specs=pl.BlockSpec(memory_space=pl.ANY),
            scratch_shapes=[pltpu.VMEM((zero_rows * ROW_TILE, LANES), F32), pltpu.SemaphoreType.DMA((2,))],
        ),
        out_shape=jax.ShapeDtypeStruct((rows_out * ROW_TILE, LANES), F32),
        compiler_params=_cparams(1),
        name="moe_dispatch",
    )(pad_pos, pad_len, dest3, h1t)


def _select_matrix():
    j = np.arange(2 * LANES)
    sel = np.zeros((2 * LANES, 2 * LANES), np.float32)
    sel[j, (j % 2) * LANES + j // 2] = 1.0
    return jnp.asarray(sel, BF16)


def _expert_kernel(be_ref, nu_ref, nx_ref, so_ref, x_ref, bg_ref, bl_ref, bd_ref, sel_ref, wgu_hbm, wd_hbm, y_ref,
                   wgu_buf, wdn_buf, wg_sc, wl_sc, wd_sc, sem, *, tm):
    i = pl.program_id(0)
    e = be_ref[i]
    slot = so_ref[e]
    active = i < nu_ref[0]
    new_expert = jnp.logical_or(i == 0, e != be_ref[jnp.maximum(i - 1, 0)])

    def weight_copies(ex, sl):
        return (pltpu.make_async_copy(wgu_hbm.at[ex], wgu_buf.at[sl], sem.at[0, sl]),
                pltpu.make_async_copy(wd_hbm.at[ex], wdn_buf.at[sl], sem.at[1, sl]))

    @pl.when(i == 0)
    def _():
        for cp in weight_copies(e, slot):
            cp.start()

    @pl.when(jnp.logical_and(active, new_expert))
    def _():
        for cp in weight_copies(e, slot):
            cp.wait()
        nxt = nx_ref[e]

        @pl.when(nxt < N_EXPERTS)
        def _():
            for cp in weight_copies(nxt, 1 - slot):
                cp.start()

        for c in range(wgu_buf.shape[2] // (2 * LANES)):
            blk = wgu_buf[slot, :, c * 2 * LANES:(c + 1) * 2 * LANES].astype(BF16)
            r = jnp.dot(blk, sel_ref[...], preferred_element_type=F32).astype(BF16)
            wg_sc[:, c * LANES:(c + 1) * LANES] = r[:, :LANES]
            wl_sc[:, c * LANES:(c + 1) * LANES] = r[:, LANES:]
        wd_sc[...] = wdn_buf[slot].astype(BF16)

    @pl.when(active)
    def _():
        xb = _load_row_tiles(x_ref, tm).astype(BF16)
        hg = jnp.dot(xb, wg_sc[...], preferred_element_type=F32) + bg_ref[0]
        hl = jnp.dot(xb, wl_sc[...], preferred_element_type=F32) + bl_ref[0]
        glu = jnp.minimum(hg, SWIGLU_LIMIT)
        lin = jnp.clip(hl, -SWIGLU_LIMIT, SWIGLU_LIMIT)
        act = glu * _sigmoid(SWIGLU_ALPHA * glu) * (lin + 1.0)
        _store_row_tiles(y_ref, jnp.dot(act.astype(BF16), wd_sc[...], preferred_element_type=F32) + bd_ref[0])


def _experts(blk_exp, n_used, next_exp, slot_of_exp, xs, wgu, bg, bl, wd, bd, *, tm):
    MP = xs.shape[0] // ROW_TILE
    _, D, F2 = wgu.shape
    F = F2 // 2
    rows = pl.BlockSpec((tm * ROW_TILE, LANES), lambda i, be, nu, nx, so: (jnp.minimum(i, nu[0] - 1), 0))
    wsp = lambda a, b: pl.BlockSpec((1, a, b), lambda i, be, nu, nx, so: (be[i], 0, 0))
    return pl.pallas_call(
        functools.partial(_expert_kernel, tm=tm),
        grid_spec=pltpu.PrefetchScalarGridSpec(
            num_scalar_prefetch=4,
            grid=(MP // tm,),
            in_specs=[rows, wsp(1, F), wsp(1, F), wsp(1, D),
                      pl.BlockSpec((2 * LANES, 2 * LANES), lambda i, be, nu, nx, so: (0, 0)),
                      pl.BlockSpec(memory_space=pl.ANY), pl.BlockSpec(memory_space=pl.ANY)],
            out_specs=rows,
            scratch_shapes=[pltpu.VMEM((2, D, F2), F32), pltpu.VMEM((2, F, D), F32),
                            pltpu.VMEM((D, F), BF16), pltpu.VMEM((D, F), BF16), pltpu.VMEM((F, D), BF16),
                            pltpu.SemaphoreType.DMA((2, 2))],
        ),
        out_shape=jax.ShapeDtypeStruct(xs.shape, F32),
        compiler_params=_cparams(1),
        name="moe_experts",
    )(blk_exp, n_used, next_exp, slot_of_exp, xs, bg, bl, bd, _select_matrix(), wgu, wd)


def _combine_kernel(dcur_ref, dnxt_ref, h1_ref, gate_ref, g2_ref, b2_ref, ys_ref, o_ref, ybuf, sem, *, tm, nsteps):
    i = pl.program_id(0)
    slot = i % 2

    def copies(d_ref, s, issue):
        src = lambda r, kk: _row_tile(ys_ref, d_ref[0, 0, kk * tm + r])
        dst = lambda r, kk: _row_tile(ybuf.at[s, kk], r)
        _row_copies(src, dst, sem.at[s], tm, issue)

    @pl.when(i == 0)
    def _():
        copies(dcur_ref, 0, True)

    @pl.when(i + 1 < nsteps)
    def _():
        copies(dnxt_ref, 1 - slot, True)

    copies(dcur_ref, slot, False)
    acc = DN_ALPHA * _load_row_tiles(h1_ref, tm)
    for kk in range(TOP_K):
        acc = acc + gate_ref[:, kk:kk + 1] * _load_row_tiles(ybuf.at[slot, kk], tm)
    o_ref[...] = _ln(acc, g2_ref[...], b2_ref[...])


def _combine(dest3, h1t, gate, g2, b2, ys, *, tm):
    N, D = h1t.shape[0] // ROW_TILE, D_MODEL
    nsteps = N // tm
    dspec = lambda f: pl.BlockSpec((1, 1, tm * TOP_K), f, memory_space=pltpu.SMEM)
    return pl.pallas_call(
        functools.partial(_combine_kernel, tm=tm, nsteps=nsteps),
        grid=(nsteps,),
        in_specs=[
            dspec(lambda i: (i, 0, 0)),
            dspec(lambda i: (jnp.minimum(i + 1, nsteps - 1), 0, 0)),
            pl.BlockSpec((tm * ROW_TILE, LANES), lambda i: (i, 0)),
            pl.BlockSpec((tm, LANES), lambda i: (i, 0)),
            pl.BlockSpec((1, D), lambda i: (0, 0)),
            pl.BlockSpec((1, D), lambda i: (0, 0)),
            pl.BlockSpec(memory_space=pl.ANY),
        ],
        out_specs=pl.BlockSpec((tm, D), lambda i: (i, 0)),
        out_shape=jax.ShapeDtypeStruct((N, D), F32),
        scratch_shapes=[pltpu.VMEM((2, TOP_K, tm * ROW_TILE, LANES), F32), pltpu.SemaphoreType.DMA((2,))],
        compiler_params=_cparams(1),
        name="moe_combine",
    )(dest3, dest3, h1t, gate, g2, b2, ys)


def _rotary_column_order():
    idx = np.empty((ATT_HEADS * LANES,), np.int32)
    for h in range(ATT_HEADS):
        for half in range(2):
            for sub in range(2):
                for dd in range(32):
                    idx[h * LANES + half * 64 + sub * 32 + dd] = (2 * h + sub) * ATT_QK_DIM + half * 32 + dd
    return idx


def _rotary_tables(pos):
    half = ATT_QK_DIM // 2
    inv_freq = ROPE_THETA ** (-jnp.arange(half, dtype=F32) / half)
    ang = pos.astype(F32)[:, None] * inv_freq[None, :]
    cos = jnp.tile(jnp.cos(ang), (1, 4))
    sin = jnp.tile(jnp.sin(ang), (1, 4))
    sign = jnp.where(jnp.arange(LANES) < 64, -1.0, 1.0).astype(F32)
    return cos, sin * sign[None, :]


def _pick(n, prefs):
    for t in prefs:
        if n % t == 0:
            return t
    raise ValueError(f"unsupported size {n}")


def kernel(x, meta, emb_ln_g, emb_ln_b, w_in, conv_w, conv_b, gate_bias, lam_q1, lam_k1, lam_q2, lam_k2,
           att_norm_g, ml_norm_g, w_att_out, w_ml_out, w_o, ln1_g, ln1_b, w_router, b_router,
           w_gu, b_gu, w_down, b_down, ln2_g, ln2_b):
    B, S, D = x.shape
    assert D == D_MODEL and S % 512 == 0 and w_in.shape[0] == DEPTH
    N = B * S
    row2 = lambda a: a.reshape(1, -1)

    w = w_in[0]
    o_aq, o_ak, o_av, o_mqk, o_mv, o_mo, o_gt, o_ga, o_gm = 0, 512, 1024, 1536, 2560, 3072, 3584, 3592, 4616
    perm = _rotary_column_order()
    gpad = jnp.zeros((D, LANES - ML_HEADS), F32)
    w_all = jnp.concatenate([
        w[:, o_ga:o_ga + D], w[:, o_gm:o_gm + D], w[:, o_mqk:o_mqk + 2 * ML_WIDTH],
        w[:, o_aq:o_aq + 512][:, perm], w[:, o_ak:o_ak + 512][:, perm], w[:, o_av:o_av + 512],
        w[:, o_mv:o_mv + ML_WIDTH], w[:, o_mo:o_mo + ML_WIDTH],
        w[:, o_gt:o_gt + ML_HEADS], gpad, w[:, o_gt + ML_HEADS:o_gt + 2 * ML_HEADS], gpad,
    ], axis=1).astype(BF16)
    gb = gate_bias[0]
    zpad = jnp.zeros((LANES - ML_HEADS,), F32)
    gate_b = jnp.concatenate([gb[:ML_HEADS], zpad, gb[ML_HEADS:], zpad]).reshape(1, GATE_W)

    cos_x, sin_x = _rotary_tables(N_META + jnp.arange(S))
    cos_m, sin_m = _rotary_tables(jnp.maximum(jnp.arange(BLOCK) - PAD, 0))

    eg, eb = row2(emb_ln_g), row2(emb_ln_b)

    tm_in = 512
    zb, gates = _inproj(x, eg, eb, w_all, cos_x, sin_x, tm=tm_in)
    xm = jnp.concatenate([jnp.zeros((PAD, D), x.dtype), meta.astype(x.dtype)], axis=0)[None]
    zbm, gatesm = _inproj(xm, eg, eb, w_all, cos_m, sin_m, tm=BLOCK, first_valid_row=PAD)

    tq = 512
    yatt = _attention(zb, zbm, row2(lam_q1[0]), row2(lam_k1[0]), row2(lam_q2[0]), row2(lam_k2[0]),
                      row2(att_norm_g[0]), tq=tq, nh=1)
    yml = _mlstm(zb, gates, zbm, gatesm, conv_w[0], row2(conv_b[0]), gate_b, row2(ml_norm_g[0]),
                 nbps=4 if B % 4 == 0 else (2 if B % 2 == 0 else 1))

    wr32 = jnp.concatenate([w_router[0], jnp.zeros((D, LANES - N_EXPERTS), F32)], axis=1)
    wr_hi = wr32.astype(BF16)
    wr = jnp.concatenate([wr_hi, (wr32 - wr_hi.astype(F32)).astype(BF16)], axis=1)
    br = jnp.concatenate([b_router[0], jnp.full((LANES - N_EXPERTS,), NEG, F32)]).reshape(1, LANES)
    tm_out = 1024
    h1, eid, gate, rank, cnt = _outproj(
        x.reshape(N, D), yatt.reshape(N, -1), yml.reshape(N, -1), zb.reshape(N, ZB_W), eg, eb,
        w_att_out[0].astype(BF16), w_ml_out[0].astype(BF16), w_o[0].astype(BF16),
        row2(ln1_g[0]), row2(ln1_b[0]), wr, br, tm=tm_out, ts=512)

    tm_e = 512
    M = N * TOP_K
    nb = (M + N_EXPERTS * (tm_e - 1) + tm_e - 1) // tm_e
    counts = cnt[0, :N_EXPERTS].astype(jnp.int32)
    nblk = (counts + tm_e - 1) // tm_e
    cum = jnp.cumsum(nblk)
    pstart = (cum - nblk) * tm_e
    eid_t = eid[:TOP_K].astype(jnp.int32)
    base = jnp.sum(jnp.where(eid_t[None] == jnp.arange(N_EXPERTS)[:, None, None], pstart[:, None, None], 0), axis=0)
    dest = base + rank[:TOP_K].astype(jnp.int32)
    n_used = cum[-1:].astype(jnp.int32)
    blk = jnp.minimum(jnp.arange(nb, dtype=jnp.int32), n_used[0] - 1)
    blk_exp = jnp.sum(blk[:, None] >= cum[None, :], axis=1).astype(jnp.int32)

    def dest_tiles(tm):
        return dest.reshape(TOP_K, N // tm, tm).transpose(1, 0, 2).reshape(N // tm, 1, TOP_K * tm)

    tm_d, tm_c = _pick(N, (1024, 512, 256)), 256
    xs = _dispatch(pstart + counts, nblk * tm_e - counts, dest_tiles(tm_d), h1, nb * tm_e, tm=tm_d,
                   max_pad=tm_e - 1)

    has = nblk > 0
    ids = jnp.where(has, jnp.arange(N_EXPERTS, dtype=jnp.int32), N_EXPERTS)
    after = jnp.concatenate([ids[1:], jnp.full((1,), N_EXPERTS, jnp.int32)])
    next_exp = lax.cummin(after, axis=0, reverse=True).astype(jnp.int32)
    slot_of_exp = ((jnp.cumsum(has.astype(jnp.int32)) - 1) & 1).astype(jnp.int32)
    bgu = b_gu[0]
    ys = _experts(blk_exp, n_used, next_exp, slot_of_exp, xs, w_gu[0], bgu[:, None, 0::2], bgu[:, None, 1::2],
                  w_down[0], b_down[0][:, None, :], tm=tm_e)

    out = _combine(dest_tiles(tm_c), h1, gate, row2(ln2_g[0]), row2(ln2_b[0]), ys, tm=tm_c)
    return out.reshape(B, S, D)
```

```python
import functools
import math

import jax
import jax.numpy as jnp
import numpy as np
from jax import lax
from jax.experimental import pallas as pl
from jax.experimental.pallas import tpu as pltpu

D_MODEL = 1024
N_META = 16
BLOCK = 128
PAD = BLOCK - N_META
NEG = -1e30
LN_EPS = 1e-5
ATT_HEADS = 4
ATT_QK_DIM = 64
ATT_V_DIM = 128
ROPE_THETA = 10000.0
ML_HEADS = 4
ML_DH = 128
ML_WIDTH = ML_HEADS * ML_DH
CONV_K = 4
N_EXPERTS = 32
TOP_K = 4
SWIGLU_LIMIT = 7.0
SWIGLU_ALPHA = 1.702
DEPTH = 1
DN_ALPHA = (2 * DEPTH) ** 0.25
LAMBDA_INIT = 0.8 - 0.6 * math.exp(-0.3 * 0)

LANES = 128
C_GA, C_GM, C_MQK, C_AQ, C_AK, C_AV, C_MV, C_MO = 0, 1024, 2048, 3072, 3584, 4096, 4608, 5120
ZB_W = 5632
GATE_W = 2 * LANES
W_ALL = ZB_W + GATE_W
CHUNK = 512

VMEM_LIMIT = 56 * 1024 * 1024

BF16 = jnp.bfloat16
F32 = jnp.float32


def _cparams(n_axes):
    return pltpu.CompilerParams(dimension_semantics=("arbitrary",) * n_axes, vmem_limit_bytes=VMEM_LIMIT)


def _ln(x, g, b):
    mu = jnp.mean(x, axis=-1, keepdims=True)
    xc = x - mu
    var = jnp.mean(xc * xc, axis=-1, keepdims=True)
    return xc * lax.rsqrt(var + LN_EPS) * g + b


def _sigmoid(x):
    return 1.0 / (1.0 + jnp.exp(-x))


ROW_TILE = D_MODEL // LANES


def _store_row_tiles(ref, val):
    tm = val.shape[0]
    for g in range(ROW_TILE):
        ref[pl.ds(g, tm, stride=ROW_TILE), :] = val[:, g * LANES:(g + 1) * LANES]


def _load_row_tiles(ref, tm):
    return jnp.concatenate([ref[pl.ds(g, tm, stride=ROW_TILE), :] for g in range(ROW_TILE)], axis=1)


def _row_tile(ref, r):
    return ref.at[pl.ds(pl.multiple_of(r * ROW_TILE, ROW_TILE), ROW_TILE)]


def _inproj_kernel(x_ref, g_ref, b_ref, w_ref, cos_ref, sin_ref, zb_ref, gt_ref, *, first_valid_row):
    x = x_ref[0]
    tm = x.shape[0]
    hb = _ln(x, g_ref[...], b_ref[...]).astype(BF16)
    cos = cos_ref[...]
    sin = sin_ref[...]
    if first_valid_row:
        rowmask = lax.broadcasted_iota(jnp.int32, (tm, 1), 0) >= first_valid_row
    for c in range(ZB_W // CHUNK):
        z = jnp.dot(hb, w_ref[:, c * CHUNK:(c + 1) * CHUNK], preferred_element_type=F32)
        if c * CHUNK in (C_AQ, C_AK):
            parts = []
            for h in range(ATT_HEADS):
                zh = z[:, h * LANES:(h + 1) * LANES]
                parts.append(zh * cos + pltpu.roll(zh, 64, 1) * sin)
            z = jnp.concatenate(parts, axis=1)
            if c * CHUNK == C_AQ:
                z = z * (ATT_QK_DIM ** -0.5 * math.log2(math.e))
        if first_valid_row:
            z = jnp.where(rowmask, z, 0.0)
        zb_ref[0, :, c * CHUNK:(c + 1) * CHUNK] = z.astype(BF16)
    zg = jnp.dot(hb, w_ref[:, ZB_W:W_ALL], preferred_element_type=F32)
    if first_valid_row:
        zg = jnp.where(rowmask, zg, 0.0)
    gt_ref[0] = zg


def _inproj(x3, g, b, w, cos, sin, *, tm, first_valid_row=0):
    B, S, D = x3.shape
    nt = S // tm
    return pl.pallas_call(
        functools.partial(_inproj_kernel, first_valid_row=first_valid_row),
        grid=(B, nt),
        in_specs=[
            pl.BlockSpec((1, tm, D), lambda bi, i: (bi, i, 0)),
            pl.BlockSpec((1, D), lambda bi, i: (0, 0)),
            pl.BlockSpec((1, D), lambda bi, i: (0, 0)),
            pl.BlockSpec((D, W_ALL), lambda bi, i: (0, 0)),
            pl.BlockSpec((tm, LANES), lambda bi, i: (i, 0)),
            pl.BlockSpec((tm, LANES), lambda bi, i: (i, 0)),
        ],
        out_specs=[
            pl.BlockSpec((1, tm, ZB_W), lambda bi, i: (bi, i, 0)),
            pl.BlockSpec((1, tm, GATE_W), lambda bi, i: (bi, i, 0)),
        ],
        out_shape=[
            jax.ShapeDtypeStruct((B, S, ZB_W), BF16),
            jax.ShapeDtypeStruct((B, S, GATE_W), F32),
        ],
        compiler_params=_cparams(2),
        name="inproj",
    )(x3, g, b, w, cos, sin)


def _attn_kernel(q_ref, k_ref, v_ref, km_ref, vm_ref, lq1_ref, lk1_ref, lq2_ref, lk2_ref, ng_ref, o_ref,
                 sa_sc, sb_sc, pa_sc, pb_sc, aa_sc, ab_sc, m_sc, acc_sc, *, tq, tk, rc, nh):
    assert tq == 2 * tk
    qi = pl.program_id(2)
    rows = 2 * tq
    heads = range(nh)
    cols = lambda hh: slice(hh * LANES, (hh + 1) * LANES)
    lane = lax.broadcasted_iota(jnp.int32, (tq, LANES), 1)
    is_map0 = (lane & 63) < 32
    qq = {}
    for hh in heads:
        q = q_ref[0, :, cols(hh)]
        zero = jnp.zeros_like(q)
        qq[hh] = jnp.concatenate([jnp.where(is_map0, q, zero), jnp.where(is_map0, zero, q)], axis=0)

    def qk_stage(hh, kblk, s_ref, col0=0):
        s_ref[hh, :, col0:col0 + kblk.shape[0]] = lax.dot_general(qq[hh], kblk, (((1,), (1,)), ((), ())),
                                                                  preferred_element_type=F32)

    def pv_stage(hh, p_ref, a_ref, vblk, first=False):
        width = vblk.shape[0]
        v1 = jnp.concatenate([vblk, jnp.ones_like(vblk)], axis=1)
        pv = jnp.dot(p_ref[hh, :, 0:width], v1, preferred_element_type=F32)
        if first:
            acc_sc[hh] = pv
        else:
            a = a_ref[hh]
            acc_sc[hh] = jnp.concatenate([a, a], axis=1) * acc_sc[hh] + pv

    def sm_stage(hh, s_ref, p_ref, a_ref, width, vis, first=False):
        nrep = width // LANES
        for r0 in range(0, rows, rc):
            kind = vis(r0)
            if kind == "none":
                continue
            s = s_ref[hh, r0:r0 + rc, 0:width]
            if kind != "all":
                s = kind(s)
            m_cur = jnp.max(s, axis=1, keepdims=True)
            if first:
                m_new = jnp.broadcast_to(m_cur, (rc, LANES))
            else:
                m_prev = m_sc[hh, r0:r0 + rc, :]
                m_new = jnp.maximum(m_prev, m_cur)
                a_ref[hh, r0:r0 + rc, :] = jnp.exp2(m_prev - m_new)
            m_sc[hh, r0:r0 + rc, :] = m_new
        for r0 in range(0, rows, rc):
            kind = vis(r0)
            if kind == "none":
                p_ref[hh, r0:r0 + rc, 0:width] = jnp.zeros((rc, width), BF16)
                a_ref[hh, r0:r0 + rc, :] = jnp.ones((rc, LANES), F32)
                continue
            s = s_ref[hh, r0:r0 + rc, 0:width]
            if kind != "all":
                s = kind(s)
            m_new = m_sc[hh, r0:r0 + rc, :]
            m_rep = m_new if nrep == 1 else jnp.concatenate([m_new] * nrep, axis=1)
            p_ref[hh, r0:r0 + rc, 0:width] = jnp.exp2(s - m_rep).astype(BF16)

    def first_vis(r0):
        q0 = r0 % tq
        def mask(s):
            rowq = q0 + lax.broadcasted_iota(jnp.int32, s.shape, 0)
            col = lax.broadcasted_iota(jnp.int32, s.shape, 1)
            keep = col >= tk + PAD
            if tk - 1 <= q0:
                keep = jnp.logical_or(keep, col < tk)
            else:
                keep = jnp.logical_or(keep, col <= rowq)
            return jnp.where(keep, s, NEG)
        return mask

    def diag_vis(d):
        def vis(r0):
            q0 = r0 % tq
            if d * tk + tk - 1 <= q0:
                return "all"
            if d * tk > q0 + rc - 1:
                return "none"
            def mask(s):
                rowq = q0 + lax.broadcasted_iota(jnp.int32, s.shape, 0)
                col = d * tk + lax.broadcasted_iota(jnp.int32, s.shape, 1)
                return jnp.where(col <= rowq, s, NEG)
            return mask
        return vis

    all_vis = lambda r0: "all"
    kblk = lambda hh, start: k_ref[0, pl.ds(start, tk), cols(hh)]
    vblk = lambda hh, start: v_ref[0, pl.ds(start, tk), cols(hh)]

    d0 = pl.multiple_of(qi * tq, tk)
    d1 = pl.multiple_of(qi * tq + tk, tk)
    for hh in heads:
        qk_stage(hh, kblk(hh, d0), sa_sc)
    for hh in heads:
        qk_stage(hh, km_ref[0, :, cols(hh)], sa_sc, col0=tk)
    for hh in heads:
        qk_stage(hh, kblk(hh, d1), sb_sc)
    for hh in heads:
        sm_stage(hh, sa_sc, pa_sc, None, tk + BLOCK, first_vis, first=True)
    for hh in heads:
        qk_stage(hh, kblk(hh, 0), sa_sc)
    for hh in heads:
        sm_stage(hh, sb_sc, pb_sc, ab_sc, tk, diag_vis(1))
    for hh in heads:
        pv_stage(hh, pa_sc, None, jnp.concatenate([vblk(hh, d0), vm_ref[0, :, cols(hh)]], axis=0), first=True)

    def pair(i, carry):
        a_start = pl.multiple_of(i * tq, tk)
        b_start = pl.multiple_of(i * tq + tk, tk)
        prev_b = pl.multiple_of(jnp.where(i == 0, d1, a_start - tk), tk)
        next_a = pl.multiple_of(jnp.minimum(a_start + tq, (qi - 1) * tq), tk)
        for hh in heads:
            qk_stage(hh, kblk(hh, b_start), sb_sc)
        for hh in heads:
            pv_stage(hh, pb_sc, ab_sc, vblk(hh, prev_b))
        for hh in heads:
            sm_stage(hh, sa_sc, pa_sc, aa_sc, tk, all_vis)
        for hh in heads:
            qk_stage(hh, kblk(hh, next_a), sa_sc)
        for hh in heads:
            pv_stage(hh, pa_sc, aa_sc, vblk(hh, a_start))
        for hh in heads:
            sm_stage(hh, sb_sc, pb_sc, ab_sc, tk, all_vis)
        return carry

    lax.fori_loop(0, qi, pair, 0)
    last_b = pl.multiple_of(jnp.where(qi == 0, d1, qi * tq - tk), tk)
    for hh in heads:
        pv_stage(hh, pb_sc, ab_sc, vblk(hh, last_b))

    s1 = jnp.sum(lq1_ref[...] * lk1_ref[...], axis=1, keepdims=True)
    s2 = jnp.sum(lq2_ref[...] * lk2_ref[...], axis=1, keepdims=True)
    lam = jnp.exp(s1) - jnp.exp(s2) + LAMBDA_INIT
    for hh in heads:
        o = acc_sc[hh, :, 0:ATT_V_DIM] / acc_sc[hh, :, ATT_V_DIM:]
        a = o[:tq] - lam * o[tq:]
        y = a * lax.rsqrt(jnp.mean(a * a, axis=1, keepdims=True) + LN_EPS) * ng_ref[...]
        o_ref[0, :, cols(hh)] = (y * (1.0 - LAMBDA_INIT)).astype(BF16)


def _attention(zb, zbm, lq1, lk1, lq2, lk2, ng, *, tq, nh):
    B, S, _ = zb.shape
    nq = S // tq
    tk = tq // 2
    w = nh * LANES
    qb, kb, vb = C_AQ // w, C_AK // w, C_AV // w
    lam_spec = pl.BlockSpec((1, ATT_QK_DIM), lambda b, h, i: (0, 0))
    return pl.pallas_call(
        functools.partial(_attn_kernel, tq=tq, tk=tk, rc=32, nh=nh),
        grid=(B, ATT_HEADS // nh, nq),
        in_specs=[
            pl.BlockSpec((1, tq, w), lambda b, h, i: (b, i, qb + h)),
            pl.BlockSpec((1, S, w), lambda b, h, i: (b, 0, kb + h)),
            pl.BlockSpec((1, S, w), lambda b, h, i: (b, 0, vb + h)),
            pl.BlockSpec((1, BLOCK, w), lambda b, h, i: (0, 0, kb + h)),
            pl.BlockSpec((1, BLOCK, w), lambda b, h, i: (0, 0, vb + h)),
            lam_spec, lam_spec, lam_spec, lam_spec,
            pl.BlockSpec((1, ATT_V_DIM), lambda b, h, i: (0, 0)),
        ],
        out_specs=pl.BlockSpec((1, tq, w), lambda b, h, i: (b, i, h)),
        out_shape=jax.ShapeDtypeStruct((B, S, ATT_HEADS * ATT_V_DIM), BF16),
        scratch_shapes=[
            pltpu.VMEM((nh, 2 * tq, tk + BLOCK), F32), pltpu.VMEM((nh, 2 * tq, tk), F32),
            pltpu.VMEM((nh, 2 * tq, tk + BLOCK), BF16), pltpu.VMEM((nh, 2 * tq, tk), BF16),
            pltpu.VMEM((nh, 2 * tq, LANES), F32), pltpu.VMEM((nh, 2 * tq, LANES), F32),
            pltpu.VMEM((nh, 2 * tq, LANES), F32),
            pltpu.VMEM((nh, 2 * tq, 2 * ATT_V_DIM), F32),
        ],
        compiler_params=_cparams(3),
        name="diff_attention",
    )(zb, zb, zb, zbm, zbm, lq1, lk1, lq2, lk2, ng)


def _mlstm_kernel(xqk_ref, xv_ref, xo_ref, xg_ref, zm_ref, gm_ref, cw_ref, cb_ref, gb_ref, ng_ref, sh_ref, o_ref,
                  ct_sc, n_sc, m_sc, prev_sc, *, nbps):
    c = pl.program_id(1)
    L = BLOCK

    @pl.when(c == 0)
    def _():
        ct_sc[...] = jnp.zeros_like(ct_sc)
        n_sc[...] = jnp.zeros_like(n_sc)
        m_sc[...] = jnp.zeros_like(m_sc)
        prev_sc[...] = jnp.zeros_like(prev_sc)

    is_meta = c == 0
    row = lax.broadcasted_iota(jnp.int32, (L, 1), 0)
    valid = jnp.logical_or(c > 0, row >= PAD)
    sidx = lax.broadcasted_iota(jnp.int32, (L, L), 0)
    ridx = lax.broadcasted_iota(jnp.int32, (L, L), 1)
    causal = ridx <= sidx
    tri = causal.astype(BF16)
    elems = range(nbps)
    pairs = [(bb, h) for bb in elems for h in range(ML_HEADS)]
    sl = lambda h: slice(h * ML_DH, (h + 1) * ML_DH)
    st = lambda bb, h: bb * 8 + h

    q, k, v, ig, bcs, b_t, ig_t = {}, {}, {}, {}, {}, {}, {}
    for bb in elems:
        qk_pre = jnp.where(is_meta, zm_ref[0, :, C_MQK:C_MQK + 2 * ML_WIDTH], xqk_ref[bb])
        v[bb] = jnp.where(is_meta, zm_ref[0, :, C_MV:C_MV + ML_WIDTH], xv_ref[bb])
        gates = jnp.where(is_meta, gm_ref[0], xg_ref[bb])
        prev = prev_sc[bb * L:(bb + 1) * L, :]
        ext = jnp.concatenate([qk_pre, prev], axis=0)
        prev_sc[bb * L:(bb + 1) * L, :] = qk_pre
        shifted = jnp.dot(sh_ref[...], ext, preferred_element_type=F32)
        acc = cb_ref[...] + cw_ref[CONV_K - 1:CONV_K, :] * qk_pre.astype(F32)
        for s in range(1, CONV_K):
            acc = acc + cw_ref[CONV_K - 1 - s:CONV_K - s, :] * shifted[(s - 1) * L:s * L, :]
        qk = acc * _sigmoid(acc)
        qk = jnp.where(valid, qk, 0.0)
        q[bb] = qk[:, :ML_WIDTH]
        k[bb] = qk[:, ML_WIDTH:] * (ML_DH ** -0.5)

        igv = gates[:, :LANES] + gb_ref[:, :LANES]
        fg = gates[:, LANES:] + gb_ref[:, LANES:]
        lf = jnp.minimum(fg, 0.0) - jnp.log1p(jnp.exp(-jnp.abs(fg)))
        igv = jnp.where(valid, igv, NEG)
        lf = jnp.where(valid, lf, 0.0)
        lf_hi = lf.astype(BF16)
        lf_lo = (lf - lf_hi.astype(F32)).astype(BF16)
        bcs[bb] = (jnp.dot(tri, lf_hi, preferred_element_type=F32)
                   + jnp.dot(tri, lf_lo, preferred_element_type=F32))
        ig[bb] = igv
        b_t[bb] = bcs[bb].T
        ig_t[bb] = igv.T

    bc, m_h, dlog, inter, m_s = {}, {}, {}, {}, {}
    for p in pairs:
        bb, h = p
        bc[p] = bcs[bb][:, h:h + 1]
        m_h[p] = m_sc[st(bb, h):st(bb, h) + 1, 0:1]
        dlog[p] = jnp.where(causal, (bc[p] - b_t[bb][h:h + 1, :]) + ig_t[bb][h:h + 1, :], NEG)
        inter[p] = bc[p] + m_h[p]
    for p in pairs:
        m_s[p] = jnp.maximum(inter[p], jnp.max(dlog[p], axis=1, keepdims=True))

    w_intra, w_inter, qb, kb, s, qc = {}, {}, {}, {}, {}, {}
    for p in pairs:
        bb, h = p
        w_intra[p] = jnp.exp(dlog[p] - m_s[p])
        w_inter[p] = jnp.exp(inter[p] - m_s[p])
        qb[p] = q[bb][:, sl(h)].astype(BF16)
        kb[p] = k[bb][:, sl(h)].astype(BF16)
    for p in pairs:
        bb, h = p
        s[p] = lax.dot_general(qb[p], kb[p], (((1,), (1,)), ((), ())), preferred_element_type=F32) * w_intra[p]
        qc[p] = jnp.dot(qb[p], ct_sc[bb * ML_HEADS + h].astype(BF16), preferred_element_type=F32)

    hh = {}
    for p in pairs:
        bb, h = p
        nrow = n_sc[st(bb, h):st(bb, h) + 1, :]
        num = w_inter[p] * qc[p] + jnp.dot(s[p].astype(BF16), v[bb][:, sl(h)], preferred_element_type=F32)
        den = (w_inter[p] * jnp.sum(q[bb][:, sl(h)] * nrow, axis=1, keepdims=True)
               + jnp.sum(s[p], axis=1, keepdims=True))
        hh[p] = num / jnp.maximum(jnp.abs(den), jnp.exp(-m_s[p]))

    for p in pairs:
        bb, h = p
        b_last = bc[p][L - 1:L, :]
        upd = (b_last - bc[p]) + ig[bb][:, h:h + 1]
        m_new = jnp.maximum(b_last + m_h[p], jnp.max(upd, axis=0, keepdims=True))
        w_old = jnp.exp(b_last + m_h[p] - m_new)
        w_r = jnp.exp(upd - m_new)
        vw = (v[bb][:, sl(h)].astype(F32) * w_r).astype(BF16)
        ci = bb * ML_HEADS + h
        ct_sc[ci] = w_old * ct_sc[ci] + lax.dot_general(kb[p], vw, (((0,), (0,)), ((), ())),
                                                         preferred_element_type=F32)
        r = st(bb, h)
        n_sc[r:r + 1, :] = w_old * n_sc[r:r + 1, :] + jnp.sum(k[bb][:, sl(h)] * w_r, axis=0, keepdims=True)
        m_sc[r:r + 1, :] = jnp.broadcast_to(m_new, (1, LANES))

    outs = {}
    for p in pairs:
        bb, h = p
        hn = hh[p] * lax.rsqrt(jnp.mean(hh[p] * hh[p], axis=1, keepdims=True) + LN_EPS) * ng_ref[:, sl(h)]
        outs[p] = _sigmoid(xo_ref[bb, :, sl(h)].astype(F32)) * hn

    @pl.when(c > 0)
    def _():
        for bb in elems:
            o_ref[bb] = jnp.concatenate([outs[(bb, h)] for h in range(ML_HEADS)], axis=1).astype(BF16)


def _mlstm(zb, gates, zbm, gatesm, conv_w, conv_b, gate_b, ng, *, nbps):
    B, S, _ = zb.shape
    nc = S // BLOCK + 1
    t = np.arange(BLOCK)
    shift = np.zeros(((CONV_K - 1) * BLOCK, 2 * BLOCK), np.float32)
    for s in range(1, CONV_K):
        shift[(s - 1) * BLOCK + t, np.where(t >= s, t - s, 2 * BLOCK + t - s)] = 1.0

    def xc(c):
        return jnp.maximum(c - 1, 0)

    return pl.pallas_call(
        functools.partial(_mlstm_kernel, nbps=nbps),
        grid=(B // nbps, nc),
        in_specs=[
            pl.BlockSpec((nbps, BLOCK, 2 * ML_WIDTH), lambda b, c: (b, xc(c), C_MQK // (2 * ML_WIDTH))),
            pl.BlockSpec((nbps, BLOCK, ML_WIDTH), lambda b, c: (b, xc(c), C_MV // ML_WIDTH)),
            pl.BlockSpec((nbps, BLOCK, ML_WIDTH), lambda b, c: (b, xc(c), C_MO // ML_WIDTH)),
            pl.BlockSpec((nbps, BLOCK, GATE_W), lambda b, c: (b, xc(c), 0)),
            pl.BlockSpec((1, BLOCK, ZB_W), lambda b, c: (0, 0, 0)),
            pl.BlockSpec((1, BLOCK, GATE_W), lambda b, c: (0, 0, 0)),
            pl.BlockSpec((CONV_K, 2 * ML_WIDTH), lambda b, c: (0, 0)),
            pl.BlockSpec((1, 2 * ML_WIDTH), lambda b, c: (0, 0)),
            pl.BlockSpec((1, GATE_W), lambda b, c: (0, 0)),
            pl.BlockSpec((1, ML_WIDTH), lambda b, c: (0, 0)),
            pl.BlockSpec(((CONV_K - 1) * BLOCK, 2 * BLOCK), lambda b, c: (0, 0)),
        ],
        out_specs=pl.BlockSpec((nbps, BLOCK, ML_WIDTH), lambda b, c: (b, xc(c), 0)),
        out_shape=jax.ShapeDtypeStruct((B, S, ML_WIDTH), BF16),
        scratch_shapes=[
            pltpu.VMEM((nbps * ML_HEADS, ML_DH, ML_DH), F32),
            pltpu.VMEM((nbps * 8, ML_DH), F32),
            pltpu.VMEM((nbps * 8, LANES), F32),
            pltpu.VMEM((nbps * BLOCK, 2 * ML_WIDTH), BF16),
        ],
        compiler_params=_cparams(2),
        name="mlstm",
    )(zb, zb, zb, gates, zbm, gatesm, conv_w, conv_b, gate_b, ng, jnp.asarray(shift, BF16))


def _outproj_kernel(x_ref, ya_ref, ym_ref, ga_ref, gmm_ref, eg_ref, eb_ref, wa_ref, wm_ref, wo_ref,
                    g1_ref, b1_ref, wr_ref, br_ref,
                    h1_ref, eid_ref, gate_ref, rank_ref, cnt_ref, run_sc, *, tm, ts):
    i = pl.program_id(0)
    subs = range(tm // ts)
    rows = lambda j: slice(j * ts, (j + 1) * ts)

    @pl.when(i == 0)
    def _():
        run_sc[...] = jnp.zeros_like(run_sc)

    sig = lambda g: 0.5 * jnp.tanh(0.5 * g) + 0.5
    pa = {j: jnp.dot(ya_ref[rows(j), :], wa_ref[...], preferred_element_type=F32) for j in subs}
    pm = {j: jnp.dot(ym_ref[rows(j), :], wm_ref[...], preferred_element_type=F32) for j in subs}
    h0 = {j: _ln(x_ref[rows(j), :], eg_ref[...], eb_ref[...]) for j in subs}
    merged = {j: sig(ga_ref[rows(j), :]) * pa[j].astype(BF16) + sig(gmm_ref[rows(j), :]) * pm[j].astype(BF16)
              for j in subs}
    mix = {j: jnp.dot(merged[j], wo_ref[...], preferred_element_type=F32) for j in subs}
    h1 = {j: _ln(DN_ALPHA * h0[j] + mix[j], g1_ref[...], b1_ref[...]) for j in subs}
    for j in subs:
        _store_row_tiles(h1_ref.at[pl.ds(j * ts * ROW_TILE, ts * ROW_TILE)], h1[j])

    logits = {}
    for j in subs:
        h_hi = h1[j].astype(BF16)
        h_mid = (h1[j] - h_hi.astype(F32)).astype(BF16)
        hh = jnp.dot(h_hi, wr_ref[...], preferred_element_type=F32)
        logits[j] = (hh[:, :LANES] + hh[:, LANES:]
                     + jnp.dot(h_mid, wr_ref[:, :LANES], preferred_element_type=F32) + br_ref[...])

    lane = lax.broadcasted_iota(jnp.int32, (ts, LANES), 1)
    r_i = lax.broadcasted_iota(jnp.int32, (ts, ts), 0)
    c_i = lax.broadcasted_iota(jnp.int32, (ts, ts), 1)
    strict = (c_i < r_i).astype(BF16)
    sel_e, ex, den, onehots, oh = {}, {}, {}, {}, {}
    for j in subs:
        work = logits[j]
        es, vs = [], []
        for _ in range(TOP_K):
            mv = jnp.max(work, axis=1, keepdims=True)
            e = jnp.min(jnp.where(work == mv, lane, LANES), axis=1, keepdims=True)
            es.append(e)
            vs.append(mv)
            work = jnp.where(lane == e, -jnp.inf, work)
        sel_e[j] = es
        ex[j] = [jnp.exp(v - vs[0]) for v in vs]
        den[j] = ex[j][0] + ex[j][1] + ex[j][2] + ex[j][3]
        onehots[j] = [lane == e for e in es]
        acc = jnp.zeros((ts, LANES), F32)
        for o in onehots[j]:
            acc = acc + o.astype(F32)
        oh[j] = acc

    run = run_sc[...]
    for j in subs:
        before = jnp.dot(strict, oh[j].astype(BF16), preferred_element_type=F32) + run
        run = run + jnp.sum(oh[j], axis=0, keepdims=True)
        eid = jnp.zeros((ts, LANES), F32)
        gate = jnp.zeros((ts, LANES), F32)
        rank = jnp.zeros((ts, LANES), F32)
        for kk in range(TOP_K):
            here = lane == kk
            eid = jnp.where(here, sel_e[j][kk].astype(F32), eid)
            gate = jnp.where(here, ex[j][kk] / den[j], gate)
            rk = jnp.sum(jnp.where(onehots[j][kk], before, 0.0), axis=1, keepdims=True)
            rank = jnp.where(here, rk, rank)
        gate_ref[rows(j), :] = gate
        eid_ref[:, rows(j)] = eid.T[0:8, :]
        rank_ref[:, rows(j)] = rank.T[0:8, :]
    run_sc[...] = run
    cnt_ref[...] = run


def _outproj(x2, yatt, yml, zb2, eg, eb, wa, wm, wo, g1, b1, wr, br, *, tm, ts):
    N, D = x2.shape
    vec = lambda w: pl.BlockSpec((1, w), lambda i: (0, 0))
    full = lambda a, b: pl.BlockSpec((a, b), lambda i: (0, 0))
    tile = lambda w: pl.BlockSpec((tm, w), lambda i: (i, 0))
    return pl.pallas_call(
        functools.partial(_outproj_kernel, tm=tm, ts=ts),
        grid=(N // tm,),
        in_specs=[
            tile(D), tile(ATT_HEADS * ATT_V_DIM), tile(ML_WIDTH),
            pl.BlockSpec((tm, D), lambda i: (i, C_GA // D)),
            pl.BlockSpec((tm, D), lambda i: (i, C_GM // D)),
            vec(D), vec(D),
            full(ATT_HEADS * ATT_V_DIM, D), full(ML_WIDTH, D), full(D, D),
            vec(D), vec(D), full(D, 2 * LANES), vec(LANES),
        ],
        out_specs=[pl.BlockSpec((tm * ROW_TILE, LANES), lambda i: (i, 0)),
                   pl.BlockSpec((8, tm), lambda i: (0, i)), tile(LANES),
                   pl.BlockSpec((8, tm), lambda i: (0, i)), vec(LANES)],
        out_shape=[
            jax.ShapeDtypeStruct((N * ROW_TILE, LANES), F32),
            jax.ShapeDtypeStruct((8, N), F32),
            jax.ShapeDtypeStruct((N, LANES), F32),
            jax.ShapeDtypeStruct((8, N), F32),
            jax.ShapeDtypeStruct((1, LANES), F32),
        ],
        scratch_shapes=[pltpu.VMEM((1, LANES), F32)],
        compiler_params=_cparams(1),
        name="outproj_router",
    )(x2, yatt, yml, zb2, zb2, eg, eb, wa, wm, wo, g1, b1, wr, br)


def _row_copies(src_row, dst_row, sem, n, issue):
    def body(r, carry):
        for kk in range(TOP_K):
            cp = pltpu.make_async_copy(src_row(r, kk), dst_row(r, kk), sem)
            if issue:
                cp.start(priority=kk % 2)
            else:
                cp.wait()
        return carry
    lax.fori_loop(0, n, body, 0, unroll=8)


def _dispatch_kernel(pad_pos_ref, pad_len_ref, d_ref, h_ref, xs_ref, zero_sc, sem, *, tm, max_pad):
    @pl.when(pl.program_id(0) == 0)
    def _():
        zero_sc[...] = jnp.zeros_like(zero_sc)
        bits = [1 << b for b in reversed(range(max_pad.bit_length()))]

        def pad_copies(e, issue):
            pos = pad_pos_ref[e]
            length = pad_len_ref[e]
            for bit in bits:
                take = (length & bit) != 0

                @pl.when(take)
                def _(pos=pos, bit=bit):
                    dst = xs_ref.at[pl.ds(pl.multiple_of(pos * ROW_TILE, ROW_TILE), bit * ROW_TILE)]
                    cp = pltpu.make_async_copy(zero_sc.at[pl.ds(0, bit * ROW_TILE)], dst, sem.at[1])
                    if issue:
                        cp.start()
                    else:
                        cp.wait()
                pos = pos + jnp.where(take, bit, 0)

        def per_expert(e, carry):
            pad_copies(e, True)
            pad_copies(e, False)
            return carry

        lax.fori_loop(0, N_EXPERTS, per_expert, 0)

    src = lambda r, kk: _row_tile(h_ref, r)
    dst = lambda r, kk: _row_tile(xs_ref, d_ref[0, 0, kk * tm + r])
    _row_copies(src, dst, sem.at[0], tm, True)
    _row_copies(src, dst, sem.at[0], tm, False)


def _dispatch(pad_pos, pad_len, dest3, h1t, rows_out, *, tm, max_pad):
    N = h1t.shape[0] // ROW_TILE
    zero_rows = 1 << (max_pad.bit_length() - 1)
    return pl.pallas_call(
        functools.partial(_dispatch_kernel, tm=tm, max_pad=max_pad),
        grid_spec=pltpu.PrefetchScalarGridSpec(
            num_scalar_prefetch=2,
            grid=(N // tm,),
            in_specs=[
                pl.BlockSpec((1, 1, tm * TOP_K), lambda i, pp, pn: (i, 0, 0), memory_space=pltpu.SMEM),
                pl.BlockSpec((tm * ROW_TILE, LANES), lambda i, pp, pn: (i, 0)),
            ],
            out_specs=pl.BlockSpec(memory_space=pl.ANY),
            scratch_shapes=[pltpu.VMEM((zero_rows * ROW_TILE, LANES), F32), pltpu.SemaphoreType.DMA((2,))],
        ),
        out_shape=jax.ShapeDtypeStruct((rows_out * ROW_TILE, LANES), F32),
        compiler_params=_cparams(1),
        name="moe_dispatch",
    )(pad_pos, pad_len, dest3, h1t)


def _select_matrix():
    j = np.arange(2 * LANES)
    sel = np.zeros((2 * LANES, 2 * LANES), np.float32)
    sel[j, (j % 2) * LANES + j // 2] = 1.0
    return jnp.asarray(sel, BF16)


def _expert_kernel(be_ref, nu_ref, nx_ref, so_ref, x_ref, bg_ref, bl_ref, bd_ref, sel_ref, wgu_hbm, wd_hbm, y_ref,
                   wgu_buf, wdn_buf, wg_sc, wl_sc, wd_sc, sem, *, tm):
    i = pl.program_id(0)
    e = be_ref[i]
    slot = so_ref[e]
    active = i < nu_ref[0]
    new_expert = jnp.logical_or(i == 0, e != be_ref[jnp.maximum(i - 1, 0)])

    def weight_copies(ex, sl):
        return (pltpu.make_async_copy(wgu_hbm.at[ex], wgu_buf.at[sl], sem.at[0, sl]),
                pltpu.make_async_copy(wd_hbm.at[ex], wdn_buf.at[sl], sem.at[1, sl]))

    @pl.when(i == 0)
    def _():
        for cp in weight_copies(e, slot):
            cp.start()

    @pl.when(jnp.logical_and(active, new_expert))
    def _():
        for cp in weight_copies(e, slot):
            cp.wait()
        nxt = nx_ref[e]

        @pl.when(nxt < N_EXPERTS)
        def _():
            for cp in weight_copies(nxt, 1 - slot):
                cp.start()

        for c in range(wgu_buf.shape[2] // (2 * LANES)):
            blk = wgu_buf[slot, :, c * 2 * LANES:(c + 1) * 2 * LANES].astype(BF16)
            r = jnp.dot(blk, sel_ref[...], preferred_element_type=F32).astype(BF16)
            wg_sc[:, c * LANES:(c + 1) * LANES] = r[:, :LANES]
            wl_sc[:, c * LANES:(c + 1) * LANES] = r[:, LANES:]
        wd_sc[...] = wdn_buf[slot].astype(BF16)

    @pl.when(active)
    def _():
        xb = _load_row_tiles(x_ref, tm).astype(BF16)
        hg = jnp.dot(xb, wg_sc[...], preferred_element_type=F32) + bg_ref[0]
        hl = jnp.dot(xb, wl_sc[...], preferred_element_type=F32) + bl_ref[0]
        glu = jnp.minimum(hg, SWIGLU_LIMIT)
        lin = jnp.clip(hl, -SWIGLU_LIMIT, SWIGLU_LIMIT)
        act = glu * _sigmoid(SWIGLU_ALPHA * glu) * (lin + 1.0)
        _store_row_tiles(y_ref, jnp.dot(act.astype(BF16), wd_sc[...], preferred_element_type=F32) + bd_ref[0])


def _experts(blk_exp, n_used, next_exp, slot_of_exp, xs, wgu, bg, bl, wd, bd, *, tm):
    MP = xs.shape[0] // ROW_TILE
    _, D, F2 = wgu.shape
    F = F2 // 2
    rows = pl.BlockSpec((tm * ROW_TILE, LANES), lambda i, be, nu, nx, so: (jnp.minimum(i, nu[0] - 1), 0))
    wsp = lambda a, b: pl.BlockSpec((1, a, b), lambda i, be, nu, nx, so: (be[i], 0, 0))
    return pl.pallas_call(
        functools.partial(_expert_kernel, tm=tm),
        grid_spec=pltpu.PrefetchScalarGridSpec(
            num_scalar_prefetch=4,
            grid=(MP // tm,),
            in_specs=[rows, wsp(1, F), wsp(1, F), wsp(1, D),
                      pl.BlockSpec((2 * LANES, 2 * LANES), lambda i, be, nu, nx, so: (0, 0)),
                      pl.BlockSpec(memory_space=pl.ANY), pl.BlockSpec(memory_space=pl.ANY)],
            out_specs=rows,
            scratch_shapes=[pltpu.VMEM((2, D, F2), F32), pltpu.VMEM((2, F, D), F32),
                            pltpu.VMEM((D, F), BF16), pltpu.VMEM((D, F), BF16), pltpu.VMEM((F, D), BF16),
                            pltpu.SemaphoreType.DMA((2, 2))],
        ),
        out_shape=jax.ShapeDtypeStruct(xs.shape, F32),
        compiler_params=_cparams(1),
        name="moe_experts",
    )(blk_exp, n_used, next_exp, slot_of_exp, xs, bg, bl, bd, _select_matrix(), wgu, wd)


def _combine_kernel(dcur_ref, dnxt_ref, h1_ref, gate_ref, g2_ref, b2_ref, ys_ref, o_ref, ybuf, sem, *, tm, nsteps):
    i = pl.program_id(0)
    slot = i % 2

    def copies(d_ref, s, issue):
        src = lambda r, kk: _row_tile(ys_ref, d_ref[0, 0, kk * tm + r])
        dst = lambda r, kk: _row_tile(ybuf.at[s, kk], r)
        _row_copies(src, dst, sem.at[s], tm, issue)

    @pl.when(i == 0)
    def _():
        copies(dcur_ref, 0, True)

    @pl.when(i + 1 < nsteps)
    def _():
        copies(dnxt_ref, 1 - slot, True)

    copies(dcur_ref, slot, False)
    acc = DN_ALPHA * _load_row_tiles(h1_ref, tm)
    for kk in range(TOP_K):
        acc = acc + gate_ref[:, kk:kk + 1] * _load_row_tiles(ybuf.at[slot, kk], tm)
    o_ref[...] = _ln(acc, g2_ref[...], b2_ref[...])


def _combine(dest3, h1t, gate, g2, b2, ys, *, tm):
    N, D = h1t.shape[0] // ROW_TILE, D_MODEL
    nsteps = N // tm
    dspec = lambda f: pl.BlockSpec((1, 1, tm * TOP_K), f, memory_space=pltpu.SMEM)
    return pl.pallas_call(
        functools.partial(_combine_kernel, tm=tm, nsteps=nsteps),
        grid=(nsteps,),
        in_specs=[
            dspec(lambda i: (i, 0, 0)),
            dspec(lambda i: (jnp.minimum(i + 1, nsteps - 1), 0, 0)),
            pl.BlockSpec((tm * ROW_TILE, LANES), lambda i: (i, 0)),
            pl.BlockSpec((tm, LANES), lambda i: (i, 0)),
            pl.BlockSpec((1, D), lambda i: (0, 0)),
            pl.BlockSpec((1, D), lambda i: (0, 0)),
            pl.BlockSpec(memory_space=pl.ANY),
        ],
        out_specs=pl.BlockSpec((tm, D), lambda i: (i, 0)),
        out_shape=jax.ShapeDtypeStruct((N, D), F32),
        scratch_shapes=[pltpu.VMEM((2, TOP_K, tm * ROW_TILE, LANES), F32), pltpu.SemaphoreType.DMA((2,))],
        compiler_params=_cparams(1),
        name="moe_combine",
    )(dest3, dest3, h1t, gate, g2, b2, ys)


def _rotary_column_order():
    idx = np.empty((ATT_HEADS * LANES,), np.int32)
    for h in range(ATT_HEADS):
        for half in range(2):
            for sub in range(2):
                for dd in range(32):
                    idx[h * LANES + half * 64 + sub * 32 + dd] = (2 * h + sub) * ATT_QK_DIM + half * 32 + dd
    return idx


def _rotary_tables(pos):
    half = ATT_QK_DIM // 2
    inv_freq = ROPE_THETA ** (-jnp.arange(half, dtype=F32) / half)
    ang = pos.astype(F32)[:, None] * inv_freq[None, :]
    cos = jnp.tile(jnp.cos(ang), (1, 4))
    sin = jnp.tile(jnp.sin(ang), (1, 4))
    sign = jnp.where(jnp.arange(LANES) < 64, -1.0, 1.0).astype(F32)
    return cos, sin * sign[None, :]


def _pick(n, prefs):
    for t in prefs:
        if n % t == 0:
            return t
    raise ValueError(f"unsupported size {n}")


def kernel(x, meta, emb_ln_g, emb_ln_b, w_in, conv_w, conv_b, gate_bias, lam_q1, lam_k1, lam_q2, lam_k2,
           att_norm_g, ml_norm_g, w_att_out, w_ml_out, w_o, ln1_g, ln1_b, w_router, b_router,
           w_gu, b_gu, w_down, b_down, ln2_g, ln2_b):
    B, S, D = x.shape
    assert D == D_MODEL and S % 512 == 0 and w_in.shape[0] == DEPTH
    N = B * S
    row2 = lambda a: a.reshape(1, -1)

    w = w_in[0]
    o_aq, o_ak, o_av, o_mqk, o_mv, o_mo, o_gt, o_ga, o_gm = 0, 512, 1024, 1536, 2560, 3072, 3584, 3592, 4616
    perm = _rotary_column_order()
    gpad = jnp.zeros((D, LANES - ML_HEADS), F32)
    w_all = jnp.concatenate([
        w[:, o_ga:o_ga + D], w[:, o_gm:o_gm + D], w[:, o_mqk:o_mqk + 2 * ML_WIDTH],
        w[:, o_aq:o_aq + 512][:, perm], w[:, o_ak:o_ak + 512][:, perm], w[:, o_av:o_av + 512],
        w[:, o_mv:o_mv + ML_WIDTH], w[:, o_mo:o_mo + ML_WIDTH],
        w[:, o_gt:o_gt + ML_HEADS], gpad, w[:, o_gt + ML_HEADS:o_gt + 2 * ML_HEADS], gpad,
    ], axis=1).astype(BF16)
    gb = gate_bias[0]
    zpad = jnp.zeros((LANES - ML_HEADS,), F32)
    gate_b = jnp.concatenate([gb[:ML_HEADS], zpad, gb[ML_HEADS:], zpad]).reshape(1, GATE_W)

    cos_x, sin_x = _rotary_tables(N_META + jnp.arange(S))
    cos_m, sin_m = _rotary_tables(jnp.maximum(jnp.arange(BLOCK) - PAD, 0))

    eg, eb = row2(emb_ln_g), row2(emb_ln_b)

    tm_in = 512
    zb, gates = _inproj(x, eg, eb, w_all, cos_x, sin_x, tm=tm_in)
    xm = jnp.concatenate([jnp.zeros((PAD, D), x.dtype), meta.astype(x.dtype)], axis=0)[None]
    zbm, gatesm = _inproj(xm, eg, eb, w_all, cos_m, sin_m, tm=BLOCK, first_valid_row=PAD)

    tq = 512
    yatt = _attention(zb, zbm, row2(lam_q1[0]), row2(lam_k1[0]), row2(lam_q2[0]), row2(lam_k2[0]),
                      row2(att_norm_g[0]), tq=tq, nh=1)
    yml = _mlstm(zb, gates, zbm, gatesm, conv_w[0], row2(conv_b[0]), gate_b, row2(ml_norm_g[0]),
                 nbps=4 if B % 4 == 0 else (2 if B % 2 == 0 else 1))

    wr32 = jnp.concatenate([w_router[0], jnp.zeros((D, LANES - N_EXPERTS), F32)], axis=1)
    wr_hi = wr32.astype(BF16)
    wr = jnp.concatenate([wr_hi, (wr32 - wr_hi.astype(F32)).astype(BF16)], axis=1)
    br = jnp.concatenate([b_router[0], jnp.full((LANES - N_EXPERTS,), NEG, F32)]).reshape(1, LANES)
    tm_out = 1024
    h1, eid, gate, rank, cnt = _outproj(
        x.reshape(N, D), yatt.reshape(N, -1), yml.reshape(N, -1), zb.reshape(N, ZB_W), eg, eb,
        w_att_out[0].astype(BF16), w_ml_out[0].astype(BF16), w_o[0].astype(BF16),
        row2(ln1_g[0]), row2(ln1_b[0]), wr, br, tm=tm_out, ts=512)

    tm_e = 512
    M = N * TOP_K
    nb = (M + N_EXPERTS * (tm_e - 1) + tm_e - 1) // tm_e
    counts = cnt[0, :N_EXPERTS].astype(jnp.int32)
    nblk = (counts + tm_e - 1) // tm_e
    cum = jnp.cumsum(nblk)
    pstart = (cum - nblk) * tm_e
    eid_t = eid[:TOP_K].astype(jnp.int32)
    base = jnp.sum(jnp.where(eid_t[None] == jnp.arange(N_EXPERTS)[:, None, None], pstart[:, None, None], 0), axis=0)
    dest = base + rank[:TOP_K].astype(jnp.int32)
    n_used = cum[-1:].astype(jnp.int32)
    blk = jnp.minimum(jnp.arange(nb, dtype=jnp.int32), n_used[0] - 1)
    blk_exp = jnp.sum(blk[:, None] >= cum[None, :], axis=1).astype(jnp.int32)

    def dest_tiles(tm):
        return dest.reshape(TOP_K, N // tm, tm).transpose(1, 0, 2).reshape(N // tm, 1, TOP_K * tm)

    tm_d, tm_c = _pick(N, (1024, 512, 256)), 256
    xs = _dispatch(pstart + counts, nblk * tm_e - counts, dest_tiles(tm_d), h1, nb * tm_e, tm=tm_d,
                   max_pad=tm_e - 1)

    has = nblk > 0
    ids = jnp.where(has, jnp.arange(N_EXPERTS, dtype=jnp.int32), N_EXPERTS)
    after = jnp.concatenate([ids[1:], jnp.full((1,), N_EXPERTS, jnp.int32)])
    next_exp = lax.cummin(after, axis=0, reverse=True).astype(jnp.int32)
    slot_of_exp = ((jnp.cumsum(has.astype(jnp.int32)) - 1) & 1).astype(jnp.int32)
    bgu = b_gu[0]
    ys = _experts(blk_exp, n_used, next_exp, slot_of_exp, xs, w_gu[0], bgu[:, None, 0::2], bgu[:, None, 1::2],
                  w_down[0], b_down[0][:, None, :], tm=tm_e)

    out = _combine(dest_tiles(tm_c), h1, gate, row2(ln2_g[0]), row2(ln2_b[0]), ys, tm=tm_c)
    return out.reshape(B, S, D)
```

```python
import functools
import math

import jax
import jax.numpy as jnp
import numpy as np
from jax import lax
from jax.experimental import pallas as pl
from jax.experimental.pallas import tpu as pltpu

D_MODEL = 1024
N_META = 16
BLOCK = 128
PAD = BLOCK - N_META
NEG = -1e30
LN_EPS = 1e-5
ATT_HEADS = 4
ATT_QK_DIM = 64
ATT_V_DIM = 128
ROPE_THETA = 10000.0
ML_HEADS = 4
ML_DH = 128
ML_WIDTH = ML_HEADS * ML_DH
CONV_K = 4
N_EXPERTS = 32
TOP_K = 4
SWIGLU_LIMIT = 7.0
SWIGLU_ALPHA = 1.702
DEPTH = 1
DN_ALPHA = (2 * DEPTH) ** 0.25
LAMBDA_INIT = 0.8 - 0.6 * math.exp(-0.3 * 0)

LANES = 128
C_GA, C_GM, C_MQK, C_AQ, C_AK, C_AV, C_MV, C_MO = 0, 1024, 2048, 3072, 3584, 4096, 4608, 5120
ZB_W = 5632
GATE_W = 2 * LANES
W_ALL = ZB_W + GATE_W
CHUNK = 512

VMEM_LIMIT = 56 * 1024 * 1024

BF16 = jnp.bfloat16
F32 = jnp.float32


def _cparams(n_axes):
    return pltpu.CompilerParams(dimension_semantics=("arbitrary",) * n_axes, vmem_limit_bytes=VMEM_LIMIT)


def _ln(x, g, b):
    mu = jnp.mean(x, axis=-1, keepdims=True)
    xc = x - mu
    var = jnp.mean(xc * xc, axis=-1, keepdims=True)
    return xc * lax.rsqrt(var + LN_EPS) * g + b


def _sigmoid(x):
    return 1.0 / (1.0 + jnp.exp(-x))


ROW_TILE = D_MODEL // LANES


def _store_row_tiles(ref, val):
    tm = val.shape[0]
    for g in range(ROW_TILE):
        ref[pl.ds(g, tm, stride=ROW_TILE), :] = val[:, g * LANES:(g + 1) * LANES]


def _load_row_tiles(ref, tm):
    return jnp.concatenate([ref[pl.ds(g, tm, stride=ROW_TILE), :] for g in range(ROW_TILE)], axis=1)


def _row_tile(ref, r):
    return ref.at[pl.ds(pl.multiple_of(r * ROW_TILE, ROW_TILE), ROW_TILE)]


def _inproj_kernel(x_ref, g_ref, b_ref, w_ref, cos_ref, sin_ref, zb_ref, gt_ref, *, first_valid_row):
    x = x_ref[0]
    tm = x.shape[0]
    hb = _ln(x, g_ref[...], b_ref[...]).astype(BF16)
    cos = cos_ref[...]
    sin = sin_ref[...]
    if first_valid_row:
        rowmask = lax.broadcasted_iota(jnp.int32, (tm, 1), 0) >= first_valid_row
    for c in range(ZB_W // CHUNK):
        z = jnp.dot(hb, w_ref[:, c * CHUNK:(c + 1) * CHUNK], preferred_element_type=F32)
        if c * CHUNK in (C_AQ, C_AK):
            parts = []
            for h in range(ATT_HEADS):
                zh = z[:, h * LANES:(h + 1) * LANES]
                parts.append(zh * cos + pltpu.roll(zh, 64, 1) * sin)
            z = jnp.concatenate(parts, axis=1)
            if c * CHUNK == C_AQ:
                z = z * (ATT_QK_DIM ** -0.5 * math.log2(math.e))
        if first_valid_row:
            z = jnp.where(rowmask, z, 0.0)
        zb_ref[0, :, c * CHUNK:(c + 1) * CHUNK] = z.astype(BF16)
    zg = jnp.dot(hb, w_ref[:, ZB_W:W_ALL], preferred_element_type=F32)
    if first_valid_row:
        zg = jnp.where(rowmask, zg, 0.0)
    gt_ref[0] = zg


def _inproj(x3, g, b, w, cos, sin, *, tm, first_valid_row=0):
    B, S, D = x3.shape
    nt = S // tm
    return pl.pallas_call(
        functools.partial(_inproj_kernel, first_valid_row=first_valid_row),
        grid=(B, nt),
        in_specs=[
            pl.BlockSpec((1, tm, D), lambda bi, i: (bi, i, 0)),
            pl.BlockSpec((1, D), lambda bi, i: (0, 0)),
            pl.BlockSpec((1, D), lambda bi, i: (0, 0)),
            pl.BlockSpec((D, W_ALL), lambda bi, i: (0, 0)),
            pl.BlockSpec((tm, LANES), lambda bi, i: (i, 0)),
            pl.BlockSpec((tm, LANES), lambda bi, i: (i, 0)),
        ],
        out_specs=[
            pl.BlockSpec((1, tm, ZB_W), lambda bi, i: (bi, i, 0)),
            pl.BlockSpec((1, tm, GATE_W), lambda bi, i: (bi, i, 0)),
        ],
        out_shape=[
            jax.ShapeDtypeStruct((B, S, ZB_W), BF16),
            jax.ShapeDtypeStruct((B, S, GATE_W), F32),
        ],
        compiler_params=_cparams(2),
        name="inproj",
    )(x3, g, b, w, cos, sin)


def _attn_kernel(q_ref, k_ref, v_ref, km_ref, vm_ref, lq1_ref, lk1_ref, lq2_ref, lk2_ref, ng_ref, o_ref,
                 sf_sc, sa_sc, sb_sc, pa_sc, pb_sc, aa_sc, ab_sc, m_sc, acc_sc, *, tq, tk, rc, nh):
    assert tq == 2 * tk
    qi = pl.program_id(2)
    rows = 2 * tq
    heads = range(nh)
    cols = lambda hh: slice(hh * LANES, (hh + 1) * LANES)
    lane = lax.broadcasted_iota(jnp.int32, (tq, LANES), 1)
    is_map0 = (lane & 63) < 32
    qq = {}
    for hh in heads:
        q = q_ref[0, :, cols(hh)]
        zero = jnp.zeros_like(q)
        qq[hh] = jnp.concatenate([jnp.where(is_map0, q, zero), jnp.where(is_map0, zero, q)], axis=0)

    def qk_stage(hh, kblk, s_ref, col0=0):
        s_ref[hh, :, col0:col0 + kblk.shape[0]] = lax.dot_general(qq[hh], kblk, (((1,), (1,)), ((), ())),
                                                                  preferred_element_type=F32)

    def pv_stage(hh, p_ref, a_ref, vblk, first=False):
        width = vblk.shape[0]
        v1 = jnp.concatenate([vblk, jnp.ones_like(vblk)], axis=1)
        pv = jnp.dot(p_ref[hh, :, 0:width], v1, preferred_element_type=F32)
        if first:
            acc_sc[hh] = pv
        else:
            a = a_ref[hh]
            acc_sc[hh] = jnp.concatenate([a, a], axis=1) * acc_sc[hh] + pv

    def sm_stage(hh, s_ref, p_ref, a_ref, width, vis, first=False):
        nrep = width // LANES
        for r0 in range(0, rows, rc):
            kind = vis(r0)
            if kind == "none":
                continue
            s = s_ref[hh, r0:r0 + rc, 0:width]
            if kind != "all":
                s = kind(s)
            m_cur = jnp.max(s, axis=1, keepdims=True)
            if first:
                m_new = jnp.broadcast_to(m_cur, (rc, LANES))
            else:
                m_prev = m_sc[hh, r0:r0 + rc, :]
                m_new = jnp.maximum(m_prev, m_cur)
                a_ref[hh, r0:r0 + rc, :] = jnp.exp2(m_prev - m_new)
            m_sc[hh, r0:r0 + rc, :] = m_new
        for r0 in range(0, rows, rc):
            kind = vis(r0)
            if kind == "none":
                p_ref[hh, r0:r0 + rc, 0:width] = jnp.zeros((rc, width), BF16)
                a_ref[hh, r0:r0 + rc, :] = jnp.ones((rc, LANES), F32)
                continue
            s = s_ref[hh, r0:r0 + rc, 0:width]
            if kind != "all":
                s = kind(s)
            m_new = m_sc[hh, r0:r0 + rc, :]
            m_rep = m_new if nrep == 1 else jnp.concatenate([m_new] * nrep, axis=1)
            p_ref[hh, r0:r0 + rc, 0:width] = jnp.exp2(s - m_rep).astype(BF16)

    def first_vis(r0):
        q0 = r0 % tq
        def mask(s):
            rowq = q0 + lax.broadcasted_iota(jnp.int32, s.shape, 0)
            col = lax.broadcasted_iota(jnp.int32, s.shape, 1)
            keep = col >= tk + PAD
            if tk - 1 <= q0:
                keep = jnp.logical_or(keep, col < tk)
            else:
                keep = jnp.logical_or(keep, col <= rowq)
            return jnp.where(keep, s, NEG)
        return mask

    def diag_vis(d):
        def vis(r0):
            q0 = r0 % tq
            if d * tk + tk - 1 <= q0:
                return "all"
            if d * tk > q0 + rc - 1:
                return "none"
            def mask(s):
                rowq = q0 + lax.broadcasted_iota(jnp.int32, s.shape, 0)
                col = d * tk + lax.broadcasted_iota(jnp.int32, s.shape, 1)
                return jnp.where(col <= rowq, s, NEG)
            return mask
        return vis

    all_vis = lambda r0: "all"
    kblk = lambda hh, start: k_ref[0, pl.ds(start, tk), cols(hh)]
    vblk = lambda hh, start: v_ref[0, pl.ds(start, tk), cols(hh)]

    d0 = pl.multiple_of(qi * tq, tk)
    d1 = pl.multiple_of(qi * tq + tk, tk)
    for hh in heads:
        qk_stage(hh, kblk(hh, d0), sf_sc)
    for hh in heads:
        qk_stage(hh, km_ref[0, :, cols(hh)], sf_sc, col0=tk)
    for hh in heads:
        qk_stage(hh, kblk(hh, d1), sb_sc)
    for hh in heads:
        qk_stage(hh, kblk(hh, 0), sa_sc)
    for hh in heads:
        sm_stage(hh, sf_sc, pa_sc, None, tk + BLOCK, first_vis, first=True)
    for hh in heads:
        sm_stage(hh, sb_sc, pb_sc, ab_sc, tk, diag_vis(1))
    for hh in heads:
        pv_stage(hh, pa_sc, None, jnp.concatenate([vblk(hh, d0), vm_ref[0, :, cols(hh)]], axis=0), first=True)

    def pair(i, carry):
        a_start = pl.multiple_of(i * tq, tk)
        b_start = pl.multiple_of(i * tq + tk, tk)
        prev_b = pl.multiple_of(jnp.where(i == 0, d1, a_start - tk), tk)
        next_a = pl.multiple_of(jnp.minimum(a_start + tq, (qi - 1) * tq), tk)
        for hh in heads:
            qk_stage(hh, kblk(hh, b_start), sb_sc)
        for hh in heads:
            pv_stage(hh, pb_sc, ab_sc, vblk(hh, prev_b))
        for hh in heads:
            sm_stage(hh, sa_sc, pa_sc, aa_sc, tk, all_vis)
        for hh in heads:
            qk_stage(hh, kblk(hh, next_a), sa_sc)
        for hh in heads:
            pv_stage(hh, pa_sc, aa_sc, vblk(hh, a_start))
        for hh in heads:
            sm_stage(hh, sb_sc, pb_sc, ab_sc, tk, all_vis)
        return carry

    lax.fori_loop(0, qi, pair, 0)
    last_b = pl.multiple_of(jnp.where(qi == 0, d1, qi * tq - tk), tk)
    for hh in heads:
        pv_stage(hh, pb_sc, ab_sc, vblk(hh, last_b))

    s1 = jnp.sum(lq1_ref[...] * lk1_ref[...], axis=1, keepdims=True)
    s2 = jnp.sum(lq2_ref[...] * lk2_ref[...], axis=1, keepdims=True)
    lam = jnp.exp(s1) - jnp.exp(s2) + LAMBDA_INIT
    for hh in heads:
        o = acc_sc[hh, :, 0:ATT_V_DIM] / acc_sc[hh, :, ATT_V_DIM:]
        a = o[:tq] - lam * o[tq:]
        y = a * lax.rsqrt(jnp.mean(a * a, axis=1, keepdims=True) + LN_EPS) * ng_ref[...]
        o_ref[0, :, cols(hh)] = (y * (1.0 - LAMBDA_INIT)).astype(BF16)


def _attention(zb, zbm, lq1, lk1, lq2, lk2, ng, *, tq, nh):
    B, S, _ = zb.shape
    nq = S // tq
    tk = tq // 2
    w = nh * LANES
    qb, kb, vb = C_AQ // w, C_AK // w, C_AV // w
    lam_spec = pl.BlockSpec((1, ATT_QK_DIM), lambda b, h, i: (0, 0))
    return pl.pallas_call(
        functools.partial(_attn_kernel, tq=tq, tk=tk, rc=32, nh=nh),
        grid=(B, ATT_HEADS // nh, nq),
        in_specs=[
            pl.BlockSpec((1, tq, w), lambda b, h, i: (b, i, qb + h)),
            pl.BlockSpec((1, S, w), lambda b, h, i: (b, 0, kb + h)),
            pl.BlockSpec((1, S, w), lambda b, h, i: (b, 0, vb + h)),
            pl.BlockSpec((1, BLOCK, w), lambda b, h, i: (0, 0, kb + h)),
            pl.BlockSpec((1, BLOCK, w), lambda b, h, i: (0, 0, vb + h)),
            lam_spec, lam_spec, lam_spec, lam_spec,
            pl.BlockSpec((1, ATT_V_DIM), lambda b, h, i: (0, 0)),
        ],
        out_specs=pl.BlockSpec((1, tq, w), lambda b, h, i: (b, i, h)),
        out_shape=jax.ShapeDtypeStruct((B, S, ATT_HEADS * ATT_V_DIM), BF16),
        scratch_shapes=[
            pltpu.VMEM((nh, 2 * tq, tk + BLOCK), F32),
            pltpu.VMEM((nh, 2 * tq, tk), F32), pltpu.VMEM((nh, 2 * tq, tk), F32),
            pltpu.VMEM((nh, 2 * tq, tk + BLOCK), BF16), pltpu.VMEM((nh, 2 * tq, tk), BF16),
            pltpu.VMEM((nh, 2 * tq, LANES), F32), pltpu.VMEM((nh, 2 * tq, LANES), F32),
            pltpu.VMEM((nh, 2 * tq, LANES), F32),
            pltpu.VMEM((nh, 2 * tq, 2 * ATT_V_DIM), F32),
        ],
        compiler_params=_cparams(3),
        name="diff_attention",
    )(zb, zb, zb, zbm, zbm, lq1, lk1, lq2, lk2, ng)


def _mlstm_kernel(xqk_ref, xv_ref, xo_ref, xg_ref, zm_ref, gm_ref, cw_ref, cb_ref, gb_ref, ng_ref, sh_ref, o_ref,
                  ct_sc, n_sc, m_sc, prev_sc, *, nbps):
    c = pl.program_id(1)
    L = BLOCK

    @pl.when(c == 0)
    def _():
        ct_sc[...] = jnp.zeros_like(ct_sc)
        n_sc[...] = jnp.zeros_like(n_sc)
        m_sc[...] = jnp.zeros_like(m_sc)
        prev_sc[...] = jnp.zeros_like(prev_sc)

    is_meta = c == 0
    row = lax.broadcasted_iota(jnp.int32, (L, 1), 0)
    valid = jnp.logical_or(c > 0, row >= PAD)
    sidx = lax.broadcasted_iota(jnp.int32, (L, L), 0)
    ridx = lax.broadcasted_iota(jnp.int32, (L, L), 1)
    causal = ridx <= sidx
    tri = causal.astype(BF16)
    elems = range(nbps)
    pairs = [(bb, h) for bb in elems for h in range(ML_HEADS)]
    sl = lambda h: slice(h * ML_DH, (h + 1) * ML_DH)
    st = lambda bb, h: bb * 8 + h

    q, k, v, ig, bcs, b_t, ig_t = {}, {}, {}, {}, {}, {}, {}
    for bb in elems:
        qk_pre = jnp.where(is_meta, zm_ref[0, :, C_MQK:C_MQK + 2 * ML_WIDTH], xqk_ref[bb])
        v[bb] = jnp.where(is_meta, zm_ref[0, :, C_MV:C_MV + ML_WIDTH], xv_ref[bb])
        gates = jnp.where(is_meta, gm_ref[0], xg_ref[bb])
        prev = prev_sc[bb * L:(bb + 1) * L, :]
        ext = jnp.concatenate([qk_pre, prev], axis=0)
        prev_sc[bb * L:(bb + 1) * L, :] = qk_pre
        shifted = jnp.dot(sh_ref[...], ext, preferred_element_type=F32)
        acc = cb_ref[...] + cw_ref[CONV_K - 1:CONV_K, :] * qk_pre.astype(F32)
        for s in range(1, CONV_K):
            acc = acc + cw_ref[CONV_K - 1 - s:CONV_K - s, :] * shifted[(s - 1) * L:s * L, :]
        qk = acc * _sigmoid(acc)
        qk = jnp.where(valid, qk, 0.0)
        q[bb] = qk[:, :ML_WIDTH]
        k[bb] = qk[:, ML_WIDTH:] * (ML_DH ** -0.5)

        igv = gates[:, :LANES] + gb_ref[:, :LANES]
        fg = gates[:, LANES:] + gb_ref[:, LANES:]
        lf = jnp.minimum(fg, 0.0) - jnp.log1p(jnp.exp(-jnp.abs(fg)))
        igv = jnp.where(valid, igv, NEG)
        lf = jnp.where(valid, lf, 0.0)
        lf_hi = lf.astype(BF16)
        lf_lo = (lf - lf_hi.astype(F32)).astype(BF16)
        bcs[bb] = (jnp.dot(tri, lf_hi, preferred_element_type=F32)
                   + jnp.dot(tri, lf_lo, preferred_element_type=F32))
        ig[bb] = igv
        b_t[bb] = bcs[bb].T
        ig_t[bb] = igv.T

    bc, m_h, dlog, inter, m_s = {}, {}, {}, {}, {}
    for p in pairs:
        bb, h = p
        bc[p] = bcs[bb][:, h:h + 1]
        m_h[p] = m_sc[st(bb, h):st(bb, h) + 1, 0:1]
        dlog[p] = jnp.where(causal, (bc[p] - b_t[bb][h:h + 1, :]) + ig_t[bb][h:h + 1, :], NEG)
        inter[p] = bc[p] + m_h[p]
    for p in pairs:
        m_s[p] = jnp.maximum(inter[p], jnp.max(dlog[p], axis=1, keepdims=True))

    w_intra, w_inter, qb, kb, s, qc = {}, {}, {}, {}, {}, {}
    for p in pairs:
        bb, h = p
        w_intra[p] = jnp.exp(dlog[p] - m_s[p])
        w_inter[p] = jnp.exp(inter[p] - m_s[p])
        qb[p] = q[bb][:, sl(h)].astype(BF16)
        kb[p] = k[bb][:, sl(h)].astype(BF16)
    for p in pairs:
        bb, h = p
        s[p] = lax.dot_general(qb[p], kb[p], (((1,), (1,)), ((), ())), preferred_element_type=F32) * w_intra[p]
        qc[p] = jnp.dot(qb[p], ct_sc[bb * ML_HEADS + h].astype(BF16), preferred_element_type=F32)

    hh = {}
    for p in pairs:
        bb, h = p
        nrow = n_sc[st(bb, h):st(bb, h) + 1, :]
        num = w_inter[p] * qc[p] + jnp.dot(s[p].astype(BF16), v[bb][:, sl(h)], preferred_element_type=F32)
        den = (w_inter[p] * jnp.sum(q[bb][:, sl(h)] * nrow, axis=1, keepdims=True)
               + jnp.sum(s[p], axis=1, keepdims=True))
        hh[p] = num / jnp.maximum(jnp.abs(den), jnp.exp(-m_s[p]))

    for p in pairs:
        bb, h = p
        b_last = bc[p][L - 1:L, :]
        upd = (b_last - bc[p]) + ig[bb][:, h:h + 1]
        m_new = jnp.maximum(b_last + m_h[p], jnp.max(upd, axis=0, keepdims=True))
        w_old = jnp.exp(b_last + m_h[p] - m_new)
        w_r = jnp.exp(upd - m_new)
        vw = (v[bb][:, sl(h)].astype(F32) * w_r).astype(BF16)
        ci = bb * ML_HEADS + h
        ct_sc[ci] = w_old * ct_sc[ci] + lax.dot_general(kb[p], vw, (((0,), (0,)), ((), ())),
                                                         preferred_element_type=F32)
        r = st(bb, h)
        n_sc[r:r + 1, :] = w_old * n_sc[r:r + 1, :] + jnp.sum(k[bb][:, sl(h)] * w_r, axis=0, keepdims=True)
        m_sc[r:r + 1, :] = jnp.broadcast_to(m_new, (1, LANES))

    outs = {}
    for p in pairs:
        bb, h = p
        hn = hh[p] * lax.rsqrt(jnp.mean(hh[p] * hh[p], axis=1, keepdims=True) + LN_EPS) * ng_ref[:, sl(h)]
        outs[p] = _sigmoid(xo_ref[bb, :, sl(h)].astype(F32)) * hn

    @pl.when(c > 0)
    def _():
        for bb in elems:
            o_ref[bb] = jnp.concatenate([outs[(bb, h)] for h in range(ML_HEADS)], axis=1).astype(BF16)


def _mlstm(zb, gates, zbm, gatesm, conv_w, conv_b, gate_b, ng, *, nbps):
    B, S, _ = zb.shape
    nc = S // BLOCK + 1
    t = np.arange(BLOCK)
    shift = np.zeros(((CONV_K - 1) * BLOCK, 2 * BLOCK), np.float32)
    for s in range(1, CONV_K):
        shift[(s - 1) * BLOCK + t, np.where(t >= s, t - s, 2 * BLOCK + t - s)] = 1.0

    def xc(c):
        return jnp.maximum(c - 1, 0)

    return pl.pallas_call(
        functools.partial(_mlstm_kernel, nbps=nbps),
        grid=(B // nbps, nc),
        in_specs=[
            pl.BlockSpec((nbps, BLOCK, 2 * ML_WIDTH), lambda b, c: (b, xc(c), C_MQK // (2 * ML_WIDTH))),
            pl.BlockSpec((nbps, BLOCK, ML_WIDTH), lambda b, c: (b, xc(c), C_MV // ML_WIDTH)),
            pl.BlockSpec((nbps, BLOCK, ML_WIDTH), lambda b, c: (b, xc(c), C_MO // ML_WIDTH)),
            pl.BlockSpec((nbps, BLOCK, GATE_W), lambda b, c: (b, xc(c), 0)),
            pl.BlockSpec((1, BLOCK, ZB_W), lambda b, c: (0, 0, 0)),
            pl.BlockSpec((1, BLOCK, GATE_W), lambda b, c: (0, 0, 0)),
            pl.BlockSpec((CONV_K, 2 * ML_WIDTH), lambda b, c: (0, 0)),
            pl.BlockSpec((1, 2 * ML_WIDTH), lambda b, c: (0, 0)),
            pl.BlockSpec((1, GATE_W), lambda b, c: (0, 0)),
            pl.BlockSpec((1, ML_WIDTH), lambda b, c: (0, 0)),
            pl.BlockSpec(((CONV_K - 1) * BLOCK, 2 * BLOCK), lambda b, c: (0, 0)),
        ],
        out_specs=pl.BlockSpec((nbps, BLOCK, ML_WIDTH), lambda b, c: (b, xc(c), 0)),
        out_shape=jax.ShapeDtypeStruct((B, S, ML_WIDTH), BF16),
        scratch_shapes=[
            pltpu.VMEM((nbps * ML_HEADS, ML_DH, ML_DH), F32),
            pltpu.VMEM((nbps * 8, ML_DH), F32),
            pltpu.VMEM((nbps * 8, LANES), F32),
            pltpu.VMEM((nbps * BLOCK, 2 * ML_WIDTH), BF16),
        ],
        compiler_params=_cparams(2),
        name="mlstm",
    )(zb, zb, zb, gates, zbm, gatesm, conv_w, conv_b, gate_b, ng, jnp.asarray(shift, BF16))


def _outproj_kernel(x_ref, ya_ref, ym_ref, ga_ref, gmm_ref, eg_ref, eb_ref, wa_ref, wm_ref, wo_ref,
                    g1_ref, b1_ref, wr_ref, br_ref,
                    h1_ref, eid_ref, gate_ref, rank_ref, cnt_ref, run_sc, *, tm, ts):
    i = pl.program_id(0)
    subs = range(tm // ts)
    rows = lambda j: slice(j * ts, (j + 1) * ts)

    @pl.when(i == 0)
    def _():
        run_sc[...] = jnp.zeros_like(run_sc)

    sig = lambda g: 0.5 * jnp.tanh(0.5 * g) + 0.5
    pa = {j: jnp.dot(ya_ref[rows(j), :], wa_ref[...], preferred_element_type=F32) for j in subs}
    pm = {j: jnp.dot(ym_ref[rows(j), :], wm_ref[...], preferred_element_type=F32) for j in subs}
    h0 = {j: _ln(x_ref[rows(j), :], eg_ref[...], eb_ref[...]) for j in subs}
    merged = {j: sig(ga_ref[rows(j), :]) * pa[j].astype(BF16) + sig(gmm_ref[rows(j), :]) * pm[j].astype(BF16)
              for j in subs}
    mix = {j: jnp.dot(merged[j], wo_ref[...], preferred_element_type=F32) for j in subs}
    h1 = {j: _ln(DN_ALPHA * h0[j] + mix[j], g1_ref[...], b1_ref[...]) for j in subs}
    for j in subs:
        _store_row_tiles(h1_ref.at[pl.ds(j * ts * ROW_TILE, ts * ROW_TILE)], h1[j])

    logits = {}
    for j in subs:
        h_hi = h1[j].astype(BF16)
        h_mid = (h1[j] - h_hi.astype(F32)).astype(BF16)
        hh = jnp.dot(h_hi, wr_ref[...], preferred_element_type=F32)
        logits[j] = (hh[:, :LANES] + hh[:, LANES:]
                     + jnp.dot(h_mid, wr_ref[:, :LANES], preferred_element_type=F32) + br_ref[...])

    lane = lax.broadcasted_iota(jnp.int32, (ts, LANES), 1)
    r_i = lax.broadcasted_iota(jnp.int32, (ts, ts), 0)
    c_i = lax.broadcasted_iota(jnp.int32, (ts, ts), 1)
    strict = (c_i < r_i).astype(BF16)
    sel_e, ex, den, onehots, oh = {}, {}, {}, {}, {}
    for j in subs:
        work = logits[j]
        es, vs = [], []
        for _ in range(TOP_K):
            mv = jnp.max(work, axis=1, keepdims=True)
            e = jnp.min(jnp.where(work == mv, lane, LANES), axis=1, keepdims=True)
            es.append(e)
            vs.append(mv)
            work = jnp.where(lane == e, -jnp.inf, work)
        sel_e[j] = es
        ex[j] = [jnp.exp(v - vs[0]) for v in vs]
        den[j] = ex[j][0] + ex[j][1] + ex[j][2] + ex[j][3]
        onehots[j] = [lane == e for e in es]
        acc = jnp.zeros((ts, LANES), F32)
        for o in onehots[j]:
            acc = acc + o.astype(F32)
        oh[j] = acc

    run = run_sc[...]
    for j in subs:
        before = jnp.dot(strict, oh[j].astype(BF16), preferred_element_type=F32) + run
        run = run + jnp.sum(oh[j], axis=0, keepdims=True)
        eid = jnp.zeros((ts, LANES), F32)
        gate = jnp.zeros((ts, LANES), F32)
        rank = jnp.zeros((ts, LANES), F32)
        for kk in range(TOP_K):
            here = lane == kk
            eid = jnp.where(here, sel_e[j][kk].astype(F32), eid)
            gate = jnp.where(here, ex[j][kk] / den[j], gate)
            rk = jnp.sum(jnp.where(onehots[j][kk], before, 0.0), axis=1, keepdims=True)
            rank = jnp.where(here, rk, rank)
        gate_ref[rows(j), :] = gate
        eid_ref[:, rows(j)] = eid.T[0:8, :]
        rank_ref[:, rows(j)] = rank.T[0:8, :]
    run_sc[...] = run
    cnt_ref[...] = run


def _outproj(x2, yatt, yml, zb2, eg, eb, wa, wm, wo, g1, b1, wr, br, *, tm, ts):
    N, D = x2.shape
    vec = lambda w: pl.BlockSpec((1, w), lambda i: (0, 0))
    full = lambda a, b: pl.BlockSpec((a, b), lambda i: (0, 0))
    tile = lambda w: pl.BlockSpec((tm, w), lambda i: (i, 0))
    return pl.pallas_call(
        functools.partial(_outproj_kernel, tm=tm, ts=ts),
        grid=(N // tm,),
        in_specs=[
            tile(D), tile(ATT_HEADS * ATT_V_DIM), tile(ML_WIDTH),
            pl.BlockSpec((tm, D), lambda i: (i, C_GA // D)),
            pl.BlockSpec((tm, D), lambda i: (i, C_GM // D)),
            vec(D), vec(D),
            full(ATT_HEADS * ATT_V_DIM, D), full(ML_WIDTH, D), full(D, D),
            vec(D), vec(D), full(D, 2 * LANES), vec(LANES),
        ],
        out_specs=[pl.BlockSpec((tm * ROW_TILE, LANES), lambda i: (i, 0)),
                   pl.BlockSpec((8, tm), lambda i: (0, i)), tile(LANES),
                   pl.BlockSpec((8, tm), lambda i: (0, i)), vec(LANES)],
        out_shape=[
            jax.ShapeDtypeStruct((N * ROW_TILE, LANES), F32),
            jax.ShapeDtypeStruct((8, N), F32),
            jax.ShapeDtypeStruct((N, LANES), F32),
            jax.ShapeDtypeStruct((8, N), F32),
            jax.ShapeDtypeStruct((1, LANES), F32),
        ],
        scratch_shapes=[pltpu.VMEM((1, LANES), F32)],
        compiler_params=_cparams(1),
        name="outproj_router",
    )(x2, yatt, yml, zb2, zb2, eg, eb, wa, wm, wo, g1, b1, wr, br)


def _row_copies(src_row, dst_row, sem, n, issue):
    def body(r, carry):
        for kk in range(TOP_K):
            cp = pltpu.make_async_copy(src_row(r, kk), dst_row(r, kk), sem)
            if issue:
                cp.start(priority=kk % 2)
            else:
                cp.wait()
        return carry
    lax.fori_loop(0, n, body, 0, unroll=8)


def _dispatch_kernel(pad_pos_ref, pad_len_ref, d_ref, h_ref, xs_ref, zero_sc, sem, *, tm, max_pad):
    @pl.when(pl.program_id(0) == 0)
    def _():
        zero_sc[...] = jnp.zeros_like(zero_sc)
        bits = [1 << b for b in reversed(range(max_pad.bit_length()))]

        def pad_copies(e, issue):
            pos = pad_pos_ref[e]
            length = pad_len_ref[e]
            for bit in bits:
                take = (length & bit) != 0

                @pl.when(take)
                def _(pos=pos, bit=bit):
                    dst = xs_ref.at[pl.ds(pl.multiple_of(pos * ROW_TILE, ROW_TILE), bit * ROW_TILE)]
                    cp = pltpu.make_async_copy(zero_sc.at[pl.ds(0, bit * ROW_TILE)], dst, sem.at[1])
                    if issue:
                        cp.start()
                    else:
                        cp.wait()
                pos = pos + jnp.where(take, bit, 0)

        def per_expert(e, carry):
            pad_copies(e, True)
            pad_copies(e, False)
            return carry

        lax.fori_loop(0, N_EXPERTS, per_expert, 0)

    src = lambda r, kk: _row_tile(h_ref, r)
    dst = lambda r, kk: _row_tile(xs_ref, d_ref[0, 0, kk * tm + r])
    _row_copies(src, dst, sem.at[0], tm, True)
    _row_copies(src, dst, sem.at[0], tm, False)


def _dispatch(pad_pos, pad_len, dest3, h1t, rows_out, *, tm, max_pad):
    N = h1t.shape[0] // ROW_TILE
    zero_rows = 1 << (max_pad.bit_length() - 1)
    return pl.pallas_call(
        functools.partial(_dispatch_kernel, tm=tm, max_pad=max_pad),
        grid_spec=pltpu.PrefetchScalarGridSpec(
            num_scalar_prefetch=2,
            grid=(N // tm,),
            in_specs=[
                pl.BlockSpec((1, 1, tm * TOP_K), lambda i, pp, pn: (i, 0, 0), memory_space=pltpu.SMEM),
                pl.BlockSpec((tm * ROW_TILE, LANES), lambda i, pp, pn: (i, 0)),
            ],
            out_specs=pl.BlockSpec(memory_space=pl.ANY),
            scratch_shapes=[pltpu.VMEM((zero_rows * ROW_TILE, LANES), F32), pltpu.SemaphoreType.DMA((2,))],
        ),
        out_shape=jax.ShapeDtypeStruct((rows_out * ROW_TILE, LANES), F32),
        compiler_params=_cparams(1),
        name="moe_dispatch",
    )(pad_pos, pad_len, dest3, h1t)


def _select_matrix():
    j = np.arange(2 * LANES)
    sel = np.zeros((2 * LANES, 2 * LANES), np.float32)
    sel[j, (j % 2) * LANES + j // 2] = 1.0
    return jnp.asarray(sel, BF16)


def _expert_kernel(be_ref, nu_ref, nx_ref, so_ref, x_ref, bg_ref, bl_ref, bd_ref, sel_ref, wgu_hbm, wd_hbm, y_ref,
                   wgu_buf, wdn_buf, wg_sc, wl_sc, wd_sc, sem, *, tm):
    i = pl.program_id(0)
    e = be_ref[i]
    slot = so_ref[e]
    active = i < nu_ref[0]
    new_expert = jnp.logical_or(i == 0, e != be_ref[jnp.maximum(i - 1, 0)])

    def weight_copies(ex, sl):
        return (pltpu.make_async_copy(wgu_hbm.at[ex], wgu_buf.at[sl], sem.at[0, sl]),
                pltpu.make_async_copy(wd_hbm.at[ex], wdn_buf.at[sl], sem.at[1, sl]))

    @pl.when(i == 0)
    def _():
        for cp in weight_copies(e, slot):
            cp.start()

    @pl.when(jnp.logical_and(active, new_expert))
    def _():
        for cp in weight_copies(e, slot):
            cp.wait()
        nxt = nx_ref[e]

        @pl.when(nxt < N_EXPERTS)
        def _():
            for cp in weight_copies(nxt, 1 - slot):
                cp.start()

        for c in range(wgu_buf.shape[2] // (2 * LANES)):
            blk = wgu_buf[slot, :, c * 2 * LANES:(c + 1) * 2 * LANES].astype(BF16)
            r = jnp.dot(blk, sel_ref[...], preferred_element_type=F32).astype(BF16)
            wg_sc[:, c * LANES:(c + 1) * LANES] = r[:, :LANES]
            wl_sc[:, c * LANES:(c + 1) * LANES] = r[:, LANES:]
        wd_sc[...] = wdn_buf[slot].astype(BF16)

    @pl.when(active)
    def _():
        xb = _load_row_tiles(x_ref, tm).astype(BF16)
        hg = jnp.dot(xb, wg_sc[...], preferred_element_type=F32) + bg_ref[0]
        hl = jnp.dot(xb, wl_sc[...], preferred_element_type=F32) + bl_ref[0]
        glu = jnp.minimum(hg, SWIGLU_LIMIT)
        lin = jnp.clip(hl, -SWIGLU_LIMIT, SWIGLU_LIMIT)
        act = glu * _sigmoid(SWIGLU_ALPHA * glu) * (lin + 1.0)
        _store_row_tiles(y_ref, jnp.dot(act.astype(BF16), wd_sc[...], preferred_element_type=F32) + bd_ref[0])


def _experts(blk_exp, n_used, next_exp, slot_of_exp, xs, wgu, bg, bl, wd, bd, *, tm):
    MP = xs.shape[0] // ROW_TILE
    _, D, F2 = wgu.shape
    F = F2 // 2
    rows = pl.BlockSpec((tm * ROW_TILE, LANES), lambda i, be, nu, nx, so: (jnp.minimum(i, nu[0] - 1), 0))
    wsp = lambda a, b: pl.BlockSpec((1, a, b), lambda i, be, nu, nx, so: (be[i], 0, 0))
    return pl.pallas_call(
        functools.partial(_expert_kernel, tm=tm),
        grid_spec=pltpu.PrefetchScalarGridSpec(
            num_scalar_prefetch=4,
            grid=(MP // tm,),
            in_specs=[rows, wsp(1, F), wsp(1, F), wsp(1, D),
                      pl.BlockSpec((2 * LANES, 2 * LANES), lambda i, be, nu, nx, so: (0, 0)),
                      pl.BlockSpec(memory_space=pl.ANY), pl.BlockSpec(memory_space=pl.ANY)],
            out_specs=rows,
            scratch_shapes=[pltpu.VMEM((2, D, F2), F32), pltpu.VMEM((2, F, D), F32),
                            pltpu.VMEM((D, F), BF16), pltpu.VMEM((D, F), BF16), pltpu.VMEM((F, D), BF16),
                            pltpu.SemaphoreType.DMA((2, 2))],
        ),
        out_shape=jax.ShapeDtypeStruct(xs.shape, F32),
        compiler_params=_cparams(1),
        name="moe_experts",
    )(blk_exp, n_used, next_exp, slot_of_exp, xs, bg, bl, bd, _select_matrix(), wgu, wd)


def _combine_kernel(dcur_ref, dnxt_ref, h1_ref, gate_ref, g2_ref, b2_ref, ys_ref, o_ref, ybuf, sem, *, tm, nsteps):
    i = pl.program_id(0)
    slot = i % 2

    def copies(d_ref, s, issue):
        src = lambda r, kk: _row_tile(ys_ref, d_ref[0, 0, kk * tm + r])
        dst = lambda r, kk: _row_tile(ybuf.at[s, kk], r)
        _row_copies(src, dst, sem.at[s], tm, issue)

    @pl.when(i == 0)
    def _():
        copies(dcur_ref, 0, True)

    @pl.when(i + 1 < nsteps)
    def _():
        copies(dnxt_ref, 1 - slot, True)

    copies(dcur_ref, slot, False)
    acc = DN_ALPHA * _load_row_tiles(h1_ref, tm)
    for kk in range(TOP_K):
        acc = acc + gate_ref[:, kk:kk + 1] * _load_row_tiles(ybuf.at[slot, kk], tm)
    o_ref[...] = _ln(acc, g2_ref[...], b2_ref[...])


def _combine(dest3, h1t, gate, g2, b2, ys, *, tm):
    N, D = h1t.shape[0] // ROW_TILE, D_MODEL
    nsteps = N // tm
    dspec = lambda f: pl.BlockSpec((1, 1, tm * TOP_K), f, memory_space=pltpu.SMEM)
    return pl.pallas_call(
        functools.partial(_combine_kernel, tm=tm, nsteps=nsteps),
        grid=(nsteps,),
        in_specs=[
            dspec(lambda i: (i, 0, 0)),
            dspec(lambda i: (jnp.minimum(i + 1, nsteps - 1), 0, 0)),
            pl.BlockSpec((tm * ROW_TILE, LANES), lambda i: (i, 0)),
            pl.BlockSpec((tm, LANES), lambda i: (i, 0)),
            pl.BlockSpec((1, D), lambda i: (0, 0)),
            pl.BlockSpec((1, D), lambda i: (0, 0)),
            pl.BlockSpec(memory_space=pl.ANY),
        ],
        out_specs=pl.BlockSpec((tm, D), lambda i: (i, 0)),
        out_shape=jax.ShapeDtypeStruct((N, D), F32),
        scratch_shapes=[pltpu.VMEM((2, TOP_K, tm * ROW_TILE, LANES), F32), pltpu.SemaphoreType.DMA((2,))],
        compiler_params=_cparams(1),
        name="moe_combine",
    )(dest3, dest3, h1t, gate, g2, b2, ys)


def _rotary_column_order():
    idx = np.empty((ATT_HEADS * LANES,), np.int32)
    for h in range(ATT_HEADS):
        for half in range(2):
            for sub in range(2):
                for dd in range(32):
                    idx[h * LANES + half * 64 + sub * 32 + dd] = (2 * h + sub) * ATT_QK_DIM + half * 32 + dd
    return idx


def _rotary_tables(pos):
    half = ATT_QK_DIM // 2
    inv_freq = ROPE_THETA ** (-jnp.arange(half, dtype=F32) / half)
    ang = pos.astype(F32)[:, None] * inv_freq[None, :]
    cos = jnp.tile(jnp.cos(ang), (1, 4))
    sin = jnp.tile(jnp.sin(ang), (1, 4))
    sign = jnp.where(jnp.arange(LANES) < 64, -1.0, 1.0).astype(F32)
    return cos, sin * sign[None, :]


def _pick(n, prefs):
    for t in prefs:
        if n % t == 0:
            return t
    raise ValueError(f"unsupported size {n}")


def kernel(x, meta, emb_ln_g, emb_ln_b, w_in, conv_w, conv_b, gate_bias, lam_q1, lam_k1, lam_q2, lam_k2,
           att_norm_g, ml_norm_g, w_att_out, w_ml_out, w_o, ln1_g, ln1_b, w_router, b_router,
           w_gu, b_gu, w_down, b_down, ln2_g, ln2_b):
    B, S, D = x.shape
    assert D == D_MODEL and S % 512 == 0 and w_in.shape[0] == DEPTH
    N = B * S
    row2 = lambda a: a.reshape(1, -1)

    w = w_in[0]
    o_aq, o_ak, o_av, o_mqk, o_mv, o_mo, o_gt, o_ga, o_gm = 0, 512, 1024, 1536, 2560, 3072, 3584, 3592, 4616
    perm = _rotary_column_order()
    gpad = jnp.zeros((D, LANES - ML_HEADS), F32)
    w_all = jnp.concatenate([
        w[:, o_ga:o_ga + D], w[:, o_gm:o_gm + D], w[:, o_mqk:o_mqk + 2 * ML_WIDTH],
        w[:, o_aq:o_aq + 512][:, perm], w[:, o_ak:o_ak + 512][:, perm], w[:, o_av:o_av + 512],
        w[:, o_mv:o_mv + ML_WIDTH], w[:, o_mo:o_mo + ML_WIDTH],
        w[:, o_gt:o_gt + ML_HEADS], gpad, w[:, o_gt + ML_HEADS:o_gt + 2 * ML_HEADS], gpad,
    ], axis=1).astype(BF16)
    gb = gate_bias[0]
    zpad = jnp.zeros((LANES - ML_HEADS,), F32)
    gate_b = jnp.concatenate([gb[:ML_HEADS], zpad, gb[ML_HEADS:], zpad]).reshape(1, GATE_W)

    cos_x, sin_x = _rotary_tables(N_META + jnp.arange(S))
    cos_m, sin_m = _rotary_tables(jnp.maximum(jnp.arange(BLOCK) - PAD, 0))

    eg, eb = row2(emb_ln_g), row2(emb_ln_b)

    tm_in = 512
    zb, gates = _inproj(x, eg, eb, w_all, cos_x, sin_x, tm=tm_in)
    xm = jnp.concatenate([jnp.zeros((PAD, D), x.dtype), meta.astype(x.dtype)], axis=0)[None]
    zbm, gatesm = _inproj(xm, eg, eb, w_all, cos_m, sin_m, tm=BLOCK, first_valid_row=PAD)

    tq = 512
    yatt = _attention(zb, zbm, row2(lam_q1[0]), row2(lam_k1[0]), row2(lam_q2[0]), row2(lam_k2[0]),
                      row2(att_norm_g[0]), tq=tq, nh=1)
    yml = _mlstm(zb, gates, zbm, gatesm, conv_w[0], row2(conv_b[0]), gate_b, row2(ml_norm_g[0]),
                 nbps=4 if B % 4 == 0 else (2 if B % 2 == 0 else 1))

    wr32 = jnp.concatenate([w_router[0], jnp.zeros((D, LANES - N_EXPERTS), F32)], axis=1)
    wr_hi = wr32.astype(BF16)
    wr = jnp.concatenate([wr_hi, (wr32 - wr_hi.astype(F32)).astype(BF16)], axis=1)
    br = jnp.concatenate([b_router[0], jnp.full((LANES - N_EXPERTS,), NEG, F32)]).reshape(1, LANES)
    tm_out = 1024
    h1, eid, gate, rank, cnt = _outproj(
        x.reshape(N, D), yatt.reshape(N, -1), yml.reshape(N, -1), zb.reshape(N, ZB_W), eg, eb,
        w_att_out[0].astype(BF16), w_ml_out[0].astype(BF16), w_o[0].astype(BF16),
        row2(ln1_g[0]), row2(ln1_b[0]), wr, br, tm=tm_out, ts=512)

    tm_e = 512
    M = N * TOP_K
    nb = (M + N_EXPERTS * (tm_e - 1) + tm_e - 1) // tm_e
    counts = cnt[0, :N_EXPERTS].astype(jnp.int32)
    nblk = (counts + tm_e - 1) // tm_e
    cum = jnp.cumsum(nblk)
    pstart = (cum - nblk) * tm_e
    eid_t = eid[:TOP_K].astype(jnp.int32)
    base = jnp.sum(jnp.where(eid_t[None] == jnp.arange(N_EXPERTS)[:, None, None], pstart[:, None, None], 0), axis=0)
    dest = base + rank[:TOP_K].astype(jnp.int32)
    n_used = cum[-1:].astype(jnp.int32)
    blk = jnp.minimum(jnp.arange(nb, dtype=jnp.int32), n_used[0] - 1)
    blk_exp = jnp.sum(blk[:, None] >= cum[None, :], axis=1).astype(jnp.int32)

    def dest_tiles(tm):
        return dest.reshape(TOP_K, N // tm, tm).transpose(1, 0, 2).reshape(N // tm, 1, TOP_K * tm)

    tm_d, tm_c = _pick(N, (1024, 512, 256)), 256
    xs = _dispatch(pstart + counts, nblk * tm_e - counts, dest_tiles(tm_d), h1, nb * tm_e, tm=tm_d,
                   max_pad=tm_e - 1)

    has = nblk > 0
    ids = jnp.where(has, jnp.arange(N_EXPERTS, dtype=jnp.int32), N_EXPERTS)
    after = jnp.concatenate([ids[1:], jnp.full((1,), N_EXPERTS, jnp.int32)])
    next_exp = lax.cummin(after, axis=0, reverse=True).astype(jnp.int32)
    slot_of_exp = ((jnp.cumsum(has.astype(jnp.int32)) - 1) & 1).astype(jnp.int32)
    bgu = b_gu[0]
    ys = _experts(blk_exp, n_used, next_exp, slot_of_exp, xs, w_gu[0], bgu[:, None, 0::2], bgu[:, None, 1::2],
                  w_down[0], b_down[0][:, None, :], tm=tm_e)

    out = _combine(dest_tiles(tm_c), h1, gate, row2(ln2_g[0]), row2(ln2_b[0]), ys, tm=tm_c)
    return out.reshape(B, S, D)
```

```python
import functools
import math

import jax
import jax.numpy as jnp
import numpy as np
from jax import lax
from jax.experimental import pallas as pl
from jax.experimental.pallas import tpu as pltpu

D_MODEL = 1024
N_META = 16
BLOCK = 128
PAD = BLOCK - N_META
NEG = -1e30
LN_EPS = 1e-5
ATT_HEADS = 4
ATT_QK_DIM = 64
ATT_V_DIM = 128
ROPE_THETA = 10000.0
ML_HEADS = 4
ML_DH = 128
ML_WIDTH = ML_HEADS * ML_DH
CONV_K = 4
N_EXPERTS = 32
TOP_K = 4
SWIGLU_LIMIT = 7.0
SWIGLU_ALPHA = 1.702
DEPTH = 1
DN_ALPHA = (2 * DEPTH) ** 0.25
LAMBDA_INIT = 0.8 - 0.6 * math.exp(-0.3 * 0)

LANES = 128
C_GA, C_GM, C_MQK, C_AQ, C_AK, C_AV, C_MV, C_MO = 0, 1024, 2048, 3072, 3584, 4096, 4608, 5120
ZB_W = 5632
GATE_W = 2 * LANES
W_ALL = ZB_W + GATE_W
CHUNK = 512

VMEM_LIMIT = 56 * 1024 * 1024

BF16 = jnp.bfloat16
F32 = jnp.float32


def _cparams(n_axes):
    return pltpu.CompilerParams(dimension_semantics=("arbitrary",) * n_axes, vmem_limit_bytes=VMEM_LIMIT)


def _ln(x, g, b):
    mu = jnp.mean(x, axis=-1, keepdims=True)
    xc = x - mu
    var = jnp.mean(xc * xc, axis=-1, keepdims=True)
    return xc * lax.rsqrt(var + LN_EPS) * g + b


def _sigmoid(x):
    return 1.0 / (1.0 + jnp.exp(-x))


ROW_TILE = D_MODEL // LANES


def _store_row_tiles(ref, val):
    tm = val.shape[0]
    for g in range(ROW_TILE):
        ref[pl.ds(g, tm, stride=ROW_TILE), :] = val[:, g * LANES:(g + 1) * LANES]


def _load_row_tiles(ref, tm):
    return jnp.concatenate([ref[pl.ds(g, tm, stride=ROW_TILE), :] for g in range(ROW_TILE)], axis=1)


def _row_tile(ref, r):
    return ref.at[pl.ds(pl.multiple_of(r * ROW_TILE, ROW_TILE), ROW_TILE)]


def _inproj_kernel(x_ref, g_ref, b_ref, w_ref, cos_ref, sin_ref, zb_ref, gt_ref, kh_ref, vh_ref, *, first_valid_row):
    x = x_ref[0]
    tm = x.shape[0]
    hb = _ln(x, g_ref[...], b_ref[...]).astype(BF16)
    cos = cos_ref[...]
    sin = sin_ref[...]
    if first_valid_row:
        rowmask = lax.broadcasted_iota(jnp.int32, (tm, 1), 0) >= first_valid_row
    for c in range(ZB_W // CHUNK):
        z = jnp.dot(hb, w_ref[:, c * CHUNK:(c + 1) * CHUNK], preferred_element_type=F32)
        if c * CHUNK in (C_AQ, C_AK):
            parts = []
            for h in range(ATT_HEADS):
                zh = z[:, h * LANES:(h + 1) * LANES]
                parts.append(zh * cos + pltpu.roll(zh, 64, 1) * sin)
            z = jnp.concatenate(parts, axis=1)
            if c * CHUNK == C_AQ:
                z = z * (ATT_QK_DIM ** -0.5 * math.log2(math.e))
        if first_valid_row:
            z = jnp.where(rowmask, z, 0.0)
        zb_ref[0, :, c * CHUNK:(c + 1) * CHUNK] = z.astype(BF16)
        if c * CHUNK in (C_AK, C_AV):
            hm_ref = kh_ref if c * CHUNK == C_AK else vh_ref
            for h in range(ATT_HEADS):
                hm_ref[0, h] = z[:, h * LANES:(h + 1) * LANES].astype(BF16)
    zg = jnp.dot(hb, w_ref[:, ZB_W:W_ALL], preferred_element_type=F32)
    if first_valid_row:
        zg = jnp.where(rowmask, zg, 0.0)
    gt_ref[0] = zg


def _inproj(x3, g, b, w, cos, sin, *, tm, first_valid_row=0):
    B, S, D = x3.shape
    nt = S // tm
    return pl.pallas_call(
        functools.partial(_inproj_kernel, first_valid_row=first_valid_row),
        grid=(B, nt),
        in_specs=[
            pl.BlockSpec((1, tm, D), lambda bi, i: (bi, i, 0)),
            pl.BlockSpec((1, D), lambda bi, i: (0, 0)),
            pl.BlockSpec((1, D), lambda bi, i: (0, 0)),
            pl.BlockSpec((D, W_ALL), lambda bi, i: (0, 0)),
            pl.BlockSpec((tm, LANES), lambda bi, i: (i, 0)),
            pl.BlockSpec((tm, LANES), lambda bi, i: (i, 0)),
        ],
        out_specs=[
            pl.BlockSpec((1, tm, ZB_W), lambda bi, i: (bi, i, 0)),
            pl.BlockSpec((1, tm, GATE_W), lambda bi, i: (bi, i, 0)),
            pl.BlockSpec((1, ATT_HEADS, tm, LANES), lambda bi, i: (bi, 0, i, 0)),
            pl.BlockSpec((1, ATT_HEADS, tm, LANES), lambda bi, i: (bi, 0, i, 0)),
        ],
        out_shape=[
            jax.ShapeDtypeStruct((B, S, ZB_W), BF16),
            jax.ShapeDtypeStruct((B, S, GATE_W), F32),
            jax.ShapeDtypeStruct((B, ATT_HEADS, S, LANES), BF16),
            jax.ShapeDtypeStruct((B, ATT_HEADS, S, LANES), BF16),
        ],
        compiler_params=_cparams(2),
        name="inproj",
    )(x3, g, b, w, cos, sin)


def _attn_kernel(q_ref, k_ref, v_ref, km_ref, vm_ref, lq1_ref, lk1_ref, lq2_ref, lk2_ref, ng_ref, o_ref,
                 sa_sc, sb_sc, pa_sc, pb_sc, aa_sc, ab_sc, m_sc, acc_sc, *, tq, tk, rc, nh):
    assert tq == 2 * tk
    qi = pl.program_id(2)
    rows = 2 * tq
    heads = range(nh)
    cols = lambda hh: slice(hh * LANES, (hh + 1) * LANES)
    lane = lax.broadcasted_iota(jnp.int32, (tq, LANES), 1)
    is_map0 = (lane & 63) < 32
    qq = {}
    for hh in heads:
        q = q_ref[0, :, cols(hh)]
        zero = jnp.zeros_like(q)
        qq[hh] = jnp.concatenate([jnp.where(is_map0, q, zero), jnp.where(is_map0, zero, q)], axis=0)

    def qk_stage(hh, kblk, s_ref, col0=0):
        s_ref[hh, :, col0:col0 + kblk.shape[0]] = lax.dot_general(qq[hh], kblk, (((1,), (1,)), ((), ())),
                                                                  preferred_element_type=F32)

    def pv_stage(hh, p_ref, a_ref, vblk, first=False):
        width = vblk.shape[0]
        v1 = jnp.concatenate([vblk, jnp.ones_like(vblk)], axis=1)
        pv = jnp.dot(p_ref[hh, :, 0:width], v1, preferred_element_type=F32)
        if first:
            acc_sc[hh] = pv
        else:
            a = a_ref[hh]
            acc_sc[hh] = jnp.concatenate([a, a], axis=1) * acc_sc[hh] + pv

    def sm_stage(hh, s_ref, p_ref, a_ref, width, vis, first=False):
        nrep = width // LANES
        for r0 in range(0, rows, rc):
            kind = vis(r0)
            if kind == "none":
                continue
            s = s_ref[hh, r0:r0 + rc, 0:width]
            if kind != "all":
                s = kind(s)
            m_cur = jnp.max(s, axis=1, keepdims=True)
            if first:
                m_new = jnp.broadcast_to(m_cur, (rc, LANES))
            else:
                m_prev = m_sc[hh, r0:r0 + rc, :]
                m_new = jnp.maximum(m_prev, m_cur)
                a_ref[hh, r0:r0 + rc, :] = jnp.exp2(m_prev - m_new)
            m_sc[hh, r0:r0 + rc, :] = m_new
        for r0 in range(0, rows, rc):
            kind = vis(r0)
            if kind == "none":
                p_ref[hh, r0:r0 + rc, 0:width] = jnp.zeros((rc, width), BF16)
                a_ref[hh, r0:r0 + rc, :] = jnp.ones((rc, LANES), F32)
                continue
            s = s_ref[hh, r0:r0 + rc, 0:width]
            if kind != "all":
                s = kind(s)
            m_new = m_sc[hh, r0:r0 + rc, :]
            m_rep = m_new if nrep == 1 else jnp.concatenate([m_new] * nrep, axis=1)
            p_ref[hh, r0:r0 + rc, 0:width] = jnp.exp2(s - m_rep).astype(BF16)

    def first_vis(r0):
        q0 = r0 % tq
        def mask(s):
            rowq = q0 + lax.broadcasted_iota(jnp.int32, s.shape, 0)
            col = lax.broadcasted_iota(jnp.int32, s.shape, 1)
            keep = col >= tk + PAD
            if tk - 1 <= q0:
                keep = jnp.logical_or(keep, col < tk)
            else:
                keep = jnp.logical_or(keep, col <= rowq)
            return jnp.where(keep, s, NEG)
        return mask

    def diag_vis(d):
        def vis(r0):
            q0 = r0 % tq
            if d * tk + tk - 1 <= q0:
                return "all"
            if d * tk > q0 + rc - 1:
                return "none"
            def mask(s):
                rowq = q0 + lax.broadcasted_iota(jnp.int32, s.shape, 0)
                col = d * tk + lax.broadcasted_iota(jnp.int32, s.shape, 1)
                return jnp.where(col <= rowq, s, NEG)
            return mask
        return vis

    all_vis = lambda r0: "all"
    kblk = lambda hh, start: k_ref[0, hh, pl.ds(start, tk), :]
    vblk = lambda hh, start: v_ref[0, hh, pl.ds(start, tk), :]

    d0 = pl.multiple_of(qi * tq, tk)
    d1 = pl.multiple_of(qi * tq + tk, tk)
    for hh in heads:
        qk_stage(hh, kblk(hh, d0), sa_sc)
    for hh in heads:
        qk_stage(hh, km_ref[0, :, cols(hh)], sa_sc, col0=tk)
    for hh in heads:
        qk_stage(hh, kblk(hh, d1), sb_sc)
    for hh in heads:
        sm_stage(hh, sa_sc, pa_sc, None, tk + BLOCK, first_vis, first=True)
    for hh in heads:
        qk_stage(hh, kblk(hh, 0), sa_sc)
    for hh in heads:
        sm_stage(hh, sb_sc, pb_sc, ab_sc, tk, diag_vis(1))
    for hh in heads:
        pv_stage(hh, pa_sc, None, jnp.concatenate([vblk(hh, d0), vm_ref[0, :, cols(hh)]], axis=0), first=True)

    def pair(i, carry):
        a_start = pl.multiple_of(i * tq, tk)
        b_start = pl.multiple_of(i * tq + tk, tk)
        prev_b = pl.multiple_of(jnp.where(i == 0, d1, a_start - tk), tk)
        next_a = pl.multiple_of(jnp.minimum(a_start + tq, (qi - 1) * tq), tk)
        for hh in heads:
            qk_stage(hh, kblk(hh, b_start), sb_sc)
        for hh in heads:
            pv_stage(hh, pb_sc, ab_sc, vblk(hh, prev_b))
        for hh in heads:
            sm_stage(hh, sa_sc, pa_sc, aa_sc, tk, all_vis)
        for hh in heads:
            qk_stage(hh, kblk(hh, next_a), sa_sc)
        for hh in heads:
            pv_stage(hh, pa_sc, aa_sc, vblk(hh, a_start))
        for hh in heads:
            sm_stage(hh, sb_sc, pb_sc, ab_sc, tk, all_vis)
        return carry

    lax.fori_loop(0, qi, pair, 0)
    last_b = pl.multiple_of(jnp.where(qi == 0, d1, qi * tq - tk), tk)
    for hh in heads:
        pv_stage(hh, pb_sc, ab_sc, vblk(hh, last_b))

    s1 = jnp.sum(lq1_ref[...] * lk1_ref[...], axis=1, keepdims=True)
    s2 = jnp.sum(lq2_ref[...] * lk2_ref[...], axis=1, keepdims=True)
    lam = jnp.exp(s1) - jnp.exp(s2) + LAMBDA_INIT
    for hh in heads:
        o = acc_sc[hh, :, 0:ATT_V_DIM] / acc_sc[hh, :, ATT_V_DIM:]
        a = o[:tq] - lam * o[tq:]
        y = a * lax.rsqrt(jnp.mean(a * a, axis=1, keepdims=True) + LN_EPS) * ng_ref[...]
        o_ref[0, :, cols(hh)] = (y * (1.0 - LAMBDA_INIT)).astype(BF16)


def _attention(zb, kh, vh, zbm, lq1, lk1, lq2, lk2, ng, *, tq, nh):
    B, S, _ = zb.shape
    nq = S // tq
    tk = tq // 2
    w = nh * LANES
    qb, kb, vb = C_AQ // w, C_AK // w, C_AV // w
    lam_spec = pl.BlockSpec((1, ATT_QK_DIM), lambda b, h, i: (0, 0))
    return pl.pallas_call(
        functools.partial(_attn_kernel, tq=tq, tk=tk, rc=32, nh=nh),
        grid=(B, ATT_HEADS // nh, nq),
        in_specs=[
            pl.BlockSpec((1, tq, w), lambda b, h, i: (b, i, qb + h)),
            pl.BlockSpec((1, nh, S, LANES), lambda b, h, i: (b, h, 0, 0)),
            pl.BlockSpec((1, nh, S, LANES), lambda b, h, i: (b, h, 0, 0)),
            pl.BlockSpec((1, BLOCK, w), lambda b, h, i: (0, 0, kb + h)),
            pl.BlockSpec((1, BLOCK, w), lambda b, h, i: (0, 0, vb + h)),
            lam_spec, lam_spec, lam_spec, lam_spec,
            pl.BlockSpec((1, ATT_V_DIM), lambda b, h, i: (0, 0)),
        ],
        out_specs=pl.BlockSpec((1, tq, w), lambda b, h, i: (b, i, h)),
        out_shape=jax.ShapeDtypeStruct((B, S, ATT_HEADS * ATT_V_DIM), BF16),
        scratch_shapes=[
            pltpu.VMEM((nh, 2 * tq, tk + BLOCK), F32), pltpu.VMEM((nh, 2 * tq, tk), F32),
            pltpu.VMEM((nh, 2 * tq, tk + BLOCK), BF16), pltpu.VMEM((nh, 2 * tq, tk), BF16),
            pltpu.VMEM((nh, 2 * tq, LANES), F32), pltpu.VMEM((nh, 2 * tq, LANES), F32),
            pltpu.VMEM((nh, 2 * tq, LANES), F32),
            pltpu.VMEM((nh, 2 * tq, 2 * ATT_V_DIM), F32),
        ],
        compiler_params=_cparams(3),
        name="diff_attention",
    )(zb, kh, vh, zbm, zbm, lq1, lk1, lq2, lk2, ng)


def _mlstm_kernel(xqk_ref, xv_ref, xo_ref, xg_ref, zm_ref, gm_ref, cw_ref, cb_ref, gb_ref, ng_ref, sh_ref, o_ref,
                  ct_sc, n_sc, m_sc, prev_sc, *, nbps):
    c = pl.program_id(1)
    L = BLOCK

    @pl.when(c == 0)
    def _():
        ct_sc[...] = jnp.zeros_like(ct_sc)
        n_sc[...] = jnp.zeros_like(n_sc)
        m_sc[...] = jnp.zeros_like(m_sc)
        prev_sc[...] = jnp.zeros_like(prev_sc)

    is_meta = c == 0
    row = lax.broadcasted_iota(jnp.int32, (L, 1), 0)
    valid = jnp.logical_or(c > 0, row >= PAD)
    sidx = lax.broadcasted_iota(jnp.int32, (L, L), 0)
    ridx = lax.broadcasted_iota(jnp.int32, (L, L), 1)
    causal = ridx <= sidx
    tri = causal.astype(BF16)
    elems = range(nbps)
    pairs = [(bb, h) for bb in elems for h in range(ML_HEADS)]
    sl = lambda h: slice(h * ML_DH, (h + 1) * ML_DH)
    st = lambda bb, h: bb * 8 + h

    q, k, v, ig, bcs, b_t, ig_t = {}, {}, {}, {}, {}, {}, {}
    for bb in elems:
        qk_pre = jnp.where(is_meta, zm_ref[0, :, C_MQK:C_MQK + 2 * ML_WIDTH], xqk_ref[bb])
        v[bb] = jnp.where(is_meta, zm_ref[0, :, C_MV:C_MV + ML_WIDTH], xv_ref[bb])
        gates = jnp.where(is_meta, gm_ref[0], xg_ref[bb])
        prev = prev_sc[bb * L:(bb + 1) * L, :]
        ext = jnp.concatenate([qk_pre, prev], axis=0)
        prev_sc[bb * L:(bb + 1) * L, :] = qk_pre
        shifted = jnp.dot(sh_ref[...], ext, preferred_element_type=F32)
        acc = cb_ref[...] + cw_ref[CONV_K - 1:CONV_K, :] * qk_pre.astype(F32)
        for s in range(1, CONV_K):
            acc = acc + cw_ref[CONV_K - 1 - s:CONV_K - s, :] * shifted[(s - 1) * L:s * L, :]
        qk = acc * _sigmoid(acc)
        qk = jnp.where(valid, qk, 0.0)
        q[bb] = qk[:, :ML_WIDTH]
        k[bb] = qk[:, ML_WIDTH:] * (ML_DH ** -0.5)

        igv = gates[:, :LANES] + gb_ref[:, :LANES]
        fg = gates[:, LANES:] + gb_ref[:, LANES:]
        lf = jnp.minimum(fg, 0.0) - jnp.log1p(jnp.exp(-jnp.abs(fg)))
        igv = jnp.where(valid, igv, NEG)
        lf = jnp.where(valid, lf, 0.0)
        lf_hi = lf.astype(BF16)
        lf_lo = (lf - lf_hi.astype(F32)).astype(BF16)
        bcs[bb] = (jnp.dot(tri, lf_hi, preferred_element_type=F32)
                   + jnp.dot(tri, lf_lo, preferred_element_type=F32))
        ig[bb] = igv
        b_t[bb] = bcs[bb].T
        ig_t[bb] = igv.T

    bc, m_h, dlog, inter, m_s = {}, {}, {}, {}, {}
    for p in pairs:
        bb, h = p
        bc[p] = bcs[bb][:, h:h + 1]
        m_h[p] = m_sc[st(bb, h):st(bb, h) + 1, 0:1]
        dlog[p] = jnp.where(causal, (bc[p] - b_t[bb][h:h + 1, :]) + ig_t[bb][h:h + 1, :], NEG)
        inter[p] = bc[p] + m_h[p]
    for p in pairs:
        m_s[p] = jnp.maximum(inter[p], jnp.max(dlog[p], axis=1, keepdims=True))

    w_intra, w_inter, qb, kb, s, qc = {}, {}, {}, {}, {}, {}
    for p in pairs:
        bb, h = p
        w_intra[p] = jnp.exp(dlog[p] - m_s[p])
        w_inter[p] = jnp.exp(inter[p] - m_s[p])
        qb[p] = q[bb][:, sl(h)].astype(BF16)
        kb[p] = k[bb][:, sl(h)].astype(BF16)
    for p in pairs:
        bb, h = p
        s[p] = lax.dot_general(qb[p], kb[p], (((1,), (1,)), ((), ())), preferred_element_type=F32) * w_intra[p]
        qc[p] = jnp.dot(qb[p], ct_sc[bb * ML_HEADS + h].astype(BF16), preferred_element_type=F32)

    hh = {}
    for p in pairs:
        bb, h = p
        nrow = n_sc[st(bb, h):st(bb, h) + 1, :]
        num = w_inter[p] * qc[p] + jnp.dot(s[p].astype(BF16), v[bb][:, sl(h)], preferred_element_type=F32)
        den = (w_inter[p] * jnp.sum(q[bb][:, sl(h)] * nrow, axis=1, keepdims=True)
               + jnp.sum(s[p], axis=1, keepdims=True))
        hh[p] = num / jnp.maximum(jnp.abs(den), jnp.exp(-m_s[p]))

    for p in pairs:
        bb, h = p
        b_last = bc[p][L - 1:L, :]
        upd = (b_last - bc[p]) + ig[bb][:, h:h + 1]
        m_new = jnp.maximum(b_last + m_h[p], jnp.max(upd, axis=0, keepdims=True))
        w_old = jnp.exp(b_last + m_h[p] - m_new)
        w_r = jnp.exp(upd - m_new)
        vw = (v[bb][:, sl(h)].astype(F32) * w_r).astype(BF16)
        ci = bb * ML_HEADS + h
        ct_sc[ci] = w_old * ct_sc[ci] + lax.dot_general(kb[p], vw, (((0,), (0,)), ((), ())),
                                                         preferred_element_type=F32)
        r = st(bb, h)
        n_sc[r:r + 1, :] = w_old * n_sc[r:r + 1, :] + jnp.sum(k[bb][:, sl(h)] * w_r, axis=0, keepdims=True)
        m_sc[r:r + 1, :] = jnp.broadcast_to(m_new, (1, LANES))

    outs = {}
    for p in pairs:
        bb, h = p
        hn = hh[p] * lax.rsqrt(jnp.mean(hh[p] * hh[p], axis=1, keepdims=True) + LN_EPS) * ng_ref[:, sl(h)]
        outs[p] = _sigmoid(xo_ref[bb, :, sl(h)].astype(F32)) * hn

    @pl.when(c > 0)
    def _():
        for bb in elems:
            o_ref[bb] = jnp.concatenate([outs[(bb, h)] for h in range(ML_HEADS)], axis=1).astype(BF16)


def _mlstm(zb, gates, zbm, gatesm, conv_w, conv_b, gate_b, ng, *, nbps):
    B, S, _ = zb.shape
    nc = S // BLOCK + 1
    t = np.arange(BLOCK)
    shift = np.zeros(((CONV_K - 1) * BLOCK, 2 * BLOCK), np.float32)
    for s in range(1, CONV_K):
        shift[(s - 1) * BLOCK + t, np.where(t >= s, t - s, 2 * BLOCK + t - s)] = 1.0

    def xc(c):
        return jnp.maximum(c - 1, 0)

    return pl.pallas_call(
        functools.partial(_mlstm_kernel, nbps=nbps),
        grid=(B // nbps, nc),
        in_specs=[
            pl.BlockSpec((nbps, BLOCK, 2 * ML_WIDTH), lambda b, c: (b, xc(c), C_MQK // (2 * ML_WIDTH))),
            pl.BlockSpec((nbps, BLOCK, ML_WIDTH), lambda b, c: (b, xc(c), C_MV // ML_WIDTH)),
            pl.BlockSpec((nbps, BLOCK, ML_WIDTH), lambda b, c: (b, xc(c), C_MO // ML_WIDTH)),
            pl.BlockSpec((nbps, BLOCK, GATE_W), lambda b, c: (b, xc(c), 0)),
            pl.BlockSpec((1, BLOCK, ZB_W), lambda b, c: (0, 0, 0)),
            pl.BlockSpec((1, BLOCK, GATE_W), lambda b, c: (0, 0, 0)),
            pl.BlockSpec((CONV_K, 2 * ML_WIDTH), lambda b, c: (0, 0)),
            pl.BlockSpec((1, 2 * ML_WIDTH), lambda b, c: (0, 0)),
            pl.BlockSpec((1, GATE_W), lambda b, c: (0, 0)),
            pl.BlockSpec((1, ML_WIDTH), lambda b, c: (0, 0)),
            pl.BlockSpec(((CONV_K - 1) * BLOCK, 2 * BLOCK), lambda b, c: (0, 0)),
        ],
        out_specs=pl.BlockSpec((nbps, BLOCK, ML_WIDTH), lambda b, c: (b, xc(c), 0)),
        out_shape=jax.ShapeDtypeStruct((B, S, ML_WIDTH), BF16),
        scratch_shapes=[
            pltpu.VMEM((nbps * ML_HEADS, ML_DH, ML_DH), F32),
            pltpu.VMEM((nbps * 8, ML_DH), F32),
            pltpu.VMEM((nbps * 8, LANES), F32),
            pltpu.VMEM((nbps * BLOCK, 2 * ML_WIDTH), BF16),
        ],
        compiler_params=_cparams(2),
        name="mlstm",
    )(zb, zb, zb, gates, zbm, gatesm, conv_w, conv_b, gate_b, ng, jnp.asarray(shift, BF16))


def _outproj_kernel(x_ref, ya_ref, ym_ref, ga_ref, gmm_ref, eg_ref, eb_ref, wa_ref, wm_ref, wo_ref,
                    g1_ref, b1_ref, wr_ref, br_ref,
                    h1_ref, eid_ref, gate_ref, rank_ref, cnt_ref, run_sc, *, tm, ts):
    i = pl.program_id(0)
    subs = range(tm // ts)
    rows = lambda j: slice(j * ts, (j + 1) * ts)

    @pl.when(i == 0)
    def _():
        run_sc[...] = jnp.zeros_like(run_sc)

    sig = lambda g: 0.5 * jnp.tanh(0.5 * g) + 0.5
    pa = {j: jnp.dot(ya_ref[rows(j), :], wa_ref[...], preferred_element_type=F32) for j in subs}
    pm = {j: jnp.dot(ym_ref[rows(j), :], wm_ref[...], preferred_element_type=F32) for j in subs}
    h0 = {j: _ln(x_ref[rows(j), :], eg_ref[...], eb_ref[...]) for j in subs}
    merged = {j: sig(ga_ref[rows(j), :]) * pa[j].astype(BF16) + sig(gmm_ref[rows(j), :]) * pm[j].astype(BF16)
              for j in subs}
    mix = {j: jnp.dot(merged[j], wo_ref[...], preferred_element_type=F32) for j in subs}
    h1 = {j: _ln(DN_ALPHA * h0[j] + mix[j], g1_ref[...], b1_ref[...]) for j in subs}
    for j in subs:
        _store_row_tiles(h1_ref.at[pl.ds(j * ts * ROW_TILE, ts * ROW_TILE)], h1[j])

    logits = {}
    for j in subs:
        h_hi = h1[j].astype(BF16)
        h_mid = (h1[j] - h_hi.astype(F32)).astype(BF16)
        hh = jnp.dot(h_hi, wr_ref[...], preferred_element_type=F32)
        logits[j] = (hh[:, :LANES] + hh[:, LANES:]
                     + jnp.dot(h_mid, wr_ref[:, :LANES], preferred_element_type=F32) + br_ref[...])

    lane = lax.broadcasted_iota(jnp.int32, (ts, LANES), 1)
    r_i = lax.broadcasted_iota(jnp.int32, (ts, ts), 0)
    c_i = lax.broadcasted_iota(jnp.int32, (ts, ts), 1)
    strict = (c_i < r_i).astype(BF16)
    sel_e, ex, den, onehots, oh = {}, {}, {}, {}, {}
    for j in subs:
        work = logits[j]
        es, vs = [], []
        for _ in range(TOP_K):
            mv = jnp.max(work, axis=1, keepdims=True)
            e = jnp.min(jnp.where(work == mv, lane, LANES), axis=1, keepdims=True)
            es.append(e)
            vs.append(mv)
            work = jnp.where(lane == e, -jnp.inf, work)
        sel_e[j] = es
        ex[j] = [jnp.exp(v - vs[0]) for v in vs]
        den[j] = ex[j][0] + ex[j][1] + ex[j][2] + ex[j][3]
        onehots[j] = [lane == e for e in es]
        acc = jnp.zeros((ts, LANES), F32)
        for o in onehots[j]:
            acc = acc + o.astype(F32)
        oh[j] = acc

    run = run_sc[...]
    for j in subs:
        before = jnp.dot(strict, oh[j].astype(BF16), preferred_element_type=F32) + run
        run = run + jnp.sum(oh[j], axis=0, keepdims=True)
        eid = jnp.zeros((ts, LANES), F32)
        gate = jnp.zeros((ts, LANES), F32)
        rank = jnp.zeros((ts, LANES), F32)
        for kk in range(TOP_K):
            here = lane == kk
            eid = jnp.where(here, sel_e[j][kk].astype(F32), eid)
            gate = jnp.where(here, ex[j][kk] / den[j], gate)
            rk = jnp.sum(jnp.where(onehots[j][kk], before, 0.0), axis=1, keepdims=True)
            rank = jnp.where(here, rk, rank)
        gate_ref[rows(j), :] = gate
        eid_ref[:, rows(j)] = eid.T[0:8, :]
        rank_ref[:, rows(j)] = rank.T[0:8, :]
    run_sc[...] = run
    cnt_ref[...] = run


def _outproj(x2, yatt, yml, zb2, eg, eb, wa, wm, wo, g1, b1, wr, br, *, tm, ts):
    N, D = x2.shape
    vec = lambda w: pl.BlockSpec((1, w), lambda i: (0, 0))
    full = lambda a, b: pl.BlockSpec((a, b), lambda i: (0, 0))
    tile = lambda w: pl.BlockSpec((tm, w), lambda i: (i, 0))
    return pl.pallas_call(
        functools.partial(_outproj_kernel, tm=tm, ts=ts),
        grid=(N // tm,),
        in_specs=[
            tile(D), tile(ATT_HEADS * ATT_V_DIM), tile(ML_WIDTH),
            pl.BlockSpec((tm, D), lambda i: (i, C_GA // D)),
            pl.BlockSpec((tm, D), lambda i: (i, C_GM // D)),
            vec(D), vec(D),
            full(ATT_HEADS * ATT_V_DIM, D), full(ML_WIDTH, D), full(D, D),
            vec(D), vec(D), full(D, 2 * LANES), vec(LANES),
        ],
        out_specs=[pl.BlockSpec((tm * ROW_TILE, LANES), lambda i: (i, 0)),
                   pl.BlockSpec((8, tm), lambda i: (0, i)), tile(LANES),
                   pl.BlockSpec((8, tm), lambda i: (0, i)), vec(LANES)],
        out_shape=[
            jax.ShapeDtypeStruct((N * ROW_TILE, LANES), F32),
            jax.ShapeDtypeStruct((8, N), F32),
            jax.ShapeDtypeStruct((N, LANES), F32),
            jax.ShapeDtypeStruct((8, N), F32),
            jax.ShapeDtypeStruct((1, LANES), F32),
        ],
        scratch_shapes=[pltpu.VMEM((1, LANES), F32)],
        compiler_params=_cparams(1),
        name="outproj_router",
    )(x2, yatt, yml, zb2, zb2, eg, eb, wa, wm, wo, g1, b1, wr, br)


def _row_copies(src_row, dst_row, sem, n, issue):
    def body(r, carry):
        for kk in range(TOP_K):
            cp = pltpu.make_async_copy(src_row(r, kk), dst_row(r, kk), sem)
            if issue:
                cp.start(priority=kk % 2)
            else:
                cp.wait()
        return carry
    lax.fori_loop(0, n, body, 0, unroll=8)


def _dispatch_kernel(pad_pos_ref, pad_len_ref, d_ref, h_ref, xs_ref, zero_sc, sem, *, tm, max_pad):
    @pl.when(pl.program_id(0) == 0)
    def _():
        zero_sc[...] = jnp.zeros_like(zero_sc)
        bits = [1 << b for b in reversed(range(max_pad.bit_length()))]

        def pad_copies(e, issue):
            pos = pad_pos_ref[e]
            length = pad_len_ref[e]
            for bit in bits:
                take = (length & bit) != 0

                @pl.when(take)
                def _(pos=pos, bit=bit):
                    dst = xs_ref.at[pl.ds(pl.multiple_of(pos * ROW_TILE, ROW_TILE), bit * ROW_TILE)]
                    cp = pltpu.make_async_copy(zero_sc.at[pl.ds(0, bit * ROW_TILE)], dst, sem.at[1])
                    if issue:
                        cp.start()
                    else:
                        cp.wait()
                pos = pos + jnp.where(take, bit, 0)

        def per_expert(e, carry):
            pad_copies(e, True)
            pad_copies(e, False)
            return carry

        lax.fori_loop(0, N_EXPERTS, per_expert, 0)

    src = lambda r, kk: _row_tile(h_ref, r)
    dst = lambda r, kk: _row_tile(xs_ref, d_ref[0, 0, kk * tm + r])
    _row_copies(src, dst, sem.at[0], tm, True)
    _row_copies(src, dst, sem.at[0], tm, False)


def _dispatch(pad_pos, pad_len, dest3, h1t, rows_out, *, tm, max_pad):
    N = h1t.shape[0] // ROW_TILE
    zero_rows = 1 << (max_pad.bit_length() - 1)
    return pl.pallas_call(
        functools.partial(_dispatch_kernel, tm=tm, max_pad=max_pad),
        grid_spec=pltpu.PrefetchScalarGridSpec(
            num_scalar_prefetch=2,
            grid=(N // tm,),
            in_specs=[
                pl.BlockSpec((1, 1, tm * TOP_K), lambda i, pp, pn: (i, 0, 0), memory_space=pltpu.SMEM),
                pl.BlockSpec((tm * ROW_TILE, LANES), lambda i, pp, pn: (i, 0)),
            ],
            out_specs=pl.BlockSpec(memory_space=pl.ANY),
            scratch_shapes=[pltpu.VMEM((zero_rows * ROW_TILE, LANES), F32), pltpu.SemaphoreType.DMA((2,))],
        ),
        out_shape=jax.ShapeDtypeStruct((rows_out * ROW_TILE, LANES), F32),
        compiler_params=_cparams(1),
        name="moe_dispatch",
    )(pad_pos, pad_len, dest3, h1t)


def _select_matrix():
    j = np.arange(2 * LANES)
    sel = np.zeros((2 * LANES, 2 * LANES), np.float32)
    sel[j, (j % 2) * LANES + j // 2] = 1.0
    return jnp.asarray(sel, BF16)


def _expert_kernel(be_ref, nu_ref, nx_ref, so_ref, x_ref, bg_ref, bl_ref, bd_ref, sel_ref, wgu_hbm, wd_hbm, y_ref,
                   wgu_buf, wdn_buf, wg_sc, wl_sc, wd_sc, sem, *, tm):
    i = pl.program_id(0)
    e = be_ref[i]
    slot = so_ref[e]
    active = i < nu_ref[0]
    new_expert = jnp.logical_or(i == 0, e != be_ref[jnp.maximum(i - 1, 0)])

    def weight_copies(ex, sl):
        return (pltpu.make_async_copy(wgu_hbm.at[ex], wgu_buf.at[sl], sem.at[0, sl]),
                pltpu.make_async_copy(wd_hbm.at[ex], wdn_buf.at[sl], sem.at[1, sl]))

    @pl.when(i == 0)
    def _():
        for cp in weight_copies(e, slot):
            cp.start()

    @pl.when(jnp.logical_and(active, new_expert))
    def _():
        for cp in weight_copies(e, slot):
            cp.wait()
        nxt = nx_ref[e]

        @pl.when(nxt < N_EXPERTS)
        def _():
            for cp in weight_copies(nxt, 1 - slot):
                cp.start()

        for c in range(wgu_buf.shape[2] // (2 * LANES)):
            blk = wgu_buf[slot, :, c * 2 * LANES:(c + 1) * 2 * LANES].astype(BF16)
            r = jnp.dot(blk, sel_ref[...], preferred_element_type=F32).astype(BF16)
            wg_sc[:, c * LANES:(c + 1) * LANES] = r[:, :LANES]
            wl_sc[:, c * LANES:(c + 1) * LANES] = r[:, LANES:]
        wd_sc[...] = wdn_buf[slot].astype(BF16)

    @pl.when(active)
    def _():
        xb = _load_row_tiles(x_ref, tm).astype(BF16)
        hg = jnp.dot(xb, wg_sc[...], preferred_element_type=F32) + bg_ref[0]
        hl = jnp.dot(xb, wl_sc[...], preferred_element_type=F32) + bl_ref[0]
        glu = jnp.minimum(hg, SWIGLU_LIMIT)
        lin = jnp.clip(hl, -SWIGLU_LIMIT, SWIGLU_LIMIT)
        act = glu * _sigmoid(SWIGLU_ALPHA * glu) * (lin + 1.0)
        _store_row_tiles(y_ref, jnp.dot(act.astype(BF16), wd_sc[...], preferred_element_type=F32) + bd_ref[0])


def _experts(blk_exp, n_used, next_exp, slot_of_exp, xs, wgu, bg, bl, wd, bd, *, tm):
    MP = xs.shape[0] // ROW_TILE
    _, D, F2 = wgu.shape
    F = F2 // 2
    rows = pl.BlockSpec((tm * ROW_TILE, LANES), lambda i, be, nu, nx, so: (jnp.minimum(i, nu[0] - 1), 0))
    wsp = lambda a, b: pl.BlockSpec((1, a, b), lambda i, be, nu, nx, so: (be[i], 0, 0))
    return pl.pallas_call(
        functools.partial(_expert_kernel, tm=tm),
        grid_spec=pltpu.PrefetchScalarGridSpec(
            num_scalar_prefetch=4,
            grid=(MP // tm,),
            in_specs=[rows, wsp(1, F), wsp(1, F), wsp(1, D),
                      pl.BlockSpec((2 * LANES, 2 * LANES), lambda i, be, nu, nx, so: (0, 0)),
                      pl.BlockSpec(memory_space=pl.ANY), pl.BlockSpec(memory_space=pl.ANY)],
            out_specs=rows,
            scratch_shapes=[pltpu.VMEM((2, D, F2), F32), pltpu.VMEM((2, F, D), F32),
                            pltpu.VMEM((D, F), BF16), pltpu.VMEM((D, F), BF16), pltpu.VMEM((F, D), BF16),
                            pltpu.SemaphoreType.DMA((2, 2))],
        ),
        out_shape=jax.ShapeDtypeStruct(xs.shape, F32),
        compiler_params=_cparams(1),
        name="moe_experts",
    )(blk_exp, n_used, next_exp, slot_of_exp, xs, bg, bl, bd, _select_matrix(), wgu, wd)


def _combine_kernel(dcur_ref, dnxt_ref, h1_ref, gate_ref, g2_ref, b2_ref, ys_ref, o_ref, ybuf, sem, *, tm, nsteps):
    i = pl.program_id(0)
    slot = i % 2

    def copies(d_ref, s, issue):
        src = lambda r, kk: _row_tile(ys_ref, d_ref[0, 0, kk * tm + r])
        dst = lambda r, kk: _row_tile(ybuf.at[s, kk], r)
        _row_copies(src, dst, sem.at[s], tm, issue)

    @pl.when(i == 0)
    def _():
        copies(dcur_ref, 0, True)

    @pl.when(i + 1 < nsteps)
    def _():
        copies(dnxt_ref, 1 - slot, True)

    copies(dcur_ref, slot, False)
    acc = DN_ALPHA * _load_row_tiles(h1_ref, tm)
    for kk in range(TOP_K):
        acc = acc + gate_ref[:, kk:kk + 1] * _load_row_tiles(ybuf.at[slot, kk], tm)
    o_ref[...] = _ln(acc, g2_ref[...], b2_ref[...])


def _combine(dest3, h1t, gate, g2, b2, ys, *, tm):
    N, D = h1t.shape[0] // ROW_TILE, D_MODEL
    nsteps = N // tm
    dspec = lambda f: pl.BlockSpec((1, 1, tm * TOP_K), f, memory_space=pltpu.SMEM)
    return pl.pallas_call(
        functools.partial(_combine_kernel, tm=tm, nsteps=nsteps),
        grid=(nsteps,),
        in_specs=[
            dspec(lambda i: (i, 0, 0)),
            dspec(lambda i: (jnp.minimum(i + 1, nsteps - 1), 0, 0)),
            pl.BlockSpec((tm * ROW_TILE, LANES), lambda i: (i, 0)),
            pl.BlockSpec((tm, LANES), lambda i: (i, 0)),
            pl.BlockSpec((1, D), lambda i: (0, 0)),
            pl.BlockSpec((1, D), lambda i: (0, 0)),
            pl.BlockSpec(memory_space=pl.ANY),
        ],
        out_specs=pl.BlockSpec((tm, D), lambda i: (i, 0)),
        out_shape=jax.ShapeDtypeStruct((N, D), F32),
        scratch_shapes=[pltpu.VMEM((2, TOP_K, tm * ROW_TILE, LANES), F32), pltpu.SemaphoreType.DMA((2,))],
        compiler_params=_cparams(1),
        name="moe_combine",
    )(dest3, dest3, h1t, gate, g2, b2, ys)


def _rotary_column_order():
    idx = np.empty((ATT_HEADS * LANES,), np.int32)
    for h in range(ATT_HEADS):
        for half in range(2):
            for sub in range(2):
                for dd in range(32):
                    idx[h * LANES + half * 64 + sub * 32 + dd] = (2 * h + sub) * ATT_QK_DIM + half * 32 + dd
    return idx


def _rotary_tables(pos):
    half = ATT_QK_DIM // 2
    inv_freq = ROPE_THETA ** (-jnp.arange(half, dtype=F32) / half)
    ang = pos.astype(F32)[:, None] * inv_freq[None, :]
    cos = jnp.tile(jnp.cos(ang), (1, 4))
    sin = jnp.tile(jnp.sin(ang), (1, 4))
    sign = jnp.where(jnp.arange(LANES) < 64, -1.0, 1.0).astype(F32)
    return cos, sin * sign[None, :]


def _pick(n, prefs):
    for t in prefs:
        if n % t == 0:
            return t
    raise ValueError(f"unsupported size {n}")


def kernel(x, meta, emb_ln_g, emb_ln_b, w_in, conv_w, conv_b, gate_bias, lam_q1, lam_k1, lam_q2, lam_k2,
           att_norm_g, ml_norm_g, w_att_out, w_ml_out, w_o, ln1_g, ln1_b, w_router, b_router,
           w_gu, b_gu, w_down, b_down, ln2_g, ln2_b):
    B, S, D = x.shape
    assert D == D_MODEL and S % 512 == 0 and w_in.shape[0] == DEPTH
    N = B * S
    row2 = lambda a: a.reshape(1, -1)

    w = w_in[0]
    o_aq, o_ak, o_av, o_mqk, o_mv, o_mo, o_gt, o_ga, o_gm = 0, 512, 1024, 1536, 2560, 3072, 3584, 3592, 4616
    perm = _rotary_column_order()
    gpad = jnp.zeros((D, LANES - ML_HEADS), F32)
    w_all = jnp.concatenate([
        w[:, o_ga:o_ga + D], w[:, o_gm:o_gm + D], w[:, o_mqk:o_mqk + 2 * ML_WIDTH],
        w[:, o_aq:o_aq + 512][:, perm], w[:, o_ak:o_ak + 512][:, perm], w[:, o_av:o_av + 512],
        w[:, o_mv:o_mv + ML_WIDTH], w[:, o_mo:o_mo + ML_WIDTH],
        w[:, o_gt:o_gt + ML_HEADS], gpad, w[:, o_gt + ML_HEADS:o_gt + 2 * ML_HEADS], gpad,
    ], axis=1).astype(BF16)
    gb = gate_bias[0]
    zpad = jnp.zeros((LANES - ML_HEADS,), F32)
    gate_b = jnp.concatenate([gb[:ML_HEADS], zpad, gb[ML_HEADS:], zpad]).reshape(1, GATE_W)

    cos_x, sin_x = _rotary_tables(N_META + jnp.arange(S))
    cos_m, sin_m = _rotary_tables(jnp.maximum(jnp.arange(BLOCK) - PAD, 0))

    eg, eb = row2(emb_ln_g), row2(emb_ln_b)

    tm_in = 512
    zb, gates, kh, vh = _inproj(x, eg, eb, w_all, cos_x, sin_x, tm=tm_in)
    xm = jnp.concatenate([jnp.zeros((PAD, D), x.dtype), meta.astype(x.dtype)], axis=0)[None]
    zbm, gatesm, _, _ = _inproj(xm, eg, eb, w_all, cos_m, sin_m, tm=BLOCK, first_valid_row=PAD)

    tq = 512
    yatt = _attention(zb, kh, vh, zbm, row2(lam_q1[0]), row2(lam_k1[0]), row2(lam_q2[0]), row2(lam_k2[0]),
                      row2(att_norm_g[0]), tq=tq, nh=1)
    yml = _mlstm(zb, gates, zbm, gatesm, conv_w[0], row2(conv_b[0]), gate_b, row2(ml_norm_g[0]),
                 nbps=4 if B % 4 == 0 else (2 if B % 2 == 0 else 1))

    wr32 = jnp.concatenate([w_router[0], jnp.zeros((D, LANES - N_EXPERTS), F32)], axis=1)
    wr_hi = wr32.astype(BF16)
    wr = jnp.concatenate([wr_hi, (wr32 - wr_hi.astype(F32)).astype(BF16)], axis=1)
    br = jnp.concatenate([b_router[0], jnp.full((LANES - N_EXPERTS,), NEG, F32)]).reshape(1, LANES)
    tm_out = 1024
    h1, eid, gate, rank, cnt = _outproj(
        x.reshape(N, D), yatt.reshape(N, -1), yml.reshape(N, -1), zb.reshape(N, ZB_W), eg, eb,
        w_att_out[0].astype(BF16), w_ml_out[0].astype(BF16), w_o[0].astype(BF16),
        row2(ln1_g[0]), row2(ln1_b[0]), wr, br, tm=tm_out, ts=512)

    tm_e = 512
    M = N * TOP_K
    nb = (M + N_EXPERTS * (tm_e - 1) + tm_e - 1) // tm_e
    counts = cnt[0, :N_EXPERTS].astype(jnp.int32)
    nblk = (counts + tm_e - 1) // tm_e
    cum = jnp.cumsum(nblk)
    pstart = (cum - nblk) * tm_e
    eid_t = eid[:TOP_K].astype(jnp.int32)
    base = jnp.sum(jnp.where(eid_t[None] == jnp.arange(N_EXPERTS)[:, None, None], pstart[:, None, None], 0), axis=0)
    dest = base + rank[:TOP_K].astype(jnp.int32)
    n_used = cum[-1:].astype(jnp.int32)
    blk = jnp.minimum(jnp.arange(nb, dtype=jnp.int32), n_used[0] - 1)
    blk_exp = jnp.sum(blk[:, None] >= cum[None, :], axis=1).astype(jnp.int32)

    def dest_tiles(tm):
        return dest.reshape(TOP_K, N // tm, tm).transpose(1, 0, 2).reshape(N // tm, 1, TOP_K * tm)

    tm_d, tm_c = _pick(N, (1024, 512, 256)), 256
    xs = _dispatch(pstart + counts, nblk * tm_e - counts, dest_tiles(tm_d), h1, nb * tm_e, tm=tm_d,
                   max_pad=tm_e - 1)

    has = nblk > 0
    ids = jnp.where(has, jnp.arange(N_EXPERTS, dtype=jnp.int32), N_EXPERTS)
    after = jnp.concatenate([ids[1:], jnp.full((1,), N_EXPERTS, jnp.int32)])
    next_exp = lax.cummin(after, axis=0, reverse=True).astype(jnp.int32)
    slot_of_exp = ((jnp.cumsum(has.astype(jnp.int32)) - 1) & 1).astype(jnp.int32)
    bgu = b_gu[0]
    ys = _experts(blk_exp, n_used, next_exp, slot_of_exp, xs, w_gu[0], bgu[:, None, 0::2], bgu[:, None, 1::2],
                  w_down[0], b_down[0][:, None, :], tm=tm_e)

    out = _combine(dest_tiles(tm_c), h1, gate, row2(ln2_g[0]), row2(ln2_b[0]), ys, tm=tm_c)
    return out.reshape(B, S, D)
```

```python
import functools
import math

import jax
import jax.numpy as jnp
import numpy as np
from jax import lax
from jax.experimental import pallas as pl
from jax.experimental.pallas import tpu as pltpu

D_MODEL = 1024
N_META = 16
BLOCK = 128
PAD = BLOCK - N_META
NEG = -1e30
LN_EPS = 1e-5
ATT_HEADS = 4
ATT_QK_DIM = 64
ATT_V_DIM = 128
ROPE_THETA = 10000.0
ML_HEADS = 4
ML_DH = 128
ML_WIDTH = ML_HEADS * ML_DH
CONV_K = 4
N_EXPERTS = 32
TOP_K = 4
SWIGLU_LIMIT = 7.0
SWIGLU_ALPHA = 1.702
DEPTH = 1
DN_ALPHA = (2 * DEPTH) ** 0.25
LAMBDA_INIT = 0.8 - 0.6 * math.exp(-0.3 * 0)

LANES = 128
C_GA, C_GM, C_MQK, C_AQ, C_AK, C_AV, C_MV, C_MO = 0, 1024, 2048, 3072, 3584, 4096, 4608, 5120
ZB_W = 5632
GATE_W = 2 * LANES
W_ALL = ZB_W + GATE_W
CHUNK = 512

VMEM_LIMIT = 56 * 1024 * 1024

BF16 = jnp.bfloat16
F32 = jnp.float32


def _cparams(n_axes):
    return pltpu.CompilerParams(dimension_semantics=("arbitrary",) * n_axes, vmem_limit_bytes=VMEM_LIMIT)


def _ln(x, g, b):
    mu = jnp.mean(x, axis=-1, keepdims=True)
    xc = x - mu
    var = jnp.mean(xc * xc, axis=-1, keepdims=True)
    return xc * lax.rsqrt(var + LN_EPS) * g + b


def _sigmoid(x):
    return 1.0 / (1.0 + jnp.exp(-x))


ROW_TILE = D_MODEL // LANES


def _store_row_tiles(ref, val):
    tm = val.shape[0]
    for g in range(ROW_TILE):
        ref[pl.ds(g, tm, stride=ROW_TILE), :] = val[:, g * LANES:(g + 1) * LANES]


def _load_row_tiles(ref, tm):
    return jnp.concatenate([ref[pl.ds(g, tm, stride=ROW_TILE), :] for g in range(ROW_TILE)], axis=1)


def _row_tile(ref, r):
    return ref.at[pl.ds(pl.multiple_of(r * ROW_TILE, ROW_TILE), ROW_TILE)]


def _inproj_kernel(x_ref, g_ref, b_ref, w_ref, cos_ref, sin_ref, zb_ref, gt_ref, *, first_valid_row):
    x = x_ref[0]
    tm = x.shape[0]
    hb = _ln(x, g_ref[...], b_ref[...]).astype(BF16)
    cos = cos_ref[...]
    sin = sin_ref[...]
    if first_valid_row:
        rowmask = lax.broadcasted_iota(jnp.int32, (tm, 1), 0) >= first_valid_row
    for c in range(ZB_W // CHUNK):
        z = jnp.dot(hb, w_ref[:, c * CHUNK:(c + 1) * CHUNK], preferred_element_type=F32)
        if c * CHUNK in (C_AQ, C_AK):
            parts = []
            for h in range(ATT_HEADS):
                zh = z[:, h * LANES:(h + 1) * LANES]
                parts.append(zh * cos + pltpu.roll(zh, 64, 1) * sin)
            z = jnp.concatenate(parts, axis=1)
            if c * CHUNK == C_AQ:
                z = z * (ATT_QK_DIM ** -0.5 * math.log2(math.e))
        if first_valid_row:
            z = jnp.where(rowmask, z, 0.0)
        zb_ref[0, :, c * CHUNK:(c + 1) * CHUNK] = z.astype(BF16)
    zg = jnp.dot(hb, w_ref[:, ZB_W:W_ALL], preferred_element_type=F32)
    if first_valid_row:
        zg = jnp.where(rowmask, zg, 0.0)
    gt_ref[0] = zg


def _inproj(x3, g, b, w, cos, sin, *, tm, first_valid_row=0):
    B, S, D = x3.shape
    nt = S // tm
    return pl.pallas_call(
        functools.partial(_inproj_kernel, first_valid_row=first_valid_row),
        grid=(B, nt),
        in_specs=[
            pl.BlockSpec((1, tm, D), lambda bi, i: (bi, i, 0)),
            pl.BlockSpec((1, D), lambda bi, i: (0, 0)),
            pl.BlockSpec((1, D), lambda bi, i: (0, 0)),
            pl.BlockSpec((D, W_ALL), lambda bi, i: (0, 0), pipeline_mode=pl.Buffered(1)),
            pl.BlockSpec((tm, LANES), lambda bi, i: (i, 0)),
            pl.BlockSpec((tm, LANES), lambda bi, i: (i, 0)),
        ],
        out_specs=[
            pl.BlockSpec((1, tm, ZB_W), lambda bi, i: (bi, i, 0)),
            pl.BlockSpec((1, tm, GATE_W), lambda bi, i: (bi, i, 0)),
        ],
        out_shape=[
            jax.ShapeDtypeStruct((B, S, ZB_W), BF16),
            jax.ShapeDtypeStruct((B, S, GATE_W), F32),
        ],
        compiler_params=_cparams(2),
        name="inproj",
    )(x3, g, b, w, cos, sin)


def _attn_kernel(q_ref, k_ref, v_ref, km_ref, vm_ref, lq1_ref, lk1_ref, lq2_ref, lk2_ref, ng_ref, o_ref,
                 sa_sc, sb_sc, pa_sc, pb_sc, aa_sc, ab_sc, m_sc, acc_sc, *, tq, tk, rc, nh):
    assert tq == 2 * tk
    qi = pl.program_id(2)
    rows = 2 * tq
    heads = range(nh)
    cols = lambda hh: slice(hh * LANES, (hh + 1) * LANES)
    lane = lax.broadcasted_iota(jnp.int32, (tq, LANES), 1)
    is_map0 = (lane & 63) < 32
    qq = {}
    for hh in heads:
        q = q_ref[0, :, cols(hh)]
        zero = jnp.zeros_like(q)
        qq[hh] = jnp.concatenate([jnp.where(is_map0, q, zero), jnp.where(is_map0, zero, q)], axis=0)

    def qk_stage(hh, kblk, s_ref, col0=0):
        s_ref[hh, :, col0:col0 + kblk.shape[0]] = lax.dot_general(qq[hh], kblk, (((1,), (1,)), ((), ())),
                                                                  preferred_element_type=F32)

    def pv_stage(hh, p_ref, a_ref, vblk, first=False):
        width = vblk.shape[0]
        v1 = jnp.concatenate([vblk, jnp.ones_like(vblk)], axis=1)
        pv = jnp.dot(p_ref[hh, :, 0:width], v1, preferred_element_type=F32)
        if first:
            acc_sc[hh] = pv
        else:
            a = a_ref[hh]
            acc_sc[hh] = jnp.concatenate([a, a], axis=1) * acc_sc[hh] + pv

    def sm_stage(hh, s_ref, p_ref, a_ref, width, vis, first=False):
        nrep = width // LANES
        for r0 in range(0, rows, rc):
            kind = vis(r0)
            if kind == "none":
                continue
            s = s_ref[hh, r0:r0 + rc, 0:width]
            if kind != "all":
                s = kind(s)
            m_cur = jnp.max(s, axis=1, keepdims=True)
            if first:
                m_new = jnp.broadcast_to(m_cur, (rc, LANES))
            else:
                m_prev = m_sc[hh, r0:r0 + rc, :]
                m_new = jnp.maximum(m_prev, m_cur)
                a_ref[hh, r0:r0 + rc, :] = jnp.exp2(m_prev - m_new)
            m_sc[hh, r0:r0 + rc, :] = m_new
        for r0 in range(0, rows, rc):
            kind = vis(r0)
            if kind == "none":
                p_ref[hh, r0:r0 + rc, 0:width] = jnp.zeros((rc, width), BF16)
                a_ref[hh, r0:r0 + rc, :] = jnp.ones((rc, LANES), F32)
                continue
            s = s_ref[hh, r0:r0 + rc, 0:width]
            if kind != "all":
                s = kind(s)
            m_new = m_sc[hh, r0:r0 + rc, :]
            m_rep = m_new if nrep == 1 else jnp.concatenate([m_new] * nrep, axis=1)
            p_ref[hh, r0:r0 + rc, 0:width] = jnp.exp2(s - m_rep).astype(BF16)

    def first_vis(r0):
        q0 = r0 % tq
        def mask(s):
            rowq = q0 + lax.broadcasted_iota(jnp.int32, s.shape, 0)
            col = lax.broadcasted_iota(jnp.int32, s.shape, 1)
            keep = col >= tk + PAD
            if tk - 1 <= q0:
                keep = jnp.logical_or(keep, col < tk)
            else:
                keep = jnp.logical_or(keep, col <= rowq)
            return jnp.where(keep, s, NEG)
        return mask

    def diag_vis(d):
        def vis(r0):
            q0 = r0 % tq
            if d * tk + tk - 1 <= q0:
                return "all"
            if d * tk > q0 + rc - 1:
                return "none"
            def mask(s):
                rowq = q0 + lax.broadcasted_iota(jnp.int32, s.shape, 0)
                col = d * tk + lax.broadcasted_iota(jnp.int32, s.shape, 1)
                return jnp.where(col <= rowq, s, NEG)
            return mask
        return vis

    all_vis = lambda r0: "all"
    kblk = lambda hh, start: k_ref[0, pl.ds(start, tk), cols(hh)]
    vblk = lambda hh, start: v_ref[0, pl.ds(start, tk), cols(hh)]

    d0 = pl.multiple_of(qi * tq, tk)
    d1 = pl.multiple_of(qi * tq + tk, tk)
    for hh in heads:
        qk_stage(hh, kblk(hh, d0), sa_sc)
    for hh in heads:
        qk_stage(hh, km_ref[0, :, cols(hh)], sa_sc, col0=tk)
    for hh in heads:
        qk_stage(hh, kblk(hh, d1), sb_sc)
    for hh in heads:
        sm_stage(hh, sa_sc, pa_sc, None, tk + BLOCK, first_vis, first=True)
    for hh in heads:
        qk_stage(hh, kblk(hh, 0), sa_sc)
    for hh in heads:
        sm_stage(hh, sb_sc, pb_sc, ab_sc, tk, diag_vis(1))
    for hh in heads:
        pv_stage(hh, pa_sc, None, jnp.concatenate([vblk(hh, d0), vm_ref[0, :, cols(hh)]], axis=0), first=True)

    def pair(i, carry):
        a_start = pl.multiple_of(i * tq, tk)
        b_start = pl.multiple_of(i * tq + tk, tk)
        prev_b = pl.multiple_of(jnp.where(i == 0, d1, a_start - tk), tk)
        next_a = pl.multiple_of(jnp.minimum(a_start + tq, (qi - 1) * tq), tk)
        for hh in heads:
            qk_stage(hh, kblk(hh, b_start), sb_sc)
        for hh in heads:
            pv_stage(hh, pb_sc, ab_sc, vblk(hh, prev_b))
        for hh in heads:
            sm_stage(hh, sa_sc, pa_sc, aa_sc, tk, all_vis)
        for hh in heads:
            qk_stage(hh, kblk(hh, next_a), sa_sc)
        for hh in heads:
            pv_stage(hh, pa_sc, aa_sc, vblk(hh, a_start))
        for hh in heads:
            sm_stage(hh, sb_sc, pb_sc, ab_sc, tk, all_vis)
        return carry

    lax.fori_loop(0, qi, pair, 0)
    last_b = pl.multiple_of(jnp.where(qi == 0, d1, qi * tq - tk), tk)
    for hh in heads:
        pv_stage(hh, pb_sc, ab_sc, vblk(hh, last_b))

    s1 = jnp.sum(lq1_ref[...] * lk1_ref[...], axis=1, keepdims=True)
    s2 = jnp.sum(lq2_ref[...] * lk2_ref[...], axis=1, keepdims=True)
    lam = jnp.exp(s1) - jnp.exp(s2) + LAMBDA_INIT
    for hh in heads:
        o = acc_sc[hh, :, 0:ATT_V_DIM] / acc_sc[hh, :, ATT_V_DIM:]
        a = o[:tq] - lam * o[tq:]
        y = a * lax.rsqrt(jnp.mean(a * a, axis=1, keepdims=True) + LN_EPS) * ng_ref[...]
        o_ref[0, :, cols(hh)] = (y * (1.0 - LAMBDA_INIT)).astype(BF16)


def _attention(zb, zbm, lq1, lk1, lq2, lk2, ng, *, tq, nh):
    B, S, _ = zb.shape
    nq = S // tq
    tk = tq // 2
    w = nh * LANES
    qb, kb, vb = C_AQ // w, C_AK // w, C_AV // w
    lam_spec = pl.BlockSpec((1, ATT_QK_DIM), lambda b, h, i: (0, 0))
    return pl.pallas_call(
        functools.partial(_attn_kernel, tq=tq, tk=tk, rc=32, nh=nh),
        grid=(B, ATT_HEADS // nh, nq),
        in_specs=[
            pl.BlockSpec((1, tq, w), lambda b, h, i: (b, i, qb + h)),
            pl.BlockSpec((1, S, w), lambda b, h, i: (b, 0, kb + h)),
            pl.BlockSpec((1, S, w), lambda b, h, i: (b, 0, vb + h)),
            pl.BlockSpec((1, BLOCK, w), lambda b, h, i: (0, 0, kb + h)),
            pl.BlockSpec((1, BLOCK, w), lambda b, h, i: (0, 0, vb + h)),
            lam_spec, lam_spec, lam_spec, lam_spec,
            pl.BlockSpec((1, ATT_V_DIM), lambda b, h, i: (0, 0)),
        ],
        out_specs=pl.BlockSpec((1, tq, w), lambda b, h, i: (b, i, h)),
        out_shape=jax.ShapeDtypeStruct((B, S, ATT_HEADS * ATT_V_DIM), BF16),
        scratch_shapes=[
            pltpu.VMEM((nh, 2 * tq, tk + BLOCK), F32), pltpu.VMEM((nh, 2 * tq, tk), F32),
            pltpu.VMEM((nh, 2 * tq, tk + BLOCK), BF16), pltpu.VMEM((nh, 2 * tq, tk), BF16),
            pltpu.VMEM((nh, 2 * tq, LANES), F32), pltpu.VMEM((nh, 2 * tq, LANES), F32),
            pltpu.VMEM((nh, 2 * tq, LANES), F32),
            pltpu.VMEM((nh, 2 * tq, 2 * ATT_V_DIM), F32),
        ],
        compiler_params=_cparams(3),
        name="diff_attention",
    )(zb, zb, zb, zbm, zbm, lq1, lk1, lq2, lk2, ng)


def _mlstm_kernel(xqk_ref, xv_ref, xo_ref, xg_ref, zm_ref, gm_ref, cw_ref, cb_ref, gb_ref, ng_ref, sh_ref, o_ref,
                  ct_sc, n_sc, m_sc, prev_sc, *, nbps):
    c = pl.program_id(1)
    L = BLOCK

    @pl.when(c == 0)
    def _():
        ct_sc[...] = jnp.zeros_like(ct_sc)
        n_sc[...] = jnp.zeros_like(n_sc)
        m_sc[...] = jnp.zeros_like(m_sc)
        prev_sc[...] = jnp.zeros_like(prev_sc)

    is_meta = c == 0
    row = lax.broadcasted_iota(jnp.int32, (L, 1), 0)
    valid = jnp.logical_or(c > 0, row >= PAD)
    sidx = lax.broadcasted_iota(jnp.int32, (L, L), 0)
    ridx = lax.broadcasted_iota(jnp.int32, (L, L), 1)
    causal = ridx <= sidx
    tri = causal.astype(BF16)
    elems = range(nbps)
    pairs = [(bb, h) for bb in elems for h in range(ML_HEADS)]
    sl = lambda h: slice(h * ML_DH, (h + 1) * ML_DH)
    st = lambda bb, h: bb * 8 + h

    q, k, v, ig, bcs, b_t, ig_t = {}, {}, {}, {}, {}, {}, {}
    for bb in elems:
        qk_pre = jnp.where(is_meta, zm_ref[0, :, C_MQK:C_MQK + 2 * ML_WIDTH], xqk_ref[bb])
        v[bb] = jnp.where(is_meta, zm_ref[0, :, C_MV:C_MV + ML_WIDTH], xv_ref[bb])
        gates = jnp.where(is_meta, gm_ref[0], xg_ref[bb])
        prev = prev_sc[bb * L:(bb + 1) * L, :]
        ext = jnp.concatenate([qk_pre, prev], axis=0)
        prev_sc[bb * L:(bb + 1) * L, :] = qk_pre
        shifted = jnp.dot(sh_ref[...], ext, preferred_element_type=F32)
        acc = cb_ref[...] + cw_ref[CONV_K - 1:CONV_K, :] * qk_pre.astype(F32)
        for s in range(1, CONV_K):
            acc = acc + cw_ref[CONV_K - 1 - s:CONV_K - s, :] * shifted[(s - 1) * L:s * L, :]
        qk = acc * _sigmoid(acc)
        qk = jnp.where(valid, qk, 0.0)
        q[bb] = qk[:, :ML_WIDTH]
        k[bb] = qk[:, ML_WIDTH:] * (ML_DH ** -0.5)

        igv = gates[:, :LANES] + gb_ref[:, :LANES]
        fg = gates[:, LANES:] + gb_ref[:, LANES:]
        lf = jnp.minimum(fg, 0.0) - jnp.log1p(jnp.exp(-jnp.abs(fg)))
        igv = jnp.where(valid, igv, NEG)
        lf = jnp.where(valid, lf, 0.0)
        lf_hi = lf.astype(BF16)
        lf_lo = (lf - lf_hi.astype(F32)).astype(BF16)
        bcs[bb] = (jnp.dot(tri, lf_hi, preferred_element_type=F32)
                   + jnp.dot(tri, lf_lo, preferred_element_type=F32))
        ig[bb] = igv
        b_t[bb] = bcs[bb].T
        ig_t[bb] = igv.T

    bc, m_h, dlog, inter, m_s = {}, {}, {}, {}, {}
    for p in pairs:
        bb, h = p
        bc[p] = bcs[bb][:, h:h + 1]
        m_h[p] = m_sc[st(bb, h):st(bb, h) + 1, 0:1]
        dlog[p] = jnp.where(causal, (bc[p] - b_t[bb][h:h + 1, :]) + ig_t[bb][h:h + 1, :], NEG)
        inter[p] = bc[p] + m_h[p]
    for p in pairs:
        m_s[p] = jnp.maximum(inter[p], jnp.max(dlog[p], axis=1, keepdims=True))

    w_intra, w_inter, qb, kb, s, qc = {}, {}, {}, {}, {}, {}
    for p in pairs:
        bb, h = p
        w_intra[p] = jnp.exp(dlog[p] - m_s[p])
        w_inter[p] = jnp.exp(inter[p] - m_s[p])
        qb[p] = q[bb][:, sl(h)].astype(BF16)
        kb[p] = k[bb][:, sl(h)].astype(BF16)
    for p in pairs:
        bb, h = p
        s[p] = lax.dot_general(qb[p], kb[p], (((1,), (1,)), ((), ())), preferred_element_type=F32) * w_intra[p]
        qc[p] = jnp.dot(qb[p], ct_sc[bb * ML_HEADS + h].astype(BF16), preferred_element_type=F32)

    hh = {}
    for p in pairs:
        bb, h = p
        nrow = n_sc[st(bb, h):st(bb, h) + 1, :]
        num = w_inter[p] * qc[p] + jnp.dot(s[p].astype(BF16), v[bb][:, sl(h)], preferred_element_type=F32)
        den = (w_inter[p] * jnp.sum(q[bb][:, sl(h)] * nrow, axis=1, keepdims=True)
               + jnp.sum(s[p], axis=1, keepdims=True))
        hh[p] = num / jnp.maximum(jnp.abs(den), jnp.exp(-m_s[p]))

    for p in pairs:
        bb, h = p
        b_last = bc[p][L - 1:L, :]
        upd = (b_last - bc[p]) + ig[bb][:, h:h + 1]
        m_new = jnp.maximum(b_last + m_h[p], jnp.max(upd, axis=0, keepdims=True))
        w_old = jnp.exp(b_last + m_h[p] - m_new)
        w_r = jnp.exp(upd - m_new)
        vw = (v[bb][:, sl(h)].astype(F32) * w_r).astype(BF16)
        ci = bb * ML_HEADS + h
        ct_sc[ci] = w_old * ct_sc[ci] + lax.dot_general(kb[p], vw, (((0,), (0,)), ((), ())),
                                                         preferred_element_type=F32)
        r = st(bb, h)
        n_sc[r:r + 1, :] = w_old * n_sc[r:r + 1, :] + jnp.sum(k[bb][:, sl(h)] * w_r, axis=0, keepdims=True)
        m_sc[r:r + 1, :] = jnp.broadcast_to(m_new, (1, LANES))

    outs = {}
    for p in pairs:
        bb, h = p
        hn = hh[p] * lax.rsqrt(jnp.mean(hh[p] * hh[p], axis=1, keepdims=True) + LN_EPS) * ng_ref[:, sl(h)]
        outs[p] = _sigmoid(xo_ref[bb, :, sl(h)].astype(F32)) * hn

    @pl.when(c > 0)
    def _():
        for bb in elems:
            o_ref[bb] = jnp.concatenate([outs[(bb, h)] for h in range(ML_HEADS)], axis=1).astype(BF16)


def _mlstm(zb, gates, zbm, gatesm, conv_w, conv_b, gate_b, ng, *, nbps):
    B, S, _ = zb.shape
    nc = S // BLOCK + 1
    t = np.arange(BLOCK)
    shift = np.zeros(((CONV_K - 1) * BLOCK, 2 * BLOCK), np.float32)
    for s in range(1, CONV_K):
        shift[(s - 1) * BLOCK + t, np.where(t >= s, t - s, 2 * BLOCK + t - s)] = 1.0

    def xc(c):
        return jnp.maximum(c - 1, 0)

    return pl.pallas_call(
        functools.partial(_mlstm_kernel, nbps=nbps),
        grid=(B // nbps, nc),
        in_specs=[
            pl.BlockSpec((nbps, BLOCK, 2 * ML_WIDTH), lambda b, c: (b, xc(c), C_MQK // (2 * ML_WIDTH))),
            pl.BlockSpec((nbps, BLOCK, ML_WIDTH), lambda b, c: (b, xc(c), C_MV // ML_WIDTH)),
            pl.BlockSpec((nbps, BLOCK, ML_WIDTH), lambda b, c: (b, xc(c), C_MO // ML_WIDTH)),
            pl.BlockSpec((nbps, BLOCK, GATE_W), lambda b, c: (b, xc(c), 0)),
            pl.BlockSpec((1, BLOCK, ZB_W), lambda b, c: (0, 0, 0)),
            pl.BlockSpec((1, BLOCK, GATE_W), lambda b, c: (0, 0, 0)),
            pl.BlockSpec((CONV_K, 2 * ML_WIDTH), lambda b, c: (0, 0)),
            pl.BlockSpec((1, 2 * ML_WIDTH), lambda b, c: (0, 0)),
            pl.BlockSpec((1, GATE_W), lambda b, c: (0, 0)),
            pl.BlockSpec((1, ML_WIDTH), lambda b, c: (0, 0)),
            pl.BlockSpec(((CONV_K - 1) * BLOCK, 2 * BLOCK), lambda b, c: (0, 0)),
        ],
        out_specs=pl.BlockSpec((nbps, BLOCK, ML_WIDTH), lambda b, c: (b, xc(c), 0)),
        out_shape=jax.ShapeDtypeStruct((B, S, ML_WIDTH), BF16),
        scratch_shapes=[
            pltpu.VMEM((nbps * ML_HEADS, ML_DH, ML_DH), F32),
            pltpu.VMEM((nbps * 8, ML_DH), F32),
            pltpu.VMEM((nbps * 8, LANES), F32),
            pltpu.VMEM((nbps * BLOCK, 2 * ML_WIDTH), BF16),
        ],
        compiler_params=_cparams(2),
        name="mlstm",
    )(zb, zb, zb, gates, zbm, gatesm, conv_w, conv_b, gate_b, ng, jnp.asarray(shift, BF16))


def _outproj_kernel(x_ref, ya_ref, ym_ref, ga_ref, gmm_ref, eg_ref, eb_ref, wa_ref, wm_ref, wo_ref,
                    g1_ref, b1_ref, wr_ref, br_ref,
                    h1_ref, eid_ref, gate_ref, rank_ref, cnt_ref, run_sc, *, tm, ts):
    i = pl.program_id(0)
    subs = range(tm // ts)
    rows = lambda j: slice(j * ts, (j + 1) * ts)

    @pl.when(i == 0)
    def _():
        run_sc[...] = jnp.zeros_like(run_sc)

    sig = lambda g: 0.5 * jnp.tanh(0.5 * g) + 0.5
    pa = {j: jnp.dot(ya_ref[rows(j), :], wa_ref[...], preferred_element_type=F32) for j in subs}
    pm = {j: jnp.dot(ym_ref[rows(j), :], wm_ref[...], preferred_element_type=F32) for j in subs}
    h0 = {j: _ln(x_ref[rows(j), :], eg_ref[...], eb_ref[...]) for j in subs}
    merged = {j: sig(ga_ref[rows(j), :]) * pa[j].astype(BF16) + sig(gmm_ref[rows(j), :]) * pm[j].astype(BF16)
              for j in subs}
    mix = {j: jnp.dot(merged[j], wo_ref[...], preferred_element_type=F32) for j in subs}
    h1 = {j: _ln(DN_ALPHA * h0[j] + mix[j], g1_ref[...], b1_ref[...]) for j in subs}
    for j in subs:
        _store_row_tiles(h1_ref.at[pl.ds(j * ts * ROW_TILE, ts * ROW_TILE)], h1[j])

    logits = {}
    for j in subs:
        h_hi = h1[j].astype(BF16)
        h_mid = (h1[j] - h_hi.astype(F32)).astype(BF16)
        hh = jnp.dot(h_hi, wr_ref[...], preferred_element_type=F32)
        logits[j] = (hh[:, :LANES] + hh[:, LANES:]
                     + jnp.dot(h_mid, wr_ref[:, :LANES], preferred_element_type=F32) + br_ref[...])

    lane = lax.broadcasted_iota(jnp.int32, (ts, LANES), 1)
    r_i = lax.broadcasted_iota(jnp.int32, (ts, ts), 0)
    c_i = lax.broadcasted_iota(jnp.int32, (ts, ts), 1)
    strict = (c_i < r_i).astype(BF16)
    sel_e, ex, den, onehots, oh = {}, {}, {}, {}, {}
    for j in subs:
        work = logits[j]
        es, vs = [], []
        for _ in range(TOP_K):
            mv = jnp.max(work, axis=1, keepdims=True)
            e = jnp.min(jnp.where(work == mv, lane, LANES), axis=1, keepdims=True)
            es.append(e)
            vs.append(mv)
            work = jnp.where(lane == e, -jnp.inf, work)
        sel_e[j] = es
        ex[j] = [jnp.exp(v - vs[0]) for v in vs]
        den[j] = ex[j][0] + ex[j][1] + ex[j][2] + ex[j][3]
        onehots[j] = [lane == e for e in es]
        acc = jnp.zeros((ts, LANES), F32)
        for o in onehots[j]:
            acc = acc + o.astype(F32)
        oh[j] = acc

    run = run_sc[...]
    for j in subs:
        before = jnp.dot(strict, oh[j].astype(BF16), preferred_element_type=F32) + run
        run = run + jnp.sum(oh[j], axis=0, keepdims=True)
        eid = jnp.zeros((ts, LANES), F32)
        gate = jnp.zeros((ts, LANES), F32)
        rank = jnp.zeros((ts, LANES), F32)
        for kk in range(TOP_K):
            here = lane == kk
            eid = jnp.where(here, sel_e[j][kk].astype(F32), eid)
            gate = jnp.where(here, ex[j][kk] / den[j], gate)
            rk = jnp.sum(jnp.where(onehots[j][kk], before, 0.0), axis=1, keepdims=True)
            rank = jnp.where(here, rk, rank)
        gate_ref[rows(j), :] = gate
        eid_ref[:, rows(j)] = eid.T[0:8, :]
        rank_ref[:, rows(j)] = rank.T[0:8, :]
    run_sc[...] = run
    cnt_ref[...] = run


def _outproj(x2, yatt, yml, zb2, eg, eb, wa, wm, wo, g1, b1, wr, br, *, tm, ts):
    N, D = x2.shape
    vec = lambda w: pl.BlockSpec((1, w), lambda i: (0, 0))
    full = lambda a, b: pl.BlockSpec((a, b), lambda i: (0, 0))
    tile = lambda w: pl.BlockSpec((tm, w), lambda i: (i, 0))
    return pl.pallas_call(
        functools.partial(_outproj_kernel, tm=tm, ts=ts),
        grid=(N // tm,),
        in_specs=[
            tile(D), tile(ATT_HEADS * ATT_V_DIM), tile(ML_WIDTH),
            pl.BlockSpec((tm, D), lambda i: (i, C_GA // D)),
            pl.BlockSpec((tm, D), lambda i: (i, C_GM // D)),
            vec(D), vec(D),
            full(ATT_HEADS * ATT_V_DIM, D), full(ML_WIDTH, D), full(D, D),
            vec(D), vec(D), full(D, 2 * LANES), vec(LANES),
        ],
        out_specs=[pl.BlockSpec((tm * ROW_TILE, LANES), lambda i: (i, 0)),
                   pl.BlockSpec((8, tm), lambda i: (0, i)), tile(LANES),
                   pl.BlockSpec((8, tm), lambda i: (0, i)), vec(LANES)],
        out_shape=[
            jax.ShapeDtypeStruct((N * ROW_TILE, LANES), F32),
            jax.ShapeDtypeStruct((8, N), F32),
            jax.ShapeDtypeStruct((N, LANES), F32),
            jax.ShapeDtypeStruct((8, N), F32),
            jax.ShapeDtypeStruct((1, LANES), F32),
        ],
        scratch_shapes=[pltpu.VMEM((1, LANES), F32)],
        compiler_params=_cparams(1),
        name="outproj_router",
    )(x2, yatt, yml, zb2, zb2, eg, eb, wa, wm, wo, g1, b1, wr, br)


def _row_copies(src_row, dst_row, sem, n, issue):
    def body(r, carry):
        for kk in range(TOP_K):
            cp = pltpu.make_async_copy(src_row(r, kk), dst_row(r, kk), sem)
            if issue:
                cp.start(priority=kk % 2)
            else:
                cp.wait()
        return carry
    lax.fori_loop(0, n, body, 0, unroll=8)


def _dispatch_kernel(pad_pos_ref, pad_len_ref, d_ref, h_ref, xs_ref, zero_sc, sem, *, tm, max_pad):
    @pl.when(pl.program_id(0) == 0)
    def _():
        zero_sc[...] = jnp.zeros_like(zero_sc)
        bits = [1 << b for b in reversed(range(max_pad.bit_length()))]

        def pad_copies(e, issue):
            pos = pad_pos_ref[e]
            length = pad_len_ref[e]
            for bit in bits:
                take = (length & bit) != 0

                @pl.when(take)
                def _(pos=pos, bit=bit):
                    dst = xs_ref.at[pl.ds(pl.multiple_of(pos * ROW_TILE, ROW_TILE), bit * ROW_TILE)]
                    cp = pltpu.make_async_copy(zero_sc.at[pl.ds(0, bit * ROW_TILE)], dst, sem.at[1])
                    if issue:
                        cp.start()
                    else:
                        cp.wait()
                pos = pos + jnp.where(take, bit, 0)

        def per_expert(e, carry):
            pad_copies(e, True)
            pad_copies(e, False)
            return carry

        lax.fori_loop(0, N_EXPERTS, per_expert, 0)

    src = lambda r, kk: _row_tile(h_ref, r)
    dst = lambda r, kk: _row_tile(xs_ref, d_ref[0, 0, kk * tm + r])
    _row_copies(src, dst, sem.at[0], tm, True)
    _row_copies(src, dst, sem.at[0], tm, False)


def _dispatch(pad_pos, pad_len, dest3, h1t, rows_out, *, tm, max_pad):
    N = h1t.shape[0] // ROW_TILE
    zero_rows = 1 << (max_pad.bit_length() - 1)
    return pl.pallas_call(
        functools.partial(_dispatch_kernel, tm=tm, max_pad=max_pad),
        grid_spec=pltpu.PrefetchScalarGridSpec(
            num_scalar_prefetch=2,
            grid=(N // tm,),
            in_specs=[
                pl.BlockSpec((1, 1, tm * TOP_K), lambda i, pp, pn: (i, 0, 0), memory_space=pltpu.SMEM),
                pl.BlockSpec((tm * ROW_TILE, LANES), lambda i, pp, pn: (i, 0)),
            ],
            out_specs=pl.BlockSpec(memory_space=pl.ANY),
            scratch_shapes=[pltpu.VMEM((zero_rows * ROW_TILE, LANES), F32), pltpu.SemaphoreType.DMA((2,))],
        ),
        out_shape=jax.ShapeDtypeStruct((rows_out * ROW_TILE, LANES), F32),
        compiler_params=_cparams(1),
        name="moe_dispatch",
    )(pad_pos, pad_len, dest3, h1t)


def _select_matrix():
    j = np.arange(2 * LANES)
    sel = np.zeros((2 * LANES, 2 * LANES), np.float32)
    sel[j, (j % 2) * LANES + j // 2] = 1.0
    return jnp.asarray(sel, BF16)


def _expert_kernel(be_ref, nu_ref, nx_ref, so_ref, x_ref, bg_ref, bl_ref, bd_ref, sel_ref, wgu_hbm, wd_hbm, y_ref,
                   wgu_buf, wdn_buf, wg_sc, wl_sc, wd_sc, sem, *, tm):
    i = pl.program_id(0)
    e = be_ref[i]
    slot = so_ref[e]
    active = i < nu_ref[0]
    new_expert = jnp.logical_or(i == 0, e != be_ref[jnp.maximum(i - 1, 0)])

    def weight_copies(ex, sl):
        return (pltpu.make_async_copy(wgu_hbm.at[ex], wgu_buf.at[sl], sem.at[0, sl]),
                pltpu.make_async_copy(wd_hbm.at[ex], wdn_buf.at[sl], sem.at[1, sl]))

    @pl.when(i == 0)
    def _():
        for cp in weight_copies(e, slot):
            cp.start()

    @pl.when(jnp.logical_and(active, new_expert))
    def _():
        for cp in weight_copies(e, slot):
            cp.wait()
        nxt = nx_ref[e]

        @pl.when(nxt < N_EXPERTS)
        def _():
            for cp in weight_copies(nxt, 1 - slot):
                cp.start()

        for c in range(wgu_buf.shape[2] // (2 * LANES)):
            blk = wgu_buf[slot, :, c * 2 * LANES:(c + 1) * 2 * LANES].astype(BF16)
            r = jnp.dot(blk, sel_ref[...], preferred_element_type=F32).astype(BF16)
            wg_sc[:, c * LANES:(c + 1) * LANES] = r[:, :LANES]
            wl_sc[:, c * LANES:(c + 1) * LANES] = r[:, LANES:]
        wd_sc[...] = wdn_buf[slot].astype(BF16)

    @pl.when(active)
    def _():
        xb = _load_row_tiles(x_ref, tm).astype(BF16)
        hg = jnp.dot(xb, wg_sc[...], preferred_element_type=F32) + bg_ref[0]
        hl = jnp.dot(xb, wl_sc[...], preferred_element_type=F32) + bl_ref[0]
        glu = jnp.minimum(hg, SWIGLU_LIMIT)
        lin = jnp.clip(hl, -SWIGLU_LIMIT, SWIGLU_LIMIT)
        act = glu * _sigmoid(SWIGLU_ALPHA * glu) * (lin + 1.0)
        _store_row_tiles(y_ref, jnp.dot(act.astype(BF16), wd_sc[...], preferred_element_type=F32) + bd_ref[0])


def _experts(blk_exp, n_used, next_exp, slot_of_exp, xs, wgu, bg, bl, wd, bd, *, tm):
    MP = xs.shape[0] // ROW_TILE
    _, D, F2 = wgu.shape
    F = F2 // 2
    rows = pl.BlockSpec((tm * ROW_TILE, LANES), lambda i, be, nu, nx, so: (jnp.minimum(i, nu[0] - 1), 0))
    wsp = lambda a, b: pl.BlockSpec((1, a, b), lambda i, be, nu, nx, so: (be[i], 0, 0))
    return pl.pallas_call(
        functools.partial(_expert_kernel, tm=tm),
        grid_spec=pltpu.PrefetchScalarGridSpec(
            num_scalar_prefetch=4,
            grid=(MP // tm,),
            in_specs=[rows, wsp(1, F), wsp(1, F), wsp(1, D),
                      pl.BlockSpec((2 * LANES, 2 * LANES), lambda i, be, nu, nx, so: (0, 0)),
                      pl.BlockSpec(memory_space=pl.ANY), pl.BlockSpec(memory_space=pl.ANY)],
            out_specs=rows,
            scratch_shapes=[pltpu.VMEM((2, D, F2), F32), pltpu.VMEM((2, F, D), F32),
                            pltpu.VMEM((D, F), BF16), pltpu.VMEM((D, F), BF16), pltpu.VMEM((F, D), BF16),
                            pltpu.SemaphoreType.DMA((2, 2))],
        ),
        out_shape=jax.ShapeDtypeStruct(xs.shape, F32),
        compiler_params=_cparams(1),
        name="moe_experts",
    )(blk_exp, n_used, next_exp, slot_of_exp, xs, bg, bl, bd, _select_matrix(), wgu, wd)


def _combine_kernel(dcur_ref, dnxt_ref, h1_ref, gate_ref, g2_ref, b2_ref, ys_ref, o_ref, ybuf, sem, *, tm, nsteps):
    i = pl.program_id(0)
    slot = i % 2

    def copies(d_ref, s, issue):
        src = lambda r, kk: _row_tile(ys_ref, d_ref[0, 0, kk * tm + r])
        dst = lambda r, kk: _row_tile(ybuf.at[s, kk], r)
        _row_copies(src, dst, sem.at[s], tm, issue)

    @pl.when(i == 0)
    def _():
        copies(dcur_ref, 0, True)

    @pl.when(i + 1 < nsteps)
    def _():
        copies(dnxt_ref, 1 - slot, True)

    copies(dcur_ref, slot, False)
    acc = DN_ALPHA * _load_row_tiles(h1_ref, tm)
    for kk in range(TOP_K):
        acc = acc + gate_ref[:, kk:kk + 1] * _load_row_tiles(ybuf.at[slot, kk], tm)
    o_ref[...] = _ln(acc, g2_ref[...], b2_ref[...])


def _combine(dest3, h1t, gate, g2, b2, ys, *, tm):
    N, D = h1t.shape[0] // ROW_TILE, D_MODEL
    nsteps = N // tm
    dspec = lambda f: pl.BlockSpec((1, 1, tm * TOP_K), f, memory_space=pltpu.SMEM)
    return pl.pallas_call(
        functools.partial(_combine_kernel, tm=tm, nsteps=nsteps),
        grid=(nsteps,),
        in_specs=[
            dspec(lambda i: (i, 0, 0)),
            dspec(lambda i: (jnp.minimum(i + 1, nsteps - 1), 0, 0)),
            pl.BlockSpec((tm * ROW_TILE, LANES), lambda i: (i, 0)),
            pl.BlockSpec((tm, LANES), lambda i: (i, 0)),
            pl.BlockSpec((1, D), lambda i: (0, 0)),
            pl.BlockSpec((1, D), lambda i: (0, 0)),
            pl.BlockSpec(memory_space=pl.ANY),
        ],
        out_specs=pl.BlockSpec((tm, D), lambda i: (i, 0)),
        out_shape=jax.ShapeDtypeStruct((N, D), F32),
        scratch_shapes=[pltpu.VMEM((2, TOP_K, tm * ROW_TILE, LANES), F32), pltpu.SemaphoreType.DMA((2,))],
        compiler_params=_cparams(1),
        name="moe_combine",
    )(dest3, dest3, h1t, gate, g2, b2, ys)


def _rotary_column_order():
    idx = np.empty((ATT_HEADS * LANES,), np.int32)
    for h in range(ATT_HEADS):
        for half in range(2):
            for sub in range(2):
                for dd in range(32):
                    idx[h * LANES + half * 64 + sub * 32 + dd] = (2 * h + sub) * ATT_QK_DIM + half * 32 + dd
    return idx


def _rotary_tables(pos):
    half = ATT_QK_DIM // 2
    inv_freq = ROPE_THETA ** (-jnp.arange(half, dtype=F32) / half)
    ang = pos.astype(F32)[:, None] * inv_freq[None, :]
    cos = jnp.tile(jnp.cos(ang), (1, 4))
    sin = jnp.tile(jnp.sin(ang), (1, 4))
    sign = jnp.where(jnp.arange(LANES) < 64, -1.0, 1.0).astype(F32)
    return cos, sin * sign[None, :]


def _pick(n, prefs):
    for t in prefs:
        if n % t == 0:
            return t
    raise ValueError(f"unsupported size {n}")


def kernel(x, meta, emb_ln_g, emb_ln_b, w_in, conv_w, conv_b, gate_bias, lam_q1, lam_k1, lam_q2, lam_k2,
           att_norm_g, ml_norm_g, w_att_out, w_ml_out, w_o, ln1_g, ln1_b, w_router, b_router,
           w_gu, b_gu, w_down, b_down, ln2_g, ln2_b):
    B, S, D = x.shape
    assert D == D_MODEL and S % 512 == 0 and w_in.shape[0] == DEPTH
    N = B * S
    row2 = lambda a: a.reshape(1, -1)

    w = w_in[0]
    o_aq, o_ak, o_av, o_mqk, o_mv, o_mo, o_gt, o_ga, o_gm = 0, 512, 1024, 1536, 2560, 3072, 3584, 3592, 4616
    perm = _rotary_column_order()
    gpad = jnp.zeros((D, LANES - ML_HEADS), F32)
    w_all = jnp.concatenate([
        w[:, o_ga:o_ga + D], w[:, o_gm:o_gm + D], w[:, o_mqk:o_mqk + 2 * ML_WIDTH],
        w[:, o_aq:o_aq + 512][:, perm], w[:, o_ak:o_ak + 512][:, perm], w[:, o_av:o_av + 512],
        w[:, o_mv:o_mv + ML_WIDTH], w[:, o_mo:o_mo + ML_WIDTH],
        w[:, o_gt:o_gt + ML_HEADS], gpad, w[:, o_gt + ML_HEADS:o_gt + 2 * ML_HEADS], gpad,
    ], axis=1).astype(BF16)
    gb = gate_bias[0]
    zpad = jnp.zeros((LANES - ML_HEADS,), F32)
    gate_b = jnp.concatenate([gb[:ML_HEADS], zpad, gb[ML_HEADS:], zpad]).reshape(1, GATE_W)

    cos_x, sin_x = _rotary_tables(N_META + jnp.arange(S))
    cos_m, sin_m = _rotary_tables(jnp.maximum(jnp.arange(BLOCK) - PAD, 0))

    eg, eb = row2(emb_ln_g), row2(emb_ln_b)

    tm_in = _pick(S, (1024, 512))
    zb, gates = _inproj(x, eg, eb, w_all, cos_x, sin_x, tm=tm_in)
    xm = jnp.concatenate([jnp.zeros((PAD, D), x.dtype), meta.astype(x.dtype)], axis=0)[None]
    zbm, gatesm = _inproj(xm, eg, eb, w_all, cos_m, sin_m, tm=BLOCK, first_valid_row=PAD)

    tq = 512
    yatt = _attention(zb, zbm, row2(lam_q1[0]), row2(lam_k1[0]), row2(lam_q2[0]), row2(lam_k2[0]),
                      row2(att_norm_g[0]), tq=tq, nh=1)
    yml = _mlstm(zb, gates, zbm, gatesm, conv_w[0], row2(conv_b[0]), gate_b, row2(ml_norm_g[0]),
                 nbps=4 if B % 4 == 0 else (2 if B % 2 == 0 else 1))

    wr32 = jnp.concatenate([w_router[0], jnp.zeros((D, LANES - N_EXPERTS), F32)], axis=1)
    wr_hi = wr32.astype(BF16)
    wr = jnp.concatenate([wr_hi, (wr32 - wr_hi.astype(F32)).astype(BF16)], axis=1)
    br = jnp.concatenate([b_router[0], jnp.full((LANES - N_EXPERTS,), NEG, F32)]).reshape(1, LANES)
    tm_out = 1024
    h1, eid, gate, rank, cnt = _outproj(
        x.reshape(N, D), yatt.reshape(N, -1), yml.reshape(N, -1), zb.reshape(N, ZB_W), eg, eb,
        w_att_out[0].astype(BF16), w_ml_out[0].astype(BF16), w_o[0].astype(BF16),
        row2(ln1_g[0]), row2(ln1_b[0]), wr, br, tm=tm_out, ts=512)

    tm_e = 512
    M = N * TOP_K
    nb = (M + N_EXPERTS * (tm_e - 1) + tm_e - 1) // tm_e
    counts = cnt[0, :N_EXPERTS].astype(jnp.int32)
    nblk = (counts + tm_e - 1) // tm_e
    cum = jnp.cumsum(nblk)
    pstart = (cum - nblk) * tm_e
    eid_t = eid[:TOP_K].astype(jnp.int32)
    base = jnp.sum(jnp.where(eid_t[None] == jnp.arange(N_EXPERTS)[:, None, None], pstart[:, None, None], 0), axis=0)
    dest = base + rank[:TOP_K].astype(jnp.int32)
    n_used = cum[-1:].astype(jnp.int32)
    blk = jnp.minimum(jnp.arange(nb, dtype=jnp.int32), n_used[0] - 1)
    blk_exp = jnp.sum(blk[:, None] >= cum[None, :], axis=1).astype(jnp.int32)

    def dest_tiles(tm):
        return dest.reshape(TOP_K, N // tm, tm).transpose(1, 0, 2).reshape(N // tm, 1, TOP_K * tm)

    tm_d, tm_c = _pick(N, (1024, 512, 256)), 256
    xs = _dispatch(pstart + counts, nblk * tm_e - counts, dest_tiles(tm_d), h1, nb * tm_e, tm=tm_d,
                   max_pad=tm_e - 1)

    has = nblk > 0
    ids = jnp.where(has, jnp.arange(N_EXPERTS, dtype=jnp.int32), N_EXPERTS)
    after = jnp.concatenate([ids[1:], jnp.full((1,), N_EXPERTS, jnp.int32)])
    next_exp = lax.cummin(after, axis=0, reverse=True).astype(jnp.int32)
    slot_of_exp = ((jnp.cumsum(has.astype(jnp.int32)) - 1) & 1).astype(jnp.int32)
    bgu = b_gu[0]
    ys = _experts(blk_exp, n_used, next_exp, slot_of_exp, xs, w_gu[0], bgu[:, None, 0::2], bgu[:, None, 1::2],
                  w_down[0], b_down[0][:, None, :], tm=tm_e)

    out = _combine(dest_tiles(tm_c), h1, gate, row2(ln2_g[0]), row2(ln2_b[0]), ys, tm=tm_c)
    return out.reshape(B, S, D)
```
